```python
import jax
import jax.numpy as jnp
from jax import lax
import numpy as np

D_MODEL = 1024
BATCH = 2
SEQ = 8192
DEPTH = 2
DEC_BATCH = 128
DEC_SEQ = 1
PAST_LEN = 8192
PAGE_SIZE = 128

D_MIX = D_MODEL
A_HEADS = 8
A_KV_HEADS = 2
A_HEAD_DIM = 64
A_GROUP = A_HEADS // A_KV_HEADS
A_WIDTH = A_HEADS * A_HEAD_DIM
KV_WIDTH = A_KV_HEADS * A_HEAD_DIM
WINDOW = 128
BLOCK = 128
ROPE_THETA = 10000.0
B_WIDTH = D_MIX // 4
B_HEAD_DIM = 64
B_HEADS = B_WIDTH // B_HEAD_DIM
DECAY_LORA = 32
AAA_LORA = 32
B_SHIFT = 3 * B_WIDTH + DECAY_LORA + AAA_LORA
LNX_EPS = 1e-5 * 8 ** 2
C_WIDTH = D_MIX - A_WIDTH - B_WIDTH
C_BLOCKS = 4
C_BLOCK_DIM = C_WIDTH // C_BLOCKS
CONV_W = 4
LRU_C = 8.0
EPS = 1e-6
IN_SIZES = (A_WIDTH, KV_WIDTH, KV_WIDTH, A_WIDTH, B_SHIFT, B_WIDTH, C_WIDTH, C_WIDTH)
P_IN = sum(IN_SIZES)
IN_OFFSETS = tuple(int(o) for o in np.cumsum(IN_SIZES)[:-1])
B_OFFSETS = (B_WIDTH, 2 * B_WIDTH, 3 * B_WIDTH, 3 * B_WIDTH + DECAY_LORA)

kernel_name = "hymba_swa_rwkv7_rglru_step"


def rmsnorm(x, g):
    xf = x.astype(jnp.float32)
    xf = xf * lax.rsqrt(jnp.mean(xf * xf, axis=-1, keepdims=True) + EPS)
    return (xf * g.astype(jnp.float32)).astype(x.dtype)


def rope(x, pos):
    half = x.shape[-1] // 2
    inv_freq = ROPE_THETA ** (-jnp.arange(half, dtype=jnp.float32) / half)
    ang = pos.astype(jnp.float32)[:, None] * inv_freq[None, :]
    cos = jnp.cos(ang)[:, None, :]
    sin = jnp.sin(ang)[:, None, :]
    xf = x.astype(jnp.float32)
    x1, x2 = xf[..., :half], xf[..., half:]
    return jnp.concatenate([x1 * cos - x2 * sin, x2 * cos + x1 * sin], axis=-1).astype(x.dtype)


def sink_attention(s, mask, sinks, v, eq):
    sink = sinks.astype(jnp.float32).reshape(A_KV_HEADS, A_GROUP)[:, :, None, None]
    s = jnp.where(mask, s, -jnp.inf)
    m = jnp.maximum(jnp.max(s, axis=-1, keepdims=True), sink)
    p = jnp.exp(s - m)
    denom = jnp.sum(p, axis=-1, keepdims=True) + jnp.exp(sink - m)
    return jnp.einsum(eq, (p / denom).astype(v.dtype), v)


def swa_banded(q, k, v, sinks):
    bsz, t = q.shape[:2]
    nb = t // BLOCK
    qb = q.reshape(bsz, nb, BLOCK, A_KV_HEADS, A_GROUP, A_HEAD_DIM)
    kb = k.reshape(bsz, nb, BLOCK, A_KV_HEADS, A_HEAD_DIM)
    vb = v.reshape(bsz, nb, BLOCK, A_KV_HEADS, A_HEAD_DIM)

    def band(xb):
        prev = jnp.concatenate([jnp.zeros_like(xb[:, :1]), xb[:, :-1]], axis=1)
        return jnp.concatenate([prev, xb], axis=2)

    kk, vv = band(kb), band(vb)
    blk = jnp.arange(nb)[:, None] * BLOCK
    qpos = blk + jnp.arange(BLOCK)[None, :]
    kpos = blk - BLOCK + jnp.arange(2 * BLOCK)[None, :]
    diff = qpos[:, :, None] - kpos[:, None, :]
    mask = (diff >= 0) & (diff <= WINDOW) & (kpos[:, None, :] >= 0)
    s = jnp.einsum("bnqkgd,bnskd->bnkgqs", qb, kk, preferred_element_type=jnp.float32) * (A_HEAD_DIM ** -0.5)
    o = sink_attention(s, mask[None, :, None, None], sinks, vv, "bnkgqs,bnskd->bnqkgd")
    return o.reshape(bsz, t, A_WIDTH)


def swa_cached(q, k, v, k_buf, v_buf, sinks, pos0):
    bsz, t = q.shape[:2]
    w_buf = k_buf.shape[1]
    keys = jnp.concatenate([k_buf.astype(k.dtype), k], axis=1)
    vals = jnp.concatenate([v_buf.astype(v.dtype), v], axis=1)
    qpos = pos0 + jnp.arange(t)
    kpos = jnp.concatenate([pos0 - w_buf + jnp.arange(w_buf), qpos])
    diff = qpos[:, None] - kpos[None, :]
    mask = (diff >= 0) & (diff <= WINDOW)
    qg = q.reshape(bsz, t, A_KV_HEADS, A_GROUP, A_HEAD_DIM)
    s = jnp.einsum("bqkgd,bskd->bkgqs", qg, keys, preferred_element_type=jnp.float32) * (A_HEAD_DIM ** -0.5)
    o = sink_attention(s, mask, sinks, vals, "bkgqs,bskd->bqkgd")
    return o.reshape(bsz, t, A_WIDTH)


def wkv_step(s, inp):
    r, w, k, v, kk, a = inp
    sa = jnp.einsum("bhvk,bhk->bhv", s, -kk)
    s = s * w[:, :, None, :] + sa[..., None] * (kk * a)[:, :, None, :] + v[..., None] * k[:, :, None, :]
    return s, jnp.einsum("bhvk,bhk->bhv", s, r)


def rwkv7_time_mix(p, g, shift_prev, wkv0, lp):
    f32 = jnp.float32
    bsz, t, _ = p.shape
    prev = jnp.concatenate([shift_prev[:, None, :].astype(p.dtype), p[:, :-1]], axis=1)
    xs = p + (prev - p) * lp["rwkv_mu"]
    r, k, v, wd, ad = jnp.split(xs, B_OFFSETS, axis=-1)
    w_log = -jax.nn.softplus(-(lp["rwkv_w0"] + jnp.tanh(wd) @ lp["rwkv_w_up"]).astype(f32)) - 0.5
    decay = jnp.exp(-jnp.exp(w_log))
    a = jax.nn.sigmoid((lp["rwkv_a0"] + ad @ lp["rwkv_a_up"]).astype(f32))

    def heads(z):
        return z.astype(f32).reshape(bsz, t, B_HEADS, B_HEAD_DIM)

    kk = heads(k * lp["rwkv_k_k"])
    kk = kk / jnp.maximum(jnp.sqrt(jnp.sum(kk * kk, axis=-1, keepdims=True)), 1e-12)
    k_mod = k.astype(f32) * (1.0 + (a - 1.0) * lp["rwkv_k_a"])
    rh, wh, kh, vh, ah = heads(r), heads(decay), heads(k_mod), heads(v), heads(a)
    seq = tuple(jnp.moveaxis(z, 1, 0) for z in (rh, wh, kh, vh, kk, ah))
    s_last, y = lax.scan(wkv_step, wkv0.astype(f32), seq)
    y = jnp.moveaxis(y, 0, 1)
    mu = jnp.mean(y, axis=-1, keepdims=True)
    var = jnp.mean(jnp.square(y - mu), axis=-1, keepdims=True)
    yn = ((y - mu) * lax.rsqrt(var + LNX_EPS)).reshape(bsz, t, B_WIDTH)
    yn = yn * lp["rwkv_lnx_w"] + lp["rwkv_lnx_b"]
    bonus = jnp.sum(rh * kh * lp["rwkv_r_k"], axis=-1, keepdims=True) * vh
    out = (yn + bonus.reshape(bsz, t, B_WIDTH)) * jax.nn.silu(g.astype(f32))
    return out.astype(p.dtype), p[:, -1], s_last.astype(wkv0.dtype)


def lru_combine(c1, c2):
    a1, b1 = c1
    a2, b2 = c2
    return a1 * a2, a2 * b1 + b2


def rglru_mix(xc, g, conv_buf, h0, pos0, lp):
    f32 = jnp.float32
    bsz, t, _ = xc.shape
    xpad = jnp.concatenate([conv_buf.astype(xc.dtype), xc], axis=1)
    conv_w = lp["lru_conv_w"]
    u = lp["lru_conv_b"] + xpad[:, 0:t] * conv_w[0]
    for i in range(1, CONV_W):
        u = u + xpad[:, i:i + t] * conv_w[i]
    new_buf = xpad[:, -(CONV_W - 1):]
    ub = u.reshape(bsz, t, C_BLOCKS, C_BLOCK_DIM)
    gr = jax.nn.sigmoid((jnp.einsum("btnd,nde->btne", ub, lp["lru_gate_a_w"]).reshape(bsz, t, C_WIDTH)
                         + lp["lru_gate_a_b"]).astype(f32))
    gi = jax.nn.sigmoid((jnp.einsum("btnd,nde->btne", ub, lp["lru_gate_x_w"]).reshape(bsz, t, C_WIDTH)
                         + lp["lru_gate_x_b"]).astype(f32))
    log_a = -LRU_C * gr * jax.nn.softplus(-lp["lru_lambda"].astype(f32))
    reset = ((pos0 + jnp.arange(t)) == 0)[None, :, None]
    a = jnp.where(reset, 0.0, jnp.exp(log_a))
    mult = jnp.where(reset, 1.0, jnp.sqrt(-jnp.expm1(2.0 * log_a)))
    b = mult * gi * u.astype(f32)
    b = b.at[:, 0].add(a[:, 0] * h0.astype(f32))
    _, h = lax.associative_scan(lru_combine, (a, b), axis=1)
    out = h * jax.nn.silu(g.astype(f32))
    return out.astype(xc.dtype), new_buf, h[:, -1].astype(h0.dtype)


def hybrid_layer(x, c, pos0, k_buf, v_buf, shift_prev, wkv0, conv_buf, h0, lp):
    bsz, t, _ = x.shape
    mod = jax.nn.silu(c) @ lp["w_mod"] + lp["b_mod"]
    shift_m, scale_m, gate_m = jnp.split(mod, 3, axis=-1)
    h = rmsnorm(x, lp["norm_pre"]) * (1.0 + scale_m[:, None]) + shift_m[:, None]
    proj = h @ lp["w_in"]
    qa, ka, va, ga, pb, gb, xc, gc = jnp.split(proj, IN_OFFSETS, axis=-1)
    pos = pos0 + jnp.arange(t)
    q = rope(qa.reshape(bsz, t, A_HEADS, A_HEAD_DIM), pos)
    k = rope(ka.reshape(bsz, t, A_KV_HEADS, A_HEAD_DIM), pos)
    v = va.reshape(bsz, t, A_KV_HEADS, A_HEAD_DIM)
    if k_buf is None:
        ya = swa_banded(q, k, v, lp["attn_sinks"])
        k_new, v_new = k[:, -WINDOW:], v[:, -WINDOW:]
    else:
        ya = swa_cached(q, k, v, k_buf, v_buf, lp["attn_sinks"], pos0)
        k_new, v_new = k, v
    ya = ya * jax.nn.silu(ga)
    yb, shift_new, wkv_new = rwkv7_time_mix(pb, gb, shift_prev, wkv0, lp)
    yc, conv_new, h_new = rglru_mix(xc, gc, conv_buf, h0, pos0, lp)
    y = jnp.concatenate([ya, yb, yc], axis=-1) @ lp["w_out"]
    x = x + gate_m[:, None] * rmsnorm(y, lp["norm_post"])
    return x, (k_new, v_new, shift_new, wkv_new, conv_new, h_new)


def setup_inputs(seed: int = 0) -> dict:
    key = jax.random.key(seed)
    ks = iter(jax.random.split(key, 48))

    def nrm(shape, scale=1.0):
        return scale * jax.random.normal(next(ks), shape, jnp.float32)

    def uni(shape, lo, hi):
        return jax.random.uniform(next(ks), shape, jnp.float32, lo, hi)

    w_buf = min(WINDOW, PAST_LEN)
    lru_a = uni((DEPTH, C_WIDTH), 0.9, 0.999) ** (1.0 / LRU_C)
    return {
        "x_prompt": nrm((BATCH, SEQ, D_MODEL)),
        "x_sample": nrm((DEC_BATCH, DEC_SEQ, D_MODEL)),
        "c_prompt": nrm((BATCH, D_MODEL)),
        "c_sample": nrm((DEC_BATCH, D_MODEL)),
        "cache_swa_k": nrm((DEPTH, DEC_BATCH, w_buf, A_KV_HEADS, A_HEAD_DIM)),
        "cache_swa_v": nrm((DEPTH, DEC_BATCH, w_buf, A_KV_HEADS, A_HEAD_DIM)),
        "state_rwkv_shift": nrm((DEPTH, DEC_BATCH, B_SHIFT)),
        "state_rwkv_wkv": nrm((DEPTH, DEC_BATCH, B_HEADS, B_HEAD_DIM, B_HEAD_DIM), 0.3),
        "state_lru_conv": nrm((DEPTH, DEC_BATCH, CONV_W - 1, C_WIDTH)),
        "state_lru_h": nrm((DEPTH, DEC_BATCH, C_WIDTH), 0.5),
        "norm_pre": 1.0 + nrm((DEPTH, D_MODEL), 0.05),
        "norm_post": 1.0 + nrm((DEPTH, D_MODEL), 0.05),
        "w_mod": nrm((DEPTH, D_MODEL, 3 * D_MODEL), 0.5 * D_MODEL ** -0.5),
        "b_mod": nrm((DEPTH, 3 * D_MODEL), 0.02),
        "w_in": nrm((DEPTH, D_MODEL, P_IN), D_MODEL ** -0.5),
        "w_out": nrm((DEPTH, D_MIX, D_MODEL), D_MIX ** -0.5),
        "attn_sinks": nrm((DEPTH, A_HEADS), 0.5),
        "rwkv_mu": uni((DEPTH, B_SHIFT), 0.0, 1.0),
        "rwkv_w0": uni((DEPTH, B_WIDTH), -6.5, -1.5),
        "rwkv_w_up": nrm((DEPTH, DECAY_LORA, B_WIDTH), 0.1),
        "rwkv_a0": nrm((DEPTH, B_WIDTH), 0.1),
        "rwkv_a_up": nrm((DEPTH, AAA_LORA, B_WIDTH), 0.1),
        "rwkv_k_k": 0.85 + nrm((DEPTH, B_WIDTH), 0.05),
        "rwkv_k_a": 1.0 + nrm((DEPTH, B_WIDTH), 0.05),
        "rwkv_r_k": nrm((DEPTH, B_HEADS, B_HEAD_DIM), 0.1),
        "rwkv_lnx_w": 1.0 + nrm((DEPTH, B_WIDTH), 0.05),
        "rwkv_lnx_b": nrm((DEPTH, B_WIDTH), 0.02),
        "lru_conv_w": nrm((DEPTH, CONV_W, C_WIDTH), CONV_W ** -0.5),
        "lru_conv_b": nrm((DEPTH, C_WIDTH), 0.02),
        "lru_gate_a_w": nrm((DEPTH, C_BLOCKS, C_BLOCK_DIM, C_BLOCK_DIM), C_BLOCK_DIM ** -0.5),
        "lru_gate_a_b": nrm((DEPTH, C_WIDTH), 0.02),
        "lru_gate_x_w": nrm((DEPTH, C_BLOCKS, C_BLOCK_DIM, C_BLOCK_DIM), C_BLOCK_DIM ** -0.5),
        "lru_gate_x_b": nrm((DEPTH, C_WIDTH), 0.02),
        "lru_lambda": jnp.log(lru_a) - jnp.log1p(-lru_a),
    }


def reference(x_prompt, x_sample, c_prompt, c_sample,
              cache_swa_k, cache_swa_v, state_rwkv_shift, state_rwkv_wkv, state_lru_conv, state_lru_h,
              norm_pre, norm_post, w_mod, b_mod, w_in, w_out, attn_sinks,
              rwkv_mu, rwkv_w0, rwkv_w_up, rwkv_a0, rwkv_a_up, rwkv_k_k, rwkv_k_a, rwkv_r_k,
              rwkv_lnx_w, rwkv_lnx_b,
              lru_conv_w, lru_conv_b, lru_gate_a_w, lru_gate_a_b, lru_gate_x_w, lru_gate_x_b, lru_lambda):
    yp, ys = x_prompt, x_sample
    bp = x_prompt.shape[0]
    dt = x_prompt.dtype
    outs_p, outs_s = [], []
    for l in range(DEPTH):
        lp = {
            "norm_pre": norm_pre[l], "norm_post": norm_post[l], "w_mod": w_mod[l], "b_mod": b_mod[l],
            "w_in": w_in[l], "w_out": w_out[l], "attn_sinks": attn_sinks[l],
            "rwkv_mu": rwkv_mu[l], "rwkv_w0": rwkv_w0[l], "rwkv_w_up": rwkv_w_up[l],
            "rwkv_a0": rwkv_a0[l], "rwkv_a_up": rwkv_a_up[l], "rwkv_k_k": rwkv_k_k[l],
            "rwkv_k_a": rwkv_k_a[l], "rwkv_r_k": rwkv_r_k[l],
            "rwkv_lnx_w": rwkv_lnx_w[l], "rwkv_lnx_b": rwkv_lnx_b[l],
            "lru_conv_w": lru_conv_w[l], "lru_conv_b": lru_conv_b[l],
            "lru_gate_a_w": lru_gate_a_w[l], "lru_gate_a_b": lru_gate_a_b[l],
            "lru_gate_x_w": lru_gate_x_w[l], "lru_gate_x_b": lru_gate_x_b[l],
            "lru_lambda": lru_lambda[l],
        }
        yp, st_p = hybrid_layer(
            yp, c_prompt, 0, None, None,
            jnp.zeros((bp, B_SHIFT), dt),
            jnp.zeros((bp, B_HEADS, B_HEAD_DIM, B_HEAD_DIM), dt),
            jnp.zeros((bp, CONV_W - 1, C_WIDTH), dt),
            jnp.zeros((bp, C_WIDTH), dt), lp)
        ys, st_s = hybrid_layer(
            ys, c_sample, PAST_LEN, cache_swa_k[l], cache_swa_v[l],
            state_rwkv_shift[l], state_rwkv_wkv[l], state_lru_conv[l], state_lru_h[l], lp)
        outs_p.append(st_p)
        outs_s.append(st_s)
    swa_k_p, swa_v_p, shift_p, wkv_p, conv_p, h_p = [jnp.stack(z) for z in zip(*outs_p)]
    swa_k_s, swa_v_s, shift_s, wkv_s, conv_s, h_s = [jnp.stack(z) for z in zip(*outs_s)]
    return (yp, ys, swa_k_p, swa_v_p, shift_p, wkv_p, conv_p, h_p,
            swa_k_s, swa_v_s, shift_s, wkv_s, conv_s, h_s)
```

```python
import functools

import numpy as np
import jax
import jax.numpy as jnp
from jax import lax
from jax.experimental import pallas as pl
from jax.experimental.pallas import tpu as pltpu

F32 = jnp.float32
BF16 = jnp.bfloat16

D_MODEL = 1024
DEPTH = 2
A_HEADS = 8
A_KV_HEADS = 2
HEAD_DIM = 64
A_WIDTH = A_HEADS * HEAD_DIM
KV_WIDTH = A_KV_HEADS * HEAD_DIM
WINDOW = 128
Q_BLOCK = 128
ROPE_THETA = 10000.0
B_WIDTH = 256
B_HEADS = 4
LORA = 32
B_SHIFT = 3 * B_WIDTH + 2 * LORA
B_SHIFT_PAD = 3 * B_WIDTH + 128
LNX_EPS = 1e-5 * 8 ** 2
C_WIDTH = 256
C_BLOCKS = 4
CONV_W = 4
LRU_C = 8.0
EPS = 1e-6
PAST_LEN = 8192

LANES = 128
SUBLANES = 8
VMEM_LIMIT_BYTES = 56 * 1024 * 1024

OFF_Q = 0
OFF_K = OFF_Q + A_WIDTH
OFF_V = OFF_K + KV_WIDTH
OFF_GA = OFF_V + KV_WIDTH
OFF_PB = OFF_GA + A_WIDTH
OFF_GB = OFF_PB + B_SHIFT_PAD
OFF_XC = OFF_GB + B_WIDTH
OFF_GC = OFF_XC + C_WIDTH
P_PAD = OFF_GC + C_WIDTH
P_ORIG_PB_END = OFF_PB + B_SHIFT

CHUNK = 64
TIME_BLOCK = 256
SAMPLE_BLOCK = 32

(V_W0, V_A0, V_KK, V_KA, V_LNW, V_LNB, V_RK, V_CB, V_GAB, V_GXB, V_LAM,
 V_CW0, V_CW1, V_CW2, V_CW3) = range(15)
VEC_ROWS = 16


def _bf(x):
    return x.astype(BF16)


def _dot(a, b):
    return jnp.dot(a, b, preferred_element_type=F32)


def _mm(a, b):
    return _dot(_bf(a), _bf(b))


def _mm_nt(a, b):
    return lax.dot_general(_bf(a), _bf(b), (((1,), (1,)), ((), ())), preferred_element_type=F32)


def _mm_tn(a, b):
    return lax.dot_general(_bf(a), _bf(b), (((0,), (0,)), ((), ())), preferred_element_type=F32)


def _split2(x):
    hi = _bf(x)
    lo = _bf(x - hi.astype(F32))
    return hi, lo


def _split3(x):
    hi = _bf(x)
    r1 = x - hi.astype(F32)
    mid = _bf(r1)
    lo = _bf(r1 - mid.astype(F32))
    return hi, mid, lo


def _mm3(a, b):
    ah, al = _split2(a)
    bh, bl = _split2(b)
    return _dot(ah, bh) + _dot(ah, bl) + _dot(al, bh)


def _mm3_nt(a, b):
    ah, al = _split2(a)
    bh, bl = _split2(b)
    dn = (((1,), (1,)), ((), ()))
    f = lambda u, v: lax.dot_general(u, v, dn, preferred_element_type=F32)
    return f(ah, bh) + f(ah, bl) + f(al, bh)


def _mm3_tn(a, b):
    ah, al = _split2(a)
    bh, bl = _split2(b)
    dn = (((0,), (0,)), ((), ()))
    f = lambda u, v: lax.dot_general(u, v, dn, preferred_element_type=F32)
    return f(ah, bh) + f(ah, bl) + f(al, bh)


def _mm_exact_rhs(a, sel):
    hi, mid, lo = _split3(a)
    return _dot(hi, sel) + _dot(mid, sel) + _dot(lo, sel)


def _mm_exact_lhs(sel, b):
    hi, lo = _split2(b)
    return _dot(sel, hi) + _dot(sel, lo)


def _iota(shape, dim):
    return lax.broadcasted_iota(jnp.int32, shape, dim)


def _sigmoid(x):
    return jax.nn.sigmoid(x)


def _silu(x):
    return x * _sigmoid(x)


def _softplus(x):
    return jnp.maximum(x, 0.0) + jnp.log1p(jnp.exp(-jnp.abs(x)))


def _head_ones(n):
    r = _iota((n, n), 0) // HEAD_DIM
    c = _iota((n, n), 1) // HEAD_DIM
    return jnp.where(r == c, 1.0, 0.0).astype(BF16)


def _head_sum(x, ones_bd):
    hi, lo = _split2(x)
    return _dot(hi, ones_bd) + _dot(lo, ones_bd)


def _rmsnorm(x, g):
    ms = jnp.mean(x * x, axis=-1, keepdims=True)
    return x * lax.rsqrt(ms + EPS) * g


def _rope128(z, cos, sin_signed):
    lane = _iota(z.shape, 1)
    first = (lane & 32) == 0
    sw = jnp.where(first, pltpu.roll(z, 96, 1), pltpu.roll(z, 32, 1))
    return z * cos + sw * sin_signed


def _rwkv_pre(xs, vec, wlora, ones_bd):
    r = xs[:, 0:B_WIDTH]
    k = xs[:, B_WIDTH:2 * B_WIDTH]
    v = xs[:, 2 * B_WIDTH:3 * B_WIDTH]
    lor = xs[:, 3 * B_WIDTH:B_SHIFT_PAD]
    lane = _iota(lor.shape, 1)
    z = jnp.where(lane < LORA, jnp.tanh(lor), lor)
    wa = _mm(z, wlora)
    zw = vec[V_W0:V_W0 + 1, :] + wa[:, :B_WIDTH]
    w_log = -_softplus(-zw) - 0.5
    ld = -jnp.exp(w_log)
    a = _sigmoid(vec[V_A0:V_A0 + 1, :] + wa[:, B_WIDTH:])
    kk = k * vec[V_KK:V_KK + 1, :]
    nrm = jnp.sqrt(_head_sum(kk * kk, ones_bd))
    kkn = kk / jnp.maximum(nrm, 1e-12)
    kmod = k * (1.0 + (a - 1.0) * vec[V_KA:V_KA + 1, :])
    return r, k, v, ld, a, kkn, kmod


def _rwkv_post(y, r, kmod, v, gb, vec, ones_bd):
    inv = 1.0 / HEAD_DIM
    mu = _head_sum(y, ones_bd) * inv
    yc = y - mu
    var = _head_sum(yc * yc, ones_bd) * inv
    yn = yc * lax.rsqrt(var + LNX_EPS)
    yn = yn * vec[V_LNW:V_LNW + 1, :] + vec[V_LNB:V_LNB + 1, :]
    bonus = _head_sum(r * kmod * vec[V_RK:V_RK + 1, :], ones_bd) * v
    return (yn + bonus) * _silu(gb)


def _lru_gates(u, vec, wgate):
    gates = _mm(u, wgate)
    gr = _sigmoid(gates[:, :C_WIDTH] + vec[V_GAB:V_GAB + 1, :])
    gi = _sigmoid(gates[:, C_WIDTH:] + vec[V_GXB:V_GXB + 1, :])
    sp = _softplus(-vec[V_LAM:V_LAM + 1, :])
    log_a = -LRU_C * gr * sp
    a = jnp.exp(log_a)
    mult = jnp.sqrt(1.0 - a * a)
    return a, mult, gi


def _mod_kernel(c_ref, w_ref, b_ref, o_ref):
    c = c_ref[...]
    o_ref[0] = _mm(_silu(c), w_ref[0]) + b_ref[0]


def _mod_call(c_all, w_mod, b_mod):
    rows = c_all.shape[0]
    n_col = 3 * D_MODEL // D_MODEL
    return pl.pallas_call(
        _mod_kernel,
        grid=(DEPTH, n_col),
        in_specs=[
            pl.BlockSpec((rows, D_MODEL), lambda l, j: (0, 0)),
            pl.BlockSpec((1, D_MODEL, D_MODEL), lambda l, j: (l, 0, j)),
            pl.BlockSpec((1, 1, D_MODEL), lambda l, j: (l, 0, j)),
        ],
        out_specs=pl.BlockSpec((1, rows, D_MODEL), lambda l, j: (l, 0, j)),
        out_shape=jax.ShapeDtypeStruct((DEPTH, rows, 3 * D_MODEL), F32),
        compiler_params=pltpu.CompilerParams(
            dimension_semantics=("arbitrary", "arbitrary"), vmem_limit_bytes=VMEM_LIMIT_BYTES),
        name="adaln_mod",
    )(c_all, w_mod, b_mod.reshape(DEPTH, 1, 3 * D_MODEL))


def _prompt_kernel(sinks_ref, x_ref, mod_ref, npre_ref, npost_ref, win_ref, wout_ref,
                   cos_ref, sin_ref, mu_ref, vec_ref, wlora_ref, wgate_ref,
                   y_ref, kv_ref, shift_ref, wkv_ref, conv_ref, hlru_ref,
                   kvprev, pprev, xcprev, hprev, gstate, proj_s, mix_s, ywkv_s):
    tb = x_ref.shape[1]
    t = pl.program_id(1)

    @pl.when(t == 0)
    def _():
        kvprev[...] = jnp.zeros_like(kvprev)
        pprev[...] = jnp.zeros_like(pprev)
        xcprev[...] = jnp.zeros_like(xcprev)
        hprev[...] = jnp.zeros_like(hprev)
        gstate[...] = jnp.zeros_like(gstate)

    vec = vec_ref[...]
    ones_bd = _head_ones(B_WIDTH)

    x = x_ref[0]
    shift_m = mod_ref[0, 0:1, :]
    scale_m = mod_ref[0, 1:2, :]
    gate_m = mod_ref[0, 2:3, :]
    h = _rmsnorm(x, npre_ref[...]) * (1.0 + scale_m) + shift_m
    proj_s[...] = _dot(_bf(h), win_ref[...])

    cos = cos_ref[...]
    sin = sin_ref[...]
    k_rot = _rope128(proj_s[:, OFF_K:OFF_K + KV_WIDTH], cos, sin)
    v_att = proj_s[:, OFF_V:OFF_V + KV_WIDTH]
    kfull = jnp.concatenate([kvprev[:, 0:KV_WIDTH], k_rot], axis=0)
    vfull = jnp.concatenate([kvprev[:, KV_WIDTH:2 * KV_WIDTH], v_att], axis=0)
    lane_kv = _iota(kfull.shape, 1)
    lo_kv = lane_kv < HEAD_DIM

    def variants(z):
        zr = pltpu.roll(z, HEAD_DIM, 1)
        a_ = _bf(jnp.where(lo_kv, z, 0.0))
        b_ = _bf(jnp.where(lo_kv, 0.0, z))
        c_ = _bf(jnp.where(lo_kv, zr, 0.0))
        d_ = _bf(jnp.where(lo_kv, 0.0, zr))
        return ((a_, d_), (c_, b_))

    kvar = variants(kfull)
    vvar = variants(vfull)

    qi = _iota((Q_BLOCK, 2 * Q_BLOCK), 0)
    sj = _iota((Q_BLOCK, 2 * Q_BLOCK), 1)
    band = (sj >= qi) & (sj <= qi + WINDOW)
    first_lo = jnp.where(t == 0, Q_BLOCK, 0)
    neg_inf = -jnp.inf

    for c in range(A_HEADS // 2):
        g = c // 2
        qcol = _rope128(proj_s[:, OFF_Q + LANES * c:OFF_Q + LANES * (c + 1)], cos, sin) * (HEAD_DIM ** -0.5)
        gacol = proj_s[:, OFF_GA + LANES * c:OFF_GA + LANES * (c + 1)]
        for i in range(tb // Q_BLOCK):
            r0 = i * Q_BLOCK
            kst = jnp.concatenate([kvar[g][0][r0:r0 + 2 * Q_BLOCK], kvar[g][1][r0:r0 + 2 * Q_BLOCK]], axis=0)
            vst = jnp.concatenate([vvar[g][0][r0:r0 + 2 * Q_BLOCK], vvar[g][1][r0:r0 + 2 * Q_BLOCK]], axis=0)
            s = lax.dot_general(_bf(qcol[r0:r0 + Q_BLOCK]), kst, (((1,), (1,)), ((), ())),
                                preferred_element_type=F32)
            mask = band & (sj >= first_lo) if i == 0 else band
            ps = []
            for hh in range(2):
                sink = sinks_ref[2 * c + hh]
                sh = jnp.where(mask, s[:, 2 * Q_BLOCK * hh:2 * Q_BLOCK * (hh + 1)], neg_inf)
                m = jnp.maximum(jnp.max(sh, axis=-1, keepdims=True), sink)
                p = jnp.exp(sh - m)
                den = jnp.sum(p, axis=-1, keepdims=True) + jnp.exp(sink - m)
                ps.append(p * (1.0 / den))
            pcat = jnp.concatenate(ps, axis=1)
            o = _dot(_bf(pcat), vst)
            mix_s[r0:r0 + Q_BLOCK, LANES * c:LANES * (c + 1)] = o * _silu(gacol[r0:r0 + Q_BLOCK])

    kv_cur_tail = jnp.concatenate([k_rot[tb - WINDOW:], v_att[tb - WINDOW:]], axis=1)

    pb = proj_s[:, OFF_PB:OFF_PB + B_SHIFT_PAD]
    row_p = _iota(pb.shape, 0)
    prev = jnp.where(row_p == 0, pprev[SUBLANES - 1:SUBLANES, :], pltpu.roll(pb, 1, 0))
    xs = pb + (prev - pb) * mu_ref[...]
    r, k, v, ld, a, kkn, kmod = _rwkv_pre(xs, vec, wlora_ref[...], ones_bd)
    kka = kkn * a

    ri = _iota((tb, tb), 0)
    ci = _iota((tb, tb), 1)
    same = (ri // CHUNK) == (ci // CHUNK)
    tri = jnp.where(same & (ci <= ri), 1.0, 0.0).astype(BF16)
    blk = jnp.where(same, 1.0, 0.0).astype(BF16)
    lcum = _mm_exact_lhs(tri, ld)
    ltot = _mm_exact_lhs(blk, ld)
    e_l = jnp.exp(lcum)
    rt = r * e_l
    at = -kkn * jnp.exp(lcum - ld)
    e_nl = jnp.exp(-lcum)
    ktil = kmod * e_nl
    btil = kka * e_nl
    e_el = jnp.exp(ltot - lcum)
    khat = kmod * e_el
    bhat = kka * e_el
    gam = jnp.exp(ltot)

    n4 = B_HEADS * CHUNK
    lane_b = _iota((CHUNK, B_WIDTH), 1) // HEAD_DIM
    r4 = _iota((n4, n4), 0)
    c4 = _iota((n4, n4), 1)
    same4 = (r4 // CHUNK) == (c4 // CHUNK)
    strict = same4 & ((r4 % CHUNK) > (c4 % CHUNK))
    incl = same4 & ((r4 % CHUNK) >= (c4 % CHUNK))
    eye4 = jnp.where(r4 == c4, 1.0, 0.0)
    bd_heads = (_iota((B_WIDTH, B_WIDTH), 0) // HEAD_DIM) == (_iota((B_WIDTH, B_WIDTH), 1) // HEAD_DIM)

    def stk(z):
        return jnp.concatenate([jnp.where(lane_b == hd, z, 0.0) for hd in range(B_HEADS)], axis=0)

    def collapse(z):
        return z[0:CHUNK] + z[CHUNK:2 * CHUNK] + z[2 * CHUNK:3 * CHUNK] + z[3 * CHUNK:4 * CHUNK]

    gmat = gstate[...]
    for c in range(tb // CHUNK):
        sl = slice(c * CHUNK, (c + 1) * CHUNK)
        a_s, r_s = stk(at[sl]), stk(rt[sl])
        b_s, k_s, v_s = stk(btil[sl]), stk(ktil[sl]), stk(v[sl])
        sc = _mm3_nt(jnp.concatenate([a_s, r_s], axis=0), jnp.concatenate([b_s, k_s], axis=0))
        lab = jnp.where(strict, sc[0:n4, 0:n4], 0.0)
        lak = jnp.where(strict, sc[0:n4, n4:], 0.0)
        mrb = jnp.where(incl, sc[n4:, 0:n4], 0.0)
        mrk = jnp.where(incl, sc[n4:, n4:], 0.0)
        pw = lab
        tinv = eye4 + lab
        for _ in range(5):
            pw = _mm3(pw, pw)
            tinv = tinv + _mm3(tinv, pw)
        w_s = _mm3(tinv, a_s)
        z_s = _mm3(tinv, _mm3(lak, v_s))
        u_s = _mm3_nt(w_s, gmat) + z_s
        y_stk = _mm3_nt(r_s, gmat) + _mm3(mrb, u_s) + _mm3(mrk, v_s)
        ywkv_s[sl, :] = collapse(y_stk)
        uv = jnp.concatenate([collapse(u_s), v[sl]], axis=0)
        bk = jnp.concatenate([bhat[sl], khat[sl]], axis=0)
        gmat = gam[c * CHUNK:c * CHUNK + 1, :] * gmat + jnp.where(bd_heads, _mm3_tn(uv, bk), 0.0)
    gstate[...] = gmat

    gb = proj_s[:, OFF_GB:OFF_GB + B_WIDTH]
    mix_s[:, A_WIDTH:A_WIDTH + B_WIDTH] = _rwkv_post(ywkv_s[...], r, kmod, v, gb, vec, ones_bd)

    xc = proj_s[:, OFF_XC:OFF_XC + C_WIDTH]
    gc = proj_s[:, OFF_GC:OFF_GC + C_WIDTH]
    xext = jnp.concatenate([xcprev[...], xc], axis=0)
    u = vec[V_CB:V_CB + 1, :] + xc * vec[V_CW3:V_CW3 + 1, :]
    for j in range(1, CONV_W):
        u = u + pltpu.roll(xext, j, 0)[SUBLANES:, :] * vec[V_CW3 - j:V_CW3 - j + 1, :]
    a_l, mult, gi = _lru_gates(u, vec, wgate_ref[...])
    row_c = _iota((tb, C_WIDTH), 0)
    reset = (row_c + t * tb) == 0
    a_l = jnp.where(reset, 0.0, a_l)
    mult = jnp.where(reset, 1.0, mult)
    b_l = mult * gi * u
    b_l = b_l + jnp.where(row_c == 0, a_l * hprev[SUBLANES - 1:SUBLANES, :], 0.0)
    d = 1
    while d < tb:
        keep = row_c >= d
        a_sh = jnp.where(keep, pltpu.roll(a_l, d, 0), 1.0)
        b_sh = jnp.where(keep, pltpu.roll(b_l, d, 0), 0.0)
        b_l = b_l + a_l * b_sh
        a_l = a_l * a_sh
        d *= 2
    h_l = b_l
    mix_s[:, A_WIDTH + B_WIDTH:] = h_l * _silu(gc)

    yo = _dot(_bf(mix_s[...]), wout_ref[...])
    y_ref[0] = x + gate_m * _rmsnorm(yo, npost_ref[...])

    kvprev[...] = kv_cur_tail
    pprev[...] = pb[tb - SUBLANES:, :]
    xcprev[...] = xc[tb - SUBLANES:, :]
    hprev[...] = h_l[tb - SUBLANES:, :]
    kv_ref[0] = kv_cur_tail
    shift_ref[0] = pb[tb - SUBLANES:, :]
    wkv_ref[0] = gmat
    conv_ref[0] = xc[tb - SUBLANES:, :]
    hlru_ref[0] = h_l[tb - SUBLANES:, :]


def _prompt_layer(x, mod3, lw, cos_t, sin_t, tb):
    bsz, seq, _ = x.shape
    nt = seq // tb
    const2 = lambda b, t: (0, 0)
    per_b3 = lambda b, t: (b, 0, 0)
    outs = pl.pallas_call(
        _prompt_kernel,
        grid=(bsz, nt),
        in_specs=[
            pl.BlockSpec(memory_space=pltpu.SMEM),
            pl.BlockSpec((1, tb, D_MODEL), lambda b, t: (b, t, 0)),
            pl.BlockSpec((1, 3, D_MODEL), per_b3),
            pl.BlockSpec((1, D_MODEL), const2),
            pl.BlockSpec((1, D_MODEL), const2),
            pl.BlockSpec((D_MODEL, P_PAD), const2),
            pl.BlockSpec((D_MODEL, D_MODEL), const2),
            pl.BlockSpec((tb, LANES), lambda b, t: (t, 0)),
            pl.BlockSpec((tb, LANES), lambda b, t: (t, 0)),
            pl.BlockSpec((1, B_SHIFT_PAD), const2),
            pl.BlockSpec((VEC_ROWS, B_WIDTH), const2),
            pl.BlockSpec((LANES, 2 * B_WIDTH), const2),
            pl.BlockSpec((C_WIDTH, 2 * C_WIDTH), const2),
        ],
        out_specs=[
            pl.BlockSpec((1, tb, D_MODEL), lambda b, t: (b, t, 0)),
            pl.BlockSpec((1, WINDOW, 2 * KV_WIDTH), per_b3),
            pl.BlockSpec((1, SUBLANES, B_SHIFT_PAD), per_b3),
            pl.BlockSpec((1, B_WIDTH, B_WIDTH), per_b3),
            pl.BlockSpec((1, SUBLANES, C_WIDTH), per_b3),
            pl.BlockSpec((1, SUBLANES, C_WIDTH), per_b3),
        ],
        out_shape=[
            jax.ShapeDtypeStruct((bsz, seq, D_MODEL), F32),
            jax.ShapeDtypeStruct((bsz, WINDOW, 2 * KV_WIDTH), F32),
            jax.ShapeDtypeStruct((bsz, SUBLANES, B_SHIFT_PAD), F32),
            jax.ShapeDtypeStruct((bsz, B_WIDTH, B_WIDTH), F32),
            jax.ShapeDtypeStruct((bsz, SUBLANES, C_WIDTH), F32),
            jax.ShapeDtypeStruct((bsz, SUBLANES, C_WIDTH), F32),
        ],
        scratch_shapes=[
            pltpu.VMEM((WINDOW, 2 * KV_WIDTH), F32),
            pltpu.VMEM((SUBLANES, B_SHIFT_PAD), F32),
            pltpu.VMEM((SUBLANES, C_WIDTH), F32),
            pltpu.VMEM((SUBLANES, C_WIDTH), F32),
            pltpu.VMEM((B_WIDTH, B_WIDTH), F32),
            pltpu.VMEM((tb, P_PAD), F32),
            pltpu.VMEM((tb, D_MODEL), F32),
            pltpu.VMEM((tb, B_WIDTH), F32),
        ],
        compiler_params=pltpu.CompilerParams(
            dimension_semantics=("arbitrary", "arbitrary"), vmem_limit_bytes=VMEM_LIMIT_BYTES),
        name="prompt_layer",
    )(lw["sinks"], x, mod3, lw["norm_pre"], lw["norm_post"], lw["w_in"], lw["w_out"], cos_t, sin_t,
      lw["mu"], lw["vec"], lw["wlora"], lw["wgate"])
    return outs


def _sample_kernel(sinks_ref, x_ref, mod_ref, npre_ref, npost_ref, win_ref, wout_ref,
                   cos_ref, sin_ref, mu_ref, vec_ref, wlora_ref, wgate_ref,
                   ck_ref, cv_ref, sprev_ref, wkv_ref, conv_ref, hl_ref, rsel_ref, esel_ref,
                   y_ref, knew_ref, vnew_ref, shift_ref, wkvo_ref, convo_ref, hlo_ref,
                   mix_s):
    bb = x_ref.shape[0]
    vec = vec_ref[...]
    ones_bd = _head_ones(B_WIDTH)

    x = x_ref[...]
    shift_m = mod_ref[0]
    scale_m = mod_ref[1]
    gate_m = mod_ref[2]
    h = _rmsnorm(x, npre_ref[...]) * (1.0 + scale_m) + shift_m
    proj = _dot(_bf(h), win_ref[...])

    cos = cos_ref[...]
    sin = sin_ref[...]
    k_new = _rope128(proj[:, OFF_K:OFF_K + KV_WIDTH], cos, sin)
    v_new = proj[:, OFF_V:OFF_V + KV_WIDTH]
    knew_ref[...] = k_new
    vnew_ref[...] = v_new
    lane = _iota((bb, LANES), 1)
    lo = lane < HEAD_DIM
    rsel = rsel_ref[...]
    n_slot = ck_ref.shape[1] // LANES
    for c in range(A_HEADS // 2):
        g = c // 2
        in_g = lo if g == 0 else jnp.logical_not(lo)
        qcol = _rope128(proj[:, OFF_Q + LANES * c:OFF_Q + LANES * (c + 1)], cos, sin) * (HEAD_DIM ** -0.5)
        qrol = pltpu.roll(qcol, HEAD_DIM, 1)
        outs = []
        for hh in range(2):
            sink = sinks_ref[2 * c + hh]
            qh = jnp.where(in_g, qcol if hh == g else qrol, 0.0)
            qt = jnp.concatenate([qh] * n_slot, axis=1)
            s_c = _mm(ck_ref[...] * qt, rsel)
            s_n = jnp.sum(k_new * qh, axis=-1, keepdims=True)
            m = jnp.maximum(jnp.maximum(jnp.max(s_c, axis=-1, keepdims=True), s_n), sink)
            p_c = jnp.exp(s_c - m)
            p_n = jnp.exp(s_n - m)
            den = jnp.sum(p_c, axis=-1, keepdims=True) + p_n + jnp.exp(sink - m)
            inv = 1.0 / den
            pe = _mm_nt(p_c * inv, rsel)
            pv = pe * cv_ref[...]
            o = pv[:, 0:LANES]
            for s_i in range(1, n_slot):
                o = o + pv[:, LANES * s_i:LANES * (s_i + 1)]
            o = o + (p_n * inv) * v_new
            outs.append(o if hh == g else pltpu.roll(o, HEAD_DIM, 1))
        ocol = jnp.where(lo, outs[0], outs[1])
        gacol = proj[:, OFF_GA + LANES * c:OFF_GA + LANES * (c + 1)]
        mix_s[:, LANES * c:LANES * (c + 1)] = ocol * _silu(gacol)

    pb = proj[:, OFF_PB:OFF_PB + B_SHIFT_PAD]
    shift_ref[...] = pb
    xs = pb + (sprev_ref[...] - pb) * mu_ref[...]
    r, k, v, ld, a, kkn, kmod = _rwkv_pre(xs, vec, wlora_ref[...], ones_bd)
    decay = jnp.exp(ld)
    kka = kkn * a
    hv = HEAD_DIM * HEAD_DIM
    n_rep = hv // LANES
    ycols = []
    for c in range(B_HEADS // 2):
        cs = slice(LANES * c, LANES * (c + 1))
        ycol = None
        for hh in range(2):
            hd = 2 * c + hh
            esel = esel_ref[hh]

            def tile_k(z):
                zc = z[:, cs]
                zr = pltpu.roll(zc, HEAD_DIM, 1)
                pair = jnp.where(lo, zc, zr) if hh == 0 else jnp.where(lo, zr, zc)
                return jnp.concatenate([pair] * n_rep, axis=1)

            st = wkv_ref[:, hv * hd:hv * (hd + 1)]
            sa = _mm_nt(st * tile_k(-kkn), esel)
            sa_e = _mm_exact_rhs(sa, esel)
            v_e = _mm_exact_rhs(v[:, cs], esel)
            st_new = st * tile_k(decay) + sa_e * tile_k(kka) + v_e * tile_k(kmod)
            wkvo_ref[:, hv * hd:hv * (hd + 1)] = st_new
            yh = _mm_nt(st_new * tile_k(r), esel)
            ycol = yh if ycol is None else ycol + yh
        ycols.append(ycol)
    y_wkv = jnp.concatenate(ycols, axis=1)
    gb = proj[:, OFF_GB:OFF_GB + B_WIDTH]
    mix_s[:, A_WIDTH:A_WIDTH + B_WIDTH] = _rwkv_post(y_wkv, r, kmod, v, gb, vec, ones_bd)

    xc = proj[:, OFF_XC:OFF_XC + C_WIDTH]
    gc = proj[:, OFF_GC:OFF_GC + C_WIDTH]
    u = vec[V_CB:V_CB + 1, :] + xc * vec[V_CW3:V_CW3 + 1, :]
    for j in range(CONV_W - 1):
        u = u + conv_ref[:, C_WIDTH * j:C_WIDTH * (j + 1)] * vec[V_CW0 + j:V_CW0 + j + 1, :]
    a_l, mult, gi = _lru_gates(u, vec, wgate_ref[...])
    h_l = a_l * hl_ref[...] + mult * gi * u
    hlo_ref[...] = h_l
    convo_ref[:, 0:2 * C_WIDTH] = conv_ref[:, C_WIDTH:3 * C_WIDTH]
    convo_ref[:, 2 * C_WIDTH:3 * C_WIDTH] = xc
    mix_s[:, A_WIDTH + B_WIDTH:] = h_l * _silu(gc)

    yo = _dot(_bf(mix_s[...]), wout_ref[...])
    y_ref[...] = x + gate_m * _rmsnorm(yo, npost_ref[...])


def _sample_layer(x, mod3, lw, cos_r, sin_r, ck, cv, sprev, wkv, conv, hl, rsel, esel):
    nb = x.shape[0]
    bb = SAMPLE_BLOCK
    n_state = wkv.shape[1]
    n_cache = ck.shape[1]
    const2 = lambda i: (0, 0)
    rows2 = lambda i: (i, 0)
    return pl.pallas_call(
        _sample_kernel,
        grid=(nb // bb,),
        in_specs=[
            pl.BlockSpec(memory_space=pltpu.SMEM),
            pl.BlockSpec((bb, D_MODEL), rows2),
            pl.BlockSpec((3, bb, D_MODEL), lambda i: (0, i, 0)),
            pl.BlockSpec((1, D_MODEL), const2),
            pl.BlockSpec((1, D_MODEL), const2),
            pl.BlockSpec((D_MODEL, P_PAD), const2),
            pl.BlockSpec((D_MODEL, D_MODEL), const2),
            pl.BlockSpec((1, LANES), const2),
            pl.BlockSpec((1, LANES), const2),
            pl.BlockSpec((1, B_SHIFT_PAD), const2),
            pl.BlockSpec((VEC_ROWS, B_WIDTH), const2),
            pl.BlockSpec((LANES, 2 * B_WIDTH), const2),
            pl.BlockSpec((C_WIDTH, 2 * C_WIDTH), const2),
            pl.BlockSpec((bb, n_cache), rows2),
            pl.BlockSpec((bb, n_cache), rows2),
            pl.BlockSpec((bb, B_SHIFT_PAD), rows2),
            pl.BlockSpec((bb, n_state), rows2),
            pl.BlockSpec((bb, 3 * C_WIDTH), rows2),
            pl.BlockSpec((bb, C_WIDTH), rows2),
            pl.BlockSpec(rsel.shape, const2),
            pl.BlockSpec(esel.shape, lambda i: (0, 0, 0)),
        ],
        out_specs=[
            pl.BlockSpec((bb, D_MODEL), rows2),
            pl.BlockSpec((bb, KV_WIDTH), rows2),
            pl.BlockSpec((bb, KV_WIDTH), rows2),
            pl.BlockSpec((bb, B_SHIFT_PAD), rows2),
            pl.BlockSpec((bb, n_state), rows2),
            pl.BlockSpec((bb, 3 * C_WIDTH), rows2),
            pl.BlockSpec((bb, C_WIDTH), rows2),
        ],
        out_shape=[
            jax.ShapeDtypeStruct((nb, D_MODEL), F32),
            jax.ShapeDtypeStruct((nb, KV_WIDTH), F32),
            jax.ShapeDtypeStruct((nb, KV_WIDTH), F32),
            jax.ShapeDtypeStruct((nb, B_SHIFT_PAD), F32),
            jax.ShapeDtypeStruct((nb, n_state), F32),
            jax.ShapeDtypeStruct((nb, 3 * C_WIDTH), F32),
            jax.ShapeDtypeStruct((nb, C_WIDTH), F32),
        ],
        scratch_shapes=[pltpu.VMEM((bb, D_MODEL), F32)],
        compiler_params=pltpu.CompilerParams(
            dimension_semantics=("arbitrary",), vmem_limit_bytes=VMEM_LIMIT_BYTES),
        name="sample_layer",
    )(lw["sinks"], x, mod3, lw["norm_pre"], lw["norm_post"], lw["w_in"], lw["w_out"], cos_r, sin_r,
      lw["mu"], lw["vec"], lw["wlora"], lw["wgate"], ck, cv, sprev, wkv, conv, hl, rsel, esel)


def _rope_tables(pos):
    half = HEAD_DIM // 2
    inv_freq = ROPE_THETA ** (-jnp.arange(half, dtype=F32) / half)
    ang = pos.astype(F32)[:, None] * inv_freq[None, :]
    cos = jnp.cos(ang)
    sin = jnp.sin(ang)
    cos_t = jnp.concatenate([cos, cos, cos, cos], axis=-1)
    sin_t = jnp.concatenate([-sin, sin, -sin, sin], axis=-1)
    return cos_t, sin_t


def _block_diag(w):
    n, d, e = w.shape
    eye = jnp.eye(n, dtype=w.dtype)
    return (eye[:, None, :, None] * w[:, :, None, :]).reshape(n * d, n * e)


def _layer_weights(l, p):
    w_in = p["w_in"][l]
    w_in_p = jnp.concatenate(
        [w_in[:, :P_ORIG_PB_END], jnp.zeros((D_MODEL, B_SHIFT_PAD - B_SHIFT), F32), w_in[:, P_ORIG_PB_END:]], axis=1)
    wlora = jnp.zeros((LANES, 2 * B_WIDTH), F32)
    wlora = wlora.at[0:LORA, 0:B_WIDTH].set(p["rwkv_w_up"][l])
    wlora = wlora.at[LORA:2 * LORA, B_WIDTH:].set(p["rwkv_a_up"][l])
    wgate = jnp.concatenate([_block_diag(p["lru_gate_a_w"][l]), _block_diag(p["lru_gate_x_w"][l])], axis=1)
    rows = [p["rwkv_w0"][l], p["rwkv_a0"][l], p["rwkv_k_k"][l], p["rwkv_k_a"][l], p["rwkv_lnx_w"][l],
            p["rwkv_lnx_b"][l], p["rwkv_r_k"][l].reshape(B_WIDTH), p["lru_conv_b"][l], p["lru_gate_a_b"][l],
            p["lru_gate_x_b"][l], p["lru_lambda"][l], p["lru_conv_w"][l][0], p["lru_conv_w"][l][1],
            p["lru_conv_w"][l][2], p["lru_conv_w"][l][3], jnp.zeros((B_WIDTH,), F32)]
    return {
        "sinks": p["attn_sinks"][l],
        "norm_pre": p["norm_pre"][l].reshape(1, D_MODEL),
        "norm_post": p["norm_post"][l].reshape(1, D_MODEL),
        "w_in": w_in_p.astype(BF16),
        "w_out": p["w_out"][l].astype(BF16),
        "mu": jnp.concatenate([p["rwkv_mu"][l], jnp.zeros((B_SHIFT_PAD - B_SHIFT,), F32)]).reshape(1, B_SHIFT_PAD),
        "vec": jnp.stack(rows),
        "wlora": wlora.astype(BF16),
        "wgate": wgate.astype(BF16),
    }


def _selectors(n_slot):
    slot = jnp.arange(n_slot * LANES) // LANES
    rsel = (slot[:, None] == jnp.arange(n_slot)[None, :]).astype(BF16)
    val = jnp.arange(HEAD_DIM * HEAD_DIM) // HEAD_DIM
    esel = jnp.stack([(jnp.arange(LANES)[:, None] == (HEAD_DIM * hh + val)[None, :]).astype(BF16)
                      for hh in range(2)])
    return rsel, esel


def _forward(x_prompt, x_sample, c_prompt, c_sample, cache_swa_k, cache_swa_v, state_rwkv_shift,
             state_rwkv_wkv, state_lru_conv, state_lru_h, p, tb):
    bp, seq, _ = x_prompt.shape
    nb = x_sample.shape[0]
    n_slot = cache_swa_k.shape[2]

    pad_rows = (-(nb + bp)) % SUBLANES
    c_all = jnp.concatenate([c_sample, c_prompt, jnp.zeros((pad_rows, D_MODEL), F32)], axis=0)
    mod = _mod_call(c_all, p["w_mod"], p["b_mod"])

    cos_p, sin_p = _rope_tables(jnp.arange(seq))
    cos_s, sin_s = _rope_tables(jnp.full((1,), PAST_LEN))
    rsel, esel = _selectors(n_slot)

    yp = x_prompt
    ys = x_sample.reshape(nb, D_MODEL)
    outs_p, outs_s = [], []
    for l in range(DEPTH):
        lw = _layer_weights(l, p)
        mod_p = mod[l, nb:nb + bp].reshape(bp, 3, D_MODEL)
        mod_s = mod[l, :nb].reshape(nb, 3, D_MODEL).transpose(1, 0, 2)
        yp, kv_p, sh_p, wkv_p, conv_p, h_p = _prompt_layer(yp, mod_p, lw, cos_p, sin_p, tb)
        outs_p.append((
            kv_p[:, :, :KV_WIDTH].reshape(bp, WINDOW, A_KV_HEADS, HEAD_DIM),
            kv_p[:, :, KV_WIDTH:].reshape(bp, WINDOW, A_KV_HEADS, HEAD_DIM),
            sh_p[:, SUBLANES - 1, :B_SHIFT],
            jnp.stack([wkv_p[:, HEAD_DIM * hd:HEAD_DIM * (hd + 1), HEAD_DIM * hd:HEAD_DIM * (hd + 1)]
                       for hd in range(B_HEADS)], axis=1),
            conv_p[:, SUBLANES - (CONV_W - 1):, :],
            h_p[:, SUBLANES - 1, :],
        ))
        sprev = jnp.concatenate([state_rwkv_shift[l], jnp.zeros((nb, B_SHIFT_PAD - B_SHIFT), F32)], axis=1)
        ys, k_s, v_s, sh_s, wkv_s, conv_s, h_s = _sample_layer(
            ys, mod_s, lw, cos_s, sin_s,
            cache_swa_k[l].reshape(nb, n_slot * KV_WIDTH), cache_swa_v[l].reshape(nb, n_slot * KV_WIDTH),
            sprev, state_rwkv_wkv[l].reshape(nb, B_HEADS * HEAD_DIM * HEAD_DIM),
            state_lru_conv[l].reshape(nb, (CONV_W - 1) * C_WIDTH), state_lru_h[l], rsel, esel)
        outs_s.append((
            k_s.reshape(nb, 1, A_KV_HEADS, HEAD_DIM),
            v_s.reshape(nb, 1, A_KV_HEADS, HEAD_DIM),
            sh_s[:, :B_SHIFT],
            wkv_s.reshape(nb, B_HEADS, HEAD_DIM, HEAD_DIM),
            conv_s.reshape(nb, CONV_W - 1, C_WIDTH),
            h_s,
        ))
    sp = [jnp.stack(z) for z in zip(*outs_p)]
    ss = [jnp.stack(z) for z in zip(*outs_s)]
    return (yp, ys.reshape(nb, 1, D_MODEL), *sp, *ss)


def kernel(x_prompt, x_sample, c_prompt, c_sample, cache_swa_k, cache_swa_v, state_rwkv_shift, state_rwkv_wkv, state_lru_conv, state_lru_h, norm_pre, norm_post, w_mod, b_mod, w_in, w_out, attn_sinks, rwkv_mu, rwkv_w0, rwkv_w_up, rwkv_a0, rwkv_a_up, rwkv_k_k, rwkv_k_a, rwkv_r_k, rwkv_lnx_w, rwkv_lnx_b, lru_conv_w, lru_conv_b, lru_gate_a_w, lru_gate_a_b, lru_gate_x_w, lru_gate_x_b, lru_lambda):
    p = dict(norm_pre=norm_pre, norm_post=norm_post, w_mod=w_mod, b_mod=b_mod, w_in=w_in, w_out=w_out,
             attn_sinks=attn_sinks, rwkv_mu=rwkv_mu, rwkv_w0=rwkv_w0, rwkv_w_up=rwkv_w_up, rwkv_a0=rwkv_a0,
             rwkv_a_up=rwkv_a_up, rwkv_k_k=rwkv_k_k, rwkv_k_a=rwkv_k_a, rwkv_r_k=rwkv_r_k,
             rwkv_lnx_w=rwkv_lnx_w, rwkv_lnx_b=rwkv_lnx_b, lru_conv_w=lru_conv_w, lru_conv_b=lru_conv_b,
             lru_gate_a_w=lru_gate_a_w, lru_gate_a_b=lru_gate_a_b, lru_gate_x_w=lru_gate_x_w,
             lru_gate_x_b=lru_gate_x_b, lru_lambda=lru_lambda)
    return _forward(x_prompt, x_sample, c_prompt, c_sample, cache_swa_k, cache_swa_v, state_rwkv_shift,
                    state_rwkv_wkv, state_lru_conv, state_lru_h, p, TIME_BLOCK)
```

```python
import functools

import numpy as np
import jax
import jax.numpy as jnp
from jax import lax
from jax.experimental import pallas as pl
from jax.experimental.pallas import tpu as pltpu

F32 = jnp.float32
BF16 = jnp.bfloat16

D_MODEL = 1024
DEPTH = 2
A_HEADS = 8
A_KV_HEADS = 2
HEAD_DIM = 64
A_WIDTH = A_HEADS * HEAD_DIM
KV_WIDTH = A_KV_HEADS * HEAD_DIM
WINDOW = 128
Q_BLOCK = 128
ROPE_THETA = 10000.0
B_WIDTH = 256
B_HEADS = 4
LORA = 32
B_SHIFT = 3 * B_WIDTH + 2 * LORA
B_SHIFT_PAD = 3 * B_WIDTH + 128
LNX_EPS = 1e-5 * 8 ** 2
C_WIDTH = 256
C_BLOCKS = 4
CONV_W = 4
LRU_C = 8.0
EPS = 1e-6
PAST_LEN = 8192

LANES = 128
SUBLANES = 8
VMEM_LIMIT_BYTES = 56 * 1024 * 1024

OFF_Q = 0
OFF_K = OFF_Q + A_WIDTH
OFF_V = OFF_K + KV_WIDTH
OFF_GA = OFF_V + KV_WIDTH
OFF_PB = OFF_GA + A_WIDTH
OFF_GB = OFF_PB + B_SHIFT_PAD
OFF_XC = OFF_GB + B_WIDTH
OFF_GC = OFF_XC + C_WIDTH
P_PAD = OFF_GC + C_WIDTH
P_ORIG_PB_END = OFF_PB + B_SHIFT

CHUNK = 64
TIME_BLOCK = 256
SAMPLE_BLOCK = 32

(V_W0, V_A0, V_KK, V_KA, V_LNW, V_LNB, V_RK, V_CB, V_GAB, V_GXB, V_LAM,
 V_CW0, V_CW1, V_CW2, V_CW3) = range(15)
VEC_ROWS = 16


def _bf(x):
    return x.astype(BF16)


def _dot(a, b):
    return jnp.dot(a, b, preferred_element_type=F32)


def _mm(a, b):
    return _dot(_bf(a), _bf(b))


def _mm_nt(a, b):
    return lax.dot_general(_bf(a), _bf(b), (((1,), (1,)), ((), ())), preferred_element_type=F32)


def _mm_tn(a, b):
    return lax.dot_general(_bf(a), _bf(b), (((0,), (0,)), ((), ())), preferred_element_type=F32)


def _split2(x):
    hi = _bf(x)
    lo = _bf(x - hi.astype(F32))
    return hi, lo


def _split3(x):
    hi = _bf(x)
    r1 = x - hi.astype(F32)
    mid = _bf(r1)
    lo = _bf(r1 - mid.astype(F32))
    return hi, mid, lo


def _mm3(a, b):
    ah, al = _split2(a)
    bh, bl = _split2(b)
    return _dot(ah, bh) + _dot(ah, bl) + _dot(al, bh)


def _mm3_nt(a, b):
    ah, al = _split2(a)
    bh, bl = _split2(b)
    dn = (((1,), (1,)), ((), ()))
    f = lambda u, v: lax.dot_general(u, v, dn, preferred_element_type=F32)
    return f(ah, bh) + f(ah, bl) + f(al, bh)


def _mm3_tn(a, b):
    ah, al = _split2(a)
    bh, bl = _split2(b)
    dn = (((0,), (0,)), ((), ()))
    f = lambda u, v: lax.dot_general(u, v, dn, preferred_element_type=F32)
    return f(ah, bh) + f(ah, bl) + f(al, bh)


_MM_SC_NT = _mm_nt
_MM_INV = _mm
_MM_SOLVE = _mm
_MM_STATE_NT = _mm_nt
_MM_STATE_TN = _mm_tn
_MM_OUT = _mm


def _mm_exact_rhs(a, sel):
    hi, mid, lo = _split3(a)
    return _dot(hi, sel) + _dot(mid, sel) + _dot(lo, sel)


def _mm_exact_lhs(sel, b):
    hi, lo = _split2(b)
    return _dot(sel, hi) + _dot(sel, lo)


def _iota(shape, dim):
    return lax.broadcasted_iota(jnp.int32, shape, dim)


def _sigmoid(x):
    return jax.nn.sigmoid(x)


def _silu(x):
    return x * _sigmoid(x)


def _softplus(x):
    return jnp.maximum(x, 0.0) + jnp.log1p(jnp.exp(-jnp.abs(x)))


def _head_ones(n):
    r = _iota((n, n), 0) // HEAD_DIM
    c = _iota((n, n), 1) // HEAD_DIM
    return jnp.where(r == c, 1.0, 0.0).astype(BF16)


def _head_sum(x, ones_bd):
    return _dot(_bf(x), ones_bd)


def _rmsnorm(x, g):
    ms = jnp.mean(x * x, axis=-1, keepdims=True)
    return x * lax.rsqrt(ms + EPS) * g


def _rope128(z, cos, sin_signed):
    lane = _iota(z.shape, 1)
    first = (lane & 32) == 0
    sw = jnp.where(first, pltpu.roll(z, 96, 1), pltpu.roll(z, 32, 1))
    return z * cos + sw * sin_signed


def _rwkv_pre(xs, vec, wlora, ones_bd):
    r = xs[:, 0:B_WIDTH]
    k = xs[:, B_WIDTH:2 * B_WIDTH]
    v = xs[:, 2 * B_WIDTH:3 * B_WIDTH]
    lor = xs[:, 3 * B_WIDTH:B_SHIFT_PAD]
    lane = _iota(lor.shape, 1)
    z = jnp.where(lane < LORA, jnp.tanh(lor), lor)
    wa = _mm(z, wlora)
    zw = vec[V_W0:V_W0 + 1, :] + wa[:, :B_WIDTH]
    w_log = -_softplus(-zw) - 0.5
    ld = -jnp.exp(w_log)
    a = _sigmoid(vec[V_A0:V_A0 + 1, :] + wa[:, B_WIDTH:])
    kk = k * vec[V_KK:V_KK + 1, :]
    nrm = jnp.sqrt(_head_sum(kk * kk, ones_bd))
    kkn = kk / jnp.maximum(nrm, 1e-12)
    kmod = k * (1.0 + (a - 1.0) * vec[V_KA:V_KA + 1, :])
    return r, k, v, ld, a, kkn, kmod


def _rwkv_post(y, r, kmod, v, gb, vec, ones_bd):
    inv = 1.0 / HEAD_DIM
    mu = _head_sum(y, ones_bd) * inv
    yc = y - mu
    var = _head_sum(yc * yc, ones_bd) * inv
    yn = yc * lax.rsqrt(var + LNX_EPS)
    yn = yn * vec[V_LNW:V_LNW + 1, :] + vec[V_LNB:V_LNB + 1, :]
    bonus = _head_sum(r * kmod * vec[V_RK:V_RK + 1, :], ones_bd) * v
    return (yn + bonus) * _silu(gb)


def _lru_gates(u, vec, wgate):
    gates = _mm(u, wgate)
    gr = _sigmoid(gates[:, :C_WIDTH] + vec[V_GAB:V_GAB + 1, :])
    gi = _sigmoid(gates[:, C_WIDTH:] + vec[V_GXB:V_GXB + 1, :])
    sp = _softplus(-vec[V_LAM:V_LAM + 1, :])
    log_a = -LRU_C * gr * sp
    a = jnp.exp(log_a)
    mult = jnp.sqrt(1.0 - a * a)
    return a, mult, gi


def _mod_kernel(c_ref, w_ref, b_ref, o_ref):
    c = c_ref[...]
    o_ref[0] = _mm(_silu(c), w_ref[0]) + b_ref[0]


def _mod_call(c_all, w_mod, b_mod):
    rows = c_all.shape[0]
    n_col = 3 * D_MODEL // D_MODEL
    return pl.pallas_call(
        _mod_kernel,
        grid=(DEPTH, n_col),
        in_specs=[
            pl.BlockSpec((rows, D_MODEL), lambda l, j: (0, 0)),
            pl.BlockSpec((1, D_MODEL, D_MODEL), lambda l, j: (l, 0, j)),
            pl.BlockSpec((1, 1, D_MODEL), lambda l, j: (l, 0, j)),
        ],
        out_specs=pl.BlockSpec((1, rows, D_MODEL), lambda l, j: (l, 0, j)),
        out_shape=jax.ShapeDtypeStruct((DEPTH, rows, 3 * D_MODEL), F32),
        compiler_params=pltpu.CompilerParams(
            dimension_semantics=("arbitrary", "arbitrary"), vmem_limit_bytes=VMEM_LIMIT_BYTES),
        name="adaln_mod",
    )(c_all, w_mod, b_mod.reshape(DEPTH, 1, 3 * D_MODEL))


def _prompt_kernel(sinks_ref, x_ref, mod_ref, npre_ref, npost_ref, win_ref, wout_ref,
                   cos_ref, sin_ref, mu_ref, vec_ref, wlora_ref, wgate_ref,
                   y_ref, kv_ref, shift_ref, wkv_ref, conv_ref, hlru_ref,
                   kvprev, pprev, xcprev, hprev, gstate, proj_s, mix_s, ywkv_s):
    tb = x_ref.shape[1]
    t = pl.program_id(1)

    @pl.when(t == 0)
    def _():
        kvprev[...] = jnp.zeros_like(kvprev)
        pprev[...] = jnp.zeros_like(pprev)
        xcprev[...] = jnp.zeros_like(xcprev)
        hprev[...] = jnp.zeros_like(hprev)
        gstate[...] = jnp.zeros_like(gstate)

    vec = vec_ref[...]
    ones_bd = _head_ones(B_WIDTH)

    x = x_ref[0]
    shift_m = mod_ref[0, 0:1, :]
    scale_m = mod_ref[0, 1:2, :]
    gate_m = mod_ref[0, 2:3, :]
    h = _rmsnorm(x, npre_ref[...]) * (1.0 + scale_m) + shift_m
    proj_s[...] = _dot(_bf(h), win_ref[...])

    cos = cos_ref[...]
    sin = sin_ref[...]
    k_rot = _rope128(proj_s[:, OFF_K:OFF_K + KV_WIDTH], cos, sin)
    v_att = proj_s[:, OFF_V:OFF_V + KV_WIDTH]
    kfull = jnp.concatenate([kvprev[:, 0:KV_WIDTH], k_rot], axis=0)
    vfull = jnp.concatenate([kvprev[:, KV_WIDTH:2 * KV_WIDTH], v_att], axis=0)
    lane_kv = _iota(kfull.shape, 1)
    lo_kv = lane_kv < HEAD_DIM

    def variants(z):
        zr = pltpu.roll(z, HEAD_DIM, 1)
        a_ = _bf(jnp.where(lo_kv, z, 0.0))
        b_ = _bf(jnp.where(lo_kv, 0.0, z))
        c_ = _bf(jnp.where(lo_kv, zr, 0.0))
        d_ = _bf(jnp.where(lo_kv, 0.0, zr))
        return ((a_, d_), (c_, b_))

    kvar = variants(kfull)
    vvar = variants(vfull)

    qi = _iota((Q_BLOCK, 2 * Q_BLOCK), 0)
    sj = _iota((Q_BLOCK, 2 * Q_BLOCK), 1)
    band = (sj >= qi) & (sj <= qi + WINDOW)
    first_lo = jnp.where(t == 0, Q_BLOCK, 0)
    neg_inf = -jnp.inf

    for c in range(A_HEADS // 2):
        g = c // 2
        qcol = _rope128(proj_s[:, OFF_Q + LANES * c:OFF_Q + LANES * (c + 1)], cos, sin) * (HEAD_DIM ** -0.5)
        gacol = proj_s[:, OFF_GA + LANES * c:OFF_GA + LANES * (c + 1)]
        for i in range(tb // Q_BLOCK):
            r0 = i * Q_BLOCK
            kst = jnp.concatenate([kvar[g][0][r0:r0 + 2 * Q_BLOCK], kvar[g][1][r0:r0 + 2 * Q_BLOCK]], axis=0)
            vst = jnp.concatenate([vvar[g][0][r0:r0 + 2 * Q_BLOCK], vvar[g][1][r0:r0 + 2 * Q_BLOCK]], axis=0)
            s = lax.dot_general(_bf(qcol[r0:r0 + Q_BLOCK]), kst, (((1,), (1,)), ((), ())),
                                preferred_element_type=F32)
            mask = band & (sj >= first_lo) if i == 0 else band
            ps = []
            for hh in range(2):
                sink = sinks_ref[2 * c + hh]
                sh = jnp.where(mask, s[:, 2 * Q_BLOCK * hh:2 * Q_BLOCK * (hh + 1)], neg_inf)
                m = jnp.maximum(jnp.max(sh, axis=-1, keepdims=True), sink)
                p = jnp.exp(sh - m)
                den = jnp.sum(p, axis=-1, keepdims=True) + jnp.exp(sink - m)
                ps.append(p * (1.0 / den))
            pcat = jnp.concatenate(ps, axis=1)
            o = _dot(_bf(pcat), vst)
            mix_s[r0:r0 + Q_BLOCK, LANES * c:LANES * (c + 1)] = o * _silu(gacol[r0:r0 + Q_BLOCK])

    kv_cur_tail = jnp.concatenate([k_rot[tb - WINDOW:], v_att[tb - WINDOW:]], axis=1)

    pb = proj_s[:, OFF_PB:OFF_PB + B_SHIFT_PAD]
    row_p = _iota(pb.shape, 0)
    prev = jnp.where(row_p == 0, pprev[SUBLANES - 1:SUBLANES, :], pltpu.roll(pb, 1, 0))
    xs = pb + (prev - pb) * mu_ref[...]
    r, k, v, ld, a, kkn, kmod = _rwkv_pre(xs, vec, wlora_ref[...], ones_bd)
    kka = kkn * a

    nck = tb // CHUNK
    sls = [slice(c * CHUNK, (c + 1) * CHUNK) for c in range(nck)]
    row_b = _iota((tb, B_WIDTH), 0) % CHUNK
    lcum = ld
    d = 1
    while d < CHUNK:
        lcum = lcum + jnp.where(row_b >= d, pltpu.roll(lcum, d, 0), 0.0)
        d *= 2
    ltot = jnp.concatenate(
        [jnp.broadcast_to(lcum[s.stop - 1:s.stop, :], (CHUNK, B_WIDTH)) for s in sls], axis=0)
    e_l = jnp.exp(lcum)
    rt = r * e_l
    at = -kkn * jnp.exp(lcum - ld)
    e_nl = jnp.exp(-lcum)
    ktil = kmod * e_nl
    btil = kka * e_nl
    e_el = jnp.exp(ltot - lcum)
    khat = kmod * e_el
    bhat = kka * e_el
    gam = jnp.exp(ltot)

    n4 = B_HEADS * CHUNK
    lane_b = _iota((CHUNK, B_WIDTH), 1) // HEAD_DIM
    r4 = _iota((n4, n4), 0)
    c4 = _iota((n4, n4), 1)
    same4 = (r4 // CHUNK) == (c4 // CHUNK)
    strict = same4 & ((r4 % CHUNK) > (c4 % CHUNK))
    incl = same4 & ((r4 % CHUNK) >= (c4 % CHUNK))
    eye4 = jnp.where(r4 == c4, 1.0, 0.0)
    bd_heads = (_iota((B_WIDTH, B_WIDTH), 0) // HEAD_DIM) == (_iota((B_WIDTH, B_WIDTH), 1) // HEAD_DIM)

    def stk(z):
        return jnp.concatenate([jnp.where(lane_b == hd, z, 0.0) for hd in range(B_HEADS)], axis=0)

    def collapse(z):
        return z[0:CHUNK] + z[CHUNK:2 * CHUNK] + z[2 * CHUNK:3 * CHUNK] + z[3 * CHUNK:4 * CHUNK]

    rng = range(nck)
    a_s = [stk(at[s]) for s in sls]
    r_s = [stk(rt[s]) for s in sls]
    b_s = [stk(btil[s]) for s in sls]
    k_s = [stk(ktil[s]) for s in sls]
    v_s = [stk(v[s]) for s in sls]
    bh_s = [stk(bhat[s]) for s in sls]
    sc = [_MM_SC_NT(jnp.concatenate([a_s[c], r_s[c]], axis=0), jnp.concatenate([b_s[c], k_s[c]], axis=0))
          for c in rng]
    lab = [jnp.where(strict, sc[c][0:n4, 0:n4], 0.0) for c in rng]
    lak = [jnp.where(strict, sc[c][0:n4, n4:], 0.0) for c in rng]
    mrb = [jnp.where(incl, sc[c][n4:, 0:n4], 0.0) for c in rng]
    mrk = [jnp.where(incl, sc[c][n4:, n4:], 0.0) for c in rng]
    tinv = [eye4 + lab[c] for c in rng]
    pw = [_MM_INV(lab[c], lab[c]) for c in rng]
    for _ in range(4):
        res = [_MM_INV(jnp.concatenate([tinv[c], pw[c]], axis=0), pw[c]) for c in rng]
        tinv = [tinv[c] + res[c][0:n4] for c in rng]
        pw = [res[c][n4:] for c in rng]
    tinv = [tinv[c] + _MM_INV(tinv[c], pw[c]) for c in rng]
    lv = [_MM_SOLVE(lak[c], v_s[c]) for c in rng]
    wz = [_MM_SOLVE(tinv[c], jnp.concatenate([a_s[c], lv[c]], axis=1)) for c in rng]
    w_s = [wz[c][:, 0:B_WIDTH] for c in rng]
    z_s = [wz[c][:, B_WIDTH:] for c in rng]
    rp = [r_s[c] + _MM_OUT(mrb[c], w_s[c]) for c in rng]
    y0 = [_MM_OUT(jnp.concatenate([mrb[c], mrk[c]], axis=1), jnp.concatenate([z_s[c], v_s[c]], axis=0))
          for c in rng]
    pp = [_MM_STATE_TN(w_s[c], bh_s[c]) for c in rng]
    qq = [jnp.where(bd_heads,
                    _MM_STATE_TN(jnp.concatenate([collapse(z_s[c]), v[sls[c]]], axis=0),
                                 jnp.concatenate([bhat[sls[c]], khat[sls[c]]], axis=0)), 0.0) for c in rng]
    gmat = gstate[...]
    for c in rng:
        ywkv_s[sls[c], :] = collapse(_MM_STATE_NT(rp[c], gmat) + y0[c])
        gmat = gam[c * CHUNK:c * CHUNK + 1, :] * gmat + _MM_OUT(gmat, pp[c]) + qq[c]
    gstate[...] = gmat

    gb = proj_s[:, OFF_GB:OFF_GB + B_WIDTH]
    mix_s[:, A_WIDTH:A_WIDTH + B_WIDTH] = _rwkv_post(ywkv_s[...], r, kmod, v, gb, vec, ones_bd)

    xc = proj_s[:, OFF_XC:OFF_XC + C_WIDTH]
    gc = proj_s[:, OFF_GC:OFF_GC + C_WIDTH]
    xext = jnp.concatenate([xcprev[...], xc], axis=0)
    u = vec[V_CB:V_CB + 1, :] + xc * vec[V_CW3:V_CW3 + 1, :]
    for j in range(1, CONV_W):
        u = u + pltpu.roll(xext, j, 0)[SUBLANES:, :] * vec[V_CW3 - j:V_CW3 - j + 1, :]
    a_l, mult, gi = _lru_gates(u, vec, wgate_ref[...])
    row_c = _iota((tb, C_WIDTH), 0)
    reset = (row_c + t * tb) == 0
    a_l = jnp.where(reset, 0.0, a_l)
    mult = jnp.where(reset, 1.0, mult)
    b_l = mult * gi * u
    b_l = b_l + jnp.where(row_c == 0, a_l * hprev[SUBLANES - 1:SUBLANES, :], 0.0)
    d = 1
    while d < tb:
        keep = row_c >= d
        a_sh = jnp.where(keep, pltpu.roll(a_l, d, 0), 1.0)
        b_sh = jnp.where(keep, pltpu.roll(b_l, d, 0), 0.0)
        b_l = b_l + a_l * b_sh
        a_l = a_l * a_sh
        d *= 2
    h_l = b_l
    mix_s[:, A_WIDTH + B_WIDTH:] = h_l * _silu(gc)

    yo = _dot(_bf(mix_s[...]), wout_ref[...])
    y_ref[0] = x + gate_m * _rmsnorm(yo, npost_ref[...])

    kvprev[...] = kv_cur_tail
    pprev[...] = pb[tb - SUBLANES:, :]
    xcprev[...] = xc[tb - SUBLANES:, :]
    hprev[...] = h_l[tb - SUBLANES:, :]
    kv_ref[0] = kv_cur_tail
    shift_ref[0] = pb[tb - SUBLANES:, :]
    wkv_ref[0] = gmat
    conv_ref[0] = xc[tb - SUBLANES:, :]
    hlru_ref[0] = h_l[tb - SUBLANES:, :]


def _prompt_layer(x, mod3, lw, cos_t, sin_t, tb):
    bsz, seq, _ = x.shape
    nt = seq // tb
    const2 = lambda b, t: (0, 0)
    per_b3 = lambda b, t: (b, 0, 0)
    outs = pl.pallas_call(
        _prompt_kernel,
        grid=(bsz, nt),
        in_specs=[
            pl.BlockSpec(memory_space=pltpu.SMEM),
            pl.BlockSpec((1, tb, D_MODEL), lambda b, t: (b, t, 0)),
            pl.BlockSpec((1, 3, D_MODEL), per_b3),
            pl.BlockSpec((1, D_MODEL), const2),
            pl.BlockSpec((1, D_MODEL), const2),
            pl.BlockSpec((D_MODEL, P_PAD), const2),
            pl.BlockSpec((D_MODEL, D_MODEL), const2),
            pl.BlockSpec((tb, LANES), lambda b, t: (t, 0)),
            pl.BlockSpec((tb, LANES), lambda b, t: (t, 0)),
            pl.BlockSpec((1, B_SHIFT_PAD), const2),
            pl.BlockSpec((VEC_ROWS, B_WIDTH), const2),
            pl.BlockSpec((LANES, 2 * B_WIDTH), const2),
            pl.BlockSpec((C_WIDTH, 2 * C_WIDTH), const2),
        ],
        out_specs=[
            pl.BlockSpec((1, tb, D_MODEL), lambda b, t: (b, t, 0)),
            pl.BlockSpec((1, WINDOW, 2 * KV_WIDTH), per_b3),
            pl.BlockSpec((1, SUBLANES, B_SHIFT_PAD), per_b3),
            pl.BlockSpec((1, B_WIDTH, B_WIDTH), per_b3),
            pl.BlockSpec((1, SUBLANES, C_WIDTH), per_b3),
            pl.BlockSpec((1, SUBLANES, C_WIDTH), per_b3),
        ],
        out_shape=[
            jax.ShapeDtypeStruct((bsz, seq, D_MODEL), F32),
            jax.ShapeDtypeStruct((bsz, WINDOW, 2 * KV_WIDTH), F32),
            jax.ShapeDtypeStruct((bsz, SUBLANES, B_SHIFT_PAD), F32),
            jax.ShapeDtypeStruct((bsz, B_WIDTH, B_WIDTH), F32),
            jax.ShapeDtypeStruct((bsz, SUBLANES, C_WIDTH), F32),
            jax.ShapeDtypeStruct((bsz, SUBLANES, C_WIDTH), F32),
        ],
        scratch_shapes=[
            pltpu.VMEM((WINDOW, 2 * KV_WIDTH), F32),
            pltpu.VMEM((SUBLANES, B_SHIFT_PAD), F32),
            pltpu.VMEM((SUBLANES, C_WIDTH), F32),
            pltpu.VMEM((SUBLANES, C_WIDTH), F32),
            pltpu.VMEM((B_WIDTH, B_WIDTH), F32),
            pltpu.VMEM((tb, P_PAD), F32),
            pltpu.VMEM((tb, D_MODEL), F32),
            pltpu.VMEM((tb, B_WIDTH), F32),
        ],
        compiler_params=pltpu.CompilerParams(
            dimension_semantics=("arbitrary", "arbitrary"), vmem_limit_bytes=VMEM_LIMIT_BYTES),
        name="prompt_layer",
    )(lw["sinks"], x, mod3, lw["norm_pre"], lw["norm_post"], lw["w_in"], lw["w_out"], cos_t, sin_t,
      lw["mu"], lw["vec"], lw["wlora"], lw["wgate"])
    return outs


def _sample_kernel(sinks_ref, x_ref, mod_ref, npre_ref, npost_ref, win_ref, wout_ref,
                   cos_ref, sin_ref, mu_ref, vec_ref, wlora_ref, wgate_ref,
                   ck_ref, cv_ref, sprev_ref, wkv_ref, conv_ref, hl_ref, rsel_ref, esel_ref,
                   y_ref, knew_ref, vnew_ref, shift_ref, wkvo_ref, convo_ref, hlo_ref,
                   mix_s):
    bb = x_ref.shape[0]
    vec = vec_ref[...]
    ones_bd = _head_ones(B_WIDTH)

    x = x_ref[...]
    shift_m = mod_ref[0]
    scale_m = mod_ref[1]
    gate_m = mod_ref[2]
    h = _rmsnorm(x, npre_ref[...]) * (1.0 + scale_m) + shift_m
    proj = _dot(_bf(h), win_ref[...])

    cos = cos_ref[...]
    sin = sin_ref[...]
    k_new = _rope128(proj[:, OFF_K:OFF_K + KV_WIDTH], cos, sin)
    v_new = proj[:, OFF_V:OFF_V + KV_WIDTH]
    knew_ref[...] = k_new
    vnew_ref[...] = v_new
    lane = _iota((bb, LANES), 1)
    lo = lane < HEAD_DIM
    rsel = rsel_ref[...]
    n_slot = ck_ref.shape[1] // LANES
    for c in range(A_HEADS // 2):
        g = c // 2
        in_g = lo if g == 0 else jnp.logical_not(lo)
        qcol = _rope128(proj[:, OFF_Q + LANES * c:OFF_Q + LANES * (c + 1)], cos, sin) * (HEAD_DIM ** -0.5)
        qrol = pltpu.roll(qcol, HEAD_DIM, 1)
        outs = []
        for hh in range(2):
            sink = sinks_ref[2 * c + hh]
            qh = jnp.where(in_g, qcol if hh == g else qrol, 0.0)
            qt = jnp.concatenate([qh] * n_slot, axis=1)
            s_c = _mm(ck_ref[...] * qt, rsel)
            s_n = jnp.sum(k_new * qh, axis=-1, keepdims=True)
            m = jnp.maximum(jnp.maximum(jnp.max(s_c, axis=-1, keepdims=True), s_n), sink)
            p_c = jnp.exp(s_c - m)
            p_n = jnp.exp(s_n - m)
            den = jnp.sum(p_c, axis=-1, keepdims=True) + p_n + jnp.exp(sink - m)
            inv = 1.0 / den
            pe = _mm_nt(p_c * inv, rsel)
            pv = pe * cv_ref[...]
            o = pv[:, 0:LANES]
            for s_i in range(1, n_slot):
                o = o + pv[:, LANES * s_i:LANES * (s_i + 1)]
            o = o + (p_n * inv) * v_new
            outs.append(o if hh == g else pltpu.roll(o, HEAD_DIM, 1))
        ocol = jnp.where(lo, outs[0], outs[1])
        gacol = proj[:, OFF_GA + LANES * c:OFF_GA + LANES * (c + 1)]
        mix_s[:, LANES * c:LANES * (c + 1)] = ocol * _silu(gacol)

    pb = proj[:, OFF_PB:OFF_PB + B_SHIFT_PAD]
    shift_ref[...] = pb
    xs = pb + (sprev_ref[...] - pb) * mu_ref[...]
    r, k, v, ld, a, kkn, kmod = _rwkv_pre(xs, vec, wlora_ref[...], ones_bd)
    decay = jnp.exp(ld)
    kka = kkn * a
    hv = HEAD_DIM * HEAD_DIM
    n_rep = hv // LANES
    ycols = []
    for c in range(B_HEADS // 2):
        cs = slice(LANES * c, LANES * (c + 1))
        ycol = None
        for hh in range(2):
            hd = 2 * c + hh
            esel = esel_ref[hh]

            def tile_k(z):
                zc = z[:, cs]
                zr = pltpu.roll(zc, HEAD_DIM, 1)
                pair = jnp.where(lo, zc, zr) if hh == 0 else jnp.where(lo, zr, zc)
                return jnp.concatenate([pair] * n_rep, axis=1)

            st = wkv_ref[:, hv * hd:hv * (hd + 1)]
            sa = _mm_nt(st * tile_k(-kkn), esel)
            sa_e = _mm_exact_rhs(sa, esel)
            v_e = _mm_exact_rhs(v[:, cs], esel)
            st_new = st * tile_k(decay) + sa_e * tile_k(kka) + v_e * tile_k(kmod)
            wkvo_ref[:, hv * hd:hv * (hd + 1)] = st_new
            yh = _mm_nt(st_new * tile_k(r), esel)
            ycol = yh if ycol is None else ycol + yh
        ycols.append(ycol)
    y_wkv = jnp.concatenate(ycols, axis=1)
    gb = proj[:, OFF_GB:OFF_GB + B_WIDTH]
    mix_s[:, A_WIDTH:A_WIDTH + B_WIDTH] = _rwkv_post(y_wkv, r, kmod, v, gb, vec, ones_bd)

    xc = proj[:, OFF_XC:OFF_XC + C_WIDTH]
    gc = proj[:, OFF_GC:OFF_GC + C_WIDTH]
    u = vec[V_CB:V_CB + 1, :] + xc * vec[V_CW3:V_CW3 + 1, :]
    for j in range(CONV_W - 1):
        u = u + conv_ref[:, C_WIDTH * j:C_WIDTH * (j + 1)] * vec[V_CW0 + j:V_CW0 + j + 1, :]
    a_l, mult, gi = _lru_gates(u, vec, wgate_ref[...])
    h_l = a_l * hl_ref[...] + mult * gi * u
    hlo_ref[...] = h_l
    convo_ref[:, 0:2 * C_WIDTH] = conv_ref[:, C_WIDTH:3 * C_WIDTH]
    convo_ref[:, 2 * C_WIDTH:3 * C_WIDTH] = xc
    mix_s[:, A_WIDTH + B_WIDTH:] = h_l * _silu(gc)

    yo = _dot(_bf(mix_s[...]), wout_ref[...])
    y_ref[...] = x + gate_m * _rmsnorm(yo, npost_ref[...])


def _sample_layer(x, mod3, lw, cos_r, sin_r, ck, cv, sprev, wkv, conv, hl, rsel, esel):
    nb = x.shape[0]
    bb = SAMPLE_BLOCK
    n_state = wkv.shape[1]
    n_cache = ck.shape[1]
    const2 = lambda i: (0, 0)
    rows2 = lambda i: (i, 0)
    return pl.pallas_call(
        _sample_kernel,
        grid=(nb // bb,),
        in_specs=[
            pl.BlockSpec(memory_space=pltpu.SMEM),
            pl.BlockSpec((bb, D_MODEL), rows2),
            pl.BlockSpec((3, bb, D_MODEL), lambda i: (0, i, 0)),
            pl.BlockSpec((1, D_MODEL), const2),
            pl.BlockSpec((1, D_MODEL), const2),
            pl.BlockSpec((D_MODEL, P_PAD), const2),
            pl.BlockSpec((D_MODEL, D_MODEL), const2),
            pl.BlockSpec((1, LANES), const2),
            pl.BlockSpec((1, LANES), const2),
            pl.BlockSpec((1, B_SHIFT_PAD), const2),
            pl.BlockSpec((VEC_ROWS, B_WIDTH), const2),
            pl.BlockSpec((LANES, 2 * B_WIDTH), const2),
            pl.BlockSpec((C_WIDTH, 2 * C_WIDTH), const2),
            pl.BlockSpec((bb, n_cache), rows2),
            pl.BlockSpec((bb, n_cache), rows2),
            pl.BlockSpec((bb, B_SHIFT_PAD), rows2),
            pl.BlockSpec((bb, n_state), rows2),
            pl.BlockSpec((bb, 3 * C_WIDTH), rows2),
            pl.BlockSpec((bb, C_WIDTH), rows2),
            pl.BlockSpec(rsel.shape, const2),
            pl.BlockSpec(esel.shape, lambda i: (0, 0, 0)),
        ],
        out_specs=[
            pl.BlockSpec((bb, D_MODEL), rows2),
            pl.BlockSpec((bb, KV_WIDTH), rows2),
            pl.BlockSpec((bb, KV_WIDTH), rows2),
            pl.BlockSpec((bb, B_SHIFT_PAD), rows2),
            pl.BlockSpec((bb, n_state), rows2),
            pl.BlockSpec((bb, 3 * C_WIDTH), rows2),
            pl.BlockSpec((bb, C_WIDTH), rows2),
        ],
        out_shape=[
            jax.ShapeDtypeStruct((nb, D_MODEL), F32),
            jax.ShapeDtypeStruct((nb, KV_WIDTH), F32),
            jax.ShapeDtypeStruct((nb, KV_WIDTH), F32),
            jax.ShapeDtypeStruct((nb, B_SHIFT_PAD), F32),
            jax.ShapeDtypeStruct((nb, n_state), F32),
            jax.ShapeDtypeStruct((nb, 3 * C_WIDTH), F32),
            jax.ShapeDtypeStruct((nb, C_WIDTH), F32),
        ],
        scratch_shapes=[pltpu.VMEM((bb, D_MODEL), F32)],
        compiler_params=pltpu.CompilerParams(
            dimension_semantics=("arbitrary",), vmem_limit_bytes=VMEM_LIMIT_BYTES),
        name="sample_layer",
    )(lw["sinks"], x, mod3, lw["norm_pre"], lw["norm_post"], lw["w_in"], lw["w_out"], cos_r, sin_r,
      lw["mu"], lw["vec"], lw["wlora"], lw["wgate"], ck, cv, sprev, wkv, conv, hl, rsel, esel)


def _rope_tables(pos):
    half = HEAD_DIM // 2
    inv_freq = ROPE_THETA ** (-jnp.arange(half, dtype=F32) / half)
    ang = pos.astype(F32)[:, None] * inv_freq[None, :]
    cos = jnp.cos(ang)
    sin = jnp.sin(ang)
    cos_t = jnp.concatenate([cos, cos, cos, cos], axis=-1)
    sin_t = jnp.concatenate([-sin, sin, -sin, sin], axis=-1)
    return cos_t, sin_t


def _block_diag(w):
    n, d, e = w.shape
    eye = jnp.eye(n, dtype=w.dtype)
    return (eye[:, None, :, None] * w[:, :, None, :]).reshape(n * d, n * e)


def _layer_weights(l, p):
    w_in = p["w_in"][l]
    w_in_p = jnp.concatenate(
        [w_in[:, :P_ORIG_PB_END], jnp.zeros((D_MODEL, B_SHIFT_PAD - B_SHIFT), F32), w_in[:, P_ORIG_PB_END:]], axis=1)
    wlora = jnp.zeros((LANES, 2 * B_WIDTH), F32)
    wlora = wlora.at[0:LORA, 0:B_WIDTH].set(p["rwkv_w_up"][l])
    wlora = wlora.at[LORA:2 * LORA, B_WIDTH:].set(p["rwkv_a_up"][l])
    wgate = jnp.concatenate([_block_diag(p["lru_gate_a_w"][l]), _block_diag(p["lru_gate_x_w"][l])], axis=1)
    rows = [p["rwkv_w0"][l], p["rwkv_a0"][l], p["rwkv_k_k"][l], p["rwkv_k_a"][l], p["rwkv_lnx_w"][l],
            p["rwkv_lnx_b"][l], p["rwkv_r_k"][l].reshape(B_WIDTH), p["lru_conv_b"][l], p["lru_gate_a_b"][l],
            p["lru_gate_x_b"][l], p["lru_lambda"][l], p["lru_conv_w"][l][0], p["lru_conv_w"][l][1],
            p["lru_conv_w"][l][2], p["lru_conv_w"][l][3], jnp.zeros((B_WIDTH,), F32)]
    return {
        "sinks": p["attn_sinks"][l],
        "norm_pre": p["norm_pre"][l].reshape(1, D_MODEL),
        "norm_post": p["norm_post"][l].reshape(1, D_MODEL),
        "w_in": w_in_p.astype(BF16),
        "w_out": p["w_out"][l].astype(BF16),
        "mu": jnp.concatenate([p["rwkv_mu"][l], jnp.zeros((B_SHIFT_PAD - B_SHIFT,), F32)]).reshape(1, B_SHIFT_PAD),
        "vec": jnp.stack(rows),
        "wlora": wlora.astype(BF16),
        "wgate": wgate.astype(BF16),
    }


def _selectors(n_slot):
    slot = jnp.arange(n_slot * LANES) // LANES
    rsel = (slot[:, None] == jnp.arange(n_slot)[None, :]).astype(BF16)
    val = jnp.arange(HEAD_DIM * HEAD_DIM) // HEAD_DIM
    esel = jnp.stack([(jnp.arange(LANES)[:, None] == (HEAD_DIM * hh + val)[None, :]).astype(BF16)
                      for hh in range(2)])
    return rsel, esel


def _forward(x_prompt, x_sample, c_prompt, c_sample, cache_swa_k, cache_swa_v, state_rwkv_shift,
             state_rwkv_wkv, state_lru_conv, state_lru_h, p, tb):
    bp, seq, _ = x_prompt.shape
    nb = x_sample.shape[0]
    n_slot = cache_swa_k.shape[2]

    pad_rows = (-(nb + bp)) % SUBLANES
    c_all = jnp.concatenate([c_sample, c_prompt, jnp.zeros((pad_rows, D_MODEL), F32)], axis=0)
    mod = _mod_call(c_all, p["w_mod"], p["b_mod"])

    cos_p, sin_p = _rope_tables(jnp.arange(seq))
    cos_s, sin_s = _rope_tables(jnp.full((1,), PAST_LEN))
    rsel, esel = _selectors(n_slot)

    yp = x_prompt
    ys = x_sample.reshape(nb, D_MODEL)
    outs_p, outs_s = [], []
    for l in range(DEPTH):
        lw = _layer_weights(l, p)
        mod_p = mod[l, nb:nb + bp].reshape(bp, 3, D_MODEL)
        mod_s = mod[l, :nb].reshape(nb, 3, D_MODEL).transpose(1, 0, 2)
        yp, kv_p, sh_p, wkv_p, conv_p, h_p = _prompt_layer(yp, mod_p, lw, cos_p, sin_p, tb)
        outs_p.append((
            kv_p[:, :, :KV_WIDTH].reshape(bp, WINDOW, A_KV_HEADS, HEAD_DIM),
            kv_p[:, :, KV_WIDTH:].reshape(bp, WINDOW, A_KV_HEADS, HEAD_DIM),
            sh_p[:, SUBLANES - 1, :B_SHIFT],
            jnp.stack([wkv_p[:, HEAD_DIM * hd:HEAD_DIM * (hd + 1), HEAD_DIM * hd:HEAD_DIM * (hd + 1)]
                       for hd in range(B_HEADS)], axis=1),
            conv_p[:, SUBLANES - (CONV_W - 1):, :],
            h_p[:, SUBLANES - 1, :],
        ))
        sprev = jnp.concatenate([state_rwkv_shift[l], jnp.zeros((nb, B_SHIFT_PAD - B_SHIFT), F32)], axis=1)
        ys, k_s, v_s, sh_s, wkv_s, conv_s, h_s = _sample_layer(
            ys, mod_s, lw, cos_s, sin_s,
            cache_swa_k[l].reshape(nb, n_slot * KV_WIDTH), cache_swa_v[l].reshape(nb, n_slot * KV_WIDTH),
            sprev, state_rwkv_wkv[l].reshape(nb, B_HEADS * HEAD_DIM * HEAD_DIM),
            state_lru_conv[l].reshape(nb, (CONV_W - 1) * C_WIDTH), state_lru_h[l], rsel, esel)
        outs_s.append((
            k_s.reshape(nb, 1, A_KV_HEADS, HEAD_DIM),
            v_s.reshape(nb, 1, A_KV_HEADS, HEAD_DIM),
            sh_s[:, :B_SHIFT],
            wkv_s.reshape(nb, B_HEADS, HEAD_DIM, HEAD_DIM),
            conv_s.reshape(nb, CONV_W - 1, C_WIDTH),
            h_s,
        ))
    sp = [jnp.stack(z) for z in zip(*outs_p)]
    ss = [jnp.stack(z) for z in zip(*outs_s)]
    return (yp, ys.reshape(nb, 1, D_MODEL), *sp, *ss)


def kernel(x_prompt, x_sample, c_prompt, c_sample, cache_swa_k, cache_swa_v, state_rwkv_shift, state_rwkv_wkv, state_lru_conv, state_lru_h, norm_pre, norm_post, w_mod, b_mod, w_in, w_out, attn_sinks, rwkv_mu, rwkv_w0, rwkv_w_up, rwkv_a0, rwkv_a_up, rwkv_k_k, rwkv_k_a, rwkv_r_k, rwkv_lnx_w, rwkv_lnx_b, lru_conv_w, lru_conv_b, lru_gate_a_w, lru_gate_a_b, lru_gate_x_w, lru_gate_x_b, lru_lambda):
    p = dict(norm_pre=norm_pre, norm_post=norm_post, w_mod=w_mod, b_mod=b_mod, w_in=w_in, w_out=w_out,
             attn_sinks=attn_sinks, rwkv_mu=rwkv_mu, rwkv_w0=rwkv_w0, rwkv_w_up=rwkv_w_up, rwkv_a0=rwkv_a0,
             rwkv_a_up=rwkv_a_up, rwkv_k_k=rwkv_k_k, rwkv_k_a=rwkv_k_a, rwkv_r_k=rwkv_r_k,
             rwkv_lnx_w=rwkv_lnx_w, rwkv_lnx_b=rwkv_lnx_b, lru_conv_w=lru_conv_w, lru_conv_b=lru_conv_b,
             lru_gate_a_w=lru_gate_a_w, lru_gate_a_b=lru_gate_a_b, lru_gate_x_w=lru_gate_x_w,
             lru_gate_x_b=lru_gate_x_b, lru_lambda=lru_lambda)
    return _forward(x_prompt, x_sample, c_prompt, c_sample, cache_swa_k, cache_swa_v, state_rwkv_shift,
                    state_rwkv_wkv, state_lru_conv, state_lru_h, p, TIME_BLOCK)
```

```python
import functools

import numpy as np
import jax
import jax.numpy as jnp
from jax import lax
from jax.experimental import pallas as pl
from jax.experimental.pallas import tpu as pltpu

F32 = jnp.float32
BF16 = jnp.bfloat16

D_MODEL = 1024
DEPTH = 2
A_HEADS = 8
A_KV_HEADS = 2
HEAD_DIM = 64
A_WIDTH = A_HEADS * HEAD_DIM
KV_WIDTH = A_KV_HEADS * HEAD_DIM
WINDOW = 128
Q_BLOCK = 128
ROPE_THETA = 10000.0
B_WIDTH = 256
B_HEADS = 4
LORA = 32
B_SHIFT = 3 * B_WIDTH + 2 * LORA
B_SHIFT_PAD = 3 * B_WIDTH + 128
LNX_EPS = 1e-5 * 8 ** 2
C_WIDTH = 256
CONV_W = 4
LRU_C = 8.0
EPS = 1e-6
PAST_LEN = 8192
EXP_NEG_HALF = float(np.exp(-0.5))

LANES = 128
SUBLANES = 8
VMEM_LIMIT_BYTES = 56 * 1024 * 1024

OFF_Q = 0
OFF_K = OFF_Q + A_WIDTH
OFF_V = OFF_K + KV_WIDTH
OFF_GA = OFF_V + KV_WIDTH
OFF_PB = OFF_GA + A_WIDTH
OFF_GB = OFF_PB + B_SHIFT_PAD
OFF_XC = OFF_GB + B_WIDTH
OFF_GC = OFF_XC + C_WIDTH
P_PAD = OFF_GC + C_WIDTH
P_ORIG_PB_END = OFF_PB + B_SHIFT

CHUNK = 64
TIME_BLOCK = 512

(V_W0, V_A0, V_KK, V_KA, V_LNW, V_LNB, V_RK, V_CB, V_GAB, V_GXB, V_LAM,
 V_CW0, V_CW1, V_CW2, V_CW3) = range(15)
VEC_ROWS = 16

(T_DECAY, T_NKK, T_KKA, T_KMOD, T_R, T_V) = range(6)

_NT = (((1,), (1,)), ((), ()))
_TN = (((0,), (0,)), ((), ()))


def _bf(x):
    return x.astype(BF16)


def _dot(a, b):
    return jnp.dot(a, b, preferred_element_type=F32)


def _dot_nt(a, b):
    return lax.dot_general(a, b, _NT, preferred_element_type=F32)


def _dot_tn(a, b):
    return lax.dot_general(a, b, _TN, preferred_element_type=F32)


def _iota(shape, dim):
    return lax.broadcasted_iota(jnp.int32, shape, dim)


def _sigmoid(x):
    return jax.nn.sigmoid(x)


def _silu(x):
    return x * _sigmoid(x)


def _softplus(x):
    return jnp.maximum(x, 0.0) + jnp.log1p(jnp.exp(-jnp.abs(x)))


def _head_ones(n):
    r = _iota((n, n), 0) // HEAD_DIM
    c = _iota((n, n), 1) // HEAD_DIM
    return jnp.where(r == c, 1.0, 0.0).astype(BF16)


def _head_sum(x, ones_bd):
    return _dot(_bf(x), ones_bd)


def _rmsnorm(x, g):
    ms = jnp.mean(x * x, axis=-1, keepdims=True)
    return x * lax.rsqrt(ms + EPS) * g


def _rope128(z, cos, sin_signed):
    lane = _iota(z.shape, 1)
    first = (lane & 32) == 0
    sw = jnp.where(first, pltpu.roll(z, 96, 1), pltpu.roll(z, 32, 1))
    return z * cos + sw * sin_signed


def _rwkv_pre(xs, vec, wlora, ones_bd):
    r = xs[:, 0:B_WIDTH]
    k = xs[:, B_WIDTH:2 * B_WIDTH]
    v = xs[:, 2 * B_WIDTH:3 * B_WIDTH]
    lor = xs[:, 3 * B_WIDTH:B_SHIFT_PAD]
    lane = _iota(lor.shape, 1)
    z = jnp.where(lane < LORA, jnp.tanh(lor), lor)
    wa = _dot(_bf(z), wlora)
    zw = vec[V_W0:V_W0 + 1, :] + wa[:, :B_WIDTH]
    ld = -EXP_NEG_HALF * _sigmoid(zw)
    a = _sigmoid(vec[V_A0:V_A0 + 1, :] + wa[:, B_WIDTH:])
    kk = k * vec[V_KK:V_KK + 1, :]
    kkn = kk * jnp.minimum(lax.rsqrt(_head_sum(kk * kk, ones_bd)), 1e12)
    kmod = k * (1.0 + (a - 1.0) * vec[V_KA:V_KA + 1, :])
    return r, k, v, ld, a, kkn, kmod


def _rwkv_post(y, r, kmod, v, gb, vec, ones_bd):
    inv = 1.0 / HEAD_DIM
    mu = _head_sum(y, ones_bd) * inv
    yc = y - mu
    var = _head_sum(yc * yc, ones_bd) * inv
    yn = yc * lax.rsqrt(var + LNX_EPS)
    yn = yn * vec[V_LNW:V_LNW + 1, :] + vec[V_LNB:V_LNB + 1, :]
    bonus = _head_sum(r * kmod * vec[V_RK:V_RK + 1, :], ones_bd) * v
    return (yn + bonus) * _silu(gb)


def _lru_gates(u, vec, wgate):
    gates = _dot(_bf(u), wgate)
    gr = _sigmoid(gates[:, :C_WIDTH] + vec[V_GAB:V_GAB + 1, :])
    gi = _sigmoid(gates[:, C_WIDTH:] + vec[V_GXB:V_GXB + 1, :])
    sp = _softplus(-vec[V_LAM:V_LAM + 1, :])
    log_a = -LRU_C * gr * sp
    a = jnp.exp(log_a)
    mult = jnp.sqrt(1.0 - a * a)
    return a, mult, gi


def _project(h, win_ref, proj_s, groups):
    for c0, c1 in groups:
        proj_s[:, c0:c1] = _dot_nt(h, win_ref[c0:c1, :])


def _mod_kernel(c_ref, w_ref, b_ref, o_ref):
    o_ref[...] = _dot(_bf(_silu(c_ref[...])), _bf(w_ref[...])) + b_ref[...]


def _mod_call(c_all, w_mod, b_mod):
    rows = c_all.shape[0]
    return pl.pallas_call(
        _mod_kernel,
        grid=(DEPTH, 3),
        in_specs=[
            pl.BlockSpec((rows, D_MODEL), lambda l, j: (0, 0)),
            pl.BlockSpec((None, D_MODEL, D_MODEL), lambda l, j: (l, 0, j)),
            pl.BlockSpec((None, 1, D_MODEL), lambda l, j: (l, 0, j)),
        ],
        out_specs=pl.BlockSpec((None, None, rows, D_MODEL), lambda l, j: (l, j, 0, 0)),
        out_shape=jax.ShapeDtypeStruct((DEPTH, 3, rows, D_MODEL), F32),
        compiler_params=pltpu.CompilerParams(
            dimension_semantics=("arbitrary", "arbitrary"), vmem_limit_bytes=VMEM_LIMIT_BYTES),
        name="adaln_mod",
    )(c_all, w_mod, b_mod.reshape(DEPTH, 1, 3 * D_MODEL))


def _prompt_kernel(sinks_ref, x_ref, mod_ref, npre_ref, npost_ref, win_ref, wout_ref,
                   cos_ref, sin_ref, mu_ref, vec_ref, wlora_ref, wgate_ref,
                   y_ref, kv_ref, shift_ref, wkv_ref, conv_ref, hlru_ref,
                   kvprev, pprev, xcprev, hprev, gstate, proj_s, mix_s, ywkv_s, *, layer):
    tb = x_ref.shape[1]
    t = pl.program_id(1)

    @pl.when(t == 0)
    def _():
        kvprev[...] = jnp.zeros_like(kvprev)
        pprev[...] = jnp.zeros_like(pprev)
        xcprev[...] = jnp.zeros_like(xcprev)
        hprev[...] = jnp.zeros_like(hprev)
        gstate[...] = jnp.zeros_like(gstate)

    vec = vec_ref[...]
    ones_bd = _head_ones(B_WIDTH)

    x = x_ref[0]
    brow = pl.ds(pl.program_id(0), 1)
    shift_m = mod_ref[0, brow, :]
    scale_m = mod_ref[1, brow, :]
    gate_m = mod_ref[2, brow, :]
    h = _bf(_rmsnorm(x, npre_ref[...]) * (1.0 + scale_m) + shift_m)
    _project(h, win_ref, proj_s, ((OFF_PB, OFF_GB), (OFF_Q, OFF_GA), (OFF_XC, OFF_GC),
                                  (OFF_GA, OFF_PB), (OFF_GB, OFF_XC), (OFF_GC, P_PAD)))

    pb = proj_s[:, OFF_PB:OFF_PB + B_SHIFT_PAD]
    row_p = _iota(pb.shape, 0)
    prev = jnp.where(row_p == 0, pprev[SUBLANES - 1:SUBLANES, :], pltpu.roll(pb, 1, 0))
    xs = pb + (prev - pb) * mu_ref[...]
    r, k, v, ld, a, kkn, kmod = _rwkv_pre(xs, vec, wlora_ref[...], ones_bd)
    kka = kkn * a

    cos = cos_ref[...]
    sin = sin_ref[...]
    k_rot = _rope128(proj_s[:, OFF_K:OFF_K + KV_WIDTH], cos, sin)
    v_att = proj_s[:, OFF_V:OFF_V + KV_WIDTH]
    kfull = jnp.concatenate([kvprev[:, 0:KV_WIDTH], k_rot], axis=0)
    vfull = jnp.concatenate([kvprev[:, KV_WIDTH:2 * KV_WIDTH], v_att], axis=0)
    lane_kv = _iota(kfull.shape, 1)
    lo_kv = lane_kv < HEAD_DIM

    def variants(z):
        zr = pltpu.roll(z, HEAD_DIM, 1)
        a_ = _bf(jnp.where(lo_kv, z, 0.0))
        b_ = _bf(jnp.where(lo_kv, 0.0, z))
        c_ = _bf(jnp.where(lo_kv, zr, 0.0))
        d_ = _bf(jnp.where(lo_kv, 0.0, zr))
        return ((a_, d_), (c_, b_))

    kvar = variants(kfull)
    vvar = variants(vfull)

    qi = _iota((Q_BLOCK, 2 * Q_BLOCK), 0)
    sj = _iota((Q_BLOCK, 2 * Q_BLOCK), 1)
    band = (sj >= qi) & (sj <= qi + WINDOW)
    first_lo = jnp.where(t == 0, Q_BLOCK, 0)
    neg_inf = -jnp.inf

    pairs = [(c, i) for c in range(A_HEADS // 2) for i in range(tb // Q_BLOCK)]

    def win_stack(var, c, i):
        g, r0 = c // 2, i * Q_BLOCK
        return jnp.concatenate([var[g][0][r0:r0 + 2 * Q_BLOCK], var[g][1][r0:r0 + 2 * Q_BLOCK]], axis=0)

    qcols = [_bf(_rope128(proj_s[:, OFF_Q + LANES * c:OFF_Q + LANES * (c + 1)], cos, sin) * (HEAD_DIM ** -0.5))
             for c in range(A_HEADS // 2)]
    scores = [_dot_nt(qcols[c][i * Q_BLOCK:(i + 1) * Q_BLOCK], win_stack(kvar, c, i))
              for c, i in pairs]

    k_tail = k_rot[tb - WINDOW:]
    v_tail = v_att[tb - WINDOW:]

    xc = proj_s[:, OFF_XC:OFF_XC + C_WIDTH]
    xext = jnp.concatenate([xcprev[...], xc], axis=0)
    u = vec[V_CB:V_CB + 1, :] + xc * vec[V_CW3:V_CW3 + 1, :]
    for j in range(1, CONV_W):
        u = u + pltpu.roll(xext, j, 0)[SUBLANES:, :] * vec[V_CW3 - j:V_CW3 - j + 1, :]
    a_l, mult, gi = _lru_gates(u, vec, wgate_ref[...])
    row_c = _iota((tb, C_WIDTH), 0)
    reset = (row_c + t * tb) == 0
    a_l = jnp.where(reset, 0.0, a_l)
    mult = jnp.where(reset, 1.0, mult)
    b_l = mult * gi * u
    b_l = b_l + jnp.where(row_c == 0, a_l * hprev[SUBLANES - 1:SUBLANES, :], 0.0)
    d = 1
    while d < tb:
        keep = row_c >= d
        a_sh = jnp.where(keep, pltpu.roll(a_l, d, 0), 1.0)
        b_sh = jnp.where(keep, pltpu.roll(b_l, d, 0), 0.0)
        b_l = b_l + a_l * b_sh
        a_l = a_l * a_sh
        d *= 2
    h_l = b_l

    nck = tb // CHUNK
    sls = [slice(c * CHUNK, (c + 1) * CHUNK) for c in range(nck)]
    row_b = _iota((tb, B_WIDTH), 0) % CHUNK
    lcum = ld
    d = 1
    while d < CHUNK:
        lcum = lcum + jnp.where(row_b >= d, pltpu.roll(lcum, d, 0), 0.0)
        d *= 2
    ltot = jnp.concatenate(
        [jnp.broadcast_to(lcum[s.stop - 1:s.stop, :], (CHUNK, B_WIDTH)) for s in sls], axis=0)
    e_l = jnp.exp(lcum)
    rt = r * e_l
    at = -kkn * jnp.exp(lcum - ld)
    e_nl = jnp.exp(-lcum)
    ktil = kmod * e_nl
    btil = kka * e_nl
    e_el = jnp.exp(ltot - lcum)
    khat = kmod * e_el
    bhat = kka * e_el
    gam = jnp.exp(ltot)

    n4 = B_HEADS * CHUNK
    r4 = _iota((n4, n4), 0)
    c4 = _iota((n4, n4), 1)
    same4 = (r4 // CHUNK) == (c4 // CHUNK)
    strict = same4 & ((r4 % CHUNK) > (c4 % CHUNK))
    incl = same4 & ((r4 % CHUNK) >= (c4 % CHUNK))
    eye4 = jnp.where(r4 == c4, 1.0, 0.0)
    bd_heads = (_iota((B_WIDTH, B_WIDTH), 0) // HEAD_DIM) == (_iota((B_WIDTH, B_WIDTH), 1) // HEAD_DIM)

    def stk(z):
        zb = _bf(z)
        return jnp.concatenate([zb] * B_HEADS, axis=0) * ones_bd

    def collapse(z):
        return z[0:CHUNK] + z[CHUNK:2 * CHUNK] + z[2 * CHUNK:3 * CHUNK] + z[3 * CHUNK:4 * CHUNK]

    rng = range(nck)
    a_s = [stk(at[s]) for s in sls]
    r_s = [stk(rt[s]) for s in sls]
    b_s = [stk(btil[s]) for s in sls]
    k_s = [stk(ktil[s]) for s in sls]
    v_s = [stk(v[s]) for s in sls]
    bh_s = [stk(bhat[s]) for s in sls]
    sc = [_dot_nt(jnp.concatenate([a_s[c], r_s[c]], axis=0), jnp.concatenate([b_s[c], k_s[c]], axis=0))
          for c in rng]

    probs = []
    for (c, i), s in zip(pairs, scores):
        mask = band & (sj >= first_lo) if i == 0 else band
        ps = []
        for hh in range(2):
            sink = sinks_ref[layer, 2 * c + hh]
            sh = jnp.where(mask, s[:, 2 * Q_BLOCK * hh:2 * Q_BLOCK * (hh + 1)], neg_inf)
            m = jnp.maximum(jnp.max(sh, axis=-1, keepdims=True), sink)
            p = jnp.exp(sh - m)
            den = jnp.sum(p, axis=-1, keepdims=True) + jnp.exp(sink - m)
            ps.append(_bf(p * (1.0 / den)))
        probs.append(jnp.concatenate(ps, axis=1))
    for (c, i), p in zip(pairs, probs):
        r0 = i * Q_BLOCK
        o = _dot(p, win_stack(vvar, c, i))
        ga = proj_s[r0:r0 + Q_BLOCK, OFF_GA + LANES * c:OFF_GA + LANES * (c + 1)]
        mix_s[r0:r0 + Q_BLOCK, LANES * c:LANES * (c + 1)] = o * _silu(ga)

    lab = [jnp.where(strict, sc[c][0:n4, 0:n4], 0.0) for c in rng]
    lak = [_bf(jnp.where(strict, sc[c][0:n4, n4:], 0.0)) for c in rng]
    mrb = [_bf(jnp.where(incl, sc[c][n4:, 0:n4], 0.0)) for c in rng]
    mrk = [_bf(jnp.where(incl, sc[c][n4:, n4:], 0.0)) for c in rng]
    tinv = [eye4 + lab[c] for c in rng]
    pw = [_bf(lab[c]) for c in rng]
    pw = [_bf(_dot(pw[c], pw[c])) for c in rng]
    for _ in range(4):
        res = [_dot(jnp.concatenate([_bf(tinv[c]), pw[c]], axis=0), pw[c]) for c in rng]
        tinv = [tinv[c] + res[c][0:n4] for c in rng]
        pw = [_bf(res[c][n4:]) for c in rng]
    tinv = [_bf(tinv[c] + _dot(_bf(tinv[c]), pw[c])) for c in rng]
    lv = [_bf(_dot(lak[c], v_s[c])) for c in rng]
    wz = [_dot(tinv[c], jnp.concatenate([a_s[c], lv[c]], axis=1)) for c in rng]
    w_s = [_bf(wz[c][:, 0:B_WIDTH]) for c in rng]
    z_s = [wz[c][:, B_WIDTH:] for c in rng]
    rp = [_bf(r_s[c].astype(F32) + _dot(mrb[c], w_s[c])) for c in rng]
    y0 = [_dot(jnp.concatenate([mrb[c], mrk[c]], axis=1), jnp.concatenate([_bf(z_s[c]), v_s[c]], axis=0))
          for c in rng]
    pp = [_bf(_dot_tn(w_s[c], bh_s[c])) for c in rng]
    qq = [jnp.where(bd_heads,
                    _dot_tn(_bf(jnp.concatenate([collapse(z_s[c]), v[sls[c]]], axis=0)),
                            _bf(jnp.concatenate([bhat[sls[c]], khat[sls[c]]], axis=0))), 0.0) for c in rng]
    gmat = gstate[...]
    for c in rng:
        gb16 = _bf(gmat)
        ywkv_s[sls[c], :] = collapse(_dot_nt(rp[c], gb16) + y0[c])
        gmat = gam[c * CHUNK:c * CHUNK + 1, :] * gmat + _dot(gb16, pp[c]) + qq[c]
    gstate[...] = gmat

    gb = proj_s[:, OFF_GB:OFF_GB + B_WIDTH]
    mix_s[:, A_WIDTH:A_WIDTH + B_WIDTH] = _rwkv_post(ywkv_s[...], r, kmod, v, gb, vec, ones_bd)

    gc = proj_s[:, OFF_GC:OFF_GC + C_WIDTH]
    mix_s[:, A_WIDTH + B_WIDTH:] = h_l * _silu(gc)

    yo = _dot(_bf(mix_s[...]), wout_ref[...])
    y_ref[0] = x + gate_m * _rmsnorm(yo, npost_ref[...])

    kvprev[:, 0:KV_WIDTH] = k_tail
    kvprev[:, KV_WIDTH:2 * KV_WIDTH] = v_tail
    pprev[...] = pb[tb - SUBLANES:, :]
    xcprev[...] = xc[tb - SUBLANES:, :]
    hprev[...] = h_l[tb - SUBLANES:, :]
    shift_ref[0] = pb[tb - SUBLANES:, :]
    wkv_ref[0] = gmat
    conv_ref[0] = xc[tb - SUBLANES:, :]
    hlru_ref[0] = h_l[tb - SUBLANES:, :]

    @pl.when(t == pl.num_programs(1) - 1)
    def _():
        kv_ref[0, 0] = k_tail.T
        kv_ref[0, 1] = v_tail.T


def _weight_specs(layer_of, grid_rank):
    def lay(*shape):
        zeros = (0,) * len(shape)
        if grid_rank == 1:
            return pl.BlockSpec((None, *shape), lambda i: (layer_of(i), *zeros))
        return pl.BlockSpec((None, *shape), lambda a, b: (layer_of(a, b), *zeros))
    return [
        lay(1, D_MODEL),
        lay(1, D_MODEL),
        lay(P_PAD, D_MODEL),
        lay(D_MODEL, D_MODEL),
    ], [
        lay(1, B_SHIFT_PAD),
        lay(VEC_ROWS, B_WIDTH),
        lay(LANES, 2 * B_WIDTH),
        lay(C_WIDTH, 2 * C_WIDTH),
    ]


def _prompt_layer(layer, x, mod, wts, cos_t, sin_t, mod_row0, tb):
    bsz, seq, _ = x.shape
    nt = seq // tb
    per_b3 = lambda b, t: (b, 0, 0)
    w_head, w_tail = _weight_specs(lambda b, t: layer, 2)
    return pl.pallas_call(
        functools.partial(_prompt_kernel, layer=layer),
        grid=(bsz, nt),
        in_specs=[
            pl.BlockSpec(memory_space=pltpu.SMEM),
            pl.BlockSpec((1, tb, D_MODEL), lambda b, t: (b, t, 0)),
            pl.BlockSpec((None, 3, SUBLANES, D_MODEL),
                         lambda b, t: (layer, 0, mod_row0 // SUBLANES, 0)),
            *w_head,
            pl.BlockSpec((tb, LANES), lambda b, t: (t, 0)),
            pl.BlockSpec((tb, LANES), lambda b, t: (t, 0)),
            *w_tail,
        ],
        out_specs=[
            pl.BlockSpec((1, tb, D_MODEL), lambda b, t: (b, t, 0)),
            pl.BlockSpec((1, 2, KV_WIDTH, WINDOW), lambda b, t: (b, 0, 0, 0)),
            pl.BlockSpec((1, SUBLANES, B_SHIFT_PAD), per_b3),
            pl.BlockSpec((1, B_WIDTH, B_WIDTH), per_b3),
            pl.BlockSpec((1, SUBLANES, C_WIDTH), per_b3),
            pl.BlockSpec((1, SUBLANES, C_WIDTH), per_b3),
        ],
        out_shape=[
            jax.ShapeDtypeStruct((bsz, seq, D_MODEL), F32),
            jax.ShapeDtypeStruct((bsz, 2, KV_WIDTH, WINDOW), F32),
            jax.ShapeDtypeStruct((bsz, SUBLANES, B_SHIFT_PAD), F32),
            jax.ShapeDtypeStruct((bsz, B_WIDTH, B_WIDTH), F32),
            jax.ShapeDtypeStruct((bsz, SUBLANES, C_WIDTH), F32),
            jax.ShapeDtypeStruct((bsz, SUBLANES, C_WIDTH), F32),
        ],
        scratch_shapes=[
            pltpu.VMEM((WINDOW, 2 * KV_WIDTH), F32),
            pltpu.VMEM((SUBLANES, B_SHIFT_PAD), F32),
            pltpu.VMEM((SUBLANES, C_WIDTH), F32),
            pltpu.VMEM((SUBLANES, C_WIDTH), F32),
            pltpu.VMEM((B_WIDTH, B_WIDTH), F32),
            pltpu.VMEM((tb, P_PAD), F32),
            pltpu.VMEM((tb, D_MODEL), F32),
            pltpu.VMEM((tb, B_WIDTH), F32),
        ],
        compiler_params=pltpu.CompilerParams(
            dimension_semantics=("arbitrary", "arbitrary"), vmem_limit_bytes=VMEM_LIMIT_BYTES),
        name="prompt_layer",
    )(wts["sinks"], x, mod, wts["norm_pre"], wts["norm_post"], wts["w_in"], wts["w_out"], cos_t, sin_t,
      wts["mu"], wts["vec"], wts["wlora"], wts["wgate"])


def _sample_kernel(sinks_ref, x_ref, mod_ref, npre_ref, npost_ref, win_ref, wout_ref,
                   cos_ref, sin_ref, mu_ref, vec_ref, wlora_ref, wgate_ref,
                   kt_ref, vt_ref, sprev_ref, wkv_ref, conv_ref, hl_ref,
                   y_ref, knew_ref, vnew_ref, shift_ref, wkvo_ref, convo_ref, hlo_ref,
                   ys_s, proj_s, mix_s, q_s, o_s, knew_s, rw_s, rkv_s, yt_s):
    layer = pl.program_id(0)
    j = pl.program_id(1)
    nb = x_ref.shape[0]
    bb = nb // B_HEADS
    n_slot = kt_ref.shape[1]
    vec = vec_ref[...]
    cos = cos_ref[...]
    sin = sin_ref[...]

    @pl.when(jnp.logical_and(layer == 0, j == 0))
    def _():
        ys_s[...] = x_ref[...]

    @pl.when(j == 0)
    def _():
        ones_bd = _head_ones(B_WIDTH)
        x = ys_s[...]
        h = _bf(_rmsnorm(x, npre_ref[...]) * (1.0 + mod_ref[1]) + mod_ref[0])
        _project(h, win_ref, proj_s, ((0, P_PAD),))
        k_new = _rope128(proj_s[:, OFF_K:OFF_K + KV_WIDTH], cos, sin)
        knew_s[...] = k_new
        knew_ref[...] = k_new.T
        vnew_ref[...] = proj_s[:, OFF_V:OFF_V + KV_WIDTH].T
        pb = proj_s[:, OFF_PB:OFF_PB + B_SHIFT_PAD]
        pb_t = pb.T
        shift_ref[...] = pb_t[0:B_SHIFT, :]
        prev = jnp.concatenate([sprev_ref[...], jnp.zeros((B_SHIFT_PAD - B_SHIFT, nb), F32)], axis=0).T
        xs = pb + (prev - pb) * mu_ref[...]
        r, k, v, ld, a, kkn, kmod = _rwkv_pre(xs, vec, wlora_ref[...], ones_bd)
        rw_s[T_DECAY] = jnp.exp(ld).T
        rw_s[T_NKK] = (-kkn).T
        rw_s[T_KKA] = (kkn * a).T
        rw_s[T_KMOD] = kmod.T
        rw_s[T_R] = r.T
        rw_s[T_V] = v.T
        rkv_s[0] = r
        rkv_s[1] = kmod
        rkv_s[2] = v

    rs = pl.ds(pl.multiple_of(j * bb, SUBLANES), bb)
    lane = _iota((bb, LANES), 1)
    lo = lane < HEAD_DIM
    for c in range(A_HEADS // 2):
        g = c // 2
        in_g = lo if g == 0 else jnp.logical_not(lo)
        qcol = _rope128(proj_s[rs, OFF_Q + LANES * c:OFF_Q + LANES * (c + 1)], cos, sin) * (HEAD_DIM ** -0.5)
        qrol = pltpu.roll(qcol, HEAD_DIM, 1)
        for hh in range(2):
            hd = 2 * c + hh
            q_s[hd * bb:(hd + 1) * bb, :] = jnp.where(in_g, qcol if hh == g else qrol, 0.0)
    knew_blk = knew_s[rs, :]
    vnew_blk = proj_s[rs, OFF_V:OFF_V + KV_WIDTH]
    hrow = _iota((A_HEADS, 1), 0)
    sink_col = jnp.zeros((A_HEADS, 1), F32)
    for hd in range(A_HEADS):
        sink_col = jnp.where(hrow == hd, sinks_ref[layer, hd], sink_col)

    qbs = [q_s[pl.ds(b, A_HEADS, stride=bb), :] for b in range(bb)]
    s_c = [_dot(_bf(qbs[b]), _bf(kt_ref[b * KV_WIDTH:(b + 1) * KV_WIDTH, :])) for b in range(bb)]
    s_n = [jnp.sum(qbs[b] * knew_blk[b:b + 1, :], axis=-1, keepdims=True) for b in range(bb)]
    p_c, p_n = [], []
    for b in range(bb):
        m = jnp.maximum(jnp.maximum(jnp.max(s_c[b], axis=-1, keepdims=True), s_n[b]), sink_col)
        e_c = jnp.exp(s_c[b] - m)
        e_n = jnp.exp(s_n[b] - m)
        inv = 1.0 / (jnp.sum(e_c, axis=-1, keepdims=True) + e_n + jnp.exp(sink_col - m))
        p_c.append(_bf(e_c * inv))
        p_n.append(e_n * inv)
    for b in range(bb):
        o = _dot_nt(p_c[b], _bf(vt_ref[b * KV_WIDTH:(b + 1) * KV_WIDTH, :]))
        o_s[pl.ds(b, A_HEADS, stride=bb), :] = o + p_n[b] * vnew_blk[b:b + 1, :]
    for c in range(A_HEADS // 2):
        g = c // 2
        halves = []
        for hh in range(2):
            oh = o_s[(2 * c + hh) * bb:(2 * c + hh + 1) * bb, :]
            halves.append(oh if hh == g else pltpu.roll(oh, HEAD_DIM, 1))
        ga = proj_s[rs, OFF_GA + LANES * c:OFF_GA + LANES * (c + 1)]
        mix_s[rs, LANES * c:LANES * (c + 1)] = jnp.where(lo, halves[0], halves[1]) * _silu(ga)

    hrows = pl.ds(pl.multiple_of(j * HEAD_DIM, HEAD_DIM), HEAD_DIM)
    w_t = rw_s[T_DECAY, hrows, :]
    nkk_t = rw_s[T_NKK, hrows, :]
    kka_t = rw_s[T_KKA, hrows, :]
    k_t = rw_s[T_KMOD, hrows, :]
    r_t = rw_s[T_R, hrows, :]

    def value_row(vi, carry):
        krows = pl.ds(pl.multiple_of(vi * HEAD_DIM, HEAD_DIM), HEAD_DIM)
        st = wkv_ref[krows, :]
        sa = jnp.sum(st * nkk_t, axis=0, keepdims=True)
        st_new = st * w_t + sa * kka_t + rw_s[T_V, pl.ds(j * HEAD_DIM + vi, 1), :] * k_t
        wkvo_ref[krows, :] = st_new
        yt_s[pl.ds(j * HEAD_DIM + vi, 1), :] = jnp.sum(st_new * r_t, axis=0, keepdims=True)
        return carry

    lax.fori_loop(0, HEAD_DIM, value_row, 0, unroll=4)

    @pl.when(j == pl.num_programs(1) - 1)
    def _():
        ones_bd = _head_ones(B_WIDTH)
        x = ys_s[...]
        gb = proj_s[:, OFF_GB:OFF_GB + B_WIDTH]
        mix_s[:, A_WIDTH:A_WIDTH + B_WIDTH] = _rwkv_post(yt_s[...].T, rkv_s[0], rkv_s[1], rkv_s[2], gb, vec, ones_bd)
        xc = proj_s[:, OFF_XC:OFF_XC + C_WIDTH]
        gc = proj_s[:, OFF_GC:OFF_GC + C_WIDTH]
        u = vec[V_CB:V_CB + 1, :] + xc * vec[V_CW3:V_CW3 + 1, :]
        for i in range(CONV_W - 1):
            u = u + conv_ref[i] * vec[V_CW0 + i:V_CW0 + i + 1, :]
        a_l, mult, gi = _lru_gates(u, vec, wgate_ref[...])
        h_l = a_l * hl_ref[...] + mult * gi * u
        hlo_ref[...] = h_l
        convo_ref[0] = conv_ref[1]
        convo_ref[1] = conv_ref[2]
        convo_ref[2] = xc
        mix_s[:, A_WIDTH + B_WIDTH:] = h_l * _silu(gc)
        yo = _dot(_bf(mix_s[...]), wout_ref[...])
        y = x + mod_ref[2] * _rmsnorm(yo, npost_ref[...])
        ys_s[...] = y
        y_ref[...] = y


def _sample_layers(x, mod, wts, cos_r, sin_r, kt, vt, sprev_t, wkv_t, conv_t, hl):
    nb = x.shape[0]
    bb = nb // B_HEADS
    n_slot = kt.shape[2]
    n_hv = HEAD_DIM * HEAD_DIM
    w_head, w_tail = _weight_specs(lambda l, j: l, 2)
    const2 = lambda l, j: (0, 0)
    lay3 = lambda l, j: (l, 0, 0)
    return pl.pallas_call(
        _sample_kernel,
        grid=(DEPTH, B_HEADS),
        in_specs=[
            pl.BlockSpec(memory_space=pltpu.SMEM),
            pl.BlockSpec((nb, D_MODEL), const2),
            pl.BlockSpec((None, 3, nb, D_MODEL), lambda l, j: (l, 0, 0, 0)),
            *w_head,
            pl.BlockSpec((1, LANES), const2),
            pl.BlockSpec((1, LANES), const2),
            *w_tail,
            pl.BlockSpec((None, bb * KV_WIDTH, n_slot), lambda l, j: (l, j, 0)),
            pl.BlockSpec((None, bb * KV_WIDTH, n_slot), lambda l, j: (l, j, 0)),
            pl.BlockSpec((None, B_SHIFT, nb), lay3),
            pl.BlockSpec((None, n_hv, nb), lambda l, j: (l, j, 0)),
            pl.BlockSpec((None, CONV_W - 1, nb, C_WIDTH), lambda l, j: (l, 0, 0, 0)),
            pl.BlockSpec((None, nb, C_WIDTH), lay3),
        ],
        out_specs=[
            pl.BlockSpec((nb, D_MODEL), const2),
            pl.BlockSpec((None, KV_WIDTH, nb), lay3),
            pl.BlockSpec((None, KV_WIDTH, nb), lay3),
            pl.BlockSpec((None, B_SHIFT, nb), lay3),
            pl.BlockSpec((None, n_hv, nb), lambda l, j: (l, j, 0)),
            pl.BlockSpec((None, CONV_W - 1, nb, C_WIDTH), lambda l, j: (l, 0, 0, 0)),
            pl.BlockSpec((None, nb, C_WIDTH), lay3),
        ],
        out_shape=[
            jax.ShapeDtypeStruct((nb, D_MODEL), F32),
            jax.ShapeDtypeStruct((DEPTH, KV_WIDTH, nb), F32),
            jax.ShapeDtypeStruct((DEPTH, KV_WIDTH, nb), F32),
            jax.ShapeDtypeStruct((DEPTH, B_SHIFT, nb), F32),
            jax.ShapeDtypeStruct((DEPTH, B_HEADS * n_hv, nb), F32),
            jax.ShapeDtypeStruct((DEPTH, CONV_W - 1, nb, C_WIDTH), F32),
            jax.ShapeDtypeStruct((DEPTH, nb, C_WIDTH), F32),
        ],
        scratch_shapes=[
            pltpu.VMEM((nb, D_MODEL), F32),
            pltpu.VMEM((nb, P_PAD), F32),
            pltpu.VMEM((nb, D_MODEL), F32),
            pltpu.VMEM((A_HEADS * bb, LANES), F32),
            pltpu.VMEM((A_HEADS * bb, LANES), F32),
            pltpu.VMEM((nb, KV_WIDTH), F32),
            pltpu.VMEM((6, B_WIDTH, nb), F32),
            pltpu.VMEM((3, nb, B_WIDTH), F32),
            pltpu.VMEM((B_WIDTH, nb), F32),
        ],
        compiler_params=pltpu.CompilerParams(
            dimension_semantics=("arbitrary", "arbitrary"), vmem_limit_bytes=VMEM_LIMIT_BYTES),
        name="sample_layers",
    )(wts["sinks"], x, mod, wts["norm_pre"], wts["norm_post"], wts["w_in"], wts["w_out"], cos_r, sin_r,
      wts["mu"], wts["vec"], wts["wlora"], wts["wgate"], kt, vt, sprev_t, wkv_t, conv_t, hl)


def _rope_tables(pos):
    half = HEAD_DIM // 2
    inv_freq = ROPE_THETA ** (-jnp.arange(half, dtype=F32) / half)
    ang = pos.astype(F32)[:, None] * inv_freq[None, :]
    cos = jnp.cos(ang)
    sin = jnp.sin(ang)
    cos_t = jnp.concatenate([cos, cos, cos, cos], axis=-1)
    sin_t = jnp.concatenate([-sin, sin, -sin, sin], axis=-1)
    return cos_t, sin_t


def _block_diag(w):
    dl, n, d, e = w.shape
    eye = jnp.eye(n, dtype=w.dtype)
    return (eye[None, :, None, :, None] * w[:, :, :, None, :]).reshape(dl, n * d, n * e)


def _prep_weights(p):
    dl = p["w_in"].shape[0]
    w_in_t = jnp.swapaxes(p["w_in"], 1, 2)
    w_in_t = jnp.concatenate(
        [w_in_t[:, :P_ORIG_PB_END], jnp.zeros((dl, B_SHIFT_PAD - B_SHIFT, D_MODEL), F32),
         w_in_t[:, P_ORIG_PB_END:]], axis=1).astype(BF16)
    z32 = jnp.zeros((dl, LORA, B_WIDTH), F32)
    wlora = jnp.concatenate([
        jnp.concatenate([p["rwkv_w_up"], z32], axis=2),
        jnp.concatenate([z32, p["rwkv_a_up"]], axis=2),
        jnp.zeros((dl, LANES - 2 * LORA, 2 * B_WIDTH), F32)], axis=1).astype(BF16)
    wgate = jnp.concatenate([_block_diag(p["lru_gate_a_w"]), _block_diag(p["lru_gate_x_w"])], axis=2).astype(BF16)
    cw = p["lru_conv_w"]
    rows = [p["rwkv_w0"], p["rwkv_a0"], p["rwkv_k_k"], p["rwkv_k_a"], p["rwkv_lnx_w"], p["rwkv_lnx_b"],
            p["rwkv_r_k"].reshape(dl, B_WIDTH), p["lru_conv_b"], p["lru_gate_a_b"], p["lru_gate_x_b"],
            p["lru_lambda"], cw[:, 0], cw[:, 1], cw[:, 2], cw[:, 3], jnp.zeros((dl, B_WIDTH), F32)]
    return {
        "sinks": p["attn_sinks"],
        "norm_pre": p["norm_pre"].reshape(dl, 1, D_MODEL),
        "norm_post": p["norm_post"].reshape(dl, 1, D_MODEL),
        "w_in": w_in_t,
        "w_out": p["w_out"].astype(BF16),
        "mu": jnp.concatenate([p["rwkv_mu"], jnp.zeros((dl, B_SHIFT_PAD - B_SHIFT), F32)],
                              axis=1).reshape(dl, 1, B_SHIFT_PAD),
        "vec": jnp.stack(rows, axis=1),
        "wlora": wlora,
        "wgate": wgate,
    }


def _forward(x_prompt, x_sample, c_prompt, c_sample, cache_swa_k, cache_swa_v, state_rwkv_shift,
             state_rwkv_wkv, state_lru_conv, state_lru_h, p, tb):
    bp, seq, _ = x_prompt.shape
    nb = x_sample.shape[0]
    n_slot = cache_swa_k.shape[2]
    assert nb % SUBLANES == 0 and nb % B_HEADS == 0 and bp <= SUBLANES and seq % tb == 0

    pad_rows = (-(nb + bp)) % SUBLANES
    c_all = jnp.concatenate([c_sample, c_prompt, jnp.zeros((pad_rows, D_MODEL), F32)], axis=0)
    mod = _mod_call(c_all, p["w_mod"], p["b_mod"])
    wts = _prep_weights(p)

    cos_p, sin_p = _rope_tables(jnp.arange(seq))
    cos_s, sin_s = _rope_tables(jnp.full((1,), PAST_LEN))

    kt = jnp.transpose(cache_swa_k, (0, 1, 3, 4, 2)).reshape(DEPTH, nb * KV_WIDTH, n_slot)
    vt = jnp.transpose(cache_swa_v, (0, 1, 3, 4, 2)).reshape(DEPTH, nb * KV_WIDTH, n_slot)
    sprev_t = jnp.swapaxes(state_rwkv_shift, 1, 2)
    wkv_t = jnp.transpose(state_rwkv_wkv, (0, 2, 3, 4, 1)).reshape(DEPTH, B_HEADS * HEAD_DIM * HEAD_DIM, nb)
    conv_t = jnp.swapaxes(state_lru_conv, 1, 2)
    ys, knew_t, vnew_t, shift_t, wkvo_t, convo_t, hlo = _sample_layers(
        x_sample.reshape(nb, D_MODEL), mod, wts, cos_s, sin_s, kt, vt, sprev_t, wkv_t, conv_t, state_lru_h)
    outs_s = (
        jnp.transpose(knew_t.reshape(DEPTH, 1, A_KV_HEADS, HEAD_DIM, nb), (0, 4, 1, 2, 3)),
        jnp.transpose(vnew_t.reshape(DEPTH, 1, A_KV_HEADS, HEAD_DIM, nb), (0, 4, 1, 2, 3)),
        jnp.swapaxes(shift_t, 1, 2),
        jnp.transpose(wkvo_t.reshape(DEPTH, B_HEADS, HEAD_DIM, HEAD_DIM, nb), (0, 4, 1, 2, 3)),
        jnp.swapaxes(convo_t, 1, 2),
        hlo,
    )

    yp = x_prompt
    outs_p = []
    for l in range(DEPTH):
        yp, kv_p, sh_p, wkv_p, conv_p, h_p = _prompt_layer(l, yp, mod, wts, cos_p, sin_p, nb, tb)
        kv_p = jnp.transpose(kv_p.reshape(bp, 2, A_KV_HEADS, HEAD_DIM, WINDOW), (1, 0, 4, 2, 3))
        outs_p.append((
            kv_p[0], kv_p[1],
            sh_p[:, SUBLANES - 1, :B_SHIFT],
            jnp.stack([wkv_p[:, HEAD_DIM * hd:HEAD_DIM * (hd + 1), HEAD_DIM * hd:HEAD_DIM * (hd + 1)]
                       for hd in range(B_HEADS)], axis=1),
            conv_p[:, SUBLANES - (CONV_W - 1):, :],
            h_p[:, SUBLANES - 1, :],
        ))
    sp = [jnp.stack(z) for z in zip(*outs_p)]
    return (yp, ys.reshape(nb, 1, D_MODEL), *sp, *outs_s)


def kernel(x_prompt, x_sample, c_prompt, c_sample, cache_swa_k, cache_swa_v, state_rwkv_shift, state_rwkv_wkv, state_lru_conv, state_lru_h, norm_pre, norm_post, w_mod, b_mod, w_in, w_out, attn_sinks, rwkv_mu, rwkv_w0, rwkv_w_up, rwkv_a0, rwkv_a_up, rwkv_k_k, rwkv_k_a, rwkv_r_k, rwkv_lnx_w, rwkv_lnx_b, lru_conv_w, lru_conv_b, lru_gate_a_w, lru_gate_a_b, lru_gate_x_w, lru_gate_x_b, lru_lambda):
    p = dict(norm_pre=norm_pre, norm_post=norm_post, w_mod=w_mod, b_mod=b_mod, w_in=w_in, w_out=w_out,
             attn_sinks=attn_sinks, rwkv_mu=rwkv_mu, rwkv_w0=rwkv_w0, rwkv_w_up=rwkv_w_up, rwkv_a0=rwkv_a0,
             rwkv_a_up=rwkv_a_up, rwkv_k_k=rwkv_k_k, rwkv_k_a=rwkv_k_a, rwkv_r_k=rwkv_r_k,
             rwkv_lnx_w=rwkv_lnx_w, rwkv_lnx_b=rwkv_lnx_b, lru_conv_w=lru_conv_w, lru_conv_b=lru_conv_b,
             lru_gate_a_w=lru_gate_a_w, lru_gate_a_b=lru_gate_a_b, lru_gate_x_w=lru_gate_x_w,
             lru_gate_x_b=lru_gate_x_b, lru_lambda=lru_lambda)
    return _forward(x_prompt, x_sample, c_prompt, c_sample, cache_swa_k, cache_swa_v, state_rwkv_shift,
                    state_rwkv_wkv, state_lru_conv, state_lru_h, p, TIME_BLOCK)
```

```python
import functools

import numpy as np
import jax
import jax.numpy as jnp
from jax import lax
from jax.experimental import pallas as pl
from jax.experimental.pallas import tpu as pltpu

F32 = jnp.float32
BF16 = jnp.bfloat16

D_MODEL = 1024
DEPTH = 2
A_HEADS = 8
A_KV_HEADS = 2
HEAD_DIM = 64
A_WIDTH = A_HEADS * HEAD_DIM
KV_WIDTH = A_KV_HEADS * HEAD_DIM
WINDOW = 128
Q_BLOCK = 128
ROPE_THETA = 10000.0
B_WIDTH = 256
B_HEADS = 4
LORA = 32
B_SHIFT = 3 * B_WIDTH + 2 * LORA
B_SHIFT_PAD = 3 * B_WIDTH + 128
LNX_EPS = 1e-5 * 8 ** 2
C_WIDTH = 256
CONV_W = 4
LRU_C = 8.0
EPS = 1e-6
PAST_LEN = 8192
EXP_NEG_HALF = float(np.exp(-0.5))
LOG2_E = float(np.log2(np.e))
Q_SCALE = HEAD_DIM ** -0.5 * LOG2_E

LANES = 128
SUBLANES = 8
VMEM_LIMIT_BYTES = 56 * 1024 * 1024

OFF_Q = 0
OFF_K = OFF_Q + A_WIDTH
OFF_V = OFF_K + KV_WIDTH
OFF_GA = OFF_V + KV_WIDTH
OFF_PB = OFF_GA + A_WIDTH
OFF_GB = OFF_PB + B_SHIFT_PAD
OFF_XC = OFF_GB + B_WIDTH
OFF_GC = OFF_XC + C_WIDTH
P_PAD = OFF_GC + C_WIDTH
P_IN = P_PAD - (B_SHIFT_PAD - B_SHIFT)

CHUNK = 64
TIME_BLOCK = 256
PROJ_TILE = 256
PHASE_LAG = 12

(V_W0, V_A0, V_KK, V_KA, V_LNW, V_LNB, V_RK, V_CB, V_GAB, V_GXB, V_LAM,
 V_CW0, V_CW1, V_CW2, V_CW3) = range(15)
VEC_ROWS = 16

(T_DECAY, T_NKK, T_KKA, T_KMOD, T_R, T_V) = range(6)

_NT = (((1,), (1,)), ((), ()))
_TN = (((0,), (0,)), ((), ()))


def _bf(x):
    return x.astype(BF16)


def _dot(a, b):
    return jnp.dot(a, b, preferred_element_type=F32)


def _dot_b(a, b):
    return jnp.dot(a, b, preferred_element_type=F32).astype(BF16)


def _dot_nt(a, b):
    return lax.dot_general(a, b, _NT, preferred_element_type=F32)


def _dot_tn(a, b):
    return lax.dot_general(a, b, _TN, preferred_element_type=F32)


def _iota(shape, dim):
    return lax.broadcasted_iota(jnp.int32, shape, dim)


def _sigmoid(x):
    return jax.nn.sigmoid(x)


def _silu(x):
    return x * _sigmoid(x)


def _softplus(x):
    return jnp.maximum(x, 0.0) + jnp.log1p(jnp.exp(-jnp.abs(x)))


def _head_ones(n):
    r = _iota((n, n), 0) // HEAD_DIM
    c = _iota((n, n), 1) // HEAD_DIM
    return jnp.where(r == c, 1.0, 0.0).astype(BF16)


def _head_sum(x, ones_bd):
    return _dot(_bf(x), ones_bd)


def _rms_scale(x):
    ms = jnp.mean(x * x, axis=-1, keepdims=True)
    return x * lax.rsqrt(ms + EPS)


def _affine_scan(a, b, h0):
    rows, n = a.shape
    groups = rows // SUBLANES
    sub = _iota((rows, n), 0) % SUBLANES
    d = 1
    while d < SUBLANES:
        keep = sub >= d
        a_sh = jnp.where(keep, pltpu.roll(a, d, 0), 1.0)
        b_sh = jnp.where(keep, pltpu.roll(b, d, 0), 0.0)
        b = b + a * b_sh
        a = a * a_sh
        d *= 2
    outs = []
    carry = h0
    for g in range(groups):
        rows_g = slice(g * SUBLANES, (g + 1) * SUBLANES)
        hg = b[rows_g] + a[rows_g] * carry
        outs.append(hg)
        carry = hg[SUBLANES - 1:SUBLANES, :]
    return jnp.concatenate(outs, axis=0)


def _cumsum_chunks(x, chunk):
    rows, n = x.shape
    sub = _iota((rows, n), 0) % SUBLANES
    d = 1
    while d < SUBLANES:
        x = x + jnp.where(sub >= d, pltpu.roll(x, d, 0), 0.0)
        d *= 2
    outs = []
    for g in range(rows // SUBLANES):
        xg = x[g * SUBLANES:(g + 1) * SUBLANES]
        if (g * SUBLANES) % chunk != 0:
            xg = xg + outs[-1][SUBLANES - 1:SUBLANES, :]
        outs.append(xg)
    return jnp.concatenate(outs, axis=0)


def _rope128(z, cos, sin_signed):
    lane = _iota(z.shape, 1)
    first = (lane & 32) == 0
    sw = jnp.where(first, pltpu.roll(z, 96, 1), pltpu.roll(z, 32, 1))
    return z * cos + sw * sin_signed


def _rwkv_pre(xs, vec, wlora, ones_bd):
    r = xs[:, 0:B_WIDTH]
    k = xs[:, B_WIDTH:2 * B_WIDTH]
    v = xs[:, 2 * B_WIDTH:3 * B_WIDTH]
    lor = xs[:, 3 * B_WIDTH:B_SHIFT_PAD]
    lane = _iota(lor.shape, 1)
    z = jnp.where(lane < LORA, jnp.tanh(lor), lor)
    wa = _dot(_bf(z), wlora)
    zw = vec[V_W0:V_W0 + 1, :] + wa[:, :B_WIDTH]
    ld = -EXP_NEG_HALF * _sigmoid(zw)
    a = _sigmoid(vec[V_A0:V_A0 + 1, :] + wa[:, B_WIDTH:])
    kk = k * vec[V_KK:V_KK + 1, :]
    kkn = kk * jnp.minimum(lax.rsqrt(_head_sum(kk * kk, ones_bd)), 1e12)
    kmod = k * (1.0 + (a - 1.0) * vec[V_KA:V_KA + 1, :])
    return r, k, v, ld, a, kkn, kmod


def _rwkv_post(y, r, kmod, v, gb, vec, ones_bd):
    inv = 1.0 / HEAD_DIM
    mu = _head_sum(y, ones_bd) * inv
    yc = y - mu
    var = _head_sum(yc * yc, ones_bd) * inv
    yn = yc * lax.rsqrt(var + LNX_EPS)
    yn = yn * vec[V_LNW:V_LNW + 1, :] + vec[V_LNB:V_LNB + 1, :]
    bonus = _head_sum(r * kmod * vec[V_RK:V_RK + 1, :], ones_bd) * v
    return (yn + bonus) * _silu(gb)


def _lru_gates(u, vec, wgate):
    gates = _dot(_bf(u), wgate)
    gr = _sigmoid(gates[:, :C_WIDTH] + vec[V_GAB:V_GAB + 1, :])
    gi = _sigmoid(gates[:, C_WIDTH:] + vec[V_GXB:V_GXB + 1, :])
    sp = _softplus(-vec[V_LAM:V_LAM + 1, :])
    log_a = -LRU_C * gr * sp
    a = jnp.exp(log_a)
    mult = jnp.sqrt(1.0 - a * a)
    return a, mult, gi


def _project(h, win_ref, proj_s, group):
    src, dst, n = group
    proj_s[:, dst:dst + n] = _dot_nt(h, win_ref[src:src + n, :])


_DELTA = B_SHIFT_PAD - B_SHIFT
GRP_PB = (OFF_PB, OFF_PB, B_SHIFT_PAD)
GRP_QKV = (OFF_Q, OFF_Q, OFF_GA - OFF_Q)
GRP_GA = (OFF_GA, OFF_GA, A_WIDTH)
GRP_GB = (OFF_GB - _DELTA, OFF_GB, B_WIDTH)
GRP_XC = (OFF_XC - _DELTA, OFF_XC, C_WIDTH)
GRP_GC = (OFF_GC - _DELTA, OFF_GC, C_WIDTH)


def _mod_kernel(c_ref, w_ref, b_ref, o_ref):
    o_ref[...] = _dot(_bf(_silu(c_ref[...])), _bf(w_ref[...])) + b_ref[...]


def _mod_call(c_all, w_mod, b_mod):
    rows = c_all.shape[0]
    return pl.pallas_call(
        _mod_kernel,
        grid=(DEPTH, 3),
        in_specs=[
            pl.BlockSpec((rows, D_MODEL), lambda l, j: (0, 0)),
            pl.BlockSpec((None, D_MODEL, D_MODEL), lambda l, j: (l, 0, j)),
            pl.BlockSpec((None, 1, D_MODEL), lambda l, j: (l, 0, j)),
        ],
        out_specs=pl.BlockSpec((None, None, rows, D_MODEL), lambda l, j: (l, j, 0, 0)),
        out_shape=jax.ShapeDtypeStruct((DEPTH, 3, rows, D_MODEL), F32),
        compiler_params=pltpu.CompilerParams(
            dimension_semantics=("arbitrary", "arbitrary"), vmem_limit_bytes=VMEM_LIMIT_BYTES),
        name="adaln_mod",
    )(c_all, w_mod, b_mod.reshape(DEPTH, 1, 3 * D_MODEL))


def _prompt_kernel(sinks_ref, x_ref, mod_ref, npre_ref, npost_ref, win_ref, wout_ref,
                   cos_ref, sin_ref, mu_ref, vec_ref, wlora_ref, wgate_ref,
                   y_ref, kv_ref, shift_ref, wkv_ref, conv_ref, hlru_ref,
                   *scratch, layer):
    n_seq = x_ref.shape[0]
    tb = x_ref.shape[1]
    t = pl.program_id(0)
    per_seq = len(scratch) // n_seq
    seq_scratch = [scratch[q * per_seq:(q + 1) * per_seq] for q in range(n_seq)]

    @pl.when(t == 0)
    def _():
        for kvprev, pprev, xcprev, hprev, gstate, _, _, _ in seq_scratch:
            kvprev[...] = jnp.zeros_like(kvprev)
            pprev[...] = jnp.zeros_like(pprev)
            xcprev[...] = jnp.zeros_like(xcprev)
            hprev[...] = jnp.zeros_like(hprev)
            gstate[...] = jnp.zeros_like(gstate)

    vec = vec_ref[...]
    ones_bd = _head_ones(B_WIDTH)
    cos = cos_ref[...]
    sin = sin_ref[...]
    nck = tb // CHUNK
    sls = [slice(c * CHUNK, (c + 1) * CHUNK) for c in range(nck)]
    rng = range(nck)
    pairs = [(c, i) for c in range(A_HEADS // 2) for i in range(tb // Q_BLOCK)]

    lo_kv = _iota((tb + WINDOW, KV_WIDTH), 1) < HEAD_DIM
    qi = _iota((Q_BLOCK, 2 * Q_BLOCK), 0)
    sj = _iota((Q_BLOCK, 2 * Q_BLOCK), 1)
    band = (sj >= qi) & (sj <= qi + WINDOW)
    first_lo = jnp.where(t == 0, Q_BLOCK, 0)
    neg_inf = -jnp.inf
    row_p = _iota((tb, B_SHIFT_PAD), 0)
    reset = (_iota((tb, C_WIDTH), 0) + t * tb) == 0

    n4 = B_HEADS * CHUNK
    r4 = _iota((n4, n4), 0)
    c4 = _iota((n4, n4), 1)
    same4 = (r4 // CHUNK) == (c4 // CHUNK)
    strict = same4 & ((r4 % CHUNK) > (c4 % CHUNK))
    incl = same4 & ((r4 % CHUNK) >= (c4 % CHUNK))
    eye4 = jnp.where(r4 == c4, 1.0, 0.0).astype(BF16)
    bd_heads = (_iota((B_WIDTH, B_WIDTH), 0) // HEAD_DIM) == (_iota((B_WIDTH, B_WIDTH), 1) // HEAD_DIM)

    def variants(z):
        zr = pltpu.roll(z, HEAD_DIM, 1)
        a_ = _bf(jnp.where(lo_kv, z, 0.0))
        b_ = _bf(jnp.where(lo_kv, 0.0, z))
        c_ = _bf(jnp.where(lo_kv, zr, 0.0))
        d_ = _bf(jnp.where(lo_kv, 0.0, zr))
        return ((a_, d_), (c_, b_))

    def win_stack(var, c, i):
        g, r0 = c // 2, i * Q_BLOCK
        return jnp.concatenate([var[g][0][r0:r0 + 2 * Q_BLOCK], var[g][1][r0:r0 + 2 * Q_BLOCK]], axis=0)

    def stk(z):
        zb = _bf(z)
        return jnp.concatenate([zb] * B_HEADS, axis=0) * ones_bd

    def collapse(z):
        return z[0:CHUNK] + z[CHUNK:2 * CHUNK] + z[2 * CHUNK:3 * CHUNK] + z[3 * CHUNK:4 * CHUNK]

    def sequence(q):
        kvprev, pprev, xcprev, hprev, gstate, proj, mix, ywkv = seq_scratch[q]

        x = x_ref[q]
        shift_m = mod_ref[0, q:q + 1, :]
        scale_m = mod_ref[1, q:q + 1, :]
        gate_m = mod_ref[2, q:q + 1, :]
        h = _bf(_rms_scale(x) * (npre_ref[...] * (1.0 + scale_m)) + shift_m)
        for src, dst, n in (GRP_PB, GRP_QKV, GRP_XC, GRP_GA, GRP_GB, GRP_GC):
            for off in range(0, n, PROJ_TILE):
                _project(h, win_ref, proj, (src + off, dst + off, min(PROJ_TILE, n - off)))
                yield

        pb = proj[:, OFF_PB:OFF_PB + B_SHIFT_PAD]
        prev = jnp.where(row_p == 0, pprev[SUBLANES - 1:SUBLANES, :], pltpu.roll(pb, 1, 0))
        xs = pb + (prev - pb) * mu_ref[...]
        yield
        r, k, v, ld, a, kkn, kmod = _rwkv_pre(xs, vec, wlora_ref[...], ones_bd)
        kka = kkn * a
        yield

        k_rot = _rope128(proj[:, OFF_K:OFF_K + KV_WIDTH], cos, sin)
        v_att = proj[:, OFF_V:OFF_V + KV_WIDTH]
        kvar = variants(jnp.concatenate([kvprev[:, 0:KV_WIDTH], k_rot], axis=0))
        vvar = variants(jnp.concatenate([kvprev[:, KV_WIDTH:2 * KV_WIDTH], v_att], axis=0))
        k_tail = k_rot[tb - WINDOW:]
        v_tail = v_att[tb - WINDOW:]
        yield
        qcols = [_bf(_rope128(proj[:, OFF_Q + LANES * c:OFF_Q + LANES * (c + 1)], cos, sin))
                 for c in range(A_HEADS // 2)]
        scores = [_dot_nt(qcols[c][i * Q_BLOCK:(i + 1) * Q_BLOCK], win_stack(kvar, c, i))
                  for c, i in pairs]
        yield

        xc = proj[:, OFF_XC:OFF_XC + C_WIDTH]
        xext = jnp.concatenate([xcprev[...], xc], axis=0)
        u = vec[V_CB:V_CB + 1, :] + xc * vec[V_CW3:V_CW3 + 1, :]
        for j in range(1, CONV_W):
            u = u + pltpu.roll(xext, j, 0)[SUBLANES:, :] * vec[V_CW3 - j:V_CW3 - j + 1, :]
        a_l, mult, gi = _lru_gates(u, vec, wgate_ref[...])
        a_l = jnp.where(reset, 0.0, a_l)
        mult = jnp.where(reset, 1.0, mult)
        b_l = mult * gi * u
        yield
        h_l = _affine_scan(a_l, b_l, hprev[SUBLANES - 1:SUBLANES, :])
        yield

        lcum = _cumsum_chunks(ld, CHUNK)
        e_l = jnp.exp(lcum)
        e_nl = jnp.exp(-lcum)
        rt = r * e_l
        at = -kkn * jnp.exp(lcum - ld)
        ktil = kmod * e_nl
        btil = kka * e_nl
        gam = [e_l[s.stop - 1:s.stop, :] for s in sls]
        khat = [ktil[sls[c]] * gam[c] for c in rng]
        bhat = [btil[sls[c]] * gam[c] for c in rng]
        yield
        a_s = [stk(at[s]) for s in sls]
        r_s = [stk(rt[s]) for s in sls]
        b_s = [stk(btil[s]) for s in sls]
        k_s = [stk(ktil[s]) for s in sls]
        yield
        v_s = [stk(v[s]) for s in sls]
        bh_s = [stk(bhat[c]) for c in rng]
        sc = [_dot_nt(jnp.concatenate([a_s[c], r_s[c]], axis=0), jnp.concatenate([b_s[c], k_s[c]], axis=0))
              for c in rng]
        yield

        probs = []
        for (c, i), s in zip(pairs, scores):
            mask = band & (sj >= first_lo) if i == 0 else band
            ps = []
            for hh in range(2):
                sink = sinks_ref[layer, 2 * c + hh] * LOG2_E
                sh = jnp.where(mask, s[:, 2 * Q_BLOCK * hh:2 * Q_BLOCK * (hh + 1)], neg_inf)
                m = jnp.maximum(jnp.max(sh, axis=-1, keepdims=True), sink)
                p = jnp.exp2(sh - m)
                den = jnp.sum(p, axis=-1, keepdims=True) + jnp.exp2(sink - m)
                ps.append(_bf(p * (1.0 / den)))
            probs.append(jnp.concatenate(ps, axis=1))
            if i == tb // Q_BLOCK - 1 and c % 2 == 1:
                yield
        for (c, i), p in zip(pairs, probs):
            r0 = i * Q_BLOCK
            o = _dot(p, win_stack(vvar, c, i))
            ga = proj[r0:r0 + Q_BLOCK, OFF_GA + LANES * c:OFF_GA + LANES * (c + 1)]
            mix[r0:r0 + Q_BLOCK, LANES * c:LANES * (c + 1)] = o * _silu(ga)
        yield

        lab = [_bf(jnp.where(strict, sc[c][0:n4, 0:n4], 0.0)) for c in rng]
        lak = [_bf(jnp.where(strict, sc[c][0:n4, n4:], 0.0)) for c in rng]
        mrb = [_bf(jnp.where(incl, sc[c][n4:, 0:n4], 0.0)) for c in rng]
        mrk = [_bf(jnp.where(incl, sc[c][n4:, n4:], 0.0)) for c in rng]
        pw = lab
        tinv = [pw[c] + eye4 for c in rng]
        pw = [_dot_b(pw[c], pw[c]) for c in rng]
        yield
        for _ in range(4):
            tinv = [_dot_b(tinv[c], pw[c] + eye4) for c in rng]
            pw = [_dot_b(pw[c], pw[c]) for c in rng]
            yield
        tinv = [_dot_b(tinv[c], pw[c] + eye4) for c in rng]
        lv = [_dot_b(lak[c], v_s[c]) for c in rng]
        yield
        w_s = [_dot_b(tinv[c], a_s[c]) for c in rng]
        z_s = [_dot(tinv[c], lv[c]) for c in rng]
        yield
        rp = [_bf(r_s[c].astype(F32) + _dot(mrb[c], w_s[c])) for c in rng]
        y0 = [_dot(jnp.concatenate([mrb[c], mrk[c]], axis=1), jnp.concatenate([_bf(z_s[c]), v_s[c]], axis=0))
              for c in rng]
        pp = [_bf(_dot_tn(w_s[c], bh_s[c])) for c in rng]
        qq = [jnp.where(bd_heads,
                        _dot_tn(_bf(jnp.concatenate([collapse(z_s[c]), v[sls[c]]], axis=0)),
                                _bf(jnp.concatenate([bhat[c], khat[c]], axis=0))), 0.0) for c in rng]
        yield
        gmat = gstate[...]
        for c in rng:
            gb16 = _bf(gmat)
            ywkv[sls[c], :] = collapse(_dot_nt(rp[c], gb16) + y0[c])
            gmat = gam[c] * gmat + _dot(gb16, pp[c]) + qq[c]
        gstate[...] = gmat
        yield

        gb = proj[:, OFF_GB:OFF_GB + B_WIDTH]
        mix[:, A_WIDTH:A_WIDTH + B_WIDTH] = _rwkv_post(ywkv[...], r, kmod, v, gb, vec, ones_bd)
        gc = proj[:, OFF_GC:OFF_GC + C_WIDTH]
        mix[:, A_WIDTH + B_WIDTH:] = h_l * _silu(gc)

        yo = _dot(_bf(mix[...]), wout_ref[...])
        y_ref[q] = x + (gate_m * npost_ref[...]) * _rms_scale(yo)

        kvprev[:, 0:KV_WIDTH] = k_tail
        kvprev[:, KV_WIDTH:2 * KV_WIDTH] = v_tail
        pprev[...] = pb[tb - SUBLANES:, :]
        xcprev[...] = xc[tb - SUBLANES:, :]
        hprev[...] = h_l[tb - SUBLANES:, :]
        shift_ref[q] = pb[tb - SUBLANES:, :]
        wkv_ref[q] = gmat
        conv_ref[q] = xc[tb - SUBLANES:, :]
        hlru_ref[q] = h_l[tb - SUBLANES:, :]

        @pl.when(t == pl.num_programs(0) - 1)
        def _():
            kv_ref[q, 0] = k_tail.T
            kv_ref[q, 1] = v_tail.T

    gens = [sequence(q) for q in range(n_seq)]
    done = [False] * n_seq
    tick = 0
    while not all(done):
        for q, g in enumerate(gens):
            if done[q] or tick < q * PHASE_LAG:
                continue
            try:
                next(g)
            except StopIteration:
                done[q] = True
        tick += 1


def _weight_specs(layer_of, grid_rank):
    def lay(*shape):
        zeros = (0,) * len(shape)
        if grid_rank == 1:
            return pl.BlockSpec((None, *shape), lambda i: (layer_of(i), *zeros))
        return pl.BlockSpec((None, *shape), lambda a, b: (layer_of(a, b), *zeros))
    return [
        lay(1, D_MODEL),
        lay(1, D_MODEL),
        lay(P_IN, D_MODEL),
        lay(D_MODEL, D_MODEL),
    ], [
        lay(1, B_SHIFT_PAD),
        lay(VEC_ROWS, B_WIDTH),
        lay(LANES, 2 * B_WIDTH),
        lay(C_WIDTH, 2 * C_WIDTH),
    ]


def _prompt_layer(layer, x, mod, wts, cos_t, sin_t, mod_row0, tb):
    bsz, seq, _ = x.shape
    nt = seq // tb
    whole3 = lambda t: (0, 0, 0)
    w_head, w_tail = _weight_specs(lambda t: layer, 1)
    return pl.pallas_call(
        functools.partial(_prompt_kernel, layer=layer),
        grid=(nt,),
        in_specs=[
            pl.BlockSpec(memory_space=pltpu.SMEM),
            pl.BlockSpec((bsz, tb, D_MODEL), lambda t: (0, t, 0)),
            pl.BlockSpec((None, 3, SUBLANES, D_MODEL),
                         lambda t: (layer, 0, mod_row0 // SUBLANES, 0)),
            *w_head,
            pl.BlockSpec((tb, LANES), lambda t: (t, 0)),
            pl.BlockSpec((tb, LANES), lambda t: (t, 0)),
            *w_tail,
        ],
        out_specs=[
            pl.BlockSpec((bsz, tb, D_MODEL), lambda t: (0, t, 0)),
            pl.BlockSpec((bsz, 2, KV_WIDTH, WINDOW), lambda t: (0, 0, 0, 0)),
            pl.BlockSpec((bsz, SUBLANES, B_SHIFT_PAD), whole3),
            pl.BlockSpec((bsz, B_WIDTH, B_WIDTH), whole3),
            pl.BlockSpec((bsz, SUBLANES, C_WIDTH), whole3),
            pl.BlockSpec((bsz, SUBLANES, C_WIDTH), whole3),
        ],
        out_shape=[
            jax.ShapeDtypeStruct((bsz, seq, D_MODEL), F32),
            jax.ShapeDtypeStruct((bsz, 2, KV_WIDTH, WINDOW), F32),
            jax.ShapeDtypeStruct((bsz, SUBLANES, B_SHIFT_PAD), F32),
            jax.ShapeDtypeStruct((bsz, B_WIDTH, B_WIDTH), F32),
            jax.ShapeDtypeStruct((bsz, SUBLANES, C_WIDTH), F32),
            jax.ShapeDtypeStruct((bsz, SUBLANES, C_WIDTH), F32),
        ],
        scratch_shapes=[
            pltpu.VMEM((WINDOW, 2 * KV_WIDTH), F32),
            pltpu.VMEM((SUBLANES, B_SHIFT_PAD), F32),
            pltpu.VMEM((SUBLANES, C_WIDTH), F32),
            pltpu.VMEM((SUBLANES, C_WIDTH), F32),
            pltpu.VMEM((B_WIDTH, B_WIDTH), F32),
            pltpu.VMEM((tb, P_PAD), F32),
            pltpu.VMEM((tb, D_MODEL), F32),
            pltpu.VMEM((tb, B_WIDTH), F32),
        ] * bsz,
        compiler_params=pltpu.CompilerParams(
            dimension_semantics=("arbitrary",), vmem_limit_bytes=VMEM_LIMIT_BYTES),
        name="prompt_layer",
    )(wts["sinks"], x, mod, wts["norm_pre"], wts["norm_post"], wts["w_in"], wts["w_out"], cos_t, sin_t,
      wts["mu"], wts["vec"], wts["wlora"], wts["wgate"])


def _sample_kernel(sinks_ref, x_ref, mod_ref, npre_ref, npost_ref, win_ref, wout_ref,
                   cos_ref, sin_ref, mu_ref, vec_ref, wlora_ref, wgate_ref,
                   kt_ref, vt_ref, sprev_ref, wkv_ref, conv_ref, hl_ref,
                   y_ref, knew_ref, vnew_ref, shift_ref, wkvo_ref, convo_ref, hlo_ref,
                   ys_s, proj_s, mix_s, q_s, o_s, knew_s, rw_s, rkv_s, yt_s):
    layer = pl.program_id(0)
    j = pl.program_id(1)
    nb = x_ref.shape[0]
    bb = nb // B_HEADS
    n_slot = kt_ref.shape[1]
    vec = vec_ref[...]
    cos = cos_ref[...]
    sin = sin_ref[...]

    @pl.when(jnp.logical_and(layer == 0, j == 0))
    def _():
        ys_s[...] = x_ref[...]

    @pl.when(j == 0)
    def _():
        ones_bd = _head_ones(B_WIDTH)
        x = ys_s[...]
        h = _bf(_rms_scale(x) * npre_ref[...] * (1.0 + mod_ref[1]) + mod_ref[0])
        for grp in (GRP_QKV, GRP_GA, GRP_PB, GRP_GB, GRP_XC, GRP_GC):
            _project(h, win_ref, proj_s, grp)
        k_new = _rope128(proj_s[:, OFF_K:OFF_K + KV_WIDTH], cos, sin)
        knew_s[...] = k_new
        knew_ref[...] = k_new.T
        vnew_ref[...] = proj_s[:, OFF_V:OFF_V + KV_WIDTH].T
        pb = proj_s[:, OFF_PB:OFF_PB + B_SHIFT_PAD]
        pb_t = pb.T
        shift_ref[...] = pb_t[0:B_SHIFT, :]
        prev = jnp.concatenate([sprev_ref[...], jnp.zeros((B_SHIFT_PAD - B_SHIFT, nb), F32)], axis=0).T
        xs = pb + (prev - pb) * mu_ref[...]
        r, k, v, ld, a, kkn, kmod = _rwkv_pre(xs, vec, wlora_ref[...], ones_bd)
        rw_s[T_DECAY] = jnp.exp(ld).T
        rw_s[T_NKK] = (-kkn).T
        rw_s[T_KKA] = (kkn * a).T
        rw_s[T_KMOD] = kmod.T
        rw_s[T_R] = r.T
        rw_s[T_V] = v.T
        rkv_s[0] = r
        rkv_s[1] = kmod
        rkv_s[2] = v

    rs = pl.ds(pl.multiple_of(j * bb, SUBLANES), bb)
    lane = _iota((bb, LANES), 1)
    lo = lane < HEAD_DIM
    for c in range(A_HEADS // 2):
        g = c // 2
        in_g = lo if g == 0 else jnp.logical_not(lo)
        qcol = _rope128(proj_s[rs, OFF_Q + LANES * c:OFF_Q + LANES * (c + 1)], cos, sin)
        qrol = pltpu.roll(qcol, HEAD_DIM, 1)
        for hh in range(2):
            hd = 2 * c + hh
            q_s[hd * bb:(hd + 1) * bb, :] = jnp.where(in_g, qcol if hh == g else qrol, 0.0)
    knew_blk = knew_s[rs, :]
    vnew_blk = proj_s[rs, OFF_V:OFF_V + KV_WIDTH]
    hrow = _iota((A_HEADS, 1), 0)
    sink_col = jnp.zeros((A_HEADS, 1), F32)
    for hd in range(A_HEADS):
        sink_col = jnp.where(hrow == hd, sinks_ref[layer, hd] * LOG2_E, sink_col)

    qbs = [q_s[pl.ds(b, A_HEADS, stride=bb), :] for b in range(bb)]
    s_c = [_dot(_bf(qbs[b]), _bf(kt_ref[b * KV_WIDTH:(b + 1) * KV_WIDTH, :])) for b in range(bb)]
    s_n = [jnp.sum(qbs[b] * knew_blk[b:b + 1, :], axis=-1, keepdims=True) for b in range(bb)]
    p_c, p_n = [], []
    for b in range(bb):
        m = jnp.maximum(jnp.maximum(jnp.max(s_c[b], axis=-1, keepdims=True), s_n[b]), sink_col)
        e_c = jnp.exp2(s_c[b] - m)
        e_n = jnp.exp2(s_n[b] - m)
        inv = 1.0 / (jnp.sum(e_c, axis=-1, keepdims=True) + e_n + jnp.exp2(sink_col - m))
        p_c.append(_bf(e_c * inv))
        p_n.append(e_n * inv)
    for b in range(bb):
        o = _dot_nt(p_c[b], _bf(vt_ref[b * KV_WIDTH:(b + 1) * KV_WIDTH, :]))
        o_s[pl.ds(b, A_HEADS, stride=bb), :] = o + p_n[b] * vnew_blk[b:b + 1, :]
    for c in range(A_HEADS // 2):
        g = c // 2
        halves = []
        for hh in range(2):
            oh = o_s[(2 * c + hh) * bb:(2 * c + hh + 1) * bb, :]
            halves.append(oh if hh == g else pltpu.roll(oh, HEAD_DIM, 1))
        ga = proj_s[rs, OFF_GA + LANES * c:OFF_GA + LANES * (c + 1)]
        mix_s[rs, LANES * c:LANES * (c + 1)] = jnp.where(lo, halves[0], halves[1]) * _silu(ga)

    hrows = pl.ds(pl.multiple_of(j * HEAD_DIM, HEAD_DIM), HEAD_DIM)
    w_t = rw_s[T_DECAY, hrows, :]
    nkk_t = rw_s[T_NKK, hrows, :]
    kka_t = rw_s[T_KKA, hrows, :]
    k_t = rw_s[T_KMOD, hrows, :]
    r_t = rw_s[T_R, hrows, :]

    def value_row(vi, carry):
        krows = pl.ds(pl.multiple_of(vi * HEAD_DIM, HEAD_DIM), HEAD_DIM)
        st = wkv_ref[krows, :]
        sa = jnp.sum(st * nkk_t, axis=0, keepdims=True)
        st_new = st * w_t + sa * kka_t + rw_s[T_V, pl.ds(j * HEAD_DIM + vi, 1), :] * k_t
        wkvo_ref[krows, :] = st_new
        yt_s[pl.ds(j * HEAD_DIM + vi, 1), :] = jnp.sum(st_new * r_t, axis=0, keepdims=True)
        return carry

    lax.fori_loop(0, HEAD_DIM, value_row, 0, unroll=4)

    @pl.when(j == pl.num_programs(1) - 1)
    def _():
        ones_bd = _head_ones(B_WIDTH)
        x = ys_s[...]
        gb = proj_s[:, OFF_GB:OFF_GB + B_WIDTH]
        mix_s[:, A_WIDTH:A_WIDTH + B_WIDTH] = _rwkv_post(yt_s[...].T, rkv_s[0], rkv_s[1], rkv_s[2], gb, vec, ones_bd)
        xc = proj_s[:, OFF_XC:OFF_XC + C_WIDTH]
        gc = proj_s[:, OFF_GC:OFF_GC + C_WIDTH]
        u = vec[V_CB:V_CB + 1, :] + xc * vec[V_CW3:V_CW3 + 1, :]
        for i in range(CONV_W - 1):
            u = u + conv_ref[i] * vec[V_CW0 + i:V_CW0 + i + 1, :]
        a_l, mult, gi = _lru_gates(u, vec, wgate_ref[...])
        h_l = a_l * hl_ref[...] + mult * gi * u
        hlo_ref[...] = h_l
        convo_ref[0] = conv_ref[1]
        convo_ref[1] = conv_ref[2]
        convo_ref[2] = xc
        mix_s[:, A_WIDTH + B_WIDTH:] = h_l * _silu(gc)
        yo = _dot(_bf(mix_s[...]), wout_ref[...])
        y = x + mod_ref[2] * (_rms_scale(yo) * npost_ref[...])
        ys_s[...] = y
        y_ref[...] = y


def _sample_layers(x, mod, wts, cos_r, sin_r, kt, vt, sprev_t, wkv_t, conv_t, hl):
    nb = x.shape[0]
    bb = nb // B_HEADS
    n_slot = kt.shape[2]
    n_hv = HEAD_DIM * HEAD_DIM
    w_head, w_tail = _weight_specs(lambda l, j: l, 2)
    const2 = lambda l, j: (0, 0)
    lay3 = lambda l, j: (l, 0, 0)
    return pl.pallas_call(
        _sample_kernel,
        grid=(DEPTH, B_HEADS),
        in_specs=[
            pl.BlockSpec(memory_space=pltpu.SMEM),
            pl.BlockSpec((nb, D_MODEL), const2),
            pl.BlockSpec((None, 3, nb, D_MODEL), lambda l, j: (l, 0, 0, 0)),
            *w_head,
            pl.BlockSpec((1, LANES), const2),
            pl.BlockSpec((1, LANES), const2),
            *w_tail,
            pl.BlockSpec((None, bb * KV_WIDTH, n_slot), lambda l, j: (l, j, 0)),
            pl.BlockSpec((None, bb * KV_WIDTH, n_slot), lambda l, j: (l, j, 0)),
            pl.BlockSpec((None, B_SHIFT, nb), lay3),
            pl.BlockSpec((None, n_hv, nb), lambda l, j: (l, j, 0)),
            pl.BlockSpec((None, CONV_W - 1, nb, C_WIDTH), lambda l, j: (l, 0, 0, 0)),
            pl.BlockSpec((None, nb, C_WIDTH), lay3),
        ],
        out_specs=[
            pl.BlockSpec((nb, D_MODEL), const2),
            pl.BlockSpec((None, KV_WIDTH, nb), lay3),
            pl.BlockSpec((None, KV_WIDTH, nb), lay3),
            pl.BlockSpec((None, B_SHIFT, nb), lay3),
            pl.BlockSpec((None, n_hv, nb), lambda l, j: (l, j, 0)),
            pl.BlockSpec((None, CONV_W - 1, nb, C_WIDTH), lambda l, j: (l, 0, 0, 0)),
            pl.BlockSpec((None, nb, C_WIDTH), lay3),
        ],
        out_shape=[
            jax.ShapeDtypeStruct((nb, D_MODEL), F32),
            jax.ShapeDtypeStruct((DEPTH, KV_WIDTH, nb), F32),
            jax.ShapeDtypeStruct((DEPTH, KV_WIDTH, nb), F32),
            jax.ShapeDtypeStruct((DEPTH, B_SHIFT, nb), F32),
            jax.ShapeDtypeStruct((DEPTH, B_HEADS * n_hv, nb), F32),
            jax.ShapeDtypeStruct((DEPTH, CONV_W - 1, nb, C_WIDTH), F32),
            jax.ShapeDtypeStruct((DEPTH, nb, C_WIDTH), F32),
        ],
        scratch_shapes=[
            pltpu.VMEM((nb, D_MODEL), F32),
            pltpu.VMEM((nb, P_PAD), F32),
            pltpu.VMEM((nb, D_MODEL), F32),
            pltpu.VMEM((A_HEADS * bb, LANES), F32),
            pltpu.VMEM((A_HEADS * bb, LANES), F32),
            pltpu.VMEM((nb, KV_WIDTH), F32),
            pltpu.VMEM((6, B_WIDTH, nb), F32),
            pltpu.VMEM((3, nb, B_WIDTH), F32),
            pltpu.VMEM((B_WIDTH, nb), F32),
        ],
        compiler_params=pltpu.CompilerParams(
            dimension_semantics=("arbitrary", "arbitrary"), vmem_limit_bytes=VMEM_LIMIT_BYTES),
        name="sample_layers",
    )(wts["sinks"], x, mod, wts["norm_pre"], wts["norm_post"], wts["w_in"], wts["w_out"], cos_r, sin_r,
      wts["mu"], wts["vec"], wts["wlora"], wts["wgate"], kt, vt, sprev_t, wkv_t, conv_t, hl)


def _rope_lanes():
    half = HEAD_DIM // 2
    inv_freq = ROPE_THETA ** (-jnp.arange(half, dtype=F32) / half)
    freq = jnp.tile(inv_freq, LANES // half)
    sign = jnp.tile(jnp.concatenate([-jnp.ones((half,), F32), jnp.ones((half,), F32)]), LANES // HEAD_DIM)
    return freq, sign


def _rope_row(pos):
    freq, sign = _rope_lanes()
    ang = jnp.float32(pos) * freq[None, :]
    return jnp.cos(ang), jnp.sin(ang) * sign[None, :]


def _rope_tables(seq):
    freq, sign = _rope_lanes()
    hi = (jnp.arange(seq // Q_BLOCK, dtype=F32) * Q_BLOCK)[:, None] * freq[None, :]
    lo = jnp.arange(Q_BLOCK, dtype=F32)[:, None] * freq[None, :]
    ch, sh = jnp.cos(hi)[:, None, :], jnp.sin(hi)[:, None, :]
    cl, sl = jnp.cos(lo)[None, :, :], jnp.sin(lo)[None, :, :]
    cos_t = (ch * cl - sh * sl).reshape(seq, LANES)
    sin_t = ((sh * cl + ch * sl) * sign).reshape(seq, LANES)
    return cos_t, sin_t


def _block_diag(w):
    dl, n, d, e = w.shape
    eye = jnp.eye(n, dtype=w.dtype)
    return (eye[None, :, None, :, None] * w[:, :, :, None, :]).reshape(dl, n * d, n * e)


def _prep_weights(p):
    dl = p["w_in"].shape[0]
    w_in_t = jnp.swapaxes(p["w_in"], 1, 2)
    q_scale = jnp.where(jnp.arange(w_in_t.shape[1]) < A_WIDTH, Q_SCALE, 1.0).astype(F32)
    w_in_t = (w_in_t * q_scale[None, :, None]).astype(BF16)
    z32 = jnp.zeros((dl, LORA, B_WIDTH), F32)
    wlora = jnp.concatenate([
        jnp.concatenate([p["rwkv_w_up"], z32], axis=2),
        jnp.concatenate([z32, p["rwkv_a_up"]], axis=2),
        jnp.zeros((dl, LANES - 2 * LORA, 2 * B_WIDTH), F32)], axis=1).astype(BF16)
    wgate = jnp.concatenate([_block_diag(p["lru_gate_a_w"]), _block_diag(p["lru_gate_x_w"])], axis=2).astype(BF16)
    cw = p["lru_conv_w"]
    rows = [p["rwkv_w0"], p["rwkv_a0"], p["rwkv_k_k"], p["rwkv_k_a"], p["rwkv_lnx_w"], p["rwkv_lnx_b"],
            p["rwkv_r_k"].reshape(dl, B_WIDTH), p["lru_conv_b"], p["lru_gate_a_b"], p["lru_gate_x_b"],
            p["lru_lambda"], cw[:, 0], cw[:, 1], cw[:, 2], cw[:, 3], jnp.zeros((dl, B_WIDTH), F32)]
    return {
        "sinks": p["attn_sinks"],
        "norm_pre": p["norm_pre"].reshape(dl, 1, D_MODEL),
        "norm_post": p["norm_post"].reshape(dl, 1, D_MODEL),
        "w_in": w_in_t,
        "w_out": p["w_out"].astype(BF16),
        "mu": jnp.concatenate([p["rwkv_mu"], jnp.zeros((dl, B_SHIFT_PAD - B_SHIFT), F32)],
                              axis=1).reshape(dl, 1, B_SHIFT_PAD),
        "vec": jnp.stack(rows, axis=1),
        "wlora": wlora,
        "wgate": wgate,
    }


def _forward(x_prompt, x_sample, c_prompt, c_sample, cache_swa_k, cache_swa_v, state_rwkv_shift,
             state_rwkv_wkv, state_lru_conv, state_lru_h, p, tb):
    bp, seq, _ = x_prompt.shape
    nb = x_sample.shape[0]
    n_slot = cache_swa_k.shape[2]
    assert nb % SUBLANES == 0 and nb % B_HEADS == 0 and bp <= SUBLANES and seq % tb == 0

    pad_rows = (-(nb + bp)) % SUBLANES
    c_all = jnp.concatenate([c_sample, c_prompt, jnp.zeros((pad_rows, D_MODEL), F32)], axis=0)
    mod = _mod_call(c_all, p["w_mod"], p["b_mod"])
    wts = _prep_weights(p)

    cos_p, sin_p = _rope_tables(seq)
    cos_s, sin_s = _rope_row(PAST_LEN)

    kt = jnp.transpose(cache_swa_k, (0, 1, 3, 4, 2)).reshape(DEPTH, nb * KV_WIDTH, n_slot)
    vt = jnp.transpose(cache_swa_v, (0, 1, 3, 4, 2)).reshape(DEPTH, nb * KV_WIDTH, n_slot)
    sprev_t = jnp.swapaxes(state_rwkv_shift, 1, 2)
    wkv_t = jnp.transpose(state_rwkv_wkv, (0, 2, 3, 4, 1)).reshape(DEPTH, B_HEADS * HEAD_DIM * HEAD_DIM, nb)
    conv_t = jnp.swapaxes(state_lru_conv, 1, 2)
    ys, knew_t, vnew_t, shift_t, wkvo_t, convo_t, hlo = _sample_layers(
        x_sample.reshape(nb, D_MODEL), mod, wts, cos_s, sin_s, kt, vt, sprev_t, wkv_t, conv_t, state_lru_h)
    outs_s = (
        jnp.transpose(knew_t.reshape(DEPTH, 1, A_KV_HEADS, HEAD_DIM, nb), (0, 4, 1, 2, 3)),
        jnp.transpose(vnew_t.reshape(DEPTH, 1, A_KV_HEADS, HEAD_DIM, nb), (0, 4, 1, 2, 3)),
        jnp.swapaxes(shift_t, 1, 2),
        jnp.transpose(wkvo_t.reshape(DEPTH, B_HEADS, HEAD_DIM, HEAD_DIM, nb), (0, 4, 1, 2, 3)),
        jnp.swapaxes(convo_t, 1, 2),
        hlo,
    )

    yp = x_prompt
    outs_p = []
    for l in range(DEPTH):
        yp, kv_p, sh_p, wkv_p, conv_p, h_p = _prompt_layer(l, yp, mod, wts, cos_p, sin_p, nb, tb)
        kv_p = jnp.transpose(kv_p.reshape(bp, 2, A_KV_HEADS, HEAD_DIM, WINDOW), (1, 0, 4, 2, 3))
        outs_p.append((
            kv_p[0], kv_p[1],
            sh_p[:, SUBLANES - 1, :B_SHIFT],
            jnp.stack([wkv_p[:, HEAD_DIM * hd:HEAD_DIM * (hd + 1), HEAD_DIM * hd:HEAD_DIM * (hd + 1)]
                       for hd in range(B_HEADS)], axis=1),
            conv_p[:, SUBLANES - (CONV_W - 1):, :],
            h_p[:, SUBLANES - 1, :],
        ))
    sp = [jnp.stack(z) for z in zip(*outs_p)]
    return (yp, ys.reshape(nb, 1, D_MODEL), *sp, *outs_s)


def kernel(x_prompt, x_sample, c_prompt, c_sample, cache_swa_k, cache_swa_v, state_rwkv_shift, state_rwkv_wkv, state_lru_conv, state_lru_h, norm_pre, norm_post, w_mod, b_mod, w_in, w_out, attn_sinks, rwkv_mu, rwkv_w0, rwkv_w_up, rwkv_a0, rwkv_a_up, rwkv_k_k, rwkv_k_a, rwkv_r_k, rwkv_lnx_w, rwkv_lnx_b, lru_conv_w, lru_conv_b, lru_gate_a_w, lru_gate_a_b, lru_gate_x_w, lru_gate_x_b, lru_lambda):
    p = dict(norm_pre=norm_pre, norm_post=norm_post, w_mod=w_mod, b_mod=b_mod, w_in=w_in, w_out=w_out,
             attn_sinks=attn_sinks, rwkv_mu=rwkv_mu, rwkv_w0=rwkv_w0, rwkv_w_up=rwkv_w_up, rwkv_a0=rwkv_a0,
             rwkv_a_up=rwkv_a_up, rwkv_k_k=rwkv_k_k, rwkv_k_a=rwkv_k_a, rwkv_r_k=rwkv_r_k,
             rwkv_lnx_w=rwkv_lnx_w, rwkv_lnx_b=rwkv_lnx_b, lru_conv_w=lru_conv_w, lru_conv_b=lru_conv_b,
             lru_gate_a_w=lru_gate_a_w, lru_gate_a_b=lru_gate_a_b, lru_gate_x_w=lru_gate_x_w,
             lru_gate_x_b=lru_gate_x_b, lru_lambda=lru_lambda)
    return _forward(x_prompt, x_sample, c_prompt, c_sample, cache_swa_k, cache_swa_v, state_rwkv_shift,
                    state_rwkv_wkv, state_lru_conv, state_lru_h, p, TIME_BLOCK)
```

```python
import functools

import numpy as np
import jax
import jax.numpy as jnp
from jax import lax
from jax.experimental import pallas as pl
from jax.experimental.pallas import tpu as pltpu

F32 = jnp.float32
BF16 = jnp.bfloat16

D_MODEL = 1024
DEPTH = 2
A_HEADS = 8
A_KV_HEADS = 2
HEAD_DIM = 64
A_WIDTH = A_HEADS * HEAD_DIM
KV_WIDTH = A_KV_HEADS * HEAD_DIM
WINDOW = 128
Q_BLOCK = 128
ROPE_THETA = 10000.0
B_WIDTH = 256
B_HEADS = 4
LORA = 32
B_SHIFT = 3 * B_WIDTH + 2 * LORA
B_SHIFT_PAD = 3 * B_WIDTH + 128
LNX_EPS = 1e-5 * 8 ** 2
C_WIDTH = 256
CONV_W = 4
LRU_C = 8.0
EPS = 1e-6
PAST_LEN = 8192
EXP_NEG_HALF = float(np.exp(-0.5))
LOG2_E = float(np.log2(np.e))
Q_SCALE = HEAD_DIM ** -0.5 * LOG2_E

LANES = 128
SUBLANES = 8
VMEM_LIMIT_BYTES = 56 * 1024 * 1024

OFF_Q = 0
OFF_K = OFF_Q + A_WIDTH
OFF_V = OFF_K + KV_WIDTH
OFF_GA = OFF_V + KV_WIDTH
OFF_PB = OFF_GA + A_WIDTH
OFF_GB = OFF_PB + B_SHIFT_PAD
OFF_XC = OFF_GB + B_WIDTH
OFF_GC = OFF_XC + C_WIDTH
P_PAD = OFF_GC + C_WIDTH
P_IN = P_PAD - (B_SHIFT_PAD - B_SHIFT)

CHUNK = 64
TIME_BLOCK = 512
SEQS_PER_STEP = 1
PROJ_TILE = 256
PHASE_LAG = 12

(V_W0, V_A0, V_KK, V_KA, V_LNW, V_LNB, V_RK, V_CB, V_GAB, V_GXB, V_LAM,
 V_CW0, V_CW1, V_CW2, V_CW3) = range(15)
VEC_ROWS = 16

(T_DECAY, T_NKK, T_KKA, T_KMOD, T_R, T_V) = range(6)

_NT = (((1,), (1,)), ((), ()))
_TN = (((0,), (0,)), ((), ()))


def _bf(x):
    return x.astype(BF16)


def _dot(a, b):
    return jnp.dot(a, b, preferred_element_type=F32)


def _dot_b(a, b):
    return jnp.dot(a, b, preferred_element_type=F32).astype(BF16)


def _dot_nt(a, b):
    return lax.dot_general(a, b, _NT, preferred_element_type=F32)


def _dot_tn(a, b):
    return lax.dot_general(a, b, _TN, preferred_element_type=F32)


def _iota(shape, dim):
    return lax.broadcasted_iota(jnp.int32, shape, dim)


def _sigmoid(x):
    return jax.nn.sigmoid(x)


def _silu(x):
    return x * _sigmoid(x)


def _softplus(x):
    return jnp.maximum(x, 0.0) + jnp.log1p(jnp.exp(-jnp.abs(x)))


def _head_ones(n):
    r = _iota((n, n), 0) // HEAD_DIM
    c = _iota((n, n), 1) // HEAD_DIM
    return jnp.where(r == c, 1.0, 0.0).astype(BF16)


def _head_sum(x, ones_bd):
    return _dot(_bf(x), ones_bd)


def _rms_scale(x):
    ms = jnp.mean(x * x, axis=-1, keepdims=True)
    return x * lax.rsqrt(ms + EPS)


def _affine_scan(a, b, h0):
    rows, n = a.shape
    groups = rows // SUBLANES
    sub = _iota((rows, n), 0) % SUBLANES
    d = 1
    while d < SUBLANES:
        keep = sub >= d
        a_sh = jnp.where(keep, pltpu.roll(a, d, 0), 1.0)
        b_sh = jnp.where(keep, pltpu.roll(b, d, 0), 0.0)
        b = b + a * b_sh
        a = a * a_sh
        d *= 2
    outs = []
    carry = h0
    for g in range(groups):
        rows_g = slice(g * SUBLANES, (g + 1) * SUBLANES)
        hg = b[rows_g] + a[rows_g] * carry
        outs.append(hg)
        carry = hg[SUBLANES - 1:SUBLANES, :]
    return jnp.concatenate(outs, axis=0)


def _cumsum_chunks(x, chunk):
    rows, n = x.shape
    sub = _iota((rows, n), 0) % SUBLANES
    d = 1
    while d < SUBLANES:
        x = x + jnp.where(sub >= d, pltpu.roll(x, d, 0), 0.0)
        d *= 2
    outs = []
    for g in range(rows // SUBLANES):
        xg = x[g * SUBLANES:(g + 1) * SUBLANES]
        if (g * SUBLANES) % chunk != 0:
            xg = xg + outs[-1][SUBLANES - 1:SUBLANES, :]
        outs.append(xg)
    return jnp.concatenate(outs, axis=0)


def _rope128(z, cos, sin_signed):
    lane = _iota(z.shape, 1)
    first = (lane & 32) == 0
    sw = jnp.where(first, pltpu.roll(z, 96, 1), pltpu.roll(z, 32, 1))
    return z * cos + sw * sin_signed


def _rwkv_pre(xs, vec, wlora, ones_bd):
    r = xs[:, 0:B_WIDTH]
    k = xs[:, B_WIDTH:2 * B_WIDTH]
    v = xs[:, 2 * B_WIDTH:3 * B_WIDTH]
    lor = xs[:, 3 * B_WIDTH:B_SHIFT_PAD]
    lane = _iota(lor.shape, 1)
    z = jnp.where(lane < LORA, jnp.tanh(lor), lor)
    wa = _dot(_bf(z), wlora)
    zw = vec[V_W0:V_W0 + 1, :] + wa[:, :B_WIDTH]
    ld = -EXP_NEG_HALF * _sigmoid(zw)
    a = _sigmoid(vec[V_A0:V_A0 + 1, :] + wa[:, B_WIDTH:])
    kk = k * vec[V_KK:V_KK + 1, :]
    kkn = kk * jnp.minimum(lax.rsqrt(_head_sum(kk * kk, ones_bd)), 1e12)
    kmod = k * (1.0 + (a - 1.0) * vec[V_KA:V_KA + 1, :])
    return r, k, v, ld, a, kkn, kmod


def _rwkv_post(y, r, kmod, v, gb, vec, ones_bd):
    inv = 1.0 / HEAD_DIM
    mu = _head_sum(y, ones_bd) * inv
    yc = y - mu
    var = _head_sum(yc * yc, ones_bd) * inv
    yn = yc * lax.rsqrt(var + LNX_EPS)
    yn = yn * vec[V_LNW:V_LNW + 1, :] + vec[V_LNB:V_LNB + 1, :]
    bonus = _head_sum(r * kmod * vec[V_RK:V_RK + 1, :], ones_bd) * v
    return (yn + bonus) * _silu(gb)


def _lru_gates(u, vec, wgate):
    gates = _dot(_bf(u), wgate)
    gr = _sigmoid(gates[:, :C_WIDTH] + vec[V_GAB:V_GAB + 1, :])
    gi = _sigmoid(gates[:, C_WIDTH:] + vec[V_GXB:V_GXB + 1, :])
    sp = _softplus(-vec[V_LAM:V_LAM + 1, :])
    log_a = -LRU_C * gr * sp
    a = jnp.exp(log_a)
    mult = jnp.sqrt(1.0 - a * a)
    return a, mult, gi


def _project(h, win_ref, proj_s, group):
    src, dst, n = group
    proj_s[:, dst:dst + n] = _dot_nt(h, win_ref[src:src + n, :])


_DELTA = B_SHIFT_PAD - B_SHIFT
GRP_PB = (OFF_PB, OFF_PB, B_SHIFT_PAD)
GRP_QKV = (OFF_Q, OFF_Q, OFF_GA - OFF_Q)
GRP_GA = (OFF_GA, OFF_GA, A_WIDTH)
GRP_GB = (OFF_GB - _DELTA, OFF_GB, B_WIDTH)
GRP_XC = (OFF_XC - _DELTA, OFF_XC, C_WIDTH)
GRP_GC = (OFF_GC - _DELTA, OFF_GC, C_WIDTH)


def _mod_kernel(c_ref, w_ref, b_ref, o_ref):
    o_ref[...] = _dot(_bf(_silu(c_ref[...])), _bf(w_ref[...])) + b_ref[...]


def _mod_call(c_all, w_mod, b_mod):
    rows = c_all.shape[0]
    return pl.pallas_call(
        _mod_kernel,
        grid=(DEPTH, 3),
        in_specs=[
            pl.BlockSpec((rows, D_MODEL), lambda l, j: (0, 0)),
            pl.BlockSpec((None, D_MODEL, D_MODEL), lambda l, j: (l, 0, j)),
            pl.BlockSpec((None, 1, D_MODEL), lambda l, j: (l, 0, j)),
        ],
        out_specs=pl.BlockSpec((None, None, rows, D_MODEL), lambda l, j: (l, j, 0, 0)),
        out_shape=jax.ShapeDtypeStruct((DEPTH, 3, rows, D_MODEL), F32),
        compiler_params=pltpu.CompilerParams(
            dimension_semantics=("arbitrary", "arbitrary"), vmem_limit_bytes=VMEM_LIMIT_BYTES),
        name="adaln_mod",
    )(c_all, w_mod, b_mod.reshape(DEPTH, 1, 3 * D_MODEL))


def _prompt_kernel(sinks_ref, x_ref, mod_ref, npre_ref, npost_ref, win_ref, wout_ref,
                   cos_ref, sin_ref, mu_ref, vec_ref, wlora_ref, wgate_ref,
                   y_ref, kv_ref, shift_ref, wkv_ref, conv_ref, hlru_ref,
                   *scratch, layer):
    n_seq = x_ref.shape[0]
    tb = x_ref.shape[1]
    t = pl.program_id(1)
    per_seq = len(scratch) // n_seq
    seq_scratch = [scratch[q * per_seq:(q + 1) * per_seq] for q in range(n_seq)]

    @pl.when(t == 0)
    def _():
        for kvprev, pprev, xcprev, hprev, gstate, _, _, _ in seq_scratch:
            kvprev[...] = jnp.zeros_like(kvprev)
            pprev[...] = jnp.zeros_like(pprev)
            xcprev[...] = jnp.zeros_like(xcprev)
            hprev[...] = jnp.zeros_like(hprev)
            gstate[...] = jnp.zeros_like(gstate)

    vec = vec_ref[...]
    ones_bd = _head_ones(B_WIDTH)
    cos = cos_ref[...]
    sin = sin_ref[...]
    nck = tb // CHUNK
    sls = [slice(c * CHUNK, (c + 1) * CHUNK) for c in range(nck)]
    rng = range(nck)
    pairs = [(c, i) for c in range(A_HEADS // 2) for i in range(tb // Q_BLOCK)]

    lo_kv = _iota((tb + WINDOW, KV_WIDTH), 1) < HEAD_DIM
    qi = _iota((Q_BLOCK, 2 * Q_BLOCK), 0)
    sj = _iota((Q_BLOCK, 2 * Q_BLOCK), 1)
    band = (sj >= qi) & (sj <= qi + WINDOW)
    first_lo = jnp.where(t == 0, Q_BLOCK, 0)
    neg_inf = -jnp.inf
    row_p = _iota((tb, B_SHIFT_PAD), 0)
    reset = (_iota((tb, C_WIDTH), 0) + t * tb) == 0

    n4 = B_HEADS * CHUNK
    r4 = _iota((n4, n4), 0)
    c4 = _iota((n4, n4), 1)
    same4 = (r4 // CHUNK) == (c4 // CHUNK)
    strict = same4 & ((r4 % CHUNK) > (c4 % CHUNK))
    incl = same4 & ((r4 % CHUNK) >= (c4 % CHUNK))
    eye4 = jnp.where(r4 == c4, 1.0, 0.0).astype(BF16)
    bd_heads = (_iota((B_WIDTH, B_WIDTH), 0) // HEAD_DIM) == (_iota((B_WIDTH, B_WIDTH), 1) // HEAD_DIM)

    def variants(z):
        zr = pltpu.roll(z, HEAD_DIM, 1)
        a_ = _bf(jnp.where(lo_kv, z, 0.0))
        b_ = _bf(jnp.where(lo_kv, 0.0, z))
        c_ = _bf(jnp.where(lo_kv, zr, 0.0))
        d_ = _bf(jnp.where(lo_kv, 0.0, zr))
        return ((a_, d_), (c_, b_))

    def win_stack(var, c, i):
        g, r0 = c // 2, i * Q_BLOCK
        return jnp.concatenate([var[g][0][r0:r0 + 2 * Q_BLOCK], var[g][1][r0:r0 + 2 * Q_BLOCK]], axis=0)

    def stk(z):
        zb = _bf(z)
        return jnp.concatenate([zb] * B_HEADS, axis=0) * ones_bd

    def collapse(z):
        return z[0:CHUNK] + z[CHUNK:2 * CHUNK] + z[2 * CHUNK:3 * CHUNK] + z[3 * CHUNK:4 * CHUNK]

    def sequence(q):
        kvprev, pprev, xcprev, hprev, gstate, proj, mix, ywkv = seq_scratch[q]

        x = x_ref[q]
        brow = pl.ds(pl.program_id(0) * n_seq + q, 1)
        shift_m = mod_ref[0, brow, :]
        scale_m = mod_ref[1, brow, :]
        gate_m = mod_ref[2, brow, :]
        h = _bf(_rms_scale(x) * (npre_ref[...] * (1.0 + scale_m)) + shift_m)
        for src, dst, n in (GRP_PB, GRP_QKV, GRP_XC, GRP_GA, GRP_GB, GRP_GC):
            for off in range(0, n, PROJ_TILE):
                _project(h, win_ref, proj, (src + off, dst + off, min(PROJ_TILE, n - off)))
                yield

        pb = proj[:, OFF_PB:OFF_PB + B_SHIFT_PAD]
        prev = jnp.where(row_p == 0, pprev[SUBLANES - 1:SUBLANES, :], pltpu.roll(pb, 1, 0))
        xs = pb + (prev - pb) * mu_ref[...]
        yield
        r, k, v, ld, a, kkn, kmod = _rwkv_pre(xs, vec, wlora_ref[...], ones_bd)
        kka = kkn * a
        yield

        k_rot = _rope128(proj[:, OFF_K:OFF_K + KV_WIDTH], cos, sin)
        v_att = proj[:, OFF_V:OFF_V + KV_WIDTH]
        kvar = variants(jnp.concatenate([kvprev[:, 0:KV_WIDTH], k_rot], axis=0))
        vvar = variants(jnp.concatenate([kvprev[:, KV_WIDTH:2 * KV_WIDTH], v_att], axis=0))
        k_tail = k_rot[tb - WINDOW:]
        v_tail = v_att[tb - WINDOW:]
        yield
        qcols = [_bf(_rope128(proj[:, OFF_Q + LANES * c:OFF_Q + LANES * (c + 1)], cos, sin))
                 for c in range(A_HEADS // 2)]
        scores = [_dot_nt(qcols[c][i * Q_BLOCK:(i + 1) * Q_BLOCK], win_stack(kvar, c, i))
                  for c, i in pairs]
        yield

        xc = proj[:, OFF_XC:OFF_XC + C_WIDTH]
        xext = jnp.concatenate([xcprev[...], xc], axis=0)
        u = vec[V_CB:V_CB + 1, :] + xc * vec[V_CW3:V_CW3 + 1, :]
        for j in range(1, CONV_W):
            u = u + pltpu.roll(xext, j, 0)[SUBLANES:, :] * vec[V_CW3 - j:V_CW3 - j + 1, :]
        a_l, mult, gi = _lru_gates(u, vec, wgate_ref[...])
        a_l = jnp.where(reset, 0.0, a_l)
        mult = jnp.where(reset, 1.0, mult)
        b_l = mult * gi * u
        yield
        h_l = _affine_scan(a_l, b_l, hprev[SUBLANES - 1:SUBLANES, :])
        yield

        lcum = _cumsum_chunks(ld, CHUNK)
        e_l = jnp.exp(lcum)
        e_nl = jnp.exp(-lcum)
        rt = r * e_l
        at = -kkn * jnp.exp(lcum - ld)
        ktil = kmod * e_nl
        btil = kka * e_nl
        gam = [e_l[s.stop - 1:s.stop, :] for s in sls]
        khat = [ktil[sls[c]] * gam[c] for c in rng]
        bhat = [btil[sls[c]] * gam[c] for c in rng]
        yield
        a_s = [stk(at[s]) for s in sls]
        r_s = [stk(rt[s]) for s in sls]
        b_s = [stk(btil[s]) for s in sls]
        k_s = [stk(ktil[s]) for s in sls]
        yield
        v_s = [stk(v[s]) for s in sls]
        bh_s = [stk(bhat[c]) for c in rng]
        sc = [_dot_nt(jnp.concatenate([a_s[c], r_s[c]], axis=0), jnp.concatenate([b_s[c], k_s[c]], axis=0))
              for c in rng]
        yield

        probs = []
        for (c, i), s in zip(pairs, scores):
            mask = band & (sj >= first_lo) if i == 0 else band
            ps = []
            for hh in range(2):
                sink = sinks_ref[layer, 2 * c + hh] * LOG2_E
                sh = jnp.where(mask, s[:, 2 * Q_BLOCK * hh:2 * Q_BLOCK * (hh + 1)], neg_inf)
                m = jnp.maximum(jnp.max(sh, axis=-1, keepdims=True), sink)
                p = jnp.exp2(sh - m)
                den = jnp.sum(p, axis=-1, keepdims=True) + jnp.exp2(sink - m)
                ps.append(_bf(p * (1.0 / den)))
            probs.append(jnp.concatenate(ps, axis=1))
            if i == tb // Q_BLOCK - 1 and c % 2 == 1:
                yield
        for (c, i), p in zip(pairs, probs):
            r0 = i * Q_BLOCK
            o = _dot(p, win_stack(vvar, c, i))
            ga = proj[r0:r0 + Q_BLOCK, OFF_GA + LANES * c:OFF_GA + LANES * (c + 1)]
            mix[r0:r0 + Q_BLOCK, LANES * c:LANES * (c + 1)] = o * _silu(ga)
        yield

        lab = [_bf(jnp.where(strict, sc[c][0:n4, 0:n4], 0.0)) for c in rng]
        lak = [_bf(jnp.where(strict, sc[c][0:n4, n4:], 0.0)) for c in rng]
        mrb = [_bf(jnp.where(incl, sc[c][n4:, 0:n4], 0.0)) for c in rng]
        mrk = [_bf(jnp.where(incl, sc[c][n4:, n4:], 0.0)) for c in rng]
        pw = lab
        tinv = [pw[c] + eye4 for c in rng]
        pw = [_dot_b(pw[c], pw[c]) for c in rng]
        yield
        for _ in range(4):
            tinv = [_dot_b(tinv[c], pw[c] + eye4) for c in rng]
            pw = [_dot_b(pw[c], pw[c]) for c in rng]
            yield
        tinv = [_dot_b(tinv[c], pw[c] + eye4) for c in rng]
        lv = [_dot_b(lak[c], v_s[c]) for c in rng]
        yield
        w_s = [_dot_b(tinv[c], a_s[c]) for c in rng]
        z_s = [_dot(tinv[c], lv[c]) for c in rng]
        yield
        rp = [_bf(r_s[c].astype(F32) + _dot(mrb[c], w_s[c])) for c in rng]
        y0 = [_dot(jnp.concatenate([mrb[c], mrk[c]], axis=1), jnp.concatenate([_bf(z_s[c]), v_s[c]], axis=0))
              for c in rng]
        pp = [_bf(_dot_tn(w_s[c], bh_s[c])) for c in rng]
        qq = [jnp.where(bd_heads,
                        _dot_tn(_bf(jnp.concatenate([collapse(z_s[c]), v[sls[c]]], axis=0)),
                                _bf(jnp.concatenate([bhat[c], khat[c]], axis=0))), 0.0) for c in rng]
        yield
        gmat = gstate[...]
        for c in rng:
            gb16 = _bf(gmat)
            ywkv[sls[c], :] = collapse(_dot_nt(rp[c], gb16) + y0[c])
            gmat = gam[c] * gmat + _dot(gb16, pp[c]) + qq[c]
        gstate[...] = gmat
        yield

        gb = proj[:, OFF_GB:OFF_GB + B_WIDTH]
        mix[:, A_WIDTH:A_WIDTH + B_WIDTH] = _rwkv_post(ywkv[...], r, kmod, v, gb, vec, ones_bd)
        gc = proj[:, OFF_GC:OFF_GC + C_WIDTH]
        mix[:, A_WIDTH + B_WIDTH:] = h_l * _silu(gc)

        yo = _dot(_bf(mix[...]), wout_ref[...])
        y_ref[q] = x + (gate_m * npost_ref[...]) * _rms_scale(yo)

        kvprev[:, 0:KV_WIDTH] = k_tail
        kvprev[:, KV_WIDTH:2 * KV_WIDTH] = v_tail
        pprev[...] = pb[tb - SUBLANES:, :]
        xcprev[...] = xc[tb - SUBLANES:, :]
        hprev[...] = h_l[tb - SUBLANES:, :]
        shift_ref[q] = pb[tb - SUBLANES:, :]
        wkv_ref[q] = gmat
        conv_ref[q] = xc[tb - SUBLANES:, :]
        hlru_ref[q] = h_l[tb - SUBLANES:, :]

        @pl.when(t == pl.num_programs(1) - 1)
        def _():
            kv_ref[q, 0] = k_tail.T
            kv_ref[q, 1] = v_tail.T

    gens = [sequence(q) for q in range(n_seq)]
    done = [False] * n_seq
    tick = 0
    while not all(done):
        for q, g in enumerate(gens):
            if done[q] or tick < q * PHASE_LAG:
                continue
            try:
                next(g)
            except StopIteration:
                done[q] = True
        tick += 1


def _weight_specs(layer_of, grid_rank):
    def lay(*shape):
        zeros = (0,) * len(shape)
        if grid_rank == 1:
            return pl.BlockSpec((None, *shape), lambda i: (layer_of(i), *zeros))
        return pl.BlockSpec((None, *shape), lambda a, b: (layer_of(a, b), *zeros))
    return [
        lay(1, D_MODEL),
        lay(1, D_MODEL),
        lay(P_IN, D_MODEL),
        lay(D_MODEL, D_MODEL),
    ], [
        lay(1, B_SHIFT_PAD),
        lay(VEC_ROWS, B_WIDTH),
        lay(LANES, 2 * B_WIDTH),
        lay(C_WIDTH, 2 * C_WIDTH),
    ]


def _prompt_layer(layer, x, mod, wts, cos_t, sin_t, mod_row0, tb):
    bsz, seq, _ = x.shape
    nt = seq // tb
    sps = SEQS_PER_STEP
    per_b3 = lambda b, t: (b, 0, 0)
    w_head, w_tail = _weight_specs(lambda b, t: layer, 2)
    return pl.pallas_call(
        functools.partial(_prompt_kernel, layer=layer),
        grid=(bsz // sps, nt),
        in_specs=[
            pl.BlockSpec(memory_space=pltpu.SMEM),
            pl.BlockSpec((sps, tb, D_MODEL), lambda b, t: (b, t, 0)),
            pl.BlockSpec((None, 3, SUBLANES, D_MODEL),
                         lambda b, t: (layer, 0, mod_row0 // SUBLANES, 0)),
            *w_head,
            pl.BlockSpec((tb, LANES), lambda b, t: (t, 0)),
            pl.BlockSpec((tb, LANES), lambda b, t: (t, 0)),
            *w_tail,
        ],
        out_specs=[
            pl.BlockSpec((sps, tb, D_MODEL), lambda b, t: (b, t, 0)),
            pl.BlockSpec((sps, 2, KV_WIDTH, WINDOW), lambda b, t: (b, 0, 0, 0)),
            pl.BlockSpec((sps, SUBLANES, B_SHIFT_PAD), per_b3),
            pl.BlockSpec((sps, B_WIDTH, B_WIDTH), per_b3),
            pl.BlockSpec((sps, SUBLANES, C_WIDTH), per_b3),
            pl.BlockSpec((sps, SUBLANES, C_WIDTH), per_b3),
        ],
        out_shape=[
            jax.ShapeDtypeStruct((bsz, seq, D_MODEL), F32),
            jax.ShapeDtypeStruct((bsz, 2, KV_WIDTH, WINDOW), F32),
            jax.ShapeDtypeStruct((bsz, SUBLANES, B_SHIFT_PAD), F32),
            jax.ShapeDtypeStruct((bsz, B_WIDTH, B_WIDTH), F32),
            jax.ShapeDtypeStruct((bsz, SUBLANES, C_WIDTH), F32),
            jax.ShapeDtypeStruct((bsz, SUBLANES, C_WIDTH), F32),
        ],
        scratch_shapes=[
            pltpu.VMEM((WINDOW, 2 * KV_WIDTH), F32),
            pltpu.VMEM((SUBLANES, B_SHIFT_PAD), F32),
            pltpu.VMEM((SUBLANES, C_WIDTH), F32),
            pltpu.VMEM((SUBLANES, C_WIDTH), F32),
            pltpu.VMEM((B_WIDTH, B_WIDTH), F32),
            pltpu.VMEM((tb, P_PAD), F32),
            pltpu.VMEM((tb, D_MODEL), F32),
            pltpu.VMEM((tb, B_WIDTH), F32),
        ] * sps,
        compiler_params=pltpu.CompilerParams(
            dimension_semantics=("arbitrary", "arbitrary"), vmem_limit_bytes=VMEM_LIMIT_BYTES),
        name="prompt_layer",
    )(wts["sinks"], x, mod, wts["norm_pre"], wts["norm_post"], wts["w_in"], wts["w_out"], cos_t, sin_t,
      wts["mu"], wts["vec"], wts["wlora"], wts["wgate"])


def _sample_kernel(sinks_ref, x_ref, mod_ref, npre_ref, npost_ref, win_ref, wout_ref,
                   cos_ref, sin_ref, mu_ref, vec_ref, wlora_ref, wgate_ref,
                   kt_ref, vt_ref, sprev_ref, wkv_ref, conv_ref, hl_ref,
                   y_ref, knew_ref, vnew_ref, shift_ref, wkvo_ref, convo_ref, hlo_ref,
                   ys_s, proj_s, mix_s, q_s, o_s, knew_s, rw_s, rkv_s, yt_s):
    layer = pl.program_id(0)
    j = pl.program_id(1)
    nb = x_ref.shape[0]
    bb = nb // B_HEADS
    n_slot = kt_ref.shape[1]
    vec = vec_ref[...]
    cos = cos_ref[...]
    sin = sin_ref[...]

    @pl.when(jnp.logical_and(layer == 0, j == 0))
    def _():
        ys_s[...] = x_ref[...]

    @pl.when(j == 0)
    def _():
        ones_bd = _head_ones(B_WIDTH)
        x = ys_s[...]
        h = _bf(_rms_scale(x) * npre_ref[...] * (1.0 + mod_ref[1]) + mod_ref[0])
        for grp in (GRP_QKV, GRP_GA, GRP_PB, GRP_GB, GRP_XC, GRP_GC):
            _project(h, win_ref, proj_s, grp)
        k_new = _rope128(proj_s[:, OFF_K:OFF_K + KV_WIDTH], cos, sin)
        knew_s[...] = k_new
        knew_ref[...] = k_new.T
        vnew_ref[...] = proj_s[:, OFF_V:OFF_V + KV_WIDTH].T
        pb = proj_s[:, OFF_PB:OFF_PB + B_SHIFT_PAD]
        pb_t = pb.T
        shift_ref[...] = pb_t[0:B_SHIFT, :]
        prev = jnp.concatenate([sprev_ref[...], jnp.zeros((B_SHIFT_PAD - B_SHIFT, nb), F32)], axis=0).T
        xs = pb + (prev - pb) * mu_ref[...]
        r, k, v, ld, a, kkn, kmod = _rwkv_pre(xs, vec, wlora_ref[...], ones_bd)
        rw_s[T_DECAY] = jnp.exp(ld).T
        rw_s[T_NKK] = (-kkn).T
        rw_s[T_KKA] = (kkn * a).T
        rw_s[T_KMOD] = kmod.T
        rw_s[T_R] = r.T
        rw_s[T_V] = v.T
        rkv_s[0] = r
        rkv_s[1] = kmod
        rkv_s[2] = v

    rs = pl.ds(pl.multiple_of(j * bb, SUBLANES), bb)
    lane = _iota((bb, LANES), 1)
    lo = lane < HEAD_DIM
    for c in range(A_HEADS // 2):
        g = c // 2
        in_g = lo if g == 0 else jnp.logical_not(lo)
        qcol = _rope128(proj_s[rs, OFF_Q + LANES * c:OFF_Q + LANES * (c + 1)], cos, sin)
        qrol = pltpu.roll(qcol, HEAD_DIM, 1)
        for hh in range(2):
            hd = 2 * c + hh
            q_s[hd * bb:(hd + 1) * bb, :] = jnp.where(in_g, qcol if hh == g else qrol, 0.0)
    knew_blk = knew_s[rs, :]
    vnew_blk = proj_s[rs, OFF_V:OFF_V + KV_WIDTH]
    hrow = _iota((A_HEADS, 1), 0)
    sink_col = jnp.zeros((A_HEADS, 1), F32)
    for hd in range(A_HEADS):
        sink_col = jnp.where(hrow == hd, sinks_ref[layer, hd] * LOG2_E, sink_col)

    qbs = [q_s[pl.ds(b, A_HEADS, stride=bb), :] for b in range(bb)]
    s_c = [_dot(_bf(qbs[b]), _bf(kt_ref[b * KV_WIDTH:(b + 1) * KV_WIDTH, :])) for b in range(bb)]
    s_n = [jnp.sum(qbs[b] * knew_blk[b:b + 1, :], axis=-1, keepdims=True) for b in range(bb)]
    p_c, p_n = [], []
    for b in range(bb):
        m = jnp.maximum(jnp.maximum(jnp.max(s_c[b], axis=-1, keepdims=True), s_n[b]), sink_col)
        e_c = jnp.exp2(s_c[b] - m)
        e_n = jnp.exp2(s_n[b] - m)
        inv = 1.0 / (jnp.sum(e_c, axis=-1, keepdims=True) + e_n + jnp.exp2(sink_col - m))
        p_c.append(_bf(e_c * inv))
        p_n.append(e_n * inv)
    for b in range(bb):
        o = _dot_nt(p_c[b], _bf(vt_ref[b * KV_WIDTH:(b + 1) * KV_WIDTH, :]))
        o_s[pl.ds(b, A_HEADS, stride=bb), :] = o + p_n[b] * vnew_blk[b:b + 1, :]
    for c in range(A_HEADS // 2):
        g = c // 2
        halves = []
        for hh in range(2):
            oh = o_s[(2 * c + hh) * bb:(2 * c + hh + 1) * bb, :]
            halves.append(oh if hh == g else pltpu.roll(oh, HEAD_DIM, 1))
        ga = proj_s[rs, OFF_GA + LANES * c:OFF_GA + LANES * (c + 1)]
        mix_s[rs, LANES * c:LANES * (c + 1)] = jnp.where(lo, halves[0], halves[1]) * _silu(ga)

    hrows = pl.ds(pl.multiple_of(j * HEAD_DIM, HEAD_DIM), HEAD_DIM)
    w_t = rw_s[T_DECAY, hrows, :]
    nkk_t = rw_s[T_NKK, hrows, :]
    kka_t = rw_s[T_KKA, hrows, :]
    k_t = rw_s[T_KMOD, hrows, :]
    r_t = rw_s[T_R, hrows, :]

    def value_row(vi, carry):
        krows = pl.ds(pl.multiple_of(vi * HEAD_DIM, HEAD_DIM), HEAD_DIM)
        st = wkv_ref[krows, :]
        sa = jnp.sum(st * nkk_t, axis=0, keepdims=True)
        st_new = st * w_t + sa * kka_t + rw_s[T_V, pl.ds(j * HEAD_DIM + vi, 1), :] * k_t
        wkvo_ref[krows, :] = st_new
        yt_s[pl.ds(j * HEAD_DIM + vi, 1), :] = jnp.sum(st_new * r_t, axis=0, keepdims=True)
        return carry

    lax.fori_loop(0, HEAD_DIM, value_row, 0, unroll=4)

    @pl.when(j == pl.num_programs(1) - 1)
    def _():
        ones_bd = _head_ones(B_WIDTH)
        x = ys_s[...]
        gb = proj_s[:, OFF_GB:OFF_GB + B_WIDTH]
        mix_s[:, A_WIDTH:A_WIDTH + B_WIDTH] = _rwkv_post(yt_s[...].T, rkv_s[0], rkv_s[1], rkv_s[2], gb, vec, ones_bd)
        xc = proj_s[:, OFF_XC:OFF_XC + C_WIDTH]
        gc = proj_s[:, OFF_GC:OFF_GC + C_WIDTH]
        u = vec[V_CB:V_CB + 1, :] + xc * vec[V_CW3:V_CW3 + 1, :]
        for i in range(CONV_W - 1):
            u = u + conv_ref[i] * vec[V_CW0 + i:V_CW0 + i + 1, :]
        a_l, mult, gi = _lru_gates(u, vec, wgate_ref[...])
        h_l = a_l * hl_ref[...] + mult * gi * u
        hlo_ref[...] = h_l
        convo_ref[0] = conv_ref[1]
        convo_ref[1] = conv_ref[2]
        convo_ref[2] = xc
        mix_s[:, A_WIDTH + B_WIDTH:] = h_l * _silu(gc)
        yo = _dot(_bf(mix_s[...]), wout_ref[...])
        y = x + mod_ref[2] * (_rms_scale(yo) * npost_ref[...])
        ys_s[...] = y
        y_ref[...] = y


def _sample_layers(x, mod, wts, cos_r, sin_r, kt, vt, sprev_t, wkv_t, conv_t, hl):
    nb = x.shape[0]
    bb = nb // B_HEADS
    n_slot = kt.shape[2]
    n_hv = HEAD_DIM * HEAD_DIM
    w_head, w_tail = _weight_specs(lambda l, j: l, 2)
    const2 = lambda l, j: (0, 0)
    lay3 = lambda l, j: (l, 0, 0)
    return pl.pallas_call(
        _sample_kernel,
        grid=(DEPTH, B_HEADS),
        in_specs=[
            pl.BlockSpec(memory_space=pltpu.SMEM),
            pl.BlockSpec((nb, D_MODEL), const2),
            pl.BlockSpec((None, 3, nb, D_MODEL), lambda l, j: (l, 0, 0, 0)),
            *w_head,
            pl.BlockSpec((1, LANES), const2),
            pl.BlockSpec((1, LANES), const2),
            *w_tail,
            pl.BlockSpec((None, bb * KV_WIDTH, n_slot), lambda l, j: (l, j, 0)),
            pl.BlockSpec((None, bb * KV_WIDTH, n_slot), lambda l, j: (l, j, 0)),
            pl.BlockSpec((None, B_SHIFT, nb), lay3),
            pl.BlockSpec((None, n_hv, nb), lambda l, j: (l, j, 0)),
            pl.BlockSpec((None, CONV_W - 1, nb, C_WIDTH), lambda l, j: (l, 0, 0, 0)),
            pl.BlockSpec((None, nb, C_WIDTH), lay3),
        ],
        out_specs=[
            pl.BlockSpec((nb, D_MODEL), const2),
            pl.BlockSpec((None, KV_WIDTH, nb), lay3),
            pl.BlockSpec((None, KV_WIDTH, nb), lay3),
            pl.BlockSpec((None, B_SHIFT, nb), lay3),
            pl.BlockSpec((None, n_hv, nb), lambda l, j: (l, j, 0)),
            pl.BlockSpec((None, CONV_W - 1, nb, C_WIDTH), lambda l, j: (l, 0, 0, 0)),
            pl.BlockSpec((None, nb, C_WIDTH), lay3),
        ],
        out_shape=[
            jax.ShapeDtypeStruct((nb, D_MODEL), F32),
            jax.ShapeDtypeStruct((DEPTH, KV_WIDTH, nb), F32),
            jax.ShapeDtypeStruct((DEPTH, KV_WIDTH, nb), F32),
            jax.ShapeDtypeStruct((DEPTH, B_SHIFT, nb), F32),
            jax.ShapeDtypeStruct((DEPTH, B_HEADS * n_hv, nb), F32),
            jax.ShapeDtypeStruct((DEPTH, CONV_W - 1, nb, C_WIDTH), F32),
            jax.ShapeDtypeStruct((DEPTH, nb, C_WIDTH), F32),
        ],
        scratch_shapes=[
            pltpu.VMEM((nb, D_MODEL), F32),
            pltpu.VMEM((nb, P_PAD), F32),
            pltpu.VMEM((nb, D_MODEL), F32),
            pltpu.VMEM((A_HEADS * bb, LANES), F32),
            pltpu.VMEM((A_HEADS * bb, LANES), F32),
            pltpu.VMEM((nb, KV_WIDTH), F32),
            pltpu.VMEM((6, B_WIDTH, nb), F32),
            pltpu.VMEM((3, nb, B_WIDTH), F32),
            pltpu.VMEM((B_WIDTH, nb), F32),
        ],
        compiler_params=pltpu.CompilerParams(
            dimension_semantics=("arbitrary", "arbitrary"), vmem_limit_bytes=VMEM_LIMIT_BYTES),
        name="sample_layers",
    )(wts["sinks"], x, mod, wts["norm_pre"], wts["norm_post"], wts["w_in"], wts["w_out"], cos_r, sin_r,
      wts["mu"], wts["vec"], wts["wlora"], wts["wgate"], kt, vt, sprev_t, wkv_t, conv_t, hl)


def _rope_lanes():
    half = HEAD_DIM // 2
    inv_freq = ROPE_THETA ** (-jnp.arange(half, dtype=F32) / half)
    freq = jnp.tile(inv_freq, LANES // half)
    sign = jnp.tile(jnp.concatenate([-jnp.ones((half,), F32), jnp.ones((half,), F32)]), LANES // HEAD_DIM)
    return freq, sign


def _rope_row(pos):
    freq, sign = _rope_lanes()
    ang = jnp.float32(pos) * freq[None, :]
    return jnp.cos(ang), jnp.sin(ang) * sign[None, :]


def _rope_tables(seq):
    freq, sign = _rope_lanes()
    hi = (jnp.arange(seq // Q_BLOCK, dtype=F32) * Q_BLOCK)[:, None] * freq[None, :]
    lo = jnp.arange(Q_BLOCK, dtype=F32)[:, None] * freq[None, :]
    ch, sh = jnp.cos(hi)[:, None, :], jnp.sin(hi)[:, None, :]
    cl, sl = jnp.cos(lo)[None, :, :], jnp.sin(lo)[None, :, :]
    cos_t = (ch * cl - sh * sl).reshape(seq, LANES)
    sin_t = ((sh * cl + ch * sl) * sign).reshape(seq, LANES)
    return cos_t, sin_t


def _block_diag(w):
    dl, n, d, e = w.shape
    eye = jnp.eye(n, dtype=w.dtype)
    return (eye[None, :, None, :, None] * w[:, :, :, None, :]).reshape(dl, n * d, n * e)


def _prep_weights(p):
    dl = p["w_in"].shape[0]
    w_in_t = jnp.swapaxes(p["w_in"], 1, 2)
    q_scale = jnp.where(jnp.arange(w_in_t.shape[1]) < A_WIDTH, Q_SCALE, 1.0).astype(F32)
    w_in_t = (w_in_t * q_scale[None, :, None]).astype(BF16)
    z32 = jnp.zeros((dl, LORA, B_WIDTH), F32)
    wlora = jnp.concatenate([
        jnp.concatenate([p["rwkv_w_up"], z32], axis=2),
        jnp.concatenate([z32, p["rwkv_a_up"]], axis=2),
        jnp.zeros((dl, LANES - 2 * LORA, 2 * B_WIDTH), F32)], axis=1).astype(BF16)
    wgate = jnp.concatenate([_block_diag(p["lru_gate_a_w"]), _block_diag(p["lru_gate_x_w"])], axis=2).astype(BF16)
    cw = p["lru_conv_w"]
    rows = [p["rwkv_w0"], p["rwkv_a0"], p["rwkv_k_k"], p["rwkv_k_a"], p["rwkv_lnx_w"], p["rwkv_lnx_b"],
            p["rwkv_r_k"].reshape(dl, B_WIDTH), p["lru_conv_b"], p["lru_gate_a_b"], p["lru_gate_x_b"],
            p["lru_lambda"], cw[:, 0], cw[:, 1], cw[:, 2], cw[:, 3], jnp.zeros((dl, B_WIDTH), F32)]
    return {
        "sinks": p["attn_sinks"],
        "norm_pre": p["norm_pre"].reshape(dl, 1, D_MODEL),
        "norm_post": p["norm_post"].reshape(dl, 1, D_MODEL),
        "w_in": w_in_t,
        "w_out": p["w_out"].astype(BF16),
        "mu": jnp.concatenate([p["rwkv_mu"], jnp.zeros((dl, B_SHIFT_PAD - B_SHIFT), F32)],
                              axis=1).reshape(dl, 1, B_SHIFT_PAD),
        "vec": jnp.stack(rows, axis=1),
        "wlora": wlora,
        "wgate": wgate,
    }


def _forward(x_prompt, x_sample, c_prompt, c_sample, cache_swa_k, cache_swa_v, state_rwkv_shift,
             state_rwkv_wkv, state_lru_conv, state_lru_h, p, tb):
    bp, seq, _ = x_prompt.shape
    nb = x_sample.shape[0]
    n_slot = cache_swa_k.shape[2]
    assert nb % SUBLANES == 0 and nb % B_HEADS == 0 and bp <= SUBLANES and seq % tb == 0

    pad_rows = (-(nb + bp)) % SUBLANES
    c_all = jnp.concatenate([c_sample, c_prompt, jnp.zeros((pad_rows, D_MODEL), F32)], axis=0)
    mod = _mod_call(c_all, p["w_mod"], p["b_mod"])
    wts = _prep_weights(p)

    cos_p, sin_p = _rope_tables(seq)
    cos_s, sin_s = _rope_row(PAST_LEN)

    kt = jnp.transpose(cache_swa_k, (0, 1, 3, 4, 2)).reshape(DEPTH, nb * KV_WIDTH, n_slot)
    vt = jnp.transpose(cache_swa_v, (0, 1, 3, 4, 2)).reshape(DEPTH, nb * KV_WIDTH, n_slot)
    sprev_t = jnp.swapaxes(state_rwkv_shift, 1, 2)
    wkv_t = jnp.transpose(state_rwkv_wkv, (0, 2, 3, 4, 1)).reshape(DEPTH, B_HEADS * HEAD_DIM * HEAD_DIM, nb)
    conv_t = jnp.swapaxes(state_lru_conv, 1, 2)
    ys, knew_t, vnew_t, shift_t, wkvo_t, convo_t, hlo = _sample_layers(
        x_sample.reshape(nb, D_MODEL), mod, wts, cos_s, sin_s, kt, vt, sprev_t, wkv_t, conv_t, state_lru_h)
    outs_s = (
        jnp.transpose(knew_t.reshape(DEPTH, 1, A_KV_HEADS, HEAD_DIM, nb), (0, 4, 1, 2, 3)),
        jnp.transpose(vnew_t.reshape(DEPTH, 1, A_KV_HEADS, HEAD_DIM, nb), (0, 4, 1, 2, 3)),
        jnp.swapaxes(shift_t, 1, 2),
        jnp.transpose(wkvo_t.reshape(DEPTH, B_HEADS, HEAD_DIM, HEAD_DIM, nb), (0, 4, 1, 2, 3)),
        jnp.swapaxes(convo_t, 1, 2),
        hlo,
    )

    yp = x_prompt
    outs_p = []
    for l in range(DEPTH):
        yp, kv_p, sh_p, wkv_p, conv_p, h_p = _prompt_layer(l, yp, mod, wts, cos_p, sin_p, nb, tb)
        kv_p = jnp.transpose(kv_p.reshape(bp, 2, A_KV_HEADS, HEAD_DIM, WINDOW), (1, 0, 4, 2, 3))
        outs_p.append((
            kv_p[0], kv_p[1],
            sh_p[:, SUBLANES - 1, :B_SHIFT],
            jnp.stack([wkv_p[:, HEAD_DIM * hd:HEAD_DIM * (hd + 1), HEAD_DIM * hd:HEAD_DIM * (hd + 1)]
                       for hd in range(B_HEADS)], axis=1),
            conv_p[:, SUBLANES - (CONV_W - 1):, :],
            h_p[:, SUBLANES - 1, :],
        ))
    sp = [jnp.stack(z) for z in zip(*outs_p)]
    return (yp, ys.reshape(nb, 1, D_MODEL), *sp, *outs_s)


def kernel(x_prompt, x_sample, c_prompt, c_sample, cache_swa_k, cache_swa_v, state_rwkv_shift, state_rwkv_wkv, state_lru_conv, state_lru_h, norm_pre, norm_post, w_mod, b_mod, w_in, w_out, attn_sinks, rwkv_mu, rwkv_w0, rwkv_w_up, rwkv_a0, rwkv_a_up, rwkv_k_k, rwkv_k_a, rwkv_r_k, rwkv_lnx_w, rwkv_lnx_b, lru_conv_w, lru_conv_b, lru_gate_a_w, lru_gate_a_b, lru_gate_x_w, lru_gate_x_b, lru_lambda):
    p = dict(norm_pre=norm_pre, norm_post=norm_post, w_mod=w_mod, b_mod=b_mod, w_in=w_in, w_out=w_out,
             attn_sinks=attn_sinks, rwkv_mu=rwkv_mu, rwkv_w0=rwkv_w0, rwkv_w_up=rwkv_w_up, rwkv_a0=rwkv_a0,
             rwkv_a_up=rwkv_a_up, rwkv_k_k=rwkv_k_k, rwkv_k_a=rwkv_k_a, rwkv_r_k=rwkv_r_k,
             rwkv_lnx_w=rwkv_lnx_w, rwkv_lnx_b=rwkv_lnx_b, lru_conv_w=lru_conv_w, lru_conv_b=lru_conv_b,
             lru_gate_a_w=lru_gate_a_w, lru_gate_a_b=lru_gate_a_b, lru_gate_x_w=lru_gate_x_w,
             lru_gate_x_b=lru_gate_x_b, lru_lambda=lru_lambda)
    return _forward(x_prompt, x_sample, c_prompt, c_sample, cache_swa_k, cache_swa_v, state_rwkv_shift,
                    state_rwkv_wkv, state_lru_conv, state_lru_h, p, TIME_BLOCK)
```

```python
import functools

import numpy as np
import jax
import jax.numpy as jnp
from jax import lax
from jax.experimental import pallas as pl
from jax.experimental.pallas import tpu as pltpu

F32 = jnp.float32
BF16 = jnp.bfloat16

D_MODEL = 1024
DEPTH = 2
A_HEADS = 8
A_KV_HEADS = 2
HEAD_DIM = 64
A_WIDTH = A_HEADS * HEAD_DIM
KV_WIDTH = A_KV_HEADS * HEAD_DIM
WINDOW = 128
Q_BLOCK = 128
ROPE_THETA = 10000.0
B_WIDTH = 256
B_HEADS = 4
LORA = 32
B_SHIFT = 3 * B_WIDTH + 2 * LORA
B_SHIFT_PAD = 3 * B_WIDTH + 128
LNX_EPS = 1e-5 * 8 ** 2
C_WIDTH = 256
CONV_W = 4
LRU_C = 8.0
EPS = 1e-6
PAST_LEN = 8192
EXP_NEG_HALF = float(np.exp(-0.5))
LOG2_E = float(np.log2(np.e))
Q_SCALE = HEAD_DIM ** -0.5 * LOG2_E

LANES = 128
SUBLANES = 8
VMEM_LIMIT_BYTES = 56 * 1024 * 1024

OFF_Q = 0
OFF_K = OFF_Q + A_WIDTH
OFF_V = OFF_K + KV_WIDTH
OFF_GA = OFF_V + KV_WIDTH
OFF_PB = OFF_GA + A_WIDTH
OFF_GB = OFF_PB + B_SHIFT_PAD
OFF_XC = OFF_GB + B_WIDTH
OFF_GC = OFF_XC + C_WIDTH
P_PAD = OFF_GC + C_WIDTH
P_IN = P_PAD - (B_SHIFT_PAD - B_SHIFT)

CHUNK = 64
TIME_BLOCK = 512
SEQS_PER_STEP = 1
PROJ_TILE = 256

(V_W0, V_A0, V_KK, V_KA, V_LNW, V_LNB, V_RK, V_CB, V_GAB, V_GXB, V_LAM,
 V_CW0, V_CW1, V_CW2, V_CW3) = range(15)
VEC_ROWS = 16

(T_DECAY, T_NKK, T_KKA, T_KMOD, T_R, T_V) = range(6)

_NT = (((1,), (1,)), ((), ()))
_TN = (((0,), (0,)), ((), ()))


def _bf(x):
    return x.astype(BF16)


def _dot(a, b):
    return jnp.dot(a, b, preferred_element_type=F32)


def _dot_b(a, b):
    return jnp.dot(a, b, preferred_element_type=F32).astype(BF16)


def _dot_nt(a, b):
    return lax.dot_general(a, b, _NT, preferred_element_type=F32)


def _dot_tn(a, b):
    return lax.dot_general(a, b, _TN, preferred_element_type=F32)


def _iota(shape, dim):
    return lax.broadcasted_iota(jnp.int32, shape, dim)


def _sigmoid(x):
    return jax.nn.sigmoid(x)


def _silu(x):
    return x * _sigmoid(x)


def _softplus(x):
    return jnp.maximum(x, 0.0) + jnp.log1p(jnp.exp(-jnp.abs(x)))


def _head_ones(n):
    r = _iota((n, n), 0) // HEAD_DIM
    c = _iota((n, n), 1) // HEAD_DIM
    return jnp.where(r == c, 1.0, 0.0).astype(BF16)


def _head_sum(x, ones_bd):
    return _dot(_bf(x), ones_bd)


def _rms_scale(x):
    ms = jnp.mean(x * x, axis=-1, keepdims=True)
    return x * lax.rsqrt(ms + EPS)


def _affine_scan(a, b, h0):
    rows, n = a.shape
    groups = rows // SUBLANES
    sub = _iota((rows, n), 0) % SUBLANES
    d = 1
    while d < SUBLANES:
        keep = sub >= d
        a_sh = jnp.where(keep, pltpu.roll(a, d, 0), 1.0)
        b_sh = jnp.where(keep, pltpu.roll(b, d, 0), 0.0)
        b = b + a * b_sh
        a = a * a_sh
        d *= 2
    outs = []
    carry = h0
    for g in range(groups):
        rows_g = slice(g * SUBLANES, (g + 1) * SUBLANES)
        hg = b[rows_g] + a[rows_g] * carry
        outs.append(hg)
        carry = hg[SUBLANES - 1:SUBLANES, :]
    return jnp.concatenate(outs, axis=0)


def _cumsum_chunks(x, chunk):
    rows, n = x.shape
    sub = _iota((rows, n), 0) % SUBLANES
    d = 1
    while d < SUBLANES:
        x = x + jnp.where(sub >= d, pltpu.roll(x, d, 0), 0.0)
        d *= 2
    outs = []
    for g in range(rows // SUBLANES):
        xg = x[g * SUBLANES:(g + 1) * SUBLANES]
        if (g * SUBLANES) % chunk != 0:
            xg = xg + outs[-1][SUBLANES - 1:SUBLANES, :]
        outs.append(xg)
    return jnp.concatenate(outs, axis=0)


def _rope128(z, cos, sin_signed):
    lane = _iota(z.shape, 1)
    first = (lane & 32) == 0
    sw = jnp.where(first, pltpu.roll(z, 96, 1), pltpu.roll(z, 32, 1))
    return z * cos + sw * sin_signed


def _rwkv_pre(xs, vec, wlora, ones_bd):
    r = xs[:, 0:B_WIDTH]
    k = xs[:, B_WIDTH:2 * B_WIDTH]
    v = xs[:, 2 * B_WIDTH:3 * B_WIDTH]
    lor = xs[:, 3 * B_WIDTH:B_SHIFT_PAD]
    lane = _iota(lor.shape, 1)
    z = jnp.where(lane < LORA, jnp.tanh(lor), lor)
    wa = _dot(_bf(z), wlora)
    zw = vec[V_W0:V_W0 + 1, :] + wa[:, :B_WIDTH]
    ld = -EXP_NEG_HALF * _sigmoid(zw)
    a = _sigmoid(vec[V_A0:V_A0 + 1, :] + wa[:, B_WIDTH:])
    kk = k * vec[V_KK:V_KK + 1, :]
    kkn = kk * jnp.minimum(lax.rsqrt(_head_sum(kk * kk, ones_bd)), 1e12)
    kmod = k * (1.0 + (a - 1.0) * vec[V_KA:V_KA + 1, :])
    return r, k, v, ld, a, kkn, kmod


def _rwkv_post(y, r, kmod, v, gb, vec, ones_bd):
    inv = 1.0 / HEAD_DIM
    mu = _head_sum(y, ones_bd) * inv
    yc = y - mu
    var = _head_sum(yc * yc, ones_bd) * inv
    yn = yc * lax.rsqrt(var + LNX_EPS)
    yn = yn * vec[V_LNW:V_LNW + 1, :] + vec[V_LNB:V_LNB + 1, :]
    bonus = _head_sum(r * kmod * vec[V_RK:V_RK + 1, :], ones_bd) * v
    return (yn + bonus) * _silu(gb)


def _lru_gates(u, vec, wgate):
    gates = _dot(_bf(u), wgate)
    gr = _sigmoid(gates[:, :C_WIDTH] + vec[V_GAB:V_GAB + 1, :])
    gi = _sigmoid(gates[:, C_WIDTH:] + vec[V_GXB:V_GXB + 1, :])
    sp = _softplus(-vec[V_LAM:V_LAM + 1, :])
    log_a = -LRU_C * gr * sp
    a = jnp.exp(log_a)
    mult = jnp.sqrt(1.0 - a * a)
    return a, mult, gi


def _project(h, win_ref, proj_s, group):
    src, dst, n = group
    proj_s[:, dst:dst + n] = _dot_nt(h, win_ref[src:src + n, :])


_DELTA = B_SHIFT_PAD - B_SHIFT
GRP_PB = (OFF_PB, OFF_PB, B_SHIFT_PAD)
GRP_QKV = (OFF_Q, OFF_Q, OFF_GA - OFF_Q)
GRP_GA = (OFF_GA, OFF_GA, A_WIDTH)
GRP_GB = (OFF_GB - _DELTA, OFF_GB, B_WIDTH)
GRP_XC = (OFF_XC - _DELTA, OFF_XC, C_WIDTH)
GRP_GC = (OFF_GC - _DELTA, OFF_GC, C_WIDTH)


def _mod_kernel(c_ref, w_ref, b_ref, o_ref):
    o_ref[...] = _dot(_bf(_silu(c_ref[...])), _bf(w_ref[...])) + b_ref[...]


def _mod_call(c_all, w_mod, b_mod):
    rows = c_all.shape[0]
    return pl.pallas_call(
        _mod_kernel,
        grid=(DEPTH, 3),
        in_specs=[
            pl.BlockSpec((rows, D_MODEL), lambda l, j: (0, 0)),
            pl.BlockSpec((None, D_MODEL, D_MODEL), lambda l, j: (l, 0, j)),
            pl.BlockSpec((None, 1, D_MODEL), lambda l, j: (l, 0, j)),
        ],
        out_specs=pl.BlockSpec((None, None, rows, D_MODEL), lambda l, j: (l, j, 0, 0)),
        out_shape=jax.ShapeDtypeStruct((DEPTH, 3, rows, D_MODEL), F32),
        compiler_params=pltpu.CompilerParams(
            dimension_semantics=("arbitrary", "arbitrary"), vmem_limit_bytes=VMEM_LIMIT_BYTES),
        name="adaln_mod",
    )(c_all, w_mod, b_mod.reshape(DEPTH, 1, 3 * D_MODEL))


def _prompt_kernel(sinks_ref, x_ref, mod_ref, npre_ref, npost_ref, win_ref, wout_ref,
                   cos_ref, sin_ref, mu_ref, vec_ref, wlora_ref, wgate_ref,
                   y_ref, kv_ref, shift_ref, wkv_ref, conv_ref, hlru_ref,
                   *scratch, layer):
    n_seq = x_ref.shape[0]
    tb = x_ref.shape[1]
    t = pl.program_id(1)
    per_seq = len(scratch) // n_seq
    seq_scratch = [scratch[q * per_seq:(q + 1) * per_seq] for q in range(n_seq)]

    @pl.when(t == 0)
    def _():
        for kvprev, pprev, xcprev, hprev, gstate, _, _, _ in seq_scratch:
            kvprev[...] = jnp.zeros_like(kvprev)
            pprev[...] = jnp.zeros_like(pprev)
            xcprev[...] = jnp.zeros_like(xcprev)
            hprev[...] = jnp.zeros_like(hprev)
            gstate[...] = jnp.zeros_like(gstate)

    vec = vec_ref[...]
    ones_bd = _head_ones(B_WIDTH)
    cos = cos_ref[...]
    sin = sin_ref[...]
    nck = tb // CHUNK
    sls = [slice(c * CHUNK, (c + 1) * CHUNK) for c in range(nck)]
    rng = range(nck)
    pairs = [(c, i) for c in range(A_HEADS // 2) for i in range(tb // Q_BLOCK)]

    lo_kv = _iota((tb + WINDOW, KV_WIDTH), 1) < HEAD_DIM
    qi = _iota((Q_BLOCK, 2 * Q_BLOCK), 0)
    sj = _iota((Q_BLOCK, 2 * Q_BLOCK), 1)
    band = (sj >= qi) & (sj <= qi + WINDOW)
    first_lo = jnp.where(t == 0, Q_BLOCK, 0)
    neg_inf = -jnp.inf
    row_p = _iota((tb, B_SHIFT_PAD), 0)
    reset = (_iota((tb, C_WIDTH), 0) + t * tb) == 0

    ri = _iota((CHUNK, B_HEADS * CHUNK), 0)
    cj = _iota((CHUNK, B_HEADS * CHUNK), 1) % CHUNK
    strict_b = jnp.where(ri > cj, 1.0, 0.0).astype(BF16)
    incl_b = jnp.where(ri >= cj, 1.0, 0.0).astype(BF16)
    eye_b = jnp.where(ri == cj, 1.0, 0.0).astype(BF16)
    bd_heads = (_iota((B_WIDTH, B_WIDTH), 0) // HEAD_DIM) == (_iota((B_WIDTH, B_WIDTH), 1) // HEAD_DIM)

    def variants(z):
        zr = pltpu.roll(z, HEAD_DIM, 1)
        a_ = _bf(jnp.where(lo_kv, z, 0.0))
        b_ = _bf(jnp.where(lo_kv, 0.0, z))
        c_ = _bf(jnp.where(lo_kv, zr, 0.0))
        d_ = _bf(jnp.where(lo_kv, 0.0, zr))
        return ((a_, d_), (c_, b_))

    def win_stack(var, c, i):
        g, r0 = c // 2, i * Q_BLOCK
        return jnp.concatenate([var[g][0][r0:r0 + 2 * Q_BLOCK], var[g][1][r0:r0 + 2 * Q_BLOCK]], axis=0)

    def stk(z):
        zb = _bf(z)
        return jnp.concatenate([zb] * B_HEADS, axis=0) * ones_bd

    def collapse(z):
        return z[0:CHUNK] + z[CHUNK:2 * CHUNK] + z[2 * CHUNK:3 * CHUNK] + z[3 * CHUNK:4 * CHUNK]

    def side(q, out):
        kvprev, _, xcprev, hprev, _, proj, mix, _ = seq_scratch[q]

        k_rot = _rope128(proj[:, OFF_K:OFF_K + KV_WIDTH], cos, sin)
        v_att = proj[:, OFF_V:OFF_V + KV_WIDTH]
        kvar = variants(jnp.concatenate([kvprev[:, 0:KV_WIDTH], k_rot], axis=0))
        vvar = variants(jnp.concatenate([kvprev[:, KV_WIDTH:2 * KV_WIDTH], v_att], axis=0))
        out["k_tail"] = k_rot[tb - WINDOW:]
        out["v_tail"] = v_att[tb - WINDOW:]
        yield
        qcols = [_bf(_rope128(proj[:, OFF_Q + LANES * c:OFF_Q + LANES * (c + 1)], cos, sin))
                 for c in range(A_HEADS // 2)]
        scores = [_dot_nt(qcols[c][i * Q_BLOCK:(i + 1) * Q_BLOCK], win_stack(kvar, c, i))
                  for c, i in pairs]
        yield

        xc = proj[:, OFF_XC:OFF_XC + C_WIDTH]
        xext = jnp.concatenate([xcprev[...], xc], axis=0)
        u = vec[V_CB:V_CB + 1, :] + xc * vec[V_CW3:V_CW3 + 1, :]
        for j in range(1, CONV_W):
            u = u + pltpu.roll(xext, j, 0)[SUBLANES:, :] * vec[V_CW3 - j:V_CW3 - j + 1, :]
        a_l, mult, gi = _lru_gates(u, vec, wgate_ref[...])
        a_l = jnp.where(reset, 0.0, a_l)
        mult = jnp.where(reset, 1.0, mult)
        b_l = mult * gi * u
        out["xc_tail"] = xc[tb - SUBLANES:, :]
        yield

        probs = []
        for (c, i), s in zip(pairs, scores):
            mask = band & (sj >= first_lo) if i == 0 else band
            ps = []
            for hh in range(2):
                sink = sinks_ref[layer, 2 * c + hh] * LOG2_E
                sh = jnp.where(mask, s[:, 2 * Q_BLOCK * hh:2 * Q_BLOCK * (hh + 1)], neg_inf)
                m = jnp.maximum(jnp.max(sh, axis=-1, keepdims=True), sink)
                p = jnp.exp2(sh - m)
                den = jnp.sum(p, axis=-1, keepdims=True) + jnp.exp2(sink - m)
                ps.append(_bf(p * (1.0 / den)))
            probs.append(jnp.concatenate(ps, axis=1))
            if i == tb // Q_BLOCK - 1:
                yield
        for (c, i), p in zip(pairs, probs):
            r0 = i * Q_BLOCK
            o = _dot(p, win_stack(vvar, c, i))
            ga = proj[r0:r0 + Q_BLOCK, OFF_GA + LANES * c:OFF_GA + LANES * (c + 1)]
            mix[r0:r0 + Q_BLOCK, LANES * c:LANES * (c + 1)] = o * _silu(ga)
        yield
        h_l = _affine_scan(a_l, b_l, hprev[SUBLANES - 1:SUBLANES, :])
        gc = proj[:, OFF_GC:OFF_GC + C_WIDTH]
        mix[:, A_WIDTH + B_WIDTH:] = h_l * _silu(gc)
        out["h_tail"] = h_l[tb - SUBLANES:, :]

    def sequence(q, out):
        kvprev, pprev, xcprev, hprev, gstate, proj, mix, ywkv = seq_scratch[q]

        x = x_ref[q]
        brow = pl.ds(pl.program_id(0) * n_seq + q, 1)
        shift_m = mod_ref[0, brow, :]
        scale_m = mod_ref[1, brow, :]
        gate_m = mod_ref[2, brow, :]
        h = _bf(_rms_scale(x) * (npre_ref[...] * (1.0 + scale_m)) + shift_m)
        for src, dst, n in (GRP_PB, GRP_QKV, GRP_XC, GRP_GA, GRP_GB, GRP_GC):
            for off in range(0, n, PROJ_TILE):
                _project(h, win_ref, proj, (src + off, dst + off, min(PROJ_TILE, n - off)))
                yield

        pb = proj[:, OFF_PB:OFF_PB + B_SHIFT_PAD]
        prev = jnp.where(row_p == 0, pprev[SUBLANES - 1:SUBLANES, :], pltpu.roll(pb, 1, 0))
        xs = pb + (prev - pb) * mu_ref[...]
        yield
        r, k, v, ld, a, kkn, kmod = _rwkv_pre(xs, vec, wlora_ref[...], ones_bd)
        kka = kkn * a
        yield

        lcum = _cumsum_chunks(ld, CHUNK)
        e_l = jnp.exp(lcum)
        e_nl = jnp.exp(-lcum)
        rt = r * e_l
        at = -kkn * jnp.exp(lcum - ld)
        ktil = kmod * e_nl
        btil = kka * e_nl
        gam = [e_l[s.stop - 1:s.stop, :] for s in sls]
        khat = [ktil[sls[c]] * gam[c] for c in rng]
        bhat = [btil[sls[c]] * gam[c] for c in rng]
        yield
        a_s = [stk(at[s]) for s in sls]
        b_s = [stk(btil[s]) for s in sls]
        k_s = [stk(ktil[s]) for s in sls]
        yield
        v_s = [stk(v[s]) for s in sls]
        bh_s = [stk(bhat[c]) for c in rng]
        sc = [_dot_nt(_bf(jnp.concatenate([at[sls[c]], rt[sls[c]]], axis=0)),
                      jnp.concatenate([b_s[c], k_s[c]], axis=0)) for c in rng]
        yield "side may start"

        pw = [_bf(sc[c][0:CHUNK, 0:B_WIDTH]) * strict_b for c in rng]
        lak = [_bf(sc[c][0:CHUNK, B_WIDTH:]) * strict_b for c in rng]
        mrb = [_bf(sc[c][CHUNK:, 0:B_WIDTH]) * incl_b for c in rng]
        mrk = [_bf(sc[c][CHUNK:, B_WIDTH:]) * incl_b for c in rng]
        tinv = [pw[c] + eye_b for c in rng]
        pw = [_dot_b(pw[c], stk(pw[c])) for c in rng]
        yield
        for _ in range(4):
            tinv = [_dot_b(tinv[c], stk(pw[c] + eye_b)) for c in rng]
            pw = [_dot_b(pw[c], stk(pw[c])) for c in rng]
            yield
        tinv = [_dot_b(tinv[c], stk(pw[c] + eye_b)) for c in rng]
        lv = [_dot_b(lak[c], v_s[c]) for c in rng]
        yield
        w_c = [_dot_b(tinv[c], a_s[c]) for c in rng]
        z_c = [_dot(tinv[c], stk(lv[c])) for c in rng]
        yield
        w_s = [stk(w_c[c]) for c in rng]
        rp = [_bf(rt[sls[c]] + _dot(mrb[c], w_s[c])) for c in rng]
        y0 = [_dot(jnp.concatenate([mrb[c], mrk[c]], axis=1), jnp.concatenate([stk(z_c[c]), v_s[c]], axis=0))
              for c in rng]
        pp = [_bf(_dot_tn(w_s[c], bh_s[c])) for c in rng]
        qq = [jnp.where(bd_heads,
                        _dot_tn(_bf(jnp.concatenate([z_c[c], v[sls[c]]], axis=0)),
                                _bf(jnp.concatenate([bhat[c], khat[c]], axis=0))), 0.0) for c in rng]
        yield
        gmat = gstate[...]
        for c in rng:
            gb16 = _bf(gmat)
            ywkv[sls[c], :] = _dot_nt(rp[c], gb16) + y0[c]
            gmat = gam[c] * gmat + _dot(gb16, pp[c]) + qq[c]
        gstate[...] = gmat
        yield "side must be done"

        gb = proj[:, OFF_GB:OFF_GB + B_WIDTH]
        mix[:, A_WIDTH:A_WIDTH + B_WIDTH] = _rwkv_post(ywkv[...], r, kmod, v, gb, vec, ones_bd)

        yo = _dot(_bf(mix[...]), wout_ref[...])
        y_ref[q] = x + (gate_m * npost_ref[...]) * _rms_scale(yo)

        k_tail, v_tail = out["k_tail"], out["v_tail"]
        kvprev[:, 0:KV_WIDTH] = k_tail
        kvprev[:, KV_WIDTH:2 * KV_WIDTH] = v_tail
        pprev[...] = pb[tb - SUBLANES:, :]
        xcprev[...] = out["xc_tail"]
        hprev[...] = out["h_tail"]
        shift_ref[q] = pb[tb - SUBLANES:, :]
        wkv_ref[q] = gmat
        conv_ref[q] = out["xc_tail"]
        hlru_ref[q] = out["h_tail"]

        @pl.when(t == pl.num_programs(1) - 1)
        def _():
            kv_ref[q, 0] = k_tail.T
            kv_ref[q, 1] = v_tail.T

    for q in range(n_seq):
        shared = {}
        main, extra = sequence(q, shared), side(q, shared)
        while next(main) != "side may start":
            pass
        extra_done = False
        while True:
            if not extra_done:
                try:
                    next(extra)
                except StopIteration:
                    extra_done = True
            if next(main) == "side must be done":
                break
        for _ in extra:
            pass
        for _ in main:
            pass


def _weight_specs(layer_of, grid_rank):
    def lay(*shape):
        zeros = (0,) * len(shape)
        if grid_rank == 1:
            return pl.BlockSpec((None, *shape), lambda i: (layer_of(i), *zeros))
        return pl.BlockSpec((None, *shape), lambda a, b: (layer_of(a, b), *zeros))
    return [
        lay(1, D_MODEL),
        lay(1, D_MODEL),
        lay(P_IN, D_MODEL),
        lay(D_MODEL, D_MODEL),
    ], [
        lay(1, B_SHIFT_PAD),
        lay(VEC_ROWS, B_WIDTH),
        lay(LANES, 2 * B_WIDTH),
        lay(C_WIDTH, 2 * C_WIDTH),
    ]


def _prompt_layer(layer, x, mod, wts, cos_t, sin_t, mod_row0, tb):
    bsz, seq, _ = x.shape
    nt = seq // tb
    sps = SEQS_PER_STEP
    per_b3 = lambda b, t: (b, 0, 0)
    w_head, w_tail = _weight_specs(lambda b, t: layer, 2)
    return pl.pallas_call(
        functools.partial(_prompt_kernel, layer=layer),
        grid=(bsz // sps, nt),
        in_specs=[
            pl.BlockSpec(memory_space=pltpu.SMEM),
            pl.BlockSpec((sps, tb, D_MODEL), lambda b, t: (b, t, 0)),
            pl.BlockSpec((None, 3, SUBLANES, D_MODEL),
                         lambda b, t: (layer, 0, mod_row0 // SUBLANES, 0)),
            *w_head,
            pl.BlockSpec((tb, LANES), lambda b, t: (t, 0)),
            pl.BlockSpec((tb, LANES), lambda b, t: (t, 0)),
            *w_tail,
        ],
        out_specs=[
            pl.BlockSpec((sps, tb, D_MODEL), lambda b, t: (b, t, 0)),
            pl.BlockSpec((sps, 2, KV_WIDTH, WINDOW), lambda b, t: (b, 0, 0, 0)),
            pl.BlockSpec((sps, SUBLANES, B_SHIFT_PAD), per_b3),
            pl.BlockSpec((sps, B_WIDTH, B_WIDTH), per_b3),
            pl.BlockSpec((sps, SUBLANES, C_WIDTH), per_b3),
            pl.BlockSpec((sps, SUBLANES, C_WIDTH), per_b3),
        ],
        out_shape=[
            jax.ShapeDtypeStruct((bsz, seq, D_MODEL), F32),
            jax.ShapeDtypeStruct((bsz, 2, KV_WIDTH, WINDOW), F32),
            jax.ShapeDtypeStruct((bsz, SUBLANES, B_SHIFT_PAD), F32),
            jax.ShapeDtypeStruct((bsz, B_WIDTH, B_WIDTH), F32),
            jax.ShapeDtypeStruct((bsz, SUBLANES, C_WIDTH), F32),
            jax.ShapeDtypeStruct((bsz, SUBLANES, C_WIDTH), F32),
        ],
        scratch_shapes=[
            pltpu.VMEM((WINDOW, 2 * KV_WIDTH), F32),
            pltpu.VMEM((SUBLANES, B_SHIFT_PAD), F32),
            pltpu.VMEM((SUBLANES, C_WIDTH), F32),
            pltpu.VMEM((SUBLANES, C_WIDTH), F32),
            pltpu.VMEM((B_WIDTH, B_WIDTH), F32),
            pltpu.VMEM((tb, P_PAD), F32),
            pltpu.VMEM((tb, D_MODEL), F32),
            pltpu.VMEM((tb, B_WIDTH), F32),
        ] * sps,
        compiler_params=pltpu.CompilerParams(
            dimension_semantics=("arbitrary", "arbitrary"), vmem_limit_bytes=VMEM_LIMIT_BYTES),
        name="prompt_layer",
    )(wts["sinks"], x, mod, wts["norm_pre"], wts["norm_post"], wts["w_in"], wts["w_out"], cos_t, sin_t,
      wts["mu"], wts["vec"], wts["wlora"], wts["wgate"])


def _sample_kernel(sinks_ref, x_ref, mod_ref, npre_ref, npost_ref, win_ref, wout_ref,
                   cos_ref, sin_ref, mu_ref, vec_ref, wlora_ref, wgate_ref,
                   kt_ref, vt_ref, sprev_ref, wkv_ref, conv_ref, hl_ref,
                   y_ref, knew_ref, vnew_ref, shift_ref, wkvo_ref, convo_ref, hlo_ref,
                   ys_s, proj_s, mix_s, q_s, o_s, knew_s, rw_s, rkv_s, yt_s):
    layer = pl.program_id(0)
    j = pl.program_id(1)
    nb = x_ref.shape[0]
    bb = nb // B_HEADS
    n_slot = kt_ref.shape[1]
    vec = vec_ref[...]
    cos = cos_ref[...]
    sin = sin_ref[...]

    @pl.when(jnp.logical_and(layer == 0, j == 0))
    def _():
        ys_s[...] = x_ref[...]

    @pl.when(j == 0)
    def _():
        ones_bd = _head_ones(B_WIDTH)
        x = ys_s[...]
        h = _bf(_rms_scale(x) * npre_ref[...] * (1.0 + mod_ref[1]) + mod_ref[0])
        for grp in (GRP_QKV, GRP_GA, GRP_PB, GRP_GB, GRP_XC, GRP_GC):
            _project(h, win_ref, proj_s, grp)
        k_new = _rope128(proj_s[:, OFF_K:OFF_K + KV_WIDTH], cos, sin)
        knew_s[...] = k_new
        knew_ref[...] = k_new.T
        vnew_ref[...] = proj_s[:, OFF_V:OFF_V + KV_WIDTH].T
        pb = proj_s[:, OFF_PB:OFF_PB + B_SHIFT_PAD]
        pb_t = pb.T
        shift_ref[...] = pb_t[0:B_SHIFT, :]
        prev = jnp.concatenate([sprev_ref[...], jnp.zeros((B_SHIFT_PAD - B_SHIFT, nb), F32)], axis=0).T
        xs = pb + (prev - pb) * mu_ref[...]
        r, k, v, ld, a, kkn, kmod = _rwkv_pre(xs, vec, wlora_ref[...], ones_bd)
        rw_s[T_DECAY] = jnp.exp(ld).T
        rw_s[T_NKK] = (-kkn).T
        rw_s[T_KKA] = (kkn * a).T
        rw_s[T_KMOD] = kmod.T
        rw_s[T_R] = r.T
        rw_s[T_V] = v.T
        rkv_s[0] = r
        rkv_s[1] = kmod
        rkv_s[2] = v

    rs = pl.ds(pl.multiple_of(j * bb, SUBLANES), bb)
    lane = _iota((bb, LANES), 1)
    lo = lane < HEAD_DIM
    for c in range(A_HEADS // 2):
        g = c // 2
        in_g = lo if g == 0 else jnp.logical_not(lo)
        qcol = _rope128(proj_s[rs, OFF_Q + LANES * c:OFF_Q + LANES * (c + 1)], cos, sin)
        qrol = pltpu.roll(qcol, HEAD_DIM, 1)
        for hh in range(2):
            hd = 2 * c + hh
            q_s[hd * bb:(hd + 1) * bb, :] = jnp.where(in_g, qcol if hh == g else qrol, 0.0)
    knew_blk = knew_s[rs, :]
    vnew_blk = proj_s[rs, OFF_V:OFF_V + KV_WIDTH]
    hrow = _iota((A_HEADS, 1), 0)
    sink_col = jnp.zeros((A_HEADS, 1), F32)
    for hd in range(A_HEADS):
        sink_col = jnp.where(hrow == hd, sinks_ref[layer, hd] * LOG2_E, sink_col)

    qbs = [q_s[pl.ds(b, A_HEADS, stride=bb), :] for b in range(bb)]
    s_c = [_dot(_bf(qbs[b]), _bf(kt_ref[b * KV_WIDTH:(b + 1) * KV_WIDTH, :])) for b in range(bb)]
    s_n = [jnp.sum(qbs[b] * knew_blk[b:b + 1, :], axis=-1, keepdims=True) for b in range(bb)]
    p_c, p_n = [], []
    for b in range(bb):
        m = jnp.maximum(jnp.maximum(jnp.max(s_c[b], axis=-1, keepdims=True), s_n[b]), sink_col)
        e_c = jnp.exp2(s_c[b] - m)
        e_n = jnp.exp2(s_n[b] - m)
        inv = 1.0 / (jnp.sum(e_c, axis=-1, keepdims=True) + e_n + jnp.exp2(sink_col - m))
        p_c.append(_bf(e_c * inv))
        p_n.append(e_n * inv)
    for b in range(bb):
        o = _dot_nt(p_c[b], _bf(vt_ref[b * KV_WIDTH:(b + 1) * KV_WIDTH, :]))
        o_s[pl.ds(b, A_HEADS, stride=bb), :] = o + p_n[b] * vnew_blk[b:b + 1, :]
    for c in range(A_HEADS // 2):
        g = c // 2
        halves = []
        for hh in range(2):
            oh = o_s[(2 * c + hh) * bb:(2 * c + hh + 1) * bb, :]
            halves.append(oh if hh == g else pltpu.roll(oh, HEAD_DIM, 1))
        ga = proj_s[rs, OFF_GA + LANES * c:OFF_GA + LANES * (c + 1)]
        mix_s[rs, LANES * c:LANES * (c + 1)] = jnp.where(lo, halves[0], halves[1]) * _silu(ga)

    hrows = pl.ds(pl.multiple_of(j * HEAD_DIM, HEAD_DIM), HEAD_DIM)
    w_t = rw_s[T_DECAY, hrows, :]
    nkk_t = rw_s[T_NKK, hrows, :]
    kka_t = rw_s[T_KKA, hrows, :]
    k_t = rw_s[T_KMOD, hrows, :]
    r_t = rw_s[T_R, hrows, :]

    def value_row(vi, carry):
        krows = pl.ds(pl.multiple_of(vi * HEAD_DIM, HEAD_DIM), HEAD_DIM)
        st = wkv_ref[krows, :]
        sa = jnp.sum(st * nkk_t, axis=0, keepdims=True)
        st_new = st * w_t + sa * kka_t + rw_s[T_V, pl.ds(j * HEAD_DIM + vi, 1), :] * k_t
        wkvo_ref[krows, :] = st_new
        yt_s[pl.ds(j * HEAD_DIM + vi, 1), :] = jnp.sum(st_new * r_t, axis=0, keepdims=True)
        return carry

    lax.fori_loop(0, HEAD_DIM, value_row, 0, unroll=4)

    @pl.when(j == pl.num_programs(1) - 1)
    def _():
        ones_bd = _head_ones(B_WIDTH)
        x = ys_s[...]
        gb = proj_s[:, OFF_GB:OFF_GB + B_WIDTH]
        mix_s[:, A_WIDTH:A_WIDTH + B_WIDTH] = _rwkv_post(yt_s[...].T, rkv_s[0], rkv_s[1], rkv_s[2], gb, vec, ones_bd)
        xc = proj_s[:, OFF_XC:OFF_XC + C_WIDTH]
        gc = proj_s[:, OFF_GC:OFF_GC + C_WIDTH]
        u = vec[V_CB:V_CB + 1, :] + xc * vec[V_CW3:V_CW3 + 1, :]
        for i in range(CONV_W - 1):
            u = u + conv_ref[i] * vec[V_CW0 + i:V_CW0 + i + 1, :]
        a_l, mult, gi = _lru_gates(u, vec, wgate_ref[...])
        h_l = a_l * hl_ref[...] + mult * gi * u
        hlo_ref[...] = h_l
        convo_ref[0] = conv_ref[1]
        convo_ref[1] = conv_ref[2]
        convo_ref[2] = xc
        mix_s[:, A_WIDTH + B_WIDTH:] = h_l * _silu(gc)
        yo = _dot(_bf(mix_s[...]), wout_ref[...])
        y = x + mod_ref[2] * (_rms_scale(yo) * npost_ref[...])
        ys_s[...] = y
        y_ref[...] = y


def _sample_layers(x, mod, wts, cos_r, sin_r, kt, vt, sprev_t, wkv_t, conv_t, hl):
    nb = x.shape[0]
    bb = nb // B_HEADS
    n_slot = kt.shape[2]
    n_hv = HEAD_DIM * HEAD_DIM
    w_head, w_tail = _weight_specs(lambda l, j: l, 2)
    const2 = lambda l, j: (0, 0)
    lay3 = lambda l, j: (l, 0, 0)
    return pl.pallas_call(
        _sample_kernel,
        grid=(DEPTH, B_HEADS),
        in_specs=[
            pl.BlockSpec(memory_space=pltpu.SMEM),
            pl.BlockSpec((nb, D_MODEL), const2),
            pl.BlockSpec((None, 3, nb, D_MODEL), lambda l, j: (l, 0, 0, 0)),
            *w_head,
            pl.BlockSpec((1, LANES), const2),
            pl.BlockSpec((1, LANES), const2),
            *w_tail,
            pl.BlockSpec((None, bb * KV_WIDTH, n_slot), lambda l, j: (l, j, 0)),
            pl.BlockSpec((None, bb * KV_WIDTH, n_slot), lambda l, j: (l, j, 0)),
            pl.BlockSpec((None, B_SHIFT, nb), lay3),
            pl.BlockSpec((None, n_hv, nb), lambda l, j: (l, j, 0)),
            pl.BlockSpec((None, CONV_W - 1, nb, C_WIDTH), lambda l, j: (l, 0, 0, 0)),
            pl.BlockSpec((None, nb, C_WIDTH), lay3),
        ],
        out_specs=[
            pl.BlockSpec((nb, D_MODEL), const2),
            pl.BlockSpec((None, KV_WIDTH, nb), lay3),
            pl.BlockSpec((None, KV_WIDTH, nb), lay3),
            pl.BlockSpec((None, B_SHIFT, nb), lay3),
            pl.BlockSpec((None, n_hv, nb), lambda l, j: (l, j, 0)),
            pl.BlockSpec((None, CONV_W - 1, nb, C_WIDTH), lambda l, j: (l, 0, 0, 0)),
            pl.BlockSpec((None, nb, C_WIDTH), lay3),
        ],
        out_shape=[
            jax.ShapeDtypeStruct((nb, D_MODEL), F32),
            jax.ShapeDtypeStruct((DEPTH, KV_WIDTH, nb), F32),
            jax.ShapeDtypeStruct((DEPTH, KV_WIDTH, nb), F32),
            jax.ShapeDtypeStruct((DEPTH, B_SHIFT, nb), F32),
            jax.ShapeDtypeStruct((DEPTH, B_HEADS * n_hv, nb), F32),
            jax.ShapeDtypeStruct((DEPTH, CONV_W - 1, nb, C_WIDTH), F32),
            jax.ShapeDtypeStruct((DEPTH, nb, C_WIDTH), F32),
        ],
        scratch_shapes=[
            pltpu.VMEM((nb, D_MODEL), F32),
            pltpu.VMEM((nb, P_PAD), F32),
            pltpu.VMEM((nb, D_MODEL), F32),
            pltpu.VMEM((A_HEADS * bb, LANES), F32),
            pltpu.VMEM((A_HEADS * bb, LANES), F32),
            pltpu.VMEM((nb, KV_WIDTH), F32),
            pltpu.VMEM((6, B_WIDTH, nb), F32),
            pltpu.VMEM((3, nb, B_WIDTH), F32),
            pltpu.VMEM((B_WIDTH, nb), F32),
        ],
        compiler_params=pltpu.CompilerParams(
            dimension_semantics=("arbitrary", "arbitrary"), vmem_limit_bytes=VMEM_LIMIT_BYTES),
        name="sample_layers",
    )(wts["sinks"], x, mod, wts["norm_pre"], wts["norm_post"], wts["w_in"], wts["w_out"], cos_r, sin_r,
      wts["mu"], wts["vec"], wts["wlora"], wts["wgate"], kt, vt, sprev_t, wkv_t, conv_t, hl)


def _rope_lanes():
    half = HEAD_DIM // 2
    inv_freq = ROPE_THETA ** (-jnp.arange(half, dtype=F32) / half)
    freq = jnp.tile(inv_freq, LANES // half)
    sign = jnp.tile(jnp.concatenate([-jnp.ones((half,), F32), jnp.ones((half,), F32)]), LANES // HEAD_DIM)
    return freq, sign


def _rope_row(pos):
    freq, sign = _rope_lanes()
    ang = jnp.float32(pos) * freq[None, :]
    return jnp.cos(ang), jnp.sin(ang) * sign[None, :]


def _rope_tables(seq):
    freq, sign = _rope_lanes()
    hi = (jnp.arange(seq // Q_BLOCK, dtype=F32) * Q_BLOCK)[:, None] * freq[None, :]
    lo = jnp.arange(Q_BLOCK, dtype=F32)[:, None] * freq[None, :]
    ch, sh = jnp.cos(hi)[:, None, :], jnp.sin(hi)[:, None, :]
    cl, sl = jnp.cos(lo)[None, :, :], jnp.sin(lo)[None, :, :]
    cos_t = (ch * cl - sh * sl).reshape(seq, LANES)
    sin_t = ((sh * cl + ch * sl) * sign).reshape(seq, LANES)
    return cos_t, sin_t


def _block_diag(w):
    dl, n, d, e = w.shape
    eye = jnp.eye(n, dtype=w.dtype)
    return (eye[None, :, None, :, None] * w[:, :, :, None, :]).reshape(dl, n * d, n * e)


def _prep_weights(p):
    dl = p["w_in"].shape[0]
    w_in_t = jnp.swapaxes(p["w_in"], 1, 2)
    q_scale = jnp.where(jnp.arange(w_in_t.shape[1]) < A_WIDTH, Q_SCALE, 1.0).astype(F32)
    w_in_t = (w_in_t * q_scale[None, :, None]).astype(BF16)
    z32 = jnp.zeros((dl, LORA, B_WIDTH), F32)
    wlora = jnp.concatenate([
        jnp.concatenate([p["rwkv_w_up"], z32], axis=2),
        jnp.concatenate([z32, p["rwkv_a_up"]], axis=2),
        jnp.zeros((dl, LANES - 2 * LORA, 2 * B_WIDTH), F32)], axis=1).astype(BF16)
    wgate = jnp.concatenate([_block_diag(p["lru_gate_a_w"]), _block_diag(p["lru_gate_x_w"])], axis=2).astype(BF16)
    cw = p["lru_conv_w"]
    rows = [p["rwkv_w0"], p["rwkv_a0"], p["rwkv_k_k"], p["rwkv_k_a"], p["rwkv_lnx_w"], p["rwkv_lnx_b"],
            p["rwkv_r_k"].reshape(dl, B_WIDTH), p["lru_conv_b"], p["lru_gate_a_b"], p["lru_gate_x_b"],
            p["lru_lambda"], cw[:, 0], cw[:, 1], cw[:, 2], cw[:, 3], jnp.zeros((dl, B_WIDTH), F32)]
    return {
        "sinks": p["attn_sinks"],
        "norm_pre": p["norm_pre"].reshape(dl, 1, D_MODEL),
        "norm_post": p["norm_post"].reshape(dl, 1, D_MODEL),
        "w_in": w_in_t,
        "w_out": p["w_out"].astype(BF16),
        "mu": jnp.concatenate([p["rwkv_mu"], jnp.zeros((dl, B_SHIFT_PAD - B_SHIFT), F32)],
                              axis=1).reshape(dl, 1, B_SHIFT_PAD),
        "vec": jnp.stack(rows, axis=1),
        "wlora": wlora,
        "wgate": wgate,
    }


def _forward(x_prompt, x_sample, c_prompt, c_sample, cache_swa_k, cache_swa_v, state_rwkv_shift,
             state_rwkv_wkv, state_lru_conv, state_lru_h, p, tb):
    bp, seq, _ = x_prompt.shape
    nb = x_sample.shape[0]
    n_slot = cache_swa_k.shape[2]
    assert nb % SUBLANES == 0 and nb % B_HEADS == 0 and bp <= SUBLANES and seq % tb == 0

    pad_rows = (-(nb + bp)) % SUBLANES
    c_all = jnp.concatenate([c_sample, c_prompt, jnp.zeros((pad_rows, D_MODEL), F32)], axis=0)
    mod = _mod_call(c_all, p["w_mod"], p["b_mod"])
    wts = _prep_weights(p)

    cos_p, sin_p = _rope_tables(seq)
    cos_s, sin_s = _rope_row(PAST_LEN)

    kt = jnp.transpose(cache_swa_k, (0, 1, 3, 4, 2)).reshape(DEPTH, nb * KV_WIDTH, n_slot)
    vt = jnp.transpose(cache_swa_v, (0, 1, 3, 4, 2)).reshape(DEPTH, nb * KV_WIDTH, n_slot)
    sprev_t = jnp.swapaxes(state_rwkv_shift, 1, 2)
    wkv_t = jnp.transpose(state_rwkv_wkv, (0, 2, 3, 4, 1)).reshape(DEPTH, B_HEADS * HEAD_DIM * HEAD_DIM, nb)
    conv_t = jnp.swapaxes(state_lru_conv, 1, 2)
    ys, knew_t, vnew_t, shift_t, wkvo_t, convo_t, hlo = _sample_layers(
        x_sample.reshape(nb, D_MODEL), mod, wts, cos_s, sin_s, kt, vt, sprev_t, wkv_t, conv_t, state_lru_h)
    outs_s = (
        jnp.transpose(knew_t.reshape(DEPTH, 1, A_KV_HEADS, HEAD_DIM, nb), (0, 4, 1, 2, 3)),
        jnp.transpose(vnew_t.reshape(DEPTH, 1, A_KV_HEADS, HEAD_DIM, nb), (0, 4, 1, 2, 3)),
        jnp.swapaxes(shift_t, 1, 2),
        jnp.transpose(wkvo_t.reshape(DEPTH, B_HEADS, HEAD_DIM, HEAD_DIM, nb), (0, 4, 1, 2, 3)),
        jnp.swapaxes(convo_t, 1, 2),
        hlo,
    )

    yp = x_prompt
    outs_p = []
    for l in range(DEPTH):
        yp, kv_p, sh_p, wkv_p, conv_p, h_p = _prompt_layer(l, yp, mod, wts, cos_p, sin_p, nb, tb)
        kv_p = jnp.transpose(kv_p.reshape(bp, 2, A_KV_HEADS, HEAD_DIM, WINDOW), (1, 0, 4, 2, 3))
        outs_p.append((
            kv_p[0], kv_p[1],
            sh_p[:, SUBLANES - 1, :B_SHIFT],
            jnp.stack([wkv_p[:, HEAD_DIM * hd:HEAD_DIM * (hd + 1), HEAD_DIM * hd:HEAD_DIM * (hd + 1)]
                       for hd in range(B_HEADS)], axis=1),
            conv_p[:, SUBLANES - (CONV_W - 1):, :],
            h_p[:, SUBLANES - 1, :],
        ))
    sp = [jnp.stack(z) for z in zip(*outs_p)]
    return (yp, ys.reshape(nb, 1, D_MODEL), *sp, *outs_s)


def kernel(x_prompt, x_sample, c_prompt, c_sample, cache_swa_k, cache_swa_v, state_rwkv_shift, state_rwkv_wkv, state_lru_conv, state_lru_h, norm_pre, norm_post, w_mod, b_mod, w_in, w_out, attn_sinks, rwkv_mu, rwkv_w0, rwkv_w_up, rwkv_a0, rwkv_a_up, rwkv_k_k, rwkv_k_a, rwkv_r_k, rwkv_lnx_w, rwkv_lnx_b, lru_conv_w, lru_conv_b, lru_gate_a_w, lru_gate_a_b, lru_gate_x_w, lru_gate_x_b, lru_lambda):
    p = dict(norm_pre=norm_pre, norm_post=norm_post, w_mod=w_mod, b_mod=b_mod, w_in=w_in, w_out=w_out,
             attn_sinks=attn_sinks, rwkv_mu=rwkv_mu, rwkv_w0=rwkv_w0, rwkv_w_up=rwkv_w_up, rwkv_a0=rwkv_a0,
             rwkv_a_up=rwkv_a_up, rwkv_k_k=rwkv_k_k, rwkv_k_a=rwkv_k_a, rwkv_r_k=rwkv_r_k,
             rwkv_lnx_w=rwkv_lnx_w, rwkv_lnx_b=rwkv_lnx_b, lru_conv_w=lru_conv_w, lru_conv_b=lru_conv_b,
             lru_gate_a_w=lru_gate_a_w, lru_gate_a_b=lru_gate_a_b, lru_gate_x_w=lru_gate_x_w,
             lru_gate_x_b=lru_gate_x_b, lru_lambda=lru_lambda)
    return _forward(x_prompt, x_sample, c_prompt, c_sample, cache_swa_k, cache_swa_v, state_rwkv_shift,
                    state_rwkv_wkv, state_lru_conv, state_lru_h, p, TIME_BLOCK)
```

```python
import functools

import numpy as np
import jax
import jax.numpy as jnp
from jax import lax
from jax.experimental import pallas as pl
from jax.experimental.pallas import tpu as pltpu

F32 = jnp.float32
BF16 = jnp.bfloat16

D_MODEL = 1024
DEPTH = 2
A_HEADS = 8
A_KV_HEADS = 2
HEAD_DIM = 64
A_WIDTH = A_HEADS * HEAD_DIM
KV_WIDTH = A_KV_HEADS * HEAD_DIM
WINDOW = 128
Q_BLOCK = 128
ROPE_THETA = 10000.0
B_WIDTH = 256
B_HEADS = 4
LORA = 32
B_SHIFT = 3 * B_WIDTH + 2 * LORA
B_SHIFT_PAD = 3 * B_WIDTH + 128
LNX_EPS = 1e-5 * 8 ** 2
C_WIDTH = 256
CONV_W = 4
LRU_C = 8.0
EPS = 1e-6
PAST_LEN = 8192
EXP_NEG_HALF = float(np.exp(-0.5))
LOG2_E = float(np.log2(np.e))
Q_SCALE = HEAD_DIM ** -0.5 * LOG2_E

LANES = 128
SUBLANES = 8
VMEM_LIMIT_BYTES = 56 * 1024 * 1024

OFF_Q = 0
OFF_K = OFF_Q + A_WIDTH
OFF_V = OFF_K + KV_WIDTH
OFF_GA = OFF_V + KV_WIDTH
OFF_PB = OFF_GA + A_WIDTH
OFF_GB = OFF_PB + B_SHIFT_PAD
OFF_XC = OFF_GB + B_WIDTH
OFF_GC = OFF_XC + C_WIDTH
P_PAD = OFF_GC + C_WIDTH
P_IN = P_PAD - (B_SHIFT_PAD - B_SHIFT)

CHUNK = 64
TIME_BLOCK = 512
SEQS_PER_STEP = 1
PROJ_TILE = 256

(V_W0, V_A0, V_KK, V_KA, V_LNW, V_LNB, V_RK, V_CB, V_GAB, V_GXB, V_LAM,
 V_CW0, V_CW1, V_CW2, V_CW3) = range(15)
VEC_ROWS = 16

(T_DECAY, T_NKK, T_KKA, T_KMOD, T_R, T_V) = range(6)

_NT = (((1,), (1,)), ((), ()))
_TN = (((0,), (0,)), ((), ()))


def _bf(x):
    return x.astype(BF16)


def _dot(a, b):
    return jnp.dot(a, b, preferred_element_type=F32)


def _dot_b(a, b):
    return jnp.dot(a, b, preferred_element_type=F32).astype(BF16)


def _dot_nt(a, b):
    return lax.dot_general(a, b, _NT, preferred_element_type=F32)


def _dot_tn(a, b):
    return lax.dot_general(a, b, _TN, preferred_element_type=F32)


def _iota(shape, dim):
    return lax.broadcasted_iota(jnp.int32, shape, dim)


def _sigmoid(x):
    return jax.nn.sigmoid(x)


def _silu(x):
    return x * _sigmoid(x)


def _softplus(x):
    return jnp.maximum(x, 0.0) + jnp.log1p(jnp.exp(-jnp.abs(x)))


def _head_ones(n):
    r = _iota((n, n), 0) // HEAD_DIM
    c = _iota((n, n), 1) // HEAD_DIM
    return jnp.where(r == c, 1.0, 0.0).astype(BF16)


def _head_sum(x, ones_bd):
    return _dot(_bf(x), ones_bd)


def _rms_scale(x):
    ms = jnp.mean(x * x, axis=-1, keepdims=True)
    return x * lax.rsqrt(ms + EPS)


def _affine_scan(a, b, h0):
    rows, n = a.shape
    groups = rows // SUBLANES
    a = a.reshape(groups, SUBLANES, n)
    b = b.reshape(groups, SUBLANES, n)
    sub = _iota((groups, SUBLANES, n), 1)
    d = 1
    while d < SUBLANES:
        keep = sub >= d
        a_sh = jnp.where(keep, pltpu.roll(a, d, 1), 1.0)
        b_sh = jnp.where(keep, pltpu.roll(b, d, 1), 0.0)
        b = b + a * b_sh
        a = a * a_sh
        d *= 2
    outs = []
    carry = h0
    for g in range(groups):
        hg = b[g] + a[g] * carry
        outs.append(hg)
        carry = hg[SUBLANES - 1:SUBLANES, :]
    return jnp.concatenate(outs, axis=0)


def _shift_rows(x, j, tail):
    rows, n = x.shape
    groups = rows // SUBLANES
    x3 = jnp.concatenate([tail, x], axis=0).reshape(groups + 1, SUBLANES, n)
    r3 = pltpu.roll(x3, j, 1)
    sub = _iota((groups, SUBLANES, n), 1)
    return jnp.where(sub >= j, r3[1:], r3[:-1]).reshape(rows, n)


def _cumsum_chunks(x, chunk):
    rows, n = x.shape
    groups = rows // SUBLANES
    x = x.reshape(groups, SUBLANES, n)
    sub = _iota((groups, SUBLANES, n), 1)
    d = 1
    while d < SUBLANES:
        x = x + jnp.where(sub >= d, pltpu.roll(x, d, 1), 0.0)
        d *= 2
    outs = []
    for g in range(groups):
        xg = x[g]
        if (g * SUBLANES) % chunk != 0:
            xg = xg + outs[-1][SUBLANES - 1:SUBLANES, :]
        outs.append(xg)
    return jnp.concatenate(outs, axis=0)


def _rope128(z, cos, sin_signed):
    lane = _iota(z.shape, 1)
    first = (lane & 32) == 0
    sw = jnp.where(first, pltpu.roll(z, 96, 1), pltpu.roll(z, 32, 1))
    return z * cos + sw * sin_signed


def _rwkv_pre(xs, vec, wlora, ones_bd):
    r = xs[:, 0:B_WIDTH]
    k = xs[:, B_WIDTH:2 * B_WIDTH]
    v = xs[:, 2 * B_WIDTH:3 * B_WIDTH]
    lor = xs[:, 3 * B_WIDTH:B_SHIFT_PAD]
    lane = _iota(lor.shape, 1)
    z = jnp.where(lane < LORA, jnp.tanh(lor), lor)
    wa = _dot(_bf(z), wlora)
    zw = vec[V_W0:V_W0 + 1, :] + wa[:, :B_WIDTH]
    ld = -EXP_NEG_HALF * _sigmoid(zw)
    a = _sigmoid(vec[V_A0:V_A0 + 1, :] + wa[:, B_WIDTH:])
    kk = k * vec[V_KK:V_KK + 1, :]
    kkn = kk * jnp.minimum(lax.rsqrt(_head_sum(kk * kk, ones_bd)), 1e12)
    kmod = k * (1.0 + (a - 1.0) * vec[V_KA:V_KA + 1, :])
    return r, k, v, ld, a, kkn, kmod


def _rwkv_post(y, r, kmod, v, gb, vec, ones_bd):
    inv = 1.0 / HEAD_DIM
    mu = _head_sum(y, ones_bd) * inv
    yc = y - mu
    var = _head_sum(yc * yc, ones_bd) * inv
    yn = yc * lax.rsqrt(var + LNX_EPS)
    yn = yn * vec[V_LNW:V_LNW + 1, :] + vec[V_LNB:V_LNB + 1, :]
    bonus = _head_sum(r * kmod * vec[V_RK:V_RK + 1, :], ones_bd) * v
    return (yn + bonus) * _silu(gb)


def _lru_gates(u, vec, wgate):
    gates = _dot(_bf(u), wgate)
    gr = _sigmoid(gates[:, :C_WIDTH] + vec[V_GAB:V_GAB + 1, :])
    gi = _sigmoid(gates[:, C_WIDTH:] + vec[V_GXB:V_GXB + 1, :])
    sp = _softplus(-vec[V_LAM:V_LAM + 1, :])
    log_a = -LRU_C * gr * sp
    a = jnp.exp(log_a)
    mult = jnp.sqrt(1.0 - a * a)
    return a, mult, gi


def _project(h, win_ref, proj_s, group):
    src, dst, n = group
    proj_s[:, dst:dst + n] = _dot_nt(h, win_ref[src:src + n, :])


_DELTA = B_SHIFT_PAD - B_SHIFT
GRP_PB = (OFF_PB, OFF_PB, B_SHIFT_PAD)
GRP_QKV = (OFF_Q, OFF_Q, OFF_GA - OFF_Q)
GRP_GA = (OFF_GA, OFF_GA, A_WIDTH)
GRP_GB = (OFF_GB - _DELTA, OFF_GB, B_WIDTH)
GRP_XC = (OFF_XC - _DELTA, OFF_XC, C_WIDTH)
GRP_GC = (OFF_GC - _DELTA, OFF_GC, C_WIDTH)


def _mod_kernel(c_ref, w_ref, b_ref, o_ref):
    o_ref[...] = _dot(_bf(_silu(c_ref[...])), _bf(w_ref[...])) + b_ref[...]


def _mod_call(c_all, w_mod, b_mod):
    rows = c_all.shape[0]
    return pl.pallas_call(
        _mod_kernel,
        grid=(DEPTH, 3),
        in_specs=[
            pl.BlockSpec((rows, D_MODEL), lambda l, j: (0, 0)),
            pl.BlockSpec((None, D_MODEL, D_MODEL), lambda l, j: (l, 0, j)),
            pl.BlockSpec((None, 1, D_MODEL), lambda l, j: (l, 0, j)),
        ],
        out_specs=pl.BlockSpec((None, None, rows, D_MODEL), lambda l, j: (l, j, 0, 0)),
        out_shape=jax.ShapeDtypeStruct((DEPTH, 3, rows, D_MODEL), F32),
        compiler_params=pltpu.CompilerParams(
            dimension_semantics=("arbitrary", "arbitrary"), vmem_limit_bytes=VMEM_LIMIT_BYTES),
        name="adaln_mod",
    )(c_all, w_mod, b_mod.reshape(DEPTH, 1, 3 * D_MODEL))


def _prompt_kernel(sinks_ref, x_ref, mod_ref, npre_ref, npost_ref, win_ref, wout_ref,
                   cos_ref, sin_ref, mu_ref, vec_ref, wlora_ref, wgate_ref,
                   y_ref, kv_ref, shift_ref, wkv_ref, conv_ref, hlru_ref,
                   *scratch, layer):
    n_seq = x_ref.shape[0]
    tb = x_ref.shape[1]
    t = pl.program_id(1)
    per_seq = len(scratch) // n_seq
    seq_scratch = [scratch[q * per_seq:(q + 1) * per_seq] for q in range(n_seq)]

    @pl.when(t == 0)
    def _():
        for kvprev, pprev, xcprev, hprev, gstate, _, _, _ in seq_scratch:
            kvprev[...] = jnp.zeros_like(kvprev)
            pprev[...] = jnp.zeros_like(pprev)
            xcprev[...] = jnp.zeros_like(xcprev)
            hprev[...] = jnp.zeros_like(hprev)
            gstate[...] = jnp.zeros_like(gstate)

    vec = vec_ref[...]
    ones_bd = _head_ones(B_WIDTH)
    cos = cos_ref[...]
    sin = sin_ref[...]
    nck = tb // CHUNK
    sls = [slice(c * CHUNK, (c + 1) * CHUNK) for c in range(nck)]
    rng = range(nck)
    pairs = [(c, i) for c in range(A_HEADS // 2) for i in range(tb // Q_BLOCK)]

    lo_kv = _iota((tb + WINDOW, KV_WIDTH), 1) < HEAD_DIM
    lo_q = _iota((Q_BLOCK, LANES), 1) < HEAD_DIM
    qi = _iota((Q_BLOCK, 2 * Q_BLOCK), 0)
    sj = _iota((Q_BLOCK, 2 * Q_BLOCK), 1)
    band = (sj >= qi) & (sj <= qi + WINDOW)
    first_lo = jnp.where(t == 0, Q_BLOCK, 0)
    neg_inf = -jnp.inf
    reset = (_iota((tb, C_WIDTH), 0) + t * tb) == 0

    ri = _iota((CHUNK, B_HEADS * CHUNK), 0)
    cj = _iota((CHUNK, B_HEADS * CHUNK), 1) % CHUNK
    strict_b = jnp.where(ri > cj, 1.0, 0.0).astype(BF16)
    incl_b = jnp.where(ri >= cj, 1.0, 0.0).astype(BF16)
    eye_f = jnp.where(ri == cj, 1.0, 0.0)
    bd_heads = (_iota((B_WIDTH, B_WIDTH), 0) // HEAD_DIM) == (_iota((B_WIDTH, B_WIDTH), 1) // HEAD_DIM)

    def variants(z):
        zr = pltpu.roll(z, HEAD_DIM, 1)
        a_ = _bf(jnp.where(lo_kv, z, 0.0))
        b_ = _bf(jnp.where(lo_kv, 0.0, z))
        c_ = _bf(jnp.where(lo_kv, zr, 0.0))
        d_ = _bf(jnp.where(lo_kv, 0.0, zr))
        return ((a_, d_), (c_, b_))

    def win_stack(var, c, i):
        g, r0 = c // 2, i * Q_BLOCK
        return jnp.concatenate([var[g][0][r0:r0 + 2 * Q_BLOCK], var[g][1][r0:r0 + 2 * Q_BLOCK]], axis=0)

    def stk(z):
        zb = _bf(z)
        return jnp.concatenate([zb] * B_HEADS, axis=0) * ones_bd

    def collapse(z):
        return z[0:CHUNK] + z[CHUNK:2 * CHUNK] + z[2 * CHUNK:3 * CHUNK] + z[3 * CHUNK:4 * CHUNK]

    def side(q, out):
        kvprev, _, xcprev, hprev, _, proj, mix, _ = seq_scratch[q]

        k_rot = _rope128(proj[:, OFF_K:OFF_K + KV_WIDTH], cos, sin)
        v_att = proj[:, OFF_V:OFF_V + KV_WIDTH]
        kvar = variants(jnp.concatenate([kvprev[:, 0:KV_WIDTH], k_rot], axis=0))
        vvar = variants(jnp.concatenate([kvprev[:, KV_WIDTH:2 * KV_WIDTH], v_att], axis=0))
        out["k_tail"] = k_rot[tb - WINDOW:]
        out["v_tail"] = v_att[tb - WINDOW:]
        yield
        qcols = [_bf(_rope128(proj[:, OFF_Q + LANES * c:OFF_Q + LANES * (c + 1)], cos, sin))
                 for c in range(A_HEADS // 2)]
        scores = [_dot_nt(qcols[c][i * Q_BLOCK:(i + 1) * Q_BLOCK], win_stack(kvar, c, i))
                  for c, i in pairs]
        yield

        xc = proj[:, OFF_XC:OFF_XC + C_WIDTH]
        u = vec[V_CB:V_CB + 1, :] + xc * vec[V_CW3:V_CW3 + 1, :]
        for j in range(1, CONV_W):
            u = u + _shift_rows(xc, j, xcprev[...]) * vec[V_CW3 - j:V_CW3 - j + 1, :]
        a_l, mult, gi = _lru_gates(u, vec, wgate_ref[...])
        a_l = jnp.where(reset, 0.0, a_l)
        mult = jnp.where(reset, 1.0, mult)
        b_l = mult * gi * u
        out["xc_tail"] = xc[tb - SUBLANES:, :]
        yield

        probs, norms = [], []
        for (c, i), s in zip(pairs, scores):
            mask = band & (sj >= first_lo) if i == 0 else band
            ps, invs = [], []
            for hh in range(2):
                sink = sinks_ref[layer, 2 * c + hh] * LOG2_E
                sh = jnp.where(mask, s[:, 2 * Q_BLOCK * hh:2 * Q_BLOCK * (hh + 1)], neg_inf)
                m = jnp.maximum(jnp.max(sh, axis=-1, keepdims=True), sink)
                p = jnp.exp2(sh - m)
                invs.append(1.0 / (jnp.sum(p, axis=-1, keepdims=True) + jnp.exp2(sink - m)))
                ps.append(_bf(p))
            probs.append(jnp.concatenate(ps, axis=1))
            norms.append(jnp.where(lo_q, invs[0], invs[1]))
            if i == tb // Q_BLOCK - 1:
                yield
        for (c, i), p, inv in zip(pairs, probs, norms):
            r0 = i * Q_BLOCK
            o = _dot(p, win_stack(vvar, c, i)) * inv
            ga = proj[r0:r0 + Q_BLOCK, OFF_GA + LANES * c:OFF_GA + LANES * (c + 1)]
            mix[r0:r0 + Q_BLOCK, LANES * c:LANES * (c + 1)] = o * _silu(ga)
        yield
        h_l = _affine_scan(a_l, b_l, hprev[SUBLANES - 1:SUBLANES, :])
        gc = proj[:, OFF_GC:OFF_GC + C_WIDTH]
        mix[:, A_WIDTH + B_WIDTH:] = h_l * _silu(gc)
        out["h_tail"] = h_l[tb - SUBLANES:, :]

    def sequence(q, out):
        kvprev, pprev, xcprev, hprev, gstate, proj, mix, ywkv = seq_scratch[q]

        x = x_ref[q]
        brow = pl.ds(pl.program_id(0) * n_seq + q, 1)
        shift_m = mod_ref[0, brow, :]
        scale_m = mod_ref[1, brow, :]
        gate_m = mod_ref[2, brow, :]
        h = _bf(_rms_scale(x) * (npre_ref[...] * (1.0 + scale_m)) + shift_m)
        for src, dst, n in (GRP_PB, GRP_QKV, GRP_XC, GRP_GA, GRP_GB, GRP_GC):
            for off in range(0, n, PROJ_TILE):
                _project(h, win_ref, proj, (src + off, dst + off, min(PROJ_TILE, n - off)))
                yield

        pb = proj[:, OFF_PB:OFF_PB + B_SHIFT_PAD]
        prev = _shift_rows(pb, 1, pprev[...])
        xs = pb + (prev - pb) * mu_ref[...]
        yield
        r, k, v, ld, a, kkn, kmod = _rwkv_pre(xs, vec, wlora_ref[...], ones_bd)
        kka = kkn * a
        yield

        lcum = _cumsum_chunks(ld, CHUNK)
        e_l = jnp.exp(lcum)
        e_nl = jnp.exp(-lcum)
        rt = r * e_l
        at = -kkn * jnp.exp(lcum - ld)
        ktil = kmod * e_nl
        btil = kka * e_nl
        gam = [e_l[s.stop - 1:s.stop, :] for s in sls]
        khat = [ktil[sls[c]] * gam[c] for c in rng]
        bhat = [btil[sls[c]] * gam[c] for c in rng]
        yield
        a_s = [stk(at[s]) for s in sls]
        b_s = [stk(btil[s]) for s in sls]
        k_s = [stk(ktil[s]) for s in sls]
        yield
        v_s = [stk(v[s]) for s in sls]
        bh_s = [stk(bhat[c]) for c in rng]
        sc = [_dot_nt(_bf(jnp.concatenate([at[sls[c]], rt[sls[c]]], axis=0)),
                      jnp.concatenate([b_s[c], k_s[c]], axis=0)) for c in rng]
        yield "side may start"

        pw = [_bf(sc[c][0:CHUNK, 0:B_WIDTH]) * strict_b for c in rng]
        lak = [_bf(sc[c][0:CHUNK, B_WIDTH:]) * strict_b for c in rng]
        mrb = [_bf(sc[c][CHUNK:, 0:B_WIDTH]) * incl_b for c in rng]
        mrk = [_bf(sc[c][CHUNK:, B_WIDTH:]) * incl_b for c in rng]
        tacc = [pw[c].astype(F32) + eye_f for c in rng]
        pw = [_dot_b(pw[c], stk(pw[c])) for c in rng]
        yield
        for _ in range(4):
            res = [_dot(jnp.concatenate([_bf(tacc[c]), pw[c]], axis=0), stk(pw[c])) for c in rng]
            tacc = [tacc[c] + res[c][0:CHUNK] for c in rng]
            pw = [_bf(res[c][CHUNK:]) for c in rng]
            yield
        tinv = [_bf(tacc[c] + _dot(_bf(tacc[c]), stk(pw[c]))) for c in rng]
        lv = [_dot_b(lak[c], v_s[c]) for c in rng]
        yield
        w_c = [_dot_b(tinv[c], a_s[c]) for c in rng]
        z_c = [_dot(tinv[c], stk(lv[c])) for c in rng]
        yield
        w_s = [stk(w_c[c]) for c in rng]
        rp = [_bf(rt[sls[c]] + _dot(mrb[c], w_s[c])) for c in rng]
        y0 = [_dot(jnp.concatenate([mrb[c], mrk[c]], axis=1), jnp.concatenate([stk(z_c[c]), v_s[c]], axis=0))
              for c in rng]
        pp = [_bf(_dot_tn(w_s[c], bh_s[c])) for c in rng]
        qq = [jnp.where(bd_heads,
                        _dot_tn(_bf(jnp.concatenate([z_c[c], v[sls[c]]], axis=0)),
                                _bf(jnp.concatenate([bhat[c], khat[c]], axis=0))), 0.0) for c in rng]
        yield
        gmat = gstate[...]
        for c in rng:
            gb16 = _bf(gmat)
            ywkv[sls[c], :] = _dot_nt(rp[c], gb16) + y0[c]
            gmat = gam[c] * gmat + _dot(gb16, pp[c]) + qq[c]
        gstate[...] = gmat
        yield "side must be done"

        gb = proj[:, OFF_GB:OFF_GB + B_WIDTH]
        mix[:, A_WIDTH:A_WIDTH + B_WIDTH] = _rwkv_post(ywkv[...], r, kmod, v, gb, vec, ones_bd)

        yo = _dot(_bf(mix[...]), wout_ref[...])
        y_ref[q] = x + (gate_m * npost_ref[...]) * _rms_scale(yo)

        k_tail, v_tail = out["k_tail"], out["v_tail"]
        kvprev[:, 0:KV_WIDTH] = k_tail
        kvprev[:, KV_WIDTH:2 * KV_WIDTH] = v_tail
        pprev[...] = pb[tb - SUBLANES:, :]
        xcprev[...] = out["xc_tail"]
        hprev[...] = out["h_tail"]
        shift_ref[q] = pb[tb - SUBLANES:, :]
        wkv_ref[q] = gmat
        conv_ref[q] = out["xc_tail"]
        hlru_ref[q] = out["h_tail"]

        @pl.when(t == pl.num_programs(1) - 1)
        def _():
            kv_ref[q, 0] = k_tail.T
            kv_ref[q, 1] = v_tail.T

    for q in range(n_seq):
        shared = {}
        main, extra = sequence(q, shared), side(q, shared)
        while next(main) != "side may start":
            pass
        extra_done = False
        while True:
            if not extra_done:
                try:
                    next(extra)
                except StopIteration:
                    extra_done = True
            if next(main) == "side must be done":
                break
        for _ in extra:
            pass
        for _ in main:
            pass


def _weight_specs(layer_of, grid_rank):
    def lay(*shape):
        zeros = (0,) * len(shape)
        if grid_rank == 1:
            return pl.BlockSpec((None, *shape), lambda i: (layer_of(i), *zeros))
        return pl.BlockSpec((None, *shape), lambda a, b: (layer_of(a, b), *zeros))
    return [
        lay(1, D_MODEL),
        lay(1, D_MODEL),
        lay(P_IN, D_MODEL),
        lay(D_MODEL, D_MODEL),
    ], [
        lay(1, B_SHIFT_PAD),
        lay(VEC_ROWS, B_WIDTH),
        lay(LANES, 2 * B_WIDTH),
        lay(C_WIDTH, 2 * C_WIDTH),
    ]


def _prompt_layer(layer, x, mod, wts, cos_t, sin_t, mod_row0, tb):
    bsz, seq, _ = x.shape
    nt = seq // tb
    sps = SEQS_PER_STEP
    per_b3 = lambda b, t: (b, 0, 0)
    w_head, w_tail = _weight_specs(lambda b, t: layer, 2)
    return pl.pallas_call(
        functools.partial(_prompt_kernel, layer=layer),
        grid=(bsz // sps, nt),
        in_specs=[
            pl.BlockSpec(memory_space=pltpu.SMEM),
            pl.BlockSpec((sps, tb, D_MODEL), lambda b, t: (b, t, 0)),
            pl.BlockSpec((None, 3, SUBLANES, D_MODEL),
                         lambda b, t: (layer, 0, mod_row0 // SUBLANES, 0)),
            *w_head,
            pl.BlockSpec((tb, LANES), lambda b, t: (t, 0)),
            pl.BlockSpec((tb, LANES), lambda b, t: (t, 0)),
            *w_tail,
        ],
        out_specs=[
            pl.BlockSpec((sps, tb, D_MODEL), lambda b, t: (b, t, 0)),
            pl.BlockSpec((sps, 2, KV_WIDTH, WINDOW), lambda b, t: (b, 0, 0, 0)),
            pl.BlockSpec((sps, SUBLANES, B_SHIFT_PAD), per_b3),
            pl.BlockSpec((sps, B_WIDTH, B_WIDTH), per_b3),
            pl.BlockSpec((sps, SUBLANES, C_WIDTH), per_b3),
            pl.BlockSpec((sps, SUBLANES, C_WIDTH), per_b3),
        ],
        out_shape=[
            jax.ShapeDtypeStruct((bsz, seq, D_MODEL), F32),
            jax.ShapeDtypeStruct((bsz, 2, KV_WIDTH, WINDOW), F32),
            jax.ShapeDtypeStruct((bsz, SUBLANES, B_SHIFT_PAD), F32),
            jax.ShapeDtypeStruct((bsz, B_WIDTH, B_WIDTH), F32),
            jax.ShapeDtypeStruct((bsz, SUBLANES, C_WIDTH), F32),
            jax.ShapeDtypeStruct((bsz, SUBLANES, C_WIDTH), F32),
        ],
        scratch_shapes=[
            pltpu.VMEM((WINDOW, 2 * KV_WIDTH), F32),
            pltpu.VMEM((SUBLANES, B_SHIFT_PAD), F32),
            pltpu.VMEM((SUBLANES, C_WIDTH), F32),
            pltpu.VMEM((SUBLANES, C_WIDTH), F32),
            pltpu.VMEM((B_WIDTH, B_WIDTH), F32),
            pltpu.VMEM((tb, P_PAD), F32),
            pltpu.VMEM((tb, D_MODEL), F32),
            pltpu.VMEM((tb, B_WIDTH), F32),
        ] * sps,
        compiler_params=pltpu.CompilerParams(
            dimension_semantics=("arbitrary", "arbitrary"), vmem_limit_bytes=VMEM_LIMIT_BYTES),
        name="prompt_layer",
    )(wts["sinks"], x, mod, wts["norm_pre"], wts["norm_post"], wts["w_in"], wts["w_out"], cos_t, sin_t,
      wts["mu"], wts["vec"], wts["wlora"], wts["wgate"])


def _sample_kernel(sinks_ref, x_ref, mod_ref, npre_ref, npost_ref, win_ref, wout_ref,
                   cos_ref, sin_ref, mu_ref, vec_ref, wlora_ref, wgate_ref,
                   kt_ref, vt_ref, sprev_ref, wkv_ref, conv_ref, hl_ref,
                   y_ref, knew_ref, vnew_ref, shift_ref, wkvo_ref, convo_ref, hlo_ref,
                   ys_s, proj_s, mix_s, q_s, o_s, knew_s, rw_s, rkv_s, yt_s):
    layer = pl.program_id(0)
    j = pl.program_id(1)
    nb = x_ref.shape[0]
    bb = nb // B_HEADS
    n_slot = kt_ref.shape[1]
    vec = vec_ref[...]
    cos = cos_ref[...]
    sin = sin_ref[...]

    @pl.when(jnp.logical_and(layer == 0, j == 0))
    def _():
        ys_s[...] = x_ref[...]

    @pl.when(j == 0)
    def _():
        ones_bd = _head_ones(B_WIDTH)
        x = ys_s[...]
        h = _bf(_rms_scale(x) * npre_ref[...] * (1.0 + mod_ref[1]) + mod_ref[0])
        for grp in (GRP_QKV, GRP_GA, GRP_PB, GRP_GB, GRP_XC, GRP_GC):
            _project(h, win_ref, proj_s, grp)
        k_new = _rope128(proj_s[:, OFF_K:OFF_K + KV_WIDTH], cos, sin)
        knew_s[...] = k_new
        knew_ref[...] = k_new.T
        vnew_ref[...] = proj_s[:, OFF_V:OFF_V + KV_WIDTH].T
        pb = proj_s[:, OFF_PB:OFF_PB + B_SHIFT_PAD]
        pb_t = pb.T
        shift_ref[...] = pb_t[0:B_SHIFT, :]
        prev = jnp.concatenate([sprev_ref[...], jnp.zeros((B_SHIFT_PAD - B_SHIFT, nb), F32)], axis=0).T
        xs = pb + (prev - pb) * mu_ref[...]
        r, k, v, ld, a, kkn, kmod = _rwkv_pre(xs, vec, wlora_ref[...], ones_bd)
        rw_s[T_DECAY] = jnp.exp(ld).T
        rw_s[T_NKK] = (-kkn).T
        rw_s[T_KKA] = (kkn * a).T
        rw_s[T_KMOD] = kmod.T
        rw_s[T_R] = r.T
        rw_s[T_V] = v.T
        rkv_s[0] = r
        rkv_s[1] = kmod
        rkv_s[2] = v

    rs = pl.ds(pl.multiple_of(j * bb, SUBLANES), bb)
    lane = _iota((bb, LANES), 1)
    lo = lane < HEAD_DIM
    for c in range(A_HEADS // 2):
        g = c // 2
        in_g = lo if g == 0 else jnp.logical_not(lo)
        qcol = _rope128(proj_s[rs, OFF_Q + LANES * c:OFF_Q + LANES * (c + 1)], cos, sin)
        qrol = pltpu.roll(qcol, HEAD_DIM, 1)
        for hh in range(2):
            hd = 2 * c + hh
            q_s[hd * bb:(hd + 1) * bb, :] = jnp.where(in_g, qcol if hh == g else qrol, 0.0)
    knew_blk = knew_s[rs, :]
    vnew_blk = proj_s[rs, OFF_V:OFF_V + KV_WIDTH]
    hrow = _iota((A_HEADS, 1), 0)
    sink_col = jnp.zeros((A_HEADS, 1), F32)
    for hd in range(A_HEADS):
        sink_col = jnp.where(hrow == hd, sinks_ref[layer, hd] * LOG2_E, sink_col)

    qbs = [q_s[pl.ds(b, A_HEADS, stride=bb), :] for b in range(bb)]
    s_c = [_dot(_bf(qbs[b]), _bf(kt_ref[b * KV_WIDTH:(b + 1) * KV_WIDTH, :])) for b in range(bb)]
    s_n = [jnp.sum(qbs[b] * knew_blk[b:b + 1, :], axis=-1, keepdims=True) for b in range(bb)]
    p_c, p_n = [], []
    for b in range(bb):
        m = jnp.maximum(jnp.maximum(jnp.max(s_c[b], axis=-1, keepdims=True), s_n[b]), sink_col)
        e_c = jnp.exp2(s_c[b] - m)
        e_n = jnp.exp2(s_n[b] - m)
        inv = 1.0 / (jnp.sum(e_c, axis=-1, keepdims=True) + e_n + jnp.exp2(sink_col - m))
        p_c.append(_bf(e_c * inv))
        p_n.append(e_n * inv)
    for b in range(bb):
        o = _dot_nt(p_c[b], _bf(vt_ref[b * KV_WIDTH:(b + 1) * KV_WIDTH, :]))
        o_s[pl.ds(b, A_HEADS, stride=bb), :] = o + p_n[b] * vnew_blk[b:b + 1, :]
    for c in range(A_HEADS // 2):
        g = c // 2
        halves = []
        for hh in range(2):
            oh = o_s[(2 * c + hh) * bb:(2 * c + hh + 1) * bb, :]
            halves.append(oh if hh == g else pltpu.roll(oh, HEAD_DIM, 1))
        ga = proj_s[rs, OFF_GA + LANES * c:OFF_GA + LANES * (c + 1)]
        mix_s[rs, LANES * c:LANES * (c + 1)] = jnp.where(lo, halves[0], halves[1]) * _silu(ga)

    hrows = pl.ds(pl.multiple_of(j * HEAD_DIM, HEAD_DIM), HEAD_DIM)
    w_t = rw_s[T_DECAY, hrows, :]
    nkk_t = rw_s[T_NKK, hrows, :]
    kka_t = rw_s[T_KKA, hrows, :]
    k_t = rw_s[T_KMOD, hrows, :]
    r_t = rw_s[T_R, hrows, :]

    def value_row(vi, carry):
        krows = pl.ds(pl.multiple_of(vi * HEAD_DIM, HEAD_DIM), HEAD_DIM)
        st = wkv_ref[krows, :]
        sa = jnp.sum(st * nkk_t, axis=0, keepdims=True)
        st_new = st * w_t + sa * kka_t + rw_s[T_V, pl.ds(j * HEAD_DIM + vi, 1), :] * k_t
        wkvo_ref[krows, :] = st_new
        yt_s[pl.ds(j * HEAD_DIM + vi, 1), :] = jnp.sum(st_new * r_t, axis=0, keepdims=True)
        return carry

    lax.fori_loop(0, HEAD_DIM, value_row, 0, unroll=4)

    @pl.when(j == pl.num_programs(1) - 1)
    def _():
        ones_bd = _head_ones(B_WIDTH)
        x = ys_s[...]
        gb = proj_s[:, OFF_GB:OFF_GB + B_WIDTH]
        mix_s[:, A_WIDTH:A_WIDTH + B_WIDTH] = _rwkv_post(yt_s[...].T, rkv_s[0], rkv_s[1], rkv_s[2], gb, vec, ones_bd)
        xc = proj_s[:, OFF_XC:OFF_XC + C_WIDTH]
        gc = proj_s[:, OFF_GC:OFF_GC + C_WIDTH]
        u = vec[V_CB:V_CB + 1, :] + xc * vec[V_CW3:V_CW3 + 1, :]
        for i in range(CONV_W - 1):
            u = u + conv_ref[i] * vec[V_CW0 + i:V_CW0 + i + 1, :]
        a_l, mult, gi = _lru_gates(u, vec, wgate_ref[...])
        h_l = a_l * hl_ref[...] + mult * gi * u
        hlo_ref[...] = h_l
        convo_ref[0] = conv_ref[1]
        convo_ref[1] = conv_ref[2]
        convo_ref[2] = xc
        mix_s[:, A_WIDTH + B_WIDTH:] = h_l * _silu(gc)
        yo = _dot(_bf(mix_s[...]), wout_ref[...])
        y = x + mod_ref[2] * (_rms_scale(yo) * npost_ref[...])
        ys_s[...] = y
        y_ref[...] = y


def _sample_layers(x, mod, wts, cos_r, sin_r, kt, vt, sprev_t, wkv_t, conv_t, hl):
    nb = x.shape[0]
    bb = nb // B_HEADS
    n_slot = kt.shape[2]
    n_hv = HEAD_DIM * HEAD_DIM
    w_head, w_tail = _weight_specs(lambda l, j: l, 2)
    const2 = lambda l, j: (0, 0)
    lay3 = lambda l, j: (l, 0, 0)
    return pl.pallas_call(
        _sample_kernel,
        grid=(DEPTH, B_HEADS),
        in_specs=[
            pl.BlockSpec(memory_space=pltpu.SMEM),
            pl.BlockSpec((nb, D_MODEL), const2),
            pl.BlockSpec((None, 3, nb, D_MODEL), lambda l, j: (l, 0, 0, 0)),
            *w_head,
            pl.BlockSpec((1, LANES), const2),
            pl.BlockSpec((1, LANES), const2),
            *w_tail,
            pl.BlockSpec((None, bb * KV_WIDTH, n_slot), lambda l, j: (l, j, 0)),
            pl.BlockSpec((None, bb * KV_WIDTH, n_slot), lambda l, j: (l, j, 0)),
            pl.BlockSpec((None, B_SHIFT, nb), lay3),
            pl.BlockSpec((None, n_hv, nb), lambda l, j: (l, j, 0)),
            pl.BlockSpec((None, CONV_W - 1, nb, C_WIDTH), lambda l, j: (l, 0, 0, 0)),
            pl.BlockSpec((None, nb, C_WIDTH), lay3),
        ],
        out_specs=[
            pl.BlockSpec((nb, D_MODEL), const2),
            pl.BlockSpec((None, KV_WIDTH, nb), lay3),
            pl.BlockSpec((None, KV_WIDTH, nb), lay3),
            pl.BlockSpec((None, B_SHIFT, nb), lay3),
            pl.BlockSpec((None, n_hv, nb), lambda l, j: (l, j, 0)),
            pl.BlockSpec((None, CONV_W - 1, nb, C_WIDTH), lambda l, j: (l, 0, 0, 0)),
            pl.BlockSpec((None, nb, C_WIDTH), lay3),
        ],
        out_shape=[
            jax.ShapeDtypeStruct((nb, D_MODEL), F32),
            jax.ShapeDtypeStruct((DEPTH, KV_WIDTH, nb), F32),
            jax.ShapeDtypeStruct((DEPTH, KV_WIDTH, nb), F32),
            jax.ShapeDtypeStruct((DEPTH, B_SHIFT, nb), F32),
            jax.ShapeDtypeStruct((DEPTH, B_HEADS * n_hv, nb), F32),
            jax.ShapeDtypeStruct((DEPTH, CONV_W - 1, nb, C_WIDTH), F32),
            jax.ShapeDtypeStruct((DEPTH, nb, C_WIDTH), F32),
        ],
        scratch_shapes=[
            pltpu.VMEM((nb, D_MODEL), F32),
            pltpu.VMEM((nb, P_PAD), F32),
            pltpu.VMEM((nb, D_MODEL), F32),
            pltpu.VMEM((A_HEADS * bb, LANES), F32),
            pltpu.VMEM((A_HEADS * bb, LANES), F32),
            pltpu.VMEM((nb, KV_WIDTH), F32),
            pltpu.VMEM((6, B_WIDTH, nb), F32),
            pltpu.VMEM((3, nb, B_WIDTH), F32),
            pltpu.VMEM((B_WIDTH, nb), F32),
        ],
        compiler_params=pltpu.CompilerParams(
            dimension_semantics=("arbitrary", "arbitrary"), vmem_limit_bytes=VMEM_LIMIT_BYTES),
        name="sample_layers",
    )(wts["sinks"], x, mod, wts["norm_pre"], wts["norm_post"], wts["w_in"], wts["w_out"], cos_r, sin_r,
      wts["mu"], wts["vec"], wts["wlora"], wts["wgate"], kt, vt, sprev_t, wkv_t, conv_t, hl)


def _rope_lanes():
    half = HEAD_DIM // 2
    inv_freq = ROPE_THETA ** (-jnp.arange(half, dtype=F32) / half)
    freq = jnp.tile(inv_freq, LANES // half)
    sign = jnp.tile(jnp.concatenate([-jnp.ones((half,), F32), jnp.ones((half,), F32)]), LANES // HEAD_DIM)
    return freq, sign


def _rope_row(pos):
    freq, sign = _rope_lanes()
    ang = jnp.float32(pos) * freq[None, :]
    return jnp.cos(ang), jnp.sin(ang) * sign[None, :]


def _rope_tables(seq):
    freq, sign = _rope_lanes()
    hi = (jnp.arange(seq // Q_BLOCK, dtype=F32) * Q_BLOCK)[:, None] * freq[None, :]
    lo = jnp.arange(Q_BLOCK, dtype=F32)[:, None] * freq[None, :]
    ch, sh = jnp.cos(hi)[:, None, :], jnp.sin(hi)[:, None, :]
    cl, sl = jnp.cos(lo)[None, :, :], jnp.sin(lo)[None, :, :]
    cos_t = (ch * cl - sh * sl).reshape(seq, LANES)
    sin_t = ((sh * cl + ch * sl) * sign).reshape(seq, LANES)
    return cos_t, sin_t


def _block_diag(w):
    dl, n, d, e = w.shape
    eye = jnp.eye(n, dtype=w.dtype)
    return (eye[None, :, None, :, None] * w[:, :, :, None, :]).reshape(dl, n * d, n * e)


def _prep_weights(p):
    dl = p["w_in"].shape[0]
    w_in_t = jnp.swapaxes(p["w_in"], 1, 2)
    q_scale = jnp.where(jnp.arange(w_in_t.shape[1]) < A_WIDTH, Q_SCALE, 1.0).astype(F32)
    w_in_t = (w_in_t * q_scale[None, :, None]).astype(BF16)
    z32 = jnp.zeros((dl, LORA, B_WIDTH), F32)
    wlora = jnp.concatenate([
        jnp.concatenate([p["rwkv_w_up"], z32], axis=2),
        jnp.concatenate([z32, p["rwkv_a_up"]], axis=2),
        jnp.zeros((dl, LANES - 2 * LORA, 2 * B_WIDTH), F32)], axis=1).astype(BF16)
    wgate = jnp.concatenate([_block_diag(p["lru_gate_a_w"]), _block_diag(p["lru_gate_x_w"])], axis=2).astype(BF16)
    cw = p["lru_conv_w"]
    rows = [p["rwkv_w0"], p["rwkv_a0"], p["rwkv_k_k"], p["rwkv_k_a"], p["rwkv_lnx_w"], p["rwkv_lnx_b"],
            p["rwkv_r_k"].reshape(dl, B_WIDTH), p["lru_conv_b"], p["lru_gate_a_b"], p["lru_gate_x_b"],
            p["lru_lambda"], cw[:, 0], cw[:, 1], cw[:, 2], cw[:, 3], jnp.zeros((dl, B_WIDTH), F32)]
    return {
        "sinks": p["attn_sinks"],
        "norm_pre": p["norm_pre"].reshape(dl, 1, D_MODEL),
        "norm_post": p["norm_post"].reshape(dl, 1, D_MODEL),
        "w_in": w_in_t,
        "w_out": p["w_out"].astype(BF16),
        "mu": jnp.concatenate([p["rwkv_mu"], jnp.zeros((dl, B_SHIFT_PAD - B_SHIFT), F32)],
                              axis=1).reshape(dl, 1, B_SHIFT_PAD),
        "vec": jnp.stack(rows, axis=1),
        "wlora": wlora,
        "wgate": wgate,
    }


def _forward(x_prompt, x_sample, c_prompt, c_sample, cache_swa_k, cache_swa_v, state_rwkv_shift,
             state_rwkv_wkv, state_lru_conv, state_lru_h, p, tb):
    bp, seq, _ = x_prompt.shape
    nb = x_sample.shape[0]
    n_slot = cache_swa_k.shape[2]
    assert nb % SUBLANES == 0 and nb % B_HEADS == 0 and bp <= SUBLANES and seq % tb == 0

    pad_rows = (-(nb + bp)) % SUBLANES
    c_all = jnp.concatenate([c_sample, c_prompt, jnp.zeros((pad_rows, D_MODEL), F32)], axis=0)
    mod = _mod_call(c_all, p["w_mod"], p["b_mod"])
    wts = _prep_weights(p)

    cos_p, sin_p = _rope_tables(seq)
    cos_s, sin_s = _rope_row(PAST_LEN)

    kt = jnp.transpose(cache_swa_k, (0, 1, 3, 4, 2)).reshape(DEPTH, nb * KV_WIDTH, n_slot)
    vt = jnp.transpose(cache_swa_v, (0, 1, 3, 4, 2)).reshape(DEPTH, nb * KV_WIDTH, n_slot)
    sprev_t = jnp.swapaxes(state_rwkv_shift, 1, 2)
    wkv_t = jnp.transpose(state_rwkv_wkv, (0, 2, 3, 4, 1)).reshape(DEPTH, B_HEADS * HEAD_DIM * HEAD_DIM, nb)
    conv_t = jnp.swapaxes(state_lru_conv, 1, 2)
    ys, knew_t, vnew_t, shift_t, wkvo_t, convo_t, hlo = _sample_layers(
        x_sample.reshape(nb, D_MODEL), mod, wts, cos_s, sin_s, kt, vt, sprev_t, wkv_t, conv_t, state_lru_h)
    outs_s = (
        jnp.transpose(knew_t.reshape(DEPTH, 1, A_KV_HEADS, HEAD_DIM, nb), (0, 4, 1, 2, 3)),
        jnp.transpose(vnew_t.reshape(DEPTH, 1, A_KV_HEADS, HEAD_DIM, nb), (0, 4, 1, 2, 3)),
        jnp.swapaxes(shift_t, 1, 2),
        jnp.transpose(wkvo_t.reshape(DEPTH, B_HEADS, HEAD_DIM, HEAD_DIM, nb), (0, 4, 1, 2, 3)),
        jnp.swapaxes(convo_t, 1, 2),
        hlo,
    )

    yp = x_prompt
    outs_p = []
    for l in range(DEPTH):
        yp, kv_p, sh_p, wkv_p, conv_p, h_p = _prompt_layer(l, yp, mod, wts, cos_p, sin_p, nb, tb)
        kv_p = jnp.transpose(kv_p.reshape(bp, 2, A_KV_HEADS, HEAD_DIM, WINDOW), (1, 0, 4, 2, 3))
        outs_p.append((
            kv_p[0], kv_p[1],
            sh_p[:, SUBLANES - 1, :B_SHIFT],
            jnp.stack([wkv_p[:, HEAD_DIM * hd:HEAD_DIM * (hd + 1), HEAD_DIM * hd:HEAD_DIM * (hd + 1)]
                       for hd in range(B_HEADS)], axis=1),
            conv_p[:, SUBLANES - (CONV_W - 1):, :],
            h_p[:, SUBLANES - 1, :],
        ))
    sp = [jnp.stack(z) for z in zip(*outs_p)]
    return (yp, ys.reshape(nb, 1, D_MODEL), *sp, *outs_s)


def kernel(x_prompt, x_sample, c_prompt, c_sample, cache_swa_k, cache_swa_v, state_rwkv_shift, state_rwkv_wkv, state_lru_conv, state_lru_h, norm_pre, norm_post, w_mod, b_mod, w_in, w_out, attn_sinks, rwkv_mu, rwkv_w0, rwkv_w_up, rwkv_a0, rwkv_a_up, rwkv_k_k, rwkv_k_a, rwkv_r_k, rwkv_lnx_w, rwkv_lnx_b, lru_conv_w, lru_conv_b, lru_gate_a_w, lru_gate_a_b, lru_gate_x_w, lru_gate_x_b, lru_lambda):
    p = dict(norm_pre=norm_pre, norm_post=norm_post, w_mod=w_mod, b_mod=b_mod, w_in=w_in, w_out=w_out,
             attn_sinks=attn_sinks, rwkv_mu=rwkv_mu, rwkv_w0=rwkv_w0, rwkv_w_up=rwkv_w_up, rwkv_a0=rwkv_a0,
             rwkv_a_up=rwkv_a_up, rwkv_k_k=rwkv_k_k, rwkv_k_a=rwkv_k_a, rwkv_r_k=rwkv_r_k,
             rwkv_lnx_w=rwkv_lnx_w, rwkv_lnx_b=rwkv_lnx_b, lru_conv_w=lru_conv_w, lru_conv_b=lru_conv_b,
             lru_gate_a_w=lru_gate_a_w, lru_gate_a_b=lru_gate_a_b, lru_gate_x_w=lru_gate_x_w,
             lru_gate_x_b=lru_gate_x_b, lru_lambda=lru_lambda)
    return _forward(x_prompt, x_sample, c_prompt, c_sample, cache_swa_k, cache_swa_v, state_rwkv_shift,
                    state_rwkv_wkv, state_lru_conv, state_lru_h, p, TIME_BLOCK)
```

```python
import functools

import numpy as np
import jax
import jax.numpy as jnp
from jax import lax
from jax.experimental import pallas as pl
from jax.experimental.pallas import tpu as pltpu

F32 = jnp.float32
BF16 = jnp.bfloat16

D_MODEL = 1024
DEPTH = 2
A_HEADS = 8
A_KV_HEADS = 2
HEAD_DIM = 64
A_WIDTH = A_HEADS * HEAD_DIM
KV_WIDTH = A_KV_HEADS * HEAD_DIM
WINDOW = 128
Q_BLOCK = 128
ROPE_THETA = 10000.0
B_WIDTH = 256
B_HEADS = 4
LORA = 32
B_SHIFT = 3 * B_WIDTH + 2 * LORA
B_SHIFT_PAD = 3 * B_WIDTH + 128
LNX_EPS = 1e-5 * 8 ** 2
C_WIDTH = 256
CONV_W = 4
LRU_C = 8.0
EPS = 1e-6
PAST_LEN = 8192
EXP_NEG_HALF = float(np.exp(-0.5))
LOG2_E = float(np.log2(np.e))
Q_SCALE = HEAD_DIM ** -0.5 * LOG2_E

LANES = 128
SUBLANES = 8
VMEM_LIMIT_BYTES = 56 * 1024 * 1024

OFF_Q = 0
OFF_K = OFF_Q + A_WIDTH
OFF_V = OFF_K + KV_WIDTH
OFF_GA = OFF_V + KV_WIDTH
OFF_PB = OFF_GA + A_WIDTH
OFF_GB = OFF_PB + B_SHIFT_PAD
OFF_XC = OFF_GB + B_WIDTH
OFF_GC = OFF_XC + C_WIDTH
P_PAD = OFF_GC + C_WIDTH
P_IN = P_PAD - (B_SHIFT_PAD - B_SHIFT)

CHUNK = 64
TIME_BLOCK = 512
SEQS_PER_STEP = 1
PROJ_TILE = 256

(V_W0, V_A0, V_KK, V_KA, V_LNW, V_LNB, V_RK, V_CB, V_GAB, V_GXB, V_LAM,
 V_CW0, V_CW1, V_CW2, V_CW3) = range(15)
VEC_ROWS = 16

(T_DECAY, T_NKK, T_KKA, T_KMOD, T_R, T_V) = range(6)

_NT = (((1,), (1,)), ((), ()))
_TN = (((0,), (0,)), ((), ()))


def _bf(x):
    return x.astype(BF16)


def _dot(a, b):
    return jnp.dot(a, b, preferred_element_type=F32)


def _dot_b(a, b):
    return jnp.dot(a, b, preferred_element_type=F32).astype(BF16)


def _dot_nt(a, b):
    return lax.dot_general(a, b, _NT, preferred_element_type=F32)


def _dot_tn(a, b):
    return lax.dot_general(a, b, _TN, preferred_element_type=F32)


def _iota(shape, dim):
    return lax.broadcasted_iota(jnp.int32, shape, dim)


def _sigmoid(x):
    return jax.nn.sigmoid(x)


def _silu(x):
    return x * _sigmoid(x)


def _softplus(x):
    return jnp.maximum(x, 0.0) + jnp.log1p(jnp.exp(-jnp.abs(x)))


def _head_ones(n):
    r = _iota((n, n), 0) // HEAD_DIM
    c = _iota((n, n), 1) // HEAD_DIM
    return jnp.where(r == c, 1.0, 0.0).astype(BF16)


def _head_sum(x, ones_bd):
    return _dot(_bf(x), ones_bd)


def _rms_scale(x):
    ms = jnp.mean(x * x, axis=-1, keepdims=True)
    return x * lax.rsqrt(ms + EPS)


def _affine_scan(a, b, h0):
    rows, n = a.shape
    groups = rows // SUBLANES
    a = a.reshape(groups, SUBLANES, n)
    b = b.reshape(groups, SUBLANES, n)
    sub = _iota((1, SUBLANES, n), 1)
    d = 1
    while d < SUBLANES:
        keep = sub >= d
        a_sh = jnp.where(keep, pltpu.roll(a, d, 1), 1.0)
        b_sh = jnp.where(keep, pltpu.roll(b, d, 1), 0.0)
        b = b + a * b_sh
        a = a * a_sh
        d *= 2
    outs = []
    carry = h0
    for g in range(groups):
        hg = b[g] + a[g] * carry
        outs.append(hg)
        carry = hg[SUBLANES - 1:SUBLANES, :]
    return jnp.concatenate(outs, axis=0)


def _shift_rows(x, j, tail):
    rows, n = x.shape
    groups = rows // SUBLANES
    x3 = jnp.concatenate([tail, x], axis=0).reshape(groups + 1, SUBLANES, n)
    r3 = pltpu.roll(x3, j, 1)
    sub = _iota((1, SUBLANES, n), 1)
    return jnp.where(sub >= j, r3[1:], r3[:-1]).reshape(rows, n)


def _cumsum_chunks(x, chunk):
    rows, n = x.shape
    groups = rows // SUBLANES
    x = x.reshape(groups, SUBLANES, n)
    sub = _iota((1, SUBLANES, n), 1)
    d = 1
    while d < SUBLANES:
        x = x + jnp.where(sub >= d, pltpu.roll(x, d, 1), 0.0)
        d *= 2
    outs = []
    for g in range(groups):
        xg = x[g]
        if (g * SUBLANES) % chunk != 0:
            xg = xg + outs[-1][SUBLANES - 1:SUBLANES, :]
        outs.append(xg)
    return jnp.concatenate(outs, axis=0)


def _rope128(z, cos, sin_signed):
    lane = _iota(z.shape, 1)
    first = (lane & 32) == 0
    sw = jnp.where(first, pltpu.roll(z, 96, 1), pltpu.roll(z, 32, 1))
    return z * cos + sw * sin_signed


def _rwkv_pre(xs, vec, wlora, ones_bd):
    r = xs[:, 0:B_WIDTH]
    k = xs[:, B_WIDTH:2 * B_WIDTH]
    v = xs[:, 2 * B_WIDTH:3 * B_WIDTH]
    lor = xs[:, 3 * B_WIDTH:B_SHIFT_PAD]
    lane = _iota(lor.shape, 1)
    z = jnp.where(lane < LORA, jnp.tanh(lor), lor)
    wa = _dot(_bf(z), wlora)
    zw = vec[V_W0:V_W0 + 1, :] + wa[:, :B_WIDTH]
    ld = -EXP_NEG_HALF * _sigmoid(zw)
    a = _sigmoid(vec[V_A0:V_A0 + 1, :] + wa[:, B_WIDTH:])
    kk = k * vec[V_KK:V_KK + 1, :]
    kkn = kk * jnp.minimum(lax.rsqrt(_head_sum(kk * kk, ones_bd)), 1e12)
    kmod = k * (1.0 + (a - 1.0) * vec[V_KA:V_KA + 1, :])
    return r, k, v, ld, a, kkn, kmod


def _rwkv_post(y, r, kmod, v, gb, vec, ones_bd):
    inv = 1.0 / HEAD_DIM
    mu = _head_sum(y, ones_bd) * inv
    yc = y - mu
    var = _head_sum(yc * yc, ones_bd) * inv
    yn = yc * lax.rsqrt(var + LNX_EPS)
    yn = yn * vec[V_LNW:V_LNW + 1, :] + vec[V_LNB:V_LNB + 1, :]
    bonus = _head_sum(r * kmod * vec[V_RK:V_RK + 1, :], ones_bd) * v
    return (yn + bonus) * _silu(gb)


def _lru_gates(u, vec, wgate):
    gates = _dot(_bf(u), wgate)
    gr = _sigmoid(gates[:, :C_WIDTH] + vec[V_GAB:V_GAB + 1, :])
    gi = _sigmoid(gates[:, C_WIDTH:] + vec[V_GXB:V_GXB + 1, :])
    sp = _softplus(-vec[V_LAM:V_LAM + 1, :])
    log_a = -LRU_C * gr * sp
    a = jnp.exp(log_a)
    mult = jnp.sqrt(1.0 - a * a)
    return a, mult, gi


def _project(h, win_ref, proj_s, group):
    src, dst, n = group
    proj_s[:, dst:dst + n] = _dot_nt(h, win_ref[src:src + n, :])


_DELTA = B_SHIFT_PAD - B_SHIFT
GRP_PB = (OFF_PB, OFF_PB, B_SHIFT_PAD)
GRP_QKV = (OFF_Q, OFF_Q, OFF_GA - OFF_Q)
GRP_GA = (OFF_GA, OFF_GA, A_WIDTH)
GRP_GB = (OFF_GB - _DELTA, OFF_GB, B_WIDTH)
GRP_XC = (OFF_XC - _DELTA, OFF_XC, C_WIDTH)
GRP_GC = (OFF_GC - _DELTA, OFF_GC, C_WIDTH)


def _mod_kernel(c_ref, w_ref, b_ref, o_ref):
    o_ref[...] = _dot(_bf(_silu(c_ref[...])), _bf(w_ref[...])) + b_ref[...]


def _mod_call(c_all, w_mod, b_mod):
    rows = c_all.shape[0]
    return pl.pallas_call(
        _mod_kernel,
        grid=(DEPTH, 3),
        in_specs=[
            pl.BlockSpec((rows, D_MODEL), lambda l, j: (0, 0)),
            pl.BlockSpec((None, D_MODEL, D_MODEL), lambda l, j: (l, 0, j)),
            pl.BlockSpec((None, 1, D_MODEL), lambda l, j: (l, 0, j)),
        ],
        out_specs=pl.BlockSpec((None, None, rows, D_MODEL), lambda l, j: (l, j, 0, 0)),
        out_shape=jax.ShapeDtypeStruct((DEPTH, 3, rows, D_MODEL), F32),
        compiler_params=pltpu.CompilerParams(
            dimension_semantics=("arbitrary", "arbitrary"), vmem_limit_bytes=VMEM_LIMIT_BYTES),
        name="adaln_mod",
    )(c_all, w_mod, b_mod.reshape(DEPTH, 1, 3 * D_MODEL))


def _prompt_kernel(sinks_ref, x_ref, mod_ref, npre_ref, npost_ref, win_ref, wout_ref,
                   cos_ref, sin_ref, mu_ref, vec_ref, wlora_ref, wgate_ref,
                   y_ref, kv_ref, shift_ref, wkv_ref, conv_ref, hlru_ref,
                   *scratch, layer):
    n_seq = x_ref.shape[0]
    tb = x_ref.shape[1]
    t = pl.program_id(1)
    per_seq = len(scratch) // n_seq
    seq_scratch = [scratch[q * per_seq:(q + 1) * per_seq] for q in range(n_seq)]

    @pl.when(t == 0)
    def _():
        for kvprev, pprev, xcprev, hprev, gstate, _, _, _ in seq_scratch:
            kvprev[...] = jnp.zeros_like(kvprev)
            pprev[...] = jnp.zeros_like(pprev)
            xcprev[...] = jnp.zeros_like(xcprev)
            hprev[...] = jnp.zeros_like(hprev)
            gstate[...] = jnp.zeros_like(gstate)

    vec = vec_ref[...]
    ones_bd = _head_ones(B_WIDTH)
    cos = cos_ref[...]
    sin = sin_ref[...]
    nck = tb // CHUNK
    sls = [slice(c * CHUNK, (c + 1) * CHUNK) for c in range(nck)]
    rng = range(nck)
    pairs = [(c, i) for c in range(A_HEADS // 2) for i in range(tb // Q_BLOCK)]

    lo_kv = _iota((tb + WINDOW, KV_WIDTH), 1) < HEAD_DIM
    lo_q = _iota((Q_BLOCK, LANES), 1) < HEAD_DIM
    qi = _iota((Q_BLOCK, 2 * Q_BLOCK), 0)
    sj = _iota((Q_BLOCK, 2 * Q_BLOCK), 1)
    band = (sj >= qi) & (sj <= qi + WINDOW)
    first_lo = jnp.where(t == 0, Q_BLOCK, 0)
    neg_inf = -jnp.inf
    reset = (_iota((SUBLANES, C_WIDTH), 0) + t * tb) == 0

    ri = _iota((CHUNK, B_HEADS * CHUNK), 0)
    cj = _iota((CHUNK, B_HEADS * CHUNK), 1) % CHUNK
    strict_b = jnp.where(ri > cj, 1.0, 0.0).astype(BF16)
    incl_b = jnp.where(ri >= cj, 1.0, 0.0).astype(BF16)
    eye_f = jnp.where(ri == cj, 1.0, 0.0)
    bd_heads = (_iota((B_WIDTH, B_WIDTH), 0) // HEAD_DIM) == (_iota((B_WIDTH, B_WIDTH), 1) // HEAD_DIM)

    def variants(z):
        zr = pltpu.roll(z, HEAD_DIM, 1)
        a_ = _bf(jnp.where(lo_kv, z, 0.0))
        b_ = _bf(jnp.where(lo_kv, 0.0, z))
        c_ = _bf(jnp.where(lo_kv, zr, 0.0))
        d_ = _bf(jnp.where(lo_kv, 0.0, zr))
        return ((a_, d_), (c_, b_))

    def win_stack(var, c, i):
        g, r0 = c // 2, i * Q_BLOCK
        return jnp.concatenate([var[g][0][r0:r0 + 2 * Q_BLOCK], var[g][1][r0:r0 + 2 * Q_BLOCK]], axis=0)

    def stk(z):
        zb = _bf(z)
        return jnp.concatenate([zb] * B_HEADS, axis=0) * ones_bd

    def side(q, out):
        kvprev, _, xcprev, hprev, _, proj, mix, _ = seq_scratch[q]

        k_rot = _rope128(proj[:, OFF_K:OFF_K + KV_WIDTH], cos, sin)
        v_att = proj[:, OFF_V:OFF_V + KV_WIDTH]
        kvar = variants(jnp.concatenate([kvprev[:, 0:KV_WIDTH], k_rot], axis=0))
        vvar = variants(jnp.concatenate([kvprev[:, KV_WIDTH:2 * KV_WIDTH], v_att], axis=0))
        out["k_tail"] = k_rot[tb - WINDOW:]
        out["v_tail"] = v_att[tb - WINDOW:]
        yield
        qcols = [_bf(_rope128(proj[:, OFF_Q + LANES * c:OFF_Q + LANES * (c + 1)], cos, sin))
                 for c in range(A_HEADS // 2)]
        scores = [_dot_nt(qcols[c][i * Q_BLOCK:(i + 1) * Q_BLOCK], win_stack(kvar, c, i))
                  for c, i in pairs]
        yield

        xc = proj[:, OFF_XC:OFF_XC + C_WIDTH]
        u = vec[V_CB:V_CB + 1, :] + xc * vec[V_CW3:V_CW3 + 1, :]
        for j in range(1, CONV_W):
            u = u + _shift_rows(xc, j, xcprev[...]) * vec[V_CW3 - j:V_CW3 - j + 1, :]
        a_l, mult, gi = _lru_gates(u, vec, wgate_ref[...])
        a_l = jnp.concatenate([jnp.where(reset, 0.0, a_l[0:SUBLANES]), a_l[SUBLANES:]], axis=0)
        mult = jnp.concatenate([jnp.where(reset, 1.0, mult[0:SUBLANES]), mult[SUBLANES:]], axis=0)
        b_l = mult * gi * u
        out["xc_tail"] = xc[tb - SUBLANES:, :]
        yield

        probs, norms = [], []
        for (c, i), s in zip(pairs, scores):
            mask = band & (sj >= first_lo) if i == 0 else band
            ps, invs = [], []
            for hh in range(2):
                sink = sinks_ref[layer, 2 * c + hh] * LOG2_E
                sh = jnp.where(mask, s[:, 2 * Q_BLOCK * hh:2 * Q_BLOCK * (hh + 1)], neg_inf)
                m = jnp.maximum(jnp.max(sh, axis=-1, keepdims=True), sink)
                p = jnp.exp2(sh - m)
                invs.append(1.0 / (jnp.sum(p, axis=-1, keepdims=True) + jnp.exp2(sink - m)))
                ps.append(_bf(p))
            probs.append(jnp.concatenate(ps, axis=1))
            norms.append(jnp.where(lo_q, invs[0], invs[1]))
            if i == tb // Q_BLOCK - 1:
                yield
        for (c, i), p, inv in zip(pairs, probs, norms):
            r0 = i * Q_BLOCK
            o = _dot(p, win_stack(vvar, c, i)) * inv
            ga = proj[r0:r0 + Q_BLOCK, OFF_GA + LANES * c:OFF_GA + LANES * (c + 1)]
            mix[r0:r0 + Q_BLOCK, LANES * c:LANES * (c + 1)] = o * _silu(ga)
        yield
        h_l = _affine_scan(a_l, b_l, hprev[SUBLANES - 1:SUBLANES, :])
        gc = proj[:, OFF_GC:OFF_GC + C_WIDTH]
        mix[:, A_WIDTH + B_WIDTH:] = h_l * _silu(gc)
        out["h_tail"] = h_l[tb - SUBLANES:, :]

    def sequence(q, out):
        kvprev, pprev, xcprev, hprev, gstate, proj, mix, ywkv = seq_scratch[q]

        x = x_ref[q]
        brow = pl.ds(pl.program_id(0) * n_seq + q, 1)
        shift_m = mod_ref[0, brow, :]
        scale_m = mod_ref[1, brow, :]
        gate_m = mod_ref[2, brow, :]
        gain = npre_ref[...] * (1.0 + scale_m)
        halves = (slice(0, tb // 2), slice(tb // 2, tb))
        prepped = [None, None]

        def project(i):
            h = _bf(_rms_scale(x[halves[i]]) * gain + shift_m)
            for src, dst, n in (GRP_PB, GRP_QKV, GRP_XC, GRP_GA, GRP_GB, GRP_GC):
                for off in range(0, n, PROJ_TILE):
                    w = min(PROJ_TILE, n - off)
                    proj[halves[i], dst + off:dst + off + w] = _dot_nt(h, win_ref[src + off:src + off + w, :])
                    yield

        def prepare(i):
            pb_h = proj[halves[i], OFF_PB:OFF_PB + B_SHIFT_PAD]
            tail = pprev[...] if i == 0 else proj[tb // 2 - SUBLANES:tb // 2, OFF_PB:OFF_PB + B_SHIFT_PAD]
            prev = _shift_rows(pb_h, 1, tail)
            xs = pb_h + (prev - pb_h) * mu_ref[...]
            yield
            r_h, _, v_h, ld, a, kkn, kmod_h = _rwkv_pre(xs, vec, wlora_ref[...], ones_bd)
            kka = kkn * a
            yield
            lcum = _cumsum_chunks(ld, CHUNK)
            e_l = jnp.exp(lcum)
            e_nl = jnp.exp(-lcum)
            prepped[i] = dict(r=r_h, v=v_h, kmod=kmod_h, e_l=e_l, rt=r_h * e_l, at=-kkn * jnp.exp(lcum - ld),
                              ktil=kmod_h * e_nl, btil=kka * e_nl)
            yield

        for _ in project(0):
            yield
        second, first = project(1), prepare(0)
        for _ in second:
            next(first, None)
            yield
        for _ in first:
            yield
        for _ in prepare(1):
            yield
        cat = lambda name: jnp.concatenate([prepped[0][name], prepped[1][name]], axis=0)
        r, v, kmod, e_l, rt, at, ktil, btil = (cat(n_) for n_ in ("r", "v", "kmod", "e_l", "rt", "at", "ktil", "btil"))
        pb = proj[:, OFF_PB:OFF_PB + B_SHIFT_PAD]
        gam = [e_l[s.stop - 1:s.stop, :] for s in sls]
        khat = [ktil[sls[c]] * gam[c] for c in rng]
        bhat = [btil[sls[c]] * gam[c] for c in rng]
        yield
        a_s = [stk(at[s]) for s in sls]
        b_s = [stk(btil[s]) for s in sls]
        k_s = [stk(ktil[s]) for s in sls]
        yield
        v_s = [stk(v[s]) for s in sls]
        bh_s = [stk(bhat[c]) for c in rng]
        sc = [_dot_nt(_bf(jnp.concatenate([at[sls[c]], rt[sls[c]]], axis=0)),
                      jnp.concatenate([b_s[c], k_s[c]], axis=0)) for c in rng]
        yield "side may start"

        pw = [_bf(sc[c][0:CHUNK, 0:B_WIDTH]) * strict_b for c in rng]
        lak = [_bf(sc[c][0:CHUNK, B_WIDTH:]) * strict_b for c in rng]
        mrb = [_bf(sc[c][CHUNK:, 0:B_WIDTH]) * incl_b for c in rng]
        mrk = [_bf(sc[c][CHUNK:, B_WIDTH:]) * incl_b for c in rng]
        tacc = [pw[c].astype(F32) + eye_f for c in rng]
        pw = [_dot_b(pw[c], stk(pw[c])) for c in rng]
        yield
        for _ in range(4):
            res = [_dot(jnp.concatenate([_bf(tacc[c]), pw[c]], axis=0), stk(pw[c])) for c in rng]
            tacc = [tacc[c] + res[c][0:CHUNK] for c in rng]
            pw = [_bf(res[c][CHUNK:]) for c in rng]
            yield
        tinv = [_bf(tacc[c] + _dot(_bf(tacc[c]), stk(pw[c]))) for c in rng]
        lv = [_dot_b(lak[c], v_s[c]) for c in rng]
        yield
        w_c = [_dot_b(tinv[c], a_s[c]) for c in rng]
        z_c = [_dot(tinv[c], stk(lv[c])) for c in rng]
        yield
        w_s = [stk(w_c[c]) for c in rng]
        rp = [_bf(rt[sls[c]] + _dot(mrb[c], w_s[c])) for c in rng]
        y0 = [_dot(jnp.concatenate([mrb[c], mrk[c]], axis=1), jnp.concatenate([stk(z_c[c]), v_s[c]], axis=0))
              for c in rng]
        pp = [_bf(_dot_tn(w_s[c], bh_s[c])) for c in rng]
        qq = [jnp.where(bd_heads,
                        _dot_tn(_bf(jnp.concatenate([z_c[c], v[sls[c]]], axis=0)),
                                _bf(jnp.concatenate([bhat[c], khat[c]], axis=0))), 0.0) for c in rng]
        yield
        gmat = gstate[...]
        for c in rng:
            gb16 = _bf(gmat)
            ywkv[sls[c], :] = _dot_nt(rp[c], gb16) + y0[c]
            gmat = gam[c] * gmat + _dot(gb16, pp[c]) + qq[c]
        gstate[...] = gmat
        yield "side must be done"

        gb = proj[:, OFF_GB:OFF_GB + B_WIDTH]
        mix[:, A_WIDTH:A_WIDTH + B_WIDTH] = _rwkv_post(ywkv[...], r, kmod, v, gb, vec, ones_bd)

        yo = _dot(_bf(mix[...]), wout_ref[...])
        y_ref[q] = x + (gate_m * npost_ref[...]) * _rms_scale(yo)

        k_tail, v_tail = out["k_tail"], out["v_tail"]
        kvprev[:, 0:KV_WIDTH] = k_tail
        kvprev[:, KV_WIDTH:2 * KV_WIDTH] = v_tail
        pprev[...] = pb[tb - SUBLANES:, :]
        xcprev[...] = out["xc_tail"]
        hprev[...] = out["h_tail"]
        shift_ref[q] = pb[tb - SUBLANES:, :]
        wkv_ref[q] = gmat
        conv_ref[q] = out["xc_tail"]
        hlru_ref[q] = out["h_tail"]

        @pl.when(t == pl.num_programs(1) - 1)
        def _():
            kv_ref[q, 0] = k_tail.T
            kv_ref[q, 1] = v_tail.T

    shared = [{} for _ in range(n_seq)]
    mains = [sequence(q, shared[q]) for q in range(n_seq)]
    extras = [side(q, shared[q]) for q in range(n_seq)]
    for marker, with_side in (("side may start", False), ("side must be done", True)):
        pending = list(range(n_seq))
        while pending:
            for q in list(pending):
                if with_side:
                    next(extras[q], None)
                if next(mains[q]) == marker:
                    pending.remove(q)
    for g in extras + mains:
        for _ in g:
            pass


def _weight_specs(layer_of, grid_rank):
    def lay(*shape):
        zeros = (0,) * len(shape)
        if grid_rank == 1:
            return pl.BlockSpec((None, *shape), lambda i: (layer_of(i), *zeros))
        return pl.BlockSpec((None, *shape), lambda a, b: (layer_of(a, b), *zeros))
    return [
        lay(1, D_MODEL),
        lay(1, D_MODEL),
        lay(P_IN, D_MODEL),
        lay(D_MODEL, D_MODEL),
    ], [
        lay(1, B_SHIFT_PAD),
        lay(VEC_ROWS, B_WIDTH),
        lay(LANES, 2 * B_WIDTH),
        lay(C_WIDTH, 2 * C_WIDTH),
    ]


def _prompt_layer(layer, x, mod, wts, cos_t, sin_t, mod_row0, tb):
    bsz, seq, _ = x.shape
    nt = seq // tb
    sps = SEQS_PER_STEP
    per_b3 = lambda b, t: (b, 0, 0)
    w_head, w_tail = _weight_specs(lambda b, t: layer, 2)
    return pl.pallas_call(
        functools.partial(_prompt_kernel, layer=layer),
        grid=(bsz // sps, nt),
        in_specs=[
            pl.BlockSpec(memory_space=pltpu.SMEM),
            pl.BlockSpec((sps, tb, D_MODEL), lambda b, t: (b, t, 0)),
            pl.BlockSpec((None, 3, SUBLANES, D_MODEL),
                         lambda b, t: (layer, 0, mod_row0 // SUBLANES, 0)),
            *w_head,
            pl.BlockSpec((tb, LANES), lambda b, t: (t, 0)),
            pl.BlockSpec((tb, LANES), lambda b, t: (t, 0)),
            *w_tail,
        ],
        out_specs=[
            pl.BlockSpec((sps, tb, D_MODEL), lambda b, t: (b, t, 0)),
            pl.BlockSpec((sps, 2, KV_WIDTH, WINDOW), lambda b, t: (b, 0, 0, 0)),
            pl.BlockSpec((sps, SUBLANES, B_SHIFT_PAD), per_b3),
            pl.BlockSpec((sps, B_WIDTH, B_WIDTH), per_b3),
            pl.BlockSpec((sps, SUBLANES, C_WIDTH), per_b3),
            pl.BlockSpec((sps, SUBLANES, C_WIDTH), per_b3),
        ],
        out_shape=[
            jax.ShapeDtypeStruct((bsz, seq, D_MODEL), F32),
            jax.ShapeDtypeStruct((bsz, 2, KV_WIDTH, WINDOW), F32),
            jax.ShapeDtypeStruct((bsz, SUBLANES, B_SHIFT_PAD), F32),
            jax.ShapeDtypeStruct((bsz, B_WIDTH, B_WIDTH), F32),
            jax.ShapeDtypeStruct((bsz, SUBLANES, C_WIDTH), F32),
            jax.ShapeDtypeStruct((bsz, SUBLANES, C_WIDTH), F32),
        ],
        scratch_shapes=[
            pltpu.VMEM((WINDOW, 2 * KV_WIDTH), F32),
            pltpu.VMEM((SUBLANES, B_SHIFT_PAD), F32),
            pltpu.VMEM((SUBLANES, C_WIDTH), F32),
            pltpu.VMEM((SUBLANES, C_WIDTH), F32),
            pltpu.VMEM((B_WIDTH, B_WIDTH), F32),
            pltpu.VMEM((tb, P_PAD), F32),
            pltpu.VMEM((tb, D_MODEL), F32),
            pltpu.VMEM((tb, B_WIDTH), F32),
        ] * sps,
        compiler_params=pltpu.CompilerParams(
            dimension_semantics=("arbitrary", "arbitrary"), vmem_limit_bytes=VMEM_LIMIT_BYTES),
        name="prompt_layer",
    )(wts["sinks"], x, mod, wts["norm_pre"], wts["norm_post"], wts["w_in"], wts["w_out"], cos_t, sin_t,
      wts["mu"], wts["vec"], wts["wlora"], wts["wgate"])


def _sample_kernel(sinks_ref, x_ref, mod_ref, npre_ref, npost_ref, win_ref, wout_ref,
                   cos_ref, sin_ref, mu_ref, vec_ref, wlora_ref, wgate_ref,
                   kt_ref, vt_ref, sprev_ref, wkv_ref, conv_ref, hl_ref,
                   y_ref, knew_ref, vnew_ref, shift_ref, wkvo_ref, convo_ref, hlo_ref,
                   ys_s, proj_s, mix_s, q_s, o_s, knew_s, rw_s, rkv_s, yt_s):
    layer = pl.program_id(0)
    j = pl.program_id(1)
    nb = x_ref.shape[0]
    bb = nb // B_HEADS
    n_slot = kt_ref.shape[1]
    vec = vec_ref[...]
    cos = cos_ref[...]
    sin = sin_ref[...]

    @pl.when(jnp.logical_and(layer == 0, j == 0))
    def _():
        ys_s[...] = x_ref[...]

    @pl.when(j == 0)
    def _():
        ones_bd = _head_ones(B_WIDTH)
        x = ys_s[...]
        h = _bf(_rms_scale(x) * npre_ref[...] * (1.0 + mod_ref[1]) + mod_ref[0])
        for grp in (GRP_QKV, GRP_GA, GRP_PB, GRP_GB, GRP_XC, GRP_GC):
            _project(h, win_ref, proj_s, grp)
        k_new = _rope128(proj_s[:, OFF_K:OFF_K + KV_WIDTH], cos, sin)
        knew_s[...] = k_new
        knew_ref[...] = k_new.T
        vnew_ref[...] = proj_s[:, OFF_V:OFF_V + KV_WIDTH].T
        pb = proj_s[:, OFF_PB:OFF_PB + B_SHIFT_PAD]
        pb_t = pb.T
        shift_ref[...] = pb_t[0:B_SHIFT, :]
        prev = jnp.concatenate([sprev_ref[...], jnp.zeros((B_SHIFT_PAD - B_SHIFT, nb), F32)], axis=0).T
        xs = pb + (prev - pb) * mu_ref[...]
        r, k, v, ld, a, kkn, kmod = _rwkv_pre(xs, vec, wlora_ref[...], ones_bd)
        rw_s[T_DECAY] = jnp.exp(ld).T
        rw_s[T_NKK] = (-kkn).T
        rw_s[T_KKA] = (kkn * a).T
        rw_s[T_KMOD] = kmod.T
        rw_s[T_R] = r.T
        rw_s[T_V] = v.T
        rkv_s[0] = r
        rkv_s[1] = kmod
        rkv_s[2] = v

    rs = pl.ds(pl.multiple_of(j * bb, SUBLANES), bb)
    lane = _iota((bb, LANES), 1)
    lo = lane < HEAD_DIM
    for c in range(A_HEADS // 2):
        g = c // 2
        in_g = lo if g == 0 else jnp.logical_not(lo)
        qcol = _rope128(proj_s[rs, OFF_Q + LANES * c:OFF_Q + LANES * (c + 1)], cos, sin)
        qrol = pltpu.roll(qcol, HEAD_DIM, 1)
        for hh in range(2):
            hd = 2 * c + hh
            q_s[hd * bb:(hd + 1) * bb, :] = jnp.where(in_g, qcol if hh == g else qrol, 0.0)
    knew_blk = knew_s[rs, :]
    vnew_blk = proj_s[rs, OFF_V:OFF_V + KV_WIDTH]
    hrow = _iota((A_HEADS, 1), 0)
    sink_col = jnp.zeros((A_HEADS, 1), F32)
    for hd in range(A_HEADS):
        sink_col = jnp.where(hrow == hd, sinks_ref[layer, hd] * LOG2_E, sink_col)

    qbs = [q_s[pl.ds(b, A_HEADS, stride=bb), :] for b in range(bb)]
    s_c = [_dot(_bf(qbs[b]), _bf(kt_ref[b * KV_WIDTH:(b + 1) * KV_WIDTH, :])) for b in range(bb)]
    s_n = [jnp.sum(qbs[b] * knew_blk[b:b + 1, :], axis=-1, keepdims=True) for b in range(bb)]
    p_c, p_n = [], []
    for b in range(bb):
        m = jnp.maximum(jnp.maximum(jnp.max(s_c[b], axis=-1, keepdims=True), s_n[b]), sink_col)
        e_c = jnp.exp2(s_c[b] - m)
        e_n = jnp.exp2(s_n[b] - m)
        inv = 1.0 / (jnp.sum(e_c, axis=-1, keepdims=True) + e_n + jnp.exp2(sink_col - m))
        p_c.append(_bf(e_c * inv))
        p_n.append(e_n * inv)
    for b in range(bb):
        o = _dot_nt(p_c[b], _bf(vt_ref[b * KV_WIDTH:(b + 1) * KV_WIDTH, :]))
        o_s[pl.ds(b, A_HEADS, stride=bb), :] = o + p_n[b] * vnew_blk[b:b + 1, :]
    for c in range(A_HEADS // 2):
        g = c // 2
        halves = []
        for hh in range(2):
            oh = o_s[(2 * c + hh) * bb:(2 * c + hh + 1) * bb, :]
            halves.append(oh if hh == g else pltpu.roll(oh, HEAD_DIM, 1))
        ga = proj_s[rs, OFF_GA + LANES * c:OFF_GA + LANES * (c + 1)]
        mix_s[rs, LANES * c:LANES * (c + 1)] = jnp.where(lo, halves[0], halves[1]) * _silu(ga)

    hrows = pl.ds(pl.multiple_of(j * HEAD_DIM, HEAD_DIM), HEAD_DIM)
    w_t = rw_s[T_DECAY, hrows, :]
    nkk_t = rw_s[T_NKK, hrows, :]
    kka_t = rw_s[T_KKA, hrows, :]
    k_t = rw_s[T_KMOD, hrows, :]
    r_t = rw_s[T_R, hrows, :]

    def value_row(vi, carry):
        krows = pl.ds(pl.multiple_of(vi * HEAD_DIM, HEAD_DIM), HEAD_DIM)
        st = wkv_ref[krows, :]
        sa = jnp.sum(st * nkk_t, axis=0, keepdims=True)
        st_new = st * w_t + sa * kka_t + rw_s[T_V, pl.ds(j * HEAD_DIM + vi, 1), :] * k_t
        wkvo_ref[krows, :] = st_new
        yt_s[pl.ds(j * HEAD_DIM + vi, 1), :] = jnp.sum(st_new * r_t, axis=0, keepdims=True)
        return carry

    lax.fori_loop(0, HEAD_DIM, value_row, 0, unroll=4)

    @pl.when(j == pl.num_programs(1) - 1)
    def _():
        ones_bd = _head_ones(B_WIDTH)
        x = ys_s[...]
        gb = proj_s[:, OFF_GB:OFF_GB + B_WIDTH]
        mix_s[:, A_WIDTH:A_WIDTH + B_WIDTH] = _rwkv_post(yt_s[...].T, rkv_s[0], rkv_s[1], rkv_s[2], gb, vec, ones_bd)
        xc = proj_s[:, OFF_XC:OFF_XC + C_WIDTH]
        gc = proj_s[:, OFF_GC:OFF_GC + C_WIDTH]
        u = vec[V_CB:V_CB + 1, :] + xc * vec[V_CW3:V_CW3 + 1, :]
        for i in range(CONV_W - 1):
            u = u + conv_ref[i] * vec[V_CW0 + i:V_CW0 + i + 1, :]
        a_l, mult, gi = _lru_gates(u, vec, wgate_ref[...])
        h_l = a_l * hl_ref[...] + mult * gi * u
        hlo_ref[...] = h_l
        convo_ref[0] = conv_ref[1]
        convo_ref[1] = conv_ref[2]
        convo_ref[2] = xc
        mix_s[:, A_WIDTH + B_WIDTH:] = h_l * _silu(gc)
        yo = _dot(_bf(mix_s[...]), wout_ref[...])
        y = x + mod_ref[2] * (_rms_scale(yo) * npost_ref[...])
        ys_s[...] = y
        y_ref[...] = y


def _sample_layers(x, mod, wts, cos_r, sin_r, kt, vt, sprev_t, wkv_t, conv_t, hl):
    nb = x.shape[0]
    bb = nb // B_HEADS
    n_slot = kt.shape[2]
    n_hv = HEAD_DIM * HEAD_DIM
    w_head, w_tail = _weight_specs(lambda l, j: l, 2)
    const2 = lambda l, j: (0, 0)
    lay3 = lambda l, j: (l, 0, 0)
    return pl.pallas_call(
        _sample_kernel,
        grid=(DEPTH, B_HEADS),
        in_specs=[
            pl.BlockSpec(memory_space=pltpu.SMEM),
            pl.BlockSpec((nb, D_MODEL), const2),
            pl.BlockSpec((None, 3, nb, D_MODEL), lambda l, j: (l, 0, 0, 0)),
            *w_head,
            pl.BlockSpec((1, LANES), const2),
            pl.BlockSpec((1, LANES), const2),
            *w_tail,
            pl.BlockSpec((None, bb * KV_WIDTH, n_slot), lambda l, j: (l, j, 0)),
            pl.BlockSpec((None, bb * KV_WIDTH, n_slot), lambda l, j: (l, j, 0)),
            pl.BlockSpec((None, B_SHIFT, nb), lay3),
            pl.BlockSpec((None, n_hv, nb), lambda l, j: (l, j, 0)),
            pl.BlockSpec((None, CONV_W - 1, nb, C_WIDTH), lambda l, j: (l, 0, 0, 0)),
            pl.BlockSpec((None, nb, C_WIDTH), lay3),
        ],
        out_specs=[
            pl.BlockSpec((nb, D_MODEL), const2),
            pl.BlockSpec((None, KV_WIDTH, nb), lay3),
            pl.BlockSpec((None, KV_WIDTH, nb), lay3),
            pl.BlockSpec((None, B_SHIFT, nb), lay3),
            pl.BlockSpec((None, n_hv, nb), lambda l, j: (l, j, 0)),
            pl.BlockSpec((None, CONV_W - 1, nb, C_WIDTH), lambda l, j: (l, 0, 0, 0)),
            pl.BlockSpec((None, nb, C_WIDTH), lay3),
        ],
        out_shape=[
            jax.ShapeDtypeStruct((nb, D_MODEL), F32),
            jax.ShapeDtypeStruct((DEPTH, KV_WIDTH, nb), F32),
            jax.ShapeDtypeStruct((DEPTH, KV_WIDTH, nb), F32),
            jax.ShapeDtypeStruct((DEPTH, B_SHIFT, nb), F32),
            jax.ShapeDtypeStruct((DEPTH, B_HEADS * n_hv, nb), F32),
            jax.ShapeDtypeStruct((DEPTH, CONV_W - 1, nb, C_WIDTH), F32),
            jax.ShapeDtypeStruct((DEPTH, nb, C_WIDTH), F32),
        ],
        scratch_shapes=[
            pltpu.VMEM((nb, D_MODEL), F32),
            pltpu.VMEM((nb, P_PAD), F32),
            pltpu.VMEM((nb, D_MODEL), F32),
            pltpu.VMEM((A_HEADS * bb, LANES), F32),
            pltpu.VMEM((A_HEADS * bb, LANES), F32),
            pltpu.VMEM((nb, KV_WIDTH), F32),
            pltpu.VMEM((6, B_WIDTH, nb), F32),
            pltpu.VMEM((3, nb, B_WIDTH), F32),
            pltpu.VMEM((B_WIDTH, nb), F32),
        ],
        compiler_params=pltpu.CompilerParams(
            dimension_semantics=("arbitrary", "arbitrary"), vmem_limit_bytes=VMEM_LIMIT_BYTES),
        name="sample_layers",
    )(wts["sinks"], x, mod, wts["norm_pre"], wts["norm_post"], wts["w_in"], wts["w_out"], cos_r, sin_r,
      wts["mu"], wts["vec"], wts["wlora"], wts["wgate"], kt, vt, sprev_t, wkv_t, conv_t, hl)


def _rope_lanes():
    half = HEAD_DIM // 2
    inv_freq = ROPE_THETA ** (-jnp.arange(half, dtype=F32) / half)
    freq = jnp.tile(inv_freq, LANES // half)
    sign = jnp.tile(jnp.concatenate([-jnp.ones((half,), F32), jnp.ones((half,), F32)]), LANES // HEAD_DIM)
    return freq, sign


def _rope_row(pos):
    freq, sign = _rope_lanes()
    ang = jnp.float32(pos) * freq[None, :]
    return jnp.cos(ang), jnp.sin(ang) * sign[None, :]


def _rope_tables(seq):
    freq, sign = _rope_lanes()
    hi = (jnp.arange(seq // Q_BLOCK, dtype=F32) * Q_BLOCK)[:, None] * freq[None, :]
    lo = jnp.arange(Q_BLOCK, dtype=F32)[:, None] * freq[None, :]
    ch, sh = jnp.cos(hi)[:, None, :], jnp.sin(hi)[:, None, :]
    cl, sl = jnp.cos(lo)[None, :, :], jnp.sin(lo)[None, :, :]
    cos_t = (ch * cl - sh * sl).reshape(seq, LANES)
    sin_t = ((sh * cl + ch * sl) * sign).reshape(seq, LANES)
    return cos_t, sin_t


def _block_diag(w):
    dl, n, d, e = w.shape
    eye = jnp.eye(n, dtype=w.dtype)
    return (eye[None, :, None, :, None] * w[:, :, :, None, :]).reshape(dl, n * d, n * e)


def _prep_weights(p):
    dl = p["w_in"].shape[0]
    w_in_t = jnp.swapaxes(p["w_in"], 1, 2)
    q_scale = jnp.where(jnp.arange(w_in_t.shape[1]) < A_WIDTH, Q_SCALE, 1.0).astype(F32)
    w_in_t = (w_in_t * q_scale[None, :, None]).astype(BF16)
    z32 = jnp.zeros((dl, LORA, B_WIDTH), F32)
    wlora = jnp.concatenate([
        jnp.concatenate([p["rwkv_w_up"], z32], axis=2),
        jnp.concatenate([z32, p["rwkv_a_up"]], axis=2),
        jnp.zeros((dl, LANES - 2 * LORA, 2 * B_WIDTH), F32)], axis=1).astype(BF16)
    wgate = jnp.concatenate([_block_diag(p["lru_gate_a_w"]), _block_diag(p["lru_gate_x_w"])], axis=2).astype(BF16)
    cw = p["lru_conv_w"]
    rows = [p["rwkv_w0"], p["rwkv_a0"], p["rwkv_k_k"], p["rwkv_k_a"], p["rwkv_lnx_w"], p["rwkv_lnx_b"],
            p["rwkv_r_k"].reshape(dl, B_WIDTH), p["lru_conv_b"], p["lru_gate_a_b"], p["lru_gate_x_b"],
            p["lru_lambda"], cw[:, 0], cw[:, 1], cw[:, 2], cw[:, 3], jnp.zeros((dl, B_WIDTH), F32)]
    return {
        "sinks": p["attn_sinks"],
        "norm_pre": p["norm_pre"].reshape(dl, 1, D_MODEL),
        "norm_post": p["norm_post"].reshape(dl, 1, D_MODEL),
        "w_in": w_in_t,
        "w_out": p["w_out"].astype(BF16),
        "mu": jnp.concatenate([p["rwkv_mu"], jnp.zeros((dl, B_SHIFT_PAD - B_SHIFT), F32)],
                              axis=1).reshape(dl, 1, B_SHIFT_PAD),
        "vec": jnp.stack(rows, axis=1),
        "wlora": wlora,
        "wgate": wgate,
    }


def _forward(x_prompt, x_sample, c_prompt, c_sample, cache_swa_k, cache_swa_v, state_rwkv_shift,
             state_rwkv_wkv, state_lru_conv, state_lru_h, p, tb):
    bp, seq, _ = x_prompt.shape
    nb = x_sample.shape[0]
    n_slot = cache_swa_k.shape[2]
    assert nb % SUBLANES == 0 and nb % B_HEADS == 0 and bp <= SUBLANES and seq % tb == 0

    pad_rows = (-(nb + bp)) % SUBLANES
    c_all = jnp.concatenate([c_sample, c_prompt, jnp.zeros((pad_rows, D_MODEL), F32)], axis=0)
    mod = _mod_call(c_all, p["w_mod"], p["b_mod"])
    wts = _prep_weights(p)

    cos_p, sin_p = _rope_tables(seq)
    cos_s, sin_s = _rope_row(PAST_LEN)

    kt = jnp.transpose(cache_swa_k, (0, 1, 3, 4, 2)).reshape(DEPTH, nb * KV_WIDTH, n_slot)
    vt = jnp.transpose(cache_swa_v, (0, 1, 3, 4, 2)).reshape(DEPTH, nb * KV_WIDTH, n_slot)
    sprev_t = jnp.swapaxes(state_rwkv_shift, 1, 2)
    wkv_t = jnp.transpose(state_rwkv_wkv, (0, 2, 3, 4, 1)).reshape(DEPTH, B_HEADS * HEAD_DIM * HEAD_DIM, nb)
    conv_t = jnp.swapaxes(state_lru_conv, 1, 2)
    ys, knew_t, vnew_t, shift_t, wkvo_t, convo_t, hlo = _sample_layers(
        x_sample.reshape(nb, D_MODEL), mod, wts, cos_s, sin_s, kt, vt, sprev_t, wkv_t, conv_t, state_lru_h)
    outs_s = (
        jnp.transpose(knew_t.reshape(DEPTH, 1, A_KV_HEADS, HEAD_DIM, nb), (0, 4, 1, 2, 3)),
        jnp.transpose(vnew_t.reshape(DEPTH, 1, A_KV_HEADS, HEAD_DIM, nb), (0, 4, 1, 2, 3)),
        jnp.swapaxes(shift_t, 1, 2),
        jnp.transpose(wkvo_t.reshape(DEPTH, B_HEADS, HEAD_DIM, HEAD_DIM, nb), (0, 4, 1, 2, 3)),
        jnp.swapaxes(convo_t, 1, 2),
        hlo,
    )

    yp = x_prompt
    outs_p = []
    for l in range(DEPTH):
        yp, kv_p, sh_p, wkv_p, conv_p, h_p = _prompt_layer(l, yp, mod, wts, cos_p, sin_p, nb, tb)
        kv_p = jnp.transpose(kv_p.reshape(bp, 2, A_KV_HEADS, HEAD_DIM, WINDOW), (1, 0, 4, 2, 3))
        outs_p.append((
            kv_p[0], kv_p[1],
            sh_p[:, SUBLANES - 1, :B_SHIFT],
            jnp.stack([wkv_p[:, HEAD_DIM * hd:HEAD_DIM * (hd + 1), HEAD_DIM * hd:HEAD_DIM * (hd + 1)]
                       for hd in range(B_HEADS)], axis=1),
            conv_p[:, SUBLANES - (CONV_W - 1):, :],
            h_p[:, SUBLANES - 1, :],
        ))
    sp = [jnp.stack(z) for z in zip(*outs_p)]
    return (yp, ys.reshape(nb, 1, D_MODEL), *sp, *outs_s)


def kernel(x_prompt, x_sample, c_prompt, c_sample, cache_swa_k, cache_swa_v, state_rwkv_shift, state_rwkv_wkv, state_lru_conv, state_lru_h, norm_pre, norm_post, w_mod, b_mod, w_in, w_out, attn_sinks, rwkv_mu, rwkv_w0, rwkv_w_up, rwkv_a0, rwkv_a_up, rwkv_k_k, rwkv_k_a, rwkv_r_k, rwkv_lnx_w, rwkv_lnx_b, lru_conv_w, lru_conv_b, lru_gate_a_w, lru_gate_a_b, lru_gate_x_w, lru_gate_x_b, lru_lambda):
    p = dict(norm_pre=norm_pre, norm_post=norm_post, w_mod=w_mod, b_mod=b_mod, w_in=w_in, w_out=w_out,
             attn_sinks=attn_sinks, rwkv_mu=rwkv_mu, rwkv_w0=rwkv_w0, rwkv_w_up=rwkv_w_up, rwkv_a0=rwkv_a0,
             rwkv_a_up=rwkv_a_up, rwkv_k_k=rwkv_k_k, rwkv_k_a=rwkv_k_a, rwkv_r_k=rwkv_r_k,
             rwkv_lnx_w=rwkv_lnx_w, rwkv_lnx_b=rwkv_lnx_b, lru_conv_w=lru_conv_w, lru_conv_b=lru_conv_b,
             lru_gate_a_w=lru_gate_a_w, lru_gate_a_b=lru_gate_a_b, lru_gate_x_w=lru_gate_x_w,
             lru_gate_x_b=lru_gate_x_b, lru_lambda=lru_lambda)
    return _forward(x_prompt, x_sample, c_prompt, c_sample, cache_swa_k, cache_swa_v, state_rwkv_shift,
                    state_rwkv_wkv, state_lru_conv, state_lru_h, p, TIME_BLOCK)
```

```python
import functools

import numpy as np
import jax
import jax.numpy as jnp
from jax import lax
from jax.experimental import pallas as pl
from jax.experimental.pallas import tpu as pltpu

F32 = jnp.float32
BF16 = jnp.bfloat16

D_MODEL = 1024
DEPTH = 2
A_HEADS = 8
A_KV_HEADS = 2
HEAD_DIM = 64
A_WIDTH = A_HEADS * HEAD_DIM
KV_WIDTH = A_KV_HEADS * HEAD_DIM
WINDOW = 128
Q_BLOCK = 128
ROPE_THETA = 10000.0
B_WIDTH = 256
B_HEADS = 4
LORA = 32
B_SHIFT = 3 * B_WIDTH + 2 * LORA
B_SHIFT_PAD = 3 * B_WIDTH + 128
LNX_EPS = 1e-5 * 8 ** 2
C_WIDTH = 256
CONV_W = 4
LRU_C = 8.0
EPS = 1e-6
PAST_LEN = 8192
EXP_NEG_HALF = float(np.exp(-0.5))
LOG2_E = float(np.log2(np.e))
Q_SCALE = HEAD_DIM ** -0.5 * LOG2_E

LANES = 128
SUBLANES = 8
VMEM_LIMIT_BYTES = 56 * 1024 * 1024

OFF_Q = 0
OFF_K = OFF_Q + A_WIDTH
OFF_V = OFF_K + KV_WIDTH
OFF_GA = OFF_V + KV_WIDTH
OFF_PB = OFF_GA + A_WIDTH
OFF_GB = OFF_PB + B_SHIFT_PAD
OFF_XC = OFF_GB + B_WIDTH
OFF_GC = OFF_XC + C_WIDTH
P_PAD = OFF_GC + C_WIDTH
P_IN = P_PAD - (B_SHIFT_PAD - B_SHIFT)

CHUNK = 64
TIME_BLOCK = 512
SEQS_PER_STEP = 1
PROJ_TILE = 256

(V_W0, V_A0, V_KK, V_KA, V_LNW, V_LNB, V_RK, V_CB, V_GAB, V_GXB, V_LAM,
 V_CW0, V_CW1, V_CW2, V_CW3) = range(15)
VEC_ROWS = 16

(T_DECAY, T_NKK, T_KKA, T_KMOD, T_R, T_V) = range(6)

_NT = (((1,), (1,)), ((), ()))
_TN = (((0,), (0,)), ((), ()))


def _bf(x):
    return x.astype(BF16)


def _dot(a, b):
    return jnp.dot(a, b, preferred_element_type=F32)


def _dot_b(a, b):
    return jnp.dot(a, b, preferred_element_type=F32).astype(BF16)


def _dot_nt(a, b):
    return lax.dot_general(a, b, _NT, preferred_element_type=F32)


def _dot_tn(a, b):
    return lax.dot_general(a, b, _TN, preferred_element_type=F32)


def _iota(shape, dim):
    return lax.broadcasted_iota(jnp.int32, shape, dim)


def _sigmoid(x):
    return jax.nn.sigmoid(x)


def _silu(x):
    return x * _sigmoid(x)


def _softplus(x):
    return jnp.maximum(x, 0.0) + jnp.log1p(jnp.exp(-jnp.abs(x)))


def _head_ones(n):
    r = _iota((n, n), 0) // HEAD_DIM
    c = _iota((n, n), 1) // HEAD_DIM
    return jnp.where(r == c, 1.0, 0.0).astype(BF16)


def _head_sum(x, ones_bd):
    return _dot(_bf(x), ones_bd)


def _rms_scale(x):
    ms = jnp.mean(x * x, axis=-1, keepdims=True)
    return x * lax.rsqrt(ms + EPS)


def _affine_scan(a, b, h0):
    rows, n = a.shape
    groups = rows // SUBLANES
    a = a.reshape(groups, SUBLANES, n)
    b = b.reshape(groups, SUBLANES, n)
    sub = _iota((groups, SUBLANES, n), 1)
    d = 1
    while d < SUBLANES:
        keep = sub >= d
        a_sh = jnp.where(keep, pltpu.roll(a, d, 1), 1.0)
        b_sh = jnp.where(keep, pltpu.roll(b, d, 1), 0.0)
        b = b + a * b_sh
        a = a * a_sh
        d *= 2
    outs = []
    carry = h0
    for g in range(groups):
        hg = b[g] + a[g] * carry
        outs.append(hg)
        carry = hg[SUBLANES - 1:SUBLANES, :]
    return jnp.concatenate(outs, axis=0)


def _shift_rows(x, j, tail):
    rows, n = x.shape
    groups = rows // SUBLANES
    x3 = jnp.concatenate([tail, x], axis=0).reshape(groups + 1, SUBLANES, n)
    r3 = pltpu.roll(x3, j, 1)
    sub = _iota((groups, SUBLANES, n), 1)
    return jnp.where(sub >= j, r3[1:], r3[:-1]).reshape(rows, n)


def _cumsum_chunks(x, chunk):
    rows, n = x.shape
    groups = rows // SUBLANES
    x = x.reshape(groups, SUBLANES, n)
    sub = _iota((groups, SUBLANES, n), 1)
    d = 1
    while d < SUBLANES:
        x = x + jnp.where(sub >= d, pltpu.roll(x, d, 1), 0.0)
        d *= 2
    outs = []
    for g in range(groups):
        xg = x[g]
        if (g * SUBLANES) % chunk != 0:
            xg = xg + outs[-1][SUBLANES - 1:SUBLANES, :]
        outs.append(xg)
    return jnp.concatenate(outs, axis=0)


def _rope128(z, cos, sin_signed):
    lane = _iota(z.shape, 1)
    first = (lane & 32) == 0
    sw = jnp.where(first, pltpu.roll(z, 96, 1), pltpu.roll(z, 32, 1))
    return z * cos + sw * sin_signed


def _rwkv_pre(xs, vec, wlora, ones_bd):
    r = xs[:, 0:B_WIDTH]
    k = xs[:, B_WIDTH:2 * B_WIDTH]
    v = xs[:, 2 * B_WIDTH:3 * B_WIDTH]
    lor = xs[:, 3 * B_WIDTH:B_SHIFT_PAD]
    lane = _iota(lor.shape, 1)
    z = jnp.where(lane < LORA, jnp.tanh(lor), lor)
    wa = _dot(_bf(z), wlora)
    zw = vec[V_W0:V_W0 + 1, :] + wa[:, :B_WIDTH]
    ld = -EXP_NEG_HALF * _sigmoid(zw)
    a = _sigmoid(vec[V_A0:V_A0 + 1, :] + wa[:, B_WIDTH:])
    kk = k * vec[V_KK:V_KK + 1, :]
    kkn = kk * jnp.minimum(lax.rsqrt(_head_sum(kk * kk, ones_bd)), 1e12)
    kmod = k * (1.0 + (a - 1.0) * vec[V_KA:V_KA + 1, :])
    return r, k, v, ld, a, kkn, kmod


def _rwkv_post(y, r, kmod, v, gb, vec, ones_bd):
    inv = 1.0 / HEAD_DIM
    mu = _head_sum(y, ones_bd) * inv
    yc = y - mu
    var = _head_sum(yc * yc, ones_bd) * inv
    yn = yc * lax.rsqrt(var + LNX_EPS)
    yn = yn * vec[V_LNW:V_LNW + 1, :] + vec[V_LNB:V_LNB + 1, :]
    bonus = _head_sum(r * kmod * vec[V_RK:V_RK + 1, :], ones_bd) * v
    return (yn + bonus) * _silu(gb)


def _lru_gates(u, vec, wgate):
    gates = _dot(_bf(u), wgate)
    gr = _sigmoid(gates[:, :C_WIDTH] + vec[V_GAB:V_GAB + 1, :])
    gi = _sigmoid(gates[:, C_WIDTH:] + vec[V_GXB:V_GXB + 1, :])
    sp = _softplus(-vec[V_LAM:V_LAM + 1, :])
    log_a = -LRU_C * gr * sp
    a = jnp.exp(log_a)
    mult = jnp.sqrt(1.0 - a * a)
    return a, mult, gi


def _project(h, win_ref, proj_s, group):
    src, dst, n = group
    proj_s[:, dst:dst + n] = _dot_nt(h, win_ref[src:src + n, :])


_DELTA = B_SHIFT_PAD - B_SHIFT
GRP_PB = (OFF_PB, OFF_PB, B_SHIFT_PAD)
GRP_QKV = (OFF_Q, OFF_Q, OFF_GA - OFF_Q)
GRP_GA = (OFF_GA, OFF_GA, A_WIDTH)
GRP_GB = (OFF_GB - _DELTA, OFF_GB, B_WIDTH)
GRP_XC = (OFF_XC - _DELTA, OFF_XC, C_WIDTH)
GRP_GC = (OFF_GC - _DELTA, OFF_GC, C_WIDTH)


def _mod_kernel(c_ref, w_ref, b_ref, o_ref):
    o_ref[...] = _dot(_bf(_silu(c_ref[...])), _bf(w_ref[...])) + b_ref[...]


def _mod_call(c_all, w_mod, b_mod):
    rows = c_all.shape[0]
    return pl.pallas_call(
        _mod_kernel,
        grid=(DEPTH, 3),
        in_specs=[
            pl.BlockSpec((rows, D_MODEL), lambda l, j: (0, 0)),
            pl.BlockSpec((None, D_MODEL, D_MODEL), lambda l, j: (l, 0, j)),
            pl.BlockSpec((None, 1, D_MODEL), lambda l, j: (l, 0, j)),
        ],
        out_specs=pl.BlockSpec((None, None, rows, D_MODEL), lambda l, j: (l, j, 0, 0)),
        out_shape=jax.ShapeDtypeStruct((DEPTH, 3, rows, D_MODEL), F32),
        compiler_params=pltpu.CompilerParams(
            dimension_semantics=("arbitrary", "arbitrary"), vmem_limit_bytes=VMEM_LIMIT_BYTES),
        name="adaln_mod",
    )(c_all, w_mod, b_mod.reshape(DEPTH, 1, 3 * D_MODEL))


def _prompt_kernel(sinks_ref, x_ref, mod_ref, npre_ref, npost_ref, win_ref, wout_ref,
                   cos_ref, sin_ref, mu_ref, vec_ref, wlora_ref, wgate_ref,
                   y_ref, kv_ref, shift_ref, wkv_ref, conv_ref, hlru_ref,
                   *scratch, layer):
    n_seq = x_ref.shape[0]
    tb = x_ref.shape[1]
    t = pl.program_id(1)
    per_seq = len(scratch) // n_seq
    seq_scratch = [scratch[q * per_seq:(q + 1) * per_seq] for q in range(n_seq)]

    @pl.when(t == 0)
    def _():
        for kvprev, pprev, xcprev, hprev, gstate, _, _, _ in seq_scratch:
            kvprev[...] = jnp.zeros_like(kvprev)
            pprev[...] = jnp.zeros_like(pprev)
            xcprev[...] = jnp.zeros_like(xcprev)
            hprev[...] = jnp.zeros_like(hprev)
            gstate[...] = jnp.zeros_like(gstate)

    vec = vec_ref[...]
    ones_bd = _head_ones(B_WIDTH)
    cos = cos_ref[...]
    sin = sin_ref[...]
    nck = tb // CHUNK
    sls = [slice(c * CHUNK, (c + 1) * CHUNK) for c in range(nck)]
    rng = range(nck)
    pairs = [(c, i) for c in range(A_HEADS // 2) for i in range(tb // Q_BLOCK)]

    lo_kv = _iota((tb + WINDOW, KV_WIDTH), 1) < HEAD_DIM
    lo_q = _iota((Q_BLOCK, LANES), 1) < HEAD_DIM
    lo_q_full = _iota((tb, LANES), 1) < HEAD_DIM
    qi = _iota((Q_BLOCK, 2 * Q_BLOCK), 0)
    sj = _iota((Q_BLOCK, 2 * Q_BLOCK), 1)
    band = (sj >= qi) & (sj <= qi + WINDOW)
    first_lo = jnp.where(t == 0, Q_BLOCK, 0)
    neg_inf = -jnp.inf
    reset = (_iota((tb, C_WIDTH), 0) + t * tb) == 0

    ri = _iota((CHUNK, B_HEADS * CHUNK), 0)
    cj = _iota((CHUNK, B_HEADS * CHUNK), 1) % CHUNK
    strict_b = jnp.where(ri > cj, 1.0, 0.0).astype(BF16)
    incl_b = jnp.where(ri >= cj, 1.0, 0.0).astype(BF16)
    eye_f = jnp.where(ri == cj, 1.0, 0.0)
    bd_heads = (_iota((B_WIDTH, B_WIDTH), 0) // HEAD_DIM) == (_iota((B_WIDTH, B_WIDTH), 1) // HEAD_DIM)

    def duplicated(z):
        zr = pltpu.roll(z, HEAD_DIM, 1)
        return (_bf(jnp.where(lo_kv, z, zr)), _bf(jnp.where(lo_kv, zr, z)))

    def window(dup, c, i):
        r0 = i * Q_BLOCK
        return dup[c // 2][r0:r0 + 2 * Q_BLOCK]

    def stk(z):
        zb = _bf(z)
        return jnp.concatenate([zb] * B_HEADS, axis=0) * ones_bd

    def collapse(z):
        return z[0:CHUNK] + z[CHUNK:2 * CHUNK] + z[2 * CHUNK:3 * CHUNK] + z[3 * CHUNK:4 * CHUNK]

    def side(q, out):
        kvprev, _, xcprev, hprev, _, proj, mix, _ = seq_scratch[q]

        k_rot = _rope128(proj[:, OFF_K:OFF_K + KV_WIDTH], cos, sin)
        v_att = proj[:, OFF_V:OFF_V + KV_WIDTH]
        kdup = duplicated(jnp.concatenate([kvprev[:, 0:KV_WIDTH], k_rot], axis=0))
        vdup = duplicated(jnp.concatenate([kvprev[:, KV_WIDTH:2 * KV_WIDTH], v_att], axis=0))
        out["k_tail"] = k_rot[tb - WINDOW:]
        out["v_tail"] = v_att[tb - WINDOW:]
        yield
        scores = []
        for c in range(A_HEADS // 2):
            qc = _rope128(proj[:, OFF_Q + LANES * c:OFF_Q + LANES * (c + 1)], cos, sin)
            q_even = _bf(jnp.where(lo_q_full, qc, 0.0))
            q_odd = _bf(jnp.where(lo_q_full, 0.0, qc))
            for i in range(tb // Q_BLOCK):
                rows = slice(i * Q_BLOCK, (i + 1) * Q_BLOCK)
                scores.append(_dot_nt(jnp.concatenate([q_even[rows], q_odd[rows]], axis=0),
                                      window(kdup, c, i)))
        yield

        xc = proj[:, OFF_XC:OFF_XC + C_WIDTH]
        u = vec[V_CB:V_CB + 1, :] + xc * vec[V_CW3:V_CW3 + 1, :]
        for j in range(1, CONV_W):
            u = u + _shift_rows(xc, j, xcprev[...]) * vec[V_CW3 - j:V_CW3 - j + 1, :]
        a_l, mult, gi = _lru_gates(u, vec, wgate_ref[...])
        a_l = jnp.where(reset, 0.0, a_l)
        mult = jnp.where(reset, 1.0, mult)
        b_l = mult * gi * u
        out["xc_tail"] = xc[tb - SUBLANES:, :]
        yield

        probs, norms = [], []
        for (c, i), s in zip(pairs, scores):
            mask = band & (sj >= first_lo) if i == 0 else band
            ps, invs = [], []
            for hh in range(2):
                sink = sinks_ref[layer, 2 * c + hh] * LOG2_E
                sh = jnp.where(mask, s[Q_BLOCK * hh:Q_BLOCK * (hh + 1), :], neg_inf)
                m = jnp.maximum(jnp.max(sh, axis=-1, keepdims=True), sink)
                p = jnp.exp2(sh - m)
                invs.append(1.0 / (jnp.sum(p, axis=-1, keepdims=True) + jnp.exp2(sink - m)))
                ps.append(_bf(p))
            probs.append(jnp.concatenate(ps, axis=0))
            norms.append(invs)
            if i == tb // Q_BLOCK - 1:
                yield
        for (c, i), p, inv in zip(pairs, probs, norms):
            r0 = i * Q_BLOCK
            o2 = _dot(p, window(vdup, c, i))
            o = jnp.where(lo_q, o2[0:Q_BLOCK] * inv[0], o2[Q_BLOCK:] * inv[1])
            ga = proj[r0:r0 + Q_BLOCK, OFF_GA + LANES * c:OFF_GA + LANES * (c + 1)]
            mix[r0:r0 + Q_BLOCK, LANES * c:LANES * (c + 1)] = o * _silu(ga)
        yield
        h_l = _affine_scan(a_l, b_l, hprev[SUBLANES - 1:SUBLANES, :])
        gc = proj[:, OFF_GC:OFF_GC + C_WIDTH]
        mix[:, A_WIDTH + B_WIDTH:] = h_l * _silu(gc)
        out["h_tail"] = h_l[tb - SUBLANES:, :]

    def sequence(q, out):
        kvprev, pprev, xcprev, hprev, gstate, proj, mix, ywkv = seq_scratch[q]

        x = x_ref[q]
        brow = pl.ds(pl.program_id(0) * n_seq + q, 1)
        shift_m = mod_ref[0, brow, :]
        scale_m = mod_ref[1, brow, :]
        gate_m = mod_ref[2, brow, :]
        h = _bf(_rms_scale(x) * (npre_ref[...] * (1.0 + scale_m)) + shift_m)
        for src, dst, n in (GRP_PB, GRP_QKV, GRP_XC, GRP_GA, GRP_GB, GRP_GC):
            for off in range(0, n, PROJ_TILE):
                _project(h, win_ref, proj, (src + off, dst + off, min(PROJ_TILE, n - off)))
                yield

        pb = proj[:, OFF_PB:OFF_PB + B_SHIFT_PAD]
        prev = _shift_rows(pb, 1, pprev[...])
        xs = pb + (prev - pb) * mu_ref[...]
        yield
        r, k, v, ld, a, kkn, kmod = _rwkv_pre(xs, vec, wlora_ref[...], ones_bd)
        kka = kkn * a
        yield

        lcum = _cumsum_chunks(ld, CHUNK)
        e_l = jnp.exp(lcum)
        e_nl = jnp.exp(-lcum)
        rt = r * e_l
        at = -kkn * jnp.exp(lcum - ld)
        ktil = kmod * e_nl
        btil = kka * e_nl
        gam = [e_l[s.stop - 1:s.stop, :] for s in sls]
        khat = [ktil[sls[c]] * gam[c] for c in rng]
        bhat = [btil[sls[c]] * gam[c] for c in rng]
        yield
        a_s = [stk(at[s]) for s in sls]
        b_s = [stk(btil[s]) for s in sls]
        k_s = [stk(ktil[s]) for s in sls]
        yield
        v_s = [stk(v[s]) for s in sls]
        bh_s = [stk(bhat[c]) for c in rng]
        sc = [_dot_nt(_bf(jnp.concatenate([at[sls[c]], rt[sls[c]]], axis=0)),
                      jnp.concatenate([b_s[c], k_s[c]], axis=0)) for c in rng]
        yield "side may start"

        pw = [_bf(sc[c][0:CHUNK, 0:B_WIDTH]) * strict_b for c in rng]
        lak = [_bf(sc[c][0:CHUNK, B_WIDTH:]) * strict_b for c in rng]
        mrb = [_bf(sc[c][CHUNK:, 0:B_WIDTH]) * incl_b for c in rng]
        mrk = [_bf(sc[c][CHUNK:, B_WIDTH:]) * incl_b for c in rng]
        tacc = [pw[c].astype(F32) + eye_f for c in rng]
        pw = [_dot_b(pw[c], stk(pw[c])) for c in rng]
        yield
        for _ in range(4):
            res = [_dot(jnp.concatenate([_bf(tacc[c]), pw[c]], axis=0), stk(pw[c])) for c in rng]
            tacc = [tacc[c] + res[c][0:CHUNK] for c in rng]
            pw = [_bf(res[c][CHUNK:]) for c in rng]
            yield
        tinv = [_bf(tacc[c] + _dot(_bf(tacc[c]), stk(pw[c]))) for c in rng]
        lv = [_dot_b(lak[c], v_s[c]) for c in rng]
        yield
        w_c = [_dot_b(tinv[c], a_s[c]) for c in rng]
        z_c = [_dot(tinv[c], stk(lv[c])) for c in rng]
        yield
        w_s = [stk(w_c[c]) for c in rng]
        rp = [_bf(rt[sls[c]] + _dot(mrb[c], w_s[c])) for c in rng]
        y0 = [_dot(jnp.concatenate([mrb[c], mrk[c]], axis=1), jnp.concatenate([stk(z_c[c]), v_s[c]], axis=0))
              for c in rng]
        pp = [_bf(_dot_tn(w_s[c], bh_s[c])) for c in rng]
        qq = [jnp.where(bd_heads,
                        _dot_tn(_bf(jnp.concatenate([z_c[c], v[sls[c]]], axis=0)),
                                _bf(jnp.concatenate([bhat[c], khat[c]], axis=0))), 0.0) for c in rng]
        yield
        qq = [collapse(qq[c]) for c in rng]
        gmat = gstate[...]
        for c in rng:
            gb16 = _bf(gmat)
            ywkv[sls[c], :] = _dot_nt(rp[c], stk(gb16)) + y0[c]
            gmat = gam[c] * gmat + _dot(gb16, pp[c]) + qq[c]
        gstate[...] = gmat
        yield "side must be done"

        gb = proj[:, OFF_GB:OFF_GB + B_WIDTH]
        mix[:, A_WIDTH:A_WIDTH + B_WIDTH] = _rwkv_post(ywkv[...], r, kmod, v, gb, vec, ones_bd)

        yo = _dot(_bf(mix[...]), wout_ref[...])
        y_ref[q] = x + (gate_m * npost_ref[...]) * _rms_scale(yo)

        k_tail, v_tail = out["k_tail"], out["v_tail"]
        kvprev[:, 0:KV_WIDTH] = k_tail
        kvprev[:, KV_WIDTH:2 * KV_WIDTH] = v_tail
        pprev[...] = pb[tb - SUBLANES:, :]
        xcprev[...] = out["xc_tail"]
        hprev[...] = out["h_tail"]
        shift_ref[q] = pb[tb - SUBLANES:, :]
        wkv_ref[q] = gmat
        conv_ref[q] = out["xc_tail"]
        hlru_ref[q] = out["h_tail"]

        @pl.when(t == pl.num_programs(1) - 1)
        def _():
            kv_ref[q, 0] = k_tail.T
            kv_ref[q, 1] = v_tail.T

    for q in range(n_seq):
        shared = {}
        main, extra = sequence(q, shared), side(q, shared)
        while next(main) != "side may start":
            pass
        extra_done = False
        while True:
            if not extra_done:
                try:
                    next(extra)
                except StopIteration:
                    extra_done = True
            if next(main) == "side must be done":
                break
        for _ in extra:
            pass
        for _ in main:
            pass


def _weight_specs(layer_of, grid_rank):
    def lay(*shape):
        zeros = (0,) * len(shape)
        if grid_rank == 1:
            return pl.BlockSpec((None, *shape), lambda i: (layer_of(i), *zeros))
        return pl.BlockSpec((None, *shape), lambda a, b: (layer_of(a, b), *zeros))
    return [
        lay(1, D_MODEL),
        lay(1, D_MODEL),
        lay(P_IN, D_MODEL),
        lay(D_MODEL, D_MODEL),
    ], [
        lay(1, B_SHIFT_PAD),
        lay(VEC_ROWS, B_WIDTH),
        lay(LANES, 2 * B_WIDTH),
        lay(C_WIDTH, 2 * C_WIDTH),
    ]


def _prompt_layer(layer, x, mod, wts, cos_t, sin_t, mod_row0, tb):
    bsz, seq, _ = x.shape
    nt = seq // tb
    sps = SEQS_PER_STEP
    per_b3 = lambda b, t: (b, 0, 0)
    w_head, w_tail = _weight_specs(lambda b, t: layer, 2)
    return pl.pallas_call(
        functools.partial(_prompt_kernel, layer=layer),
        grid=(bsz // sps, nt),
        in_specs=[
            pl.BlockSpec(memory_space=pltpu.SMEM),
            pl.BlockSpec((sps, tb, D_MODEL), lambda b, t: (b, t, 0)),
            pl.BlockSpec((None, 3, SUBLANES, D_MODEL),
                         lambda b, t: (layer, 0, mod_row0 // SUBLANES, 0)),
            *w_head,
            pl.BlockSpec((tb, LANES), lambda b, t: (t, 0)),
            pl.BlockSpec((tb, LANES), lambda b, t: (t, 0)),
            *w_tail,
        ],
        out_specs=[
            pl.BlockSpec((sps, tb, D_MODEL), lambda b, t: (b, t, 0)),
            pl.BlockSpec((sps, 2, KV_WIDTH, WINDOW), lambda b, t: (b, 0, 0, 0)),
            pl.BlockSpec((sps, SUBLANES, B_SHIFT_PAD), per_b3),
            pl.BlockSpec((sps, HEAD_DIM, B_WIDTH), per_b3),
            pl.BlockSpec((sps, SUBLANES, C_WIDTH), per_b3),
            pl.BlockSpec((sps, SUBLANES, C_WIDTH), per_b3),
        ],
        out_shape=[
            jax.ShapeDtypeStruct((bsz, seq, D_MODEL), F32),
            jax.ShapeDtypeStruct((bsz, 2, KV_WIDTH, WINDOW), F32),
            jax.ShapeDtypeStruct((bsz, SUBLANES, B_SHIFT_PAD), F32),
            jax.ShapeDtypeStruct((bsz, HEAD_DIM, B_WIDTH), F32),
            jax.ShapeDtypeStruct((bsz, SUBLANES, C_WIDTH), F32),
            jax.ShapeDtypeStruct((bsz, SUBLANES, C_WIDTH), F32),
        ],
        scratch_shapes=[
            pltpu.VMEM((WINDOW, 2 * KV_WIDTH), F32),
            pltpu.VMEM((SUBLANES, B_SHIFT_PAD), F32),
            pltpu.VMEM((SUBLANES, C_WIDTH), F32),
            pltpu.VMEM((SUBLANES, C_WIDTH), F32),
            pltpu.VMEM((HEAD_DIM, B_WIDTH), F32),
            pltpu.VMEM((tb, P_PAD), F32),
            pltpu.VMEM((tb, D_MODEL), F32),
            pltpu.VMEM((tb, B_WIDTH), F32),
        ] * sps,
        compiler_params=pltpu.CompilerParams(
            dimension_semantics=("arbitrary", "arbitrary"), vmem_limit_bytes=VMEM_LIMIT_BYTES),
        name="prompt_layer",
    )(wts["sinks"], x, mod, wts["norm_pre"], wts["norm_post"], wts["w_in"], wts["w_out"], cos_t, sin_t,
      wts["mu"], wts["vec"], wts["wlora"], wts["wgate"])


def _sample_kernel(sinks_ref, x_ref, mod_ref, npre_ref, npost_ref, win_ref, wout_ref,
                   cos_ref, sin_ref, mu_ref, vec_ref, wlora_ref, wgate_ref,
                   kt_ref, vt_ref, sprev_ref, wkv_ref, conv_ref, hl_ref,
                   y_ref, knew_ref, vnew_ref, shift_ref, wkvo_ref, convo_ref, hlo_ref,
                   ys_s, proj_s, mix_s, q_s, o_s, knew_s, rw_s, rkv_s, yt_s):
    layer = pl.program_id(0)
    j = pl.program_id(1)
    nb = x_ref.shape[0]
    bb = nb // B_HEADS
    n_slot = kt_ref.shape[1]
    vec = vec_ref[...]
    cos = cos_ref[...]
    sin = sin_ref[...]

    @pl.when(jnp.logical_and(layer == 0, j == 0))
    def _():
        ys_s[...] = x_ref[...]

    @pl.when(j == 0)
    def _():
        ones_bd = _head_ones(B_WIDTH)
        x = ys_s[...]
        h = _bf(_rms_scale(x) * npre_ref[...] * (1.0 + mod_ref[1]) + mod_ref[0])
        for grp in (GRP_QKV, GRP_GA, GRP_PB, GRP_GB, GRP_XC, GRP_GC):
            _project(h, win_ref, proj_s, grp)
        k_new = _rope128(proj_s[:, OFF_K:OFF_K + KV_WIDTH], cos, sin)
        knew_s[...] = k_new
        knew_ref[...] = k_new.T
        vnew_ref[...] = proj_s[:, OFF_V:OFF_V + KV_WIDTH].T
        pb = proj_s[:, OFF_PB:OFF_PB + B_SHIFT_PAD]
        pb_t = pb.T
        shift_ref[...] = pb_t[0:B_SHIFT, :]
        prev = jnp.concatenate([sprev_ref[...], jnp.zeros((B_SHIFT_PAD - B_SHIFT, nb), F32)], axis=0).T
        xs = pb + (prev - pb) * mu_ref[...]
        r, k, v, ld, a, kkn, kmod = _rwkv_pre(xs, vec, wlora_ref[...], ones_bd)
        rw_s[T_DECAY] = jnp.exp(ld).T
        rw_s[T_NKK] = (-kkn).T
        rw_s[T_KKA] = (kkn * a).T
        rw_s[T_KMOD] = kmod.T
        rw_s[T_R] = r.T
        rw_s[T_V] = v.T
        rkv_s[0] = r
        rkv_s[1] = kmod
        rkv_s[2] = v

    rs = pl.ds(pl.multiple_of(j * bb, SUBLANES), bb)
    lane = _iota((bb, LANES), 1)
    lo = lane < HEAD_DIM
    for c in range(A_HEADS // 2):
        g = c // 2
        in_g = lo if g == 0 else jnp.logical_not(lo)
        qcol = _rope128(proj_s[rs, OFF_Q + LANES * c:OFF_Q + LANES * (c + 1)], cos, sin)
        qrol = pltpu.roll(qcol, HEAD_DIM, 1)
        for hh in range(2):
            hd = 2 * c + hh
            q_s[hd * bb:(hd + 1) * bb, :] = jnp.where(in_g, qcol if hh == g else qrol, 0.0)
    knew_blk = knew_s[rs, :]
    vnew_blk = proj_s[rs, OFF_V:OFF_V + KV_WIDTH]
    hrow = _iota((A_HEADS, 1), 0)
    sink_col = jnp.zeros((A_HEADS, 1), F32)
    for hd in range(A_HEADS):
        sink_col = jnp.where(hrow == hd, sinks_ref[layer, hd] * LOG2_E, sink_col)

    qbs = [q_s[pl.ds(b, A_HEADS, stride=bb), :] for b in range(bb)]
    s_c = [_dot(_bf(qbs[b]), _bf(kt_ref[b * KV_WIDTH:(b + 1) * KV_WIDTH, :])) for b in range(bb)]
    s_n = [jnp.sum(qbs[b] * knew_blk[b:b + 1, :], axis=-1, keepdims=True) for b in range(bb)]
    p_c, p_n = [], []
    for b in range(bb):
        m = jnp.maximum(jnp.maximum(jnp.max(s_c[b], axis=-1, keepdims=True), s_n[b]), sink_col)
        e_c = jnp.exp2(s_c[b] - m)
        e_n = jnp.exp2(s_n[b] - m)
        inv = 1.0 / (jnp.sum(e_c, axis=-1, keepdims=True) + e_n + jnp.exp2(sink_col - m))
        p_c.append(_bf(e_c * inv))
        p_n.append(e_n * inv)
    for b in range(bb):
        o = _dot_nt(p_c[b], _bf(vt_ref[b * KV_WIDTH:(b + 1) * KV_WIDTH, :]))
        o_s[pl.ds(b, A_HEADS, stride=bb), :] = o + p_n[b] * vnew_blk[b:b + 1, :]
    for c in range(A_HEADS // 2):
        g = c // 2
        halves = []
        for hh in range(2):
            oh = o_s[(2 * c + hh) * bb:(2 * c + hh + 1) * bb, :]
            halves.append(oh if hh == g else pltpu.roll(oh, HEAD_DIM, 1))
        ga = proj_s[rs, OFF_GA + LANES * c:OFF_GA + LANES * (c + 1)]
        mix_s[rs, LANES * c:LANES * (c + 1)] = jnp.where(lo, halves[0], halves[1]) * _silu(ga)

    hrows = pl.ds(pl.multiple_of(j * HEAD_DIM, HEAD_DIM), HEAD_DIM)
    w_t = rw_s[T_DECAY, hrows, :]
    nkk_t = rw_s[T_NKK, hrows, :]
    kka_t = rw_s[T_KKA, hrows, :]
    k_t = rw_s[T_KMOD, hrows, :]
    r_t = rw_s[T_R, hrows, :]

    def value_row(vi, carry):
        krows = pl.ds(pl.multiple_of(vi * HEAD_DIM, HEAD_DIM), HEAD_DIM)
        st = wkv_ref[krows, :]
        sa = jnp.sum(st * nkk_t, axis=0, keepdims=True)
        st_new = st * w_t + sa * kka_t + rw_s[T_V, pl.ds(j * HEAD_DIM + vi, 1), :] * k_t
        wkvo_ref[krows, :] = st_new
        yt_s[pl.ds(j * HEAD_DIM + vi, 1), :] = jnp.sum(st_new * r_t, axis=0, keepdims=True)
        return carry

    lax.fori_loop(0, HEAD_DIM, value_row, 0, unroll=4)

    @pl.when(j == pl.num_programs(1) - 1)
    def _():
        ones_bd = _head_ones(B_WIDTH)
        x = ys_s[...]
        gb = proj_s[:, OFF_GB:OFF_GB + B_WIDTH]
        mix_s[:, A_WIDTH:A_WIDTH + B_WIDTH] = _rwkv_post(yt_s[...].T, rkv_s[0], rkv_s[1], rkv_s[2], gb, vec, ones_bd)
        xc = proj_s[:, OFF_XC:OFF_XC + C_WIDTH]
        gc = proj_s[:, OFF_GC:OFF_GC + C_WIDTH]
        u = vec[V_CB:V_CB + 1, :] + xc * vec[V_CW3:V_CW3 + 1, :]
        for i in range(CONV_W - 1):
            u = u + conv_ref[i] * vec[V_CW0 + i:V_CW0 + i + 1, :]
        a_l, mult, gi = _lru_gates(u, vec, wgate_ref[...])
        h_l = a_l * hl_ref[...] + mult * gi * u
        hlo_ref[...] = h_l
        convo_ref[0] = conv_ref[1]
        convo_ref[1] = conv_ref[2]
        convo_ref[2] = xc
        mix_s[:, A_WIDTH + B_WIDTH:] = h_l * _silu(gc)
        yo = _dot(_bf(mix_s[...]), wout_ref[...])
        y = x + mod_ref[2] * (_rms_scale(yo) * npost_ref[...])
        ys_s[...] = y
        y_ref[...] = y


def _sample_layers(x, mod, wts, cos_r, sin_r, kt, vt, sprev_t, wkv_t, conv_t, hl):
    nb = x.shape[0]
    bb = nb // B_HEADS
    n_slot = kt.shape[2]
    n_hv = HEAD_DIM * HEAD_DIM
    w_head, w_tail = _weight_specs(lambda l, j: l, 2)
    const2 = lambda l, j: (0, 0)
    lay3 = lambda l, j: (l, 0, 0)
    return pl.pallas_call(
        _sample_kernel,
        grid=(DEPTH, B_HEADS),
        in_specs=[
            pl.BlockSpec(memory_space=pltpu.SMEM),
            pl.BlockSpec((nb, D_MODEL), const2),
            pl.BlockSpec((None, 3, nb, D_MODEL), lambda l, j: (l, 0, 0, 0)),
            *w_head,
            pl.BlockSpec((1, LANES), const2),
            pl.BlockSpec((1, LANES), const2),
            *w_tail,
            pl.BlockSpec((None, bb * KV_WIDTH, n_slot), lambda l, j: (l, j, 0)),
            pl.BlockSpec((None, bb * KV_WIDTH, n_slot), lambda l, j: (l, j, 0)),
            pl.BlockSpec((None, B_SHIFT, nb), lay3),
            pl.BlockSpec((None, n_hv, nb), lambda l, j: (l, j, 0)),
            pl.BlockSpec((None, CONV_W - 1, nb, C_WIDTH), lambda l, j: (l, 0, 0, 0)),
            pl.BlockSpec((None, nb, C_WIDTH), lay3),
        ],
        out_specs=[
            pl.BlockSpec((nb, D_MODEL), const2),
            pl.BlockSpec((None, KV_WIDTH, nb), lay3),
            pl.BlockSpec((None, KV_WIDTH, nb), lay3),
            pl.BlockSpec((None, B_SHIFT, nb), lay3),
            pl.BlockSpec((None, n_hv, nb), lambda l, j: (l, j, 0)),
            pl.BlockSpec((None, CONV_W - 1, nb, C_WIDTH), lambda l, j: (l, 0, 0, 0)),
            pl.BlockSpec((None, nb, C_WIDTH), lay3),
        ],
        out_shape=[
            jax.ShapeDtypeStruct((nb, D_MODEL), F32),
            jax.ShapeDtypeStruct((DEPTH, KV_WIDTH, nb), F32),
            jax.ShapeDtypeStruct((DEPTH, KV_WIDTH, nb), F32),
            jax.ShapeDtypeStruct((DEPTH, B_SHIFT, nb), F32),
            jax.ShapeDtypeStruct((DEPTH, B_HEADS * n_hv, nb), F32),
            jax.ShapeDtypeStruct((DEPTH, CONV_W - 1, nb, C_WIDTH), F32),
            jax.ShapeDtypeStruct((DEPTH, nb, C_WIDTH), F32),
        ],
        scratch_shapes=[
            pltpu.VMEM((nb, D_MODEL), F32),
            pltpu.VMEM((nb, P_PAD), F32),
            pltpu.VMEM((nb, D_MODEL), F32),
            pltpu.VMEM((A_HEADS * bb, LANES), F32),
            pltpu.VMEM((A_HEADS * bb, LANES), F32),
            pltpu.VMEM((nb, KV_WIDTH), F32),
            pltpu.VMEM((6, B_WIDTH, nb), F32),
            pltpu.VMEM((3, nb, B_WIDTH), F32),
            pltpu.VMEM((B_WIDTH, nb), F32),
        ],
        compiler_params=pltpu.CompilerParams(
            dimension_semantics=("arbitrary", "arbitrary"), vmem_limit_bytes=VMEM_LIMIT_BYTES),
        name="sample_layers",
    )(wts["sinks"], x, mod, wts["norm_pre"], wts["norm_post"], wts["w_in"], wts["w_out"], cos_r, sin_r,
      wts["mu"], wts["vec"], wts["wlora"], wts["wgate"], kt, vt, sprev_t, wkv_t, conv_t, hl)


def _rope_lanes():
    half = HEAD_DIM // 2
    inv_freq = ROPE_THETA ** (-jnp.arange(half, dtype=F32) / half)
    freq = jnp.tile(inv_freq, LANES // half)
    sign = jnp.tile(jnp.concatenate([-jnp.ones((half,), F32), jnp.ones((half,), F32)]), LANES // HEAD_DIM)
    return freq, sign


def _rope_row(pos):
    freq, sign = _rope_lanes()
    ang = jnp.float32(pos) * freq[None, :]
    return jnp.cos(ang), jnp.sin(ang) * sign[None, :]


def _rope_tables(seq):
    freq, sign = _rope_lanes()
    hi = (jnp.arange(seq // Q_BLOCK, dtype=F32) * Q_BLOCK)[:, None] * freq[None, :]
    lo = jnp.arange(Q_BLOCK, dtype=F32)[:, None] * freq[None, :]
    ch, sh = jnp.cos(hi)[:, None, :], jnp.sin(hi)[:, None, :]
    cl, sl = jnp.cos(lo)[None, :, :], jnp.sin(lo)[None, :, :]
    cos_t = (ch * cl - sh * sl).reshape(seq, LANES)
    sin_t = ((sh * cl + ch * sl) * sign).reshape(seq, LANES)
    return cos_t, sin_t


def _block_diag(w):
    dl, n, d, e = w.shape
    eye = jnp.eye(n, dtype=w.dtype)
    return (eye[None, :, None, :, None] * w[:, :, :, None, :]).reshape(dl, n * d, n * e)


def _prep_weights(p):
    dl = p["w_in"].shape[0]
    w_in_t = jnp.swapaxes(p["w_in"], 1, 2)
    q_scale = jnp.where(jnp.arange(w_in_t.shape[1]) < A_WIDTH, Q_SCALE, 1.0).astype(F32)
    w_in_t = (w_in_t * q_scale[None, :, None]).astype(BF16)
    z32 = jnp.zeros((dl, LORA, B_WIDTH), F32)
    wlora = jnp.concatenate([
        jnp.concatenate([p["rwkv_w_up"], z32], axis=2),
        jnp.concatenate([z32, p["rwkv_a_up"]], axis=2),
        jnp.zeros((dl, LANES - 2 * LORA, 2 * B_WIDTH), F32)], axis=1).astype(BF16)
    wgate = jnp.concatenate([_block_diag(p["lru_gate_a_w"]), _block_diag(p["lru_gate_x_w"])], axis=2).astype(BF16)
    cw = p["lru_conv_w"]
    rows = [p["rwkv_w0"], p["rwkv_a0"], p["rwkv_k_k"], p["rwkv_k_a"], p["rwkv_lnx_w"], p["rwkv_lnx_b"],
            p["rwkv_r_k"].reshape(dl, B_WIDTH), p["lru_conv_b"], p["lru_gate_a_b"], p["lru_gate_x_b"],
            p["lru_lambda"], cw[:, 0], cw[:, 1], cw[:, 2], cw[:, 3], jnp.zeros((dl, B_WIDTH), F32)]
    return {
        "sinks": p["attn_sinks"],
        "norm_pre": p["norm_pre"].reshape(dl, 1, D_MODEL),
        "norm_post": p["norm_post"].reshape(dl, 1, D_MODEL),
        "w_in": w_in_t,
        "w_out": p["w_out"].astype(BF16),
        "mu": jnp.concatenate([p["rwkv_mu"], jnp.zeros((dl, B_SHIFT_PAD - B_SHIFT), F32)],
                              axis=1).reshape(dl, 1, B_SHIFT_PAD),
        "vec": jnp.stack(rows, axis=1),
        "wlora": wlora,
        "wgate": wgate,
    }


def _forward(x_prompt, x_sample, c_prompt, c_sample, cache_swa_k, cache_swa_v, state_rwkv_shift,
             state_rwkv_wkv, state_lru_conv, state_lru_h, p, tb):
    bp, seq, _ = x_prompt.shape
    nb = x_sample.shape[0]
    n_slot = cache_swa_k.shape[2]
    assert nb % SUBLANES == 0 and nb % B_HEADS == 0 and bp <= SUBLANES and seq % tb == 0

    pad_rows = (-(nb + bp)) % SUBLANES
    c_all = jnp.concatenate([c_sample, c_prompt, jnp.zeros((pad_rows, D_MODEL), F32)], axis=0)
    mod = _mod_call(c_all, p["w_mod"], p["b_mod"])
    wts = _prep_weights(p)

    cos_p, sin_p = _rope_tables(seq)
    cos_s, sin_s = _rope_row(PAST_LEN)

    kt = jnp.transpose(cache_swa_k, (0, 1, 3, 4, 2)).reshape(DEPTH, nb * KV_WIDTH, n_slot)
    vt = jnp.transpose(cache_swa_v, (0, 1, 3, 4, 2)).reshape(DEPTH, nb * KV_WIDTH, n_slot)
    sprev_t = jnp.swapaxes(state_rwkv_shift, 1, 2)
    wkv_t = jnp.transpose(state_rwkv_wkv, (0, 2, 3, 4, 1)).reshape(DEPTH, B_HEADS * HEAD_DIM * HEAD_DIM, nb)
    conv_t = jnp.swapaxes(state_lru_conv, 1, 2)
    ys, knew_t, vnew_t, shift_t, wkvo_t, convo_t, hlo = _sample_layers(
        x_sample.reshape(nb, D_MODEL), mod, wts, cos_s, sin_s, kt, vt, sprev_t, wkv_t, conv_t, state_lru_h)
    outs_s = (
        jnp.transpose(knew_t.reshape(DEPTH, 1, A_KV_HEADS, HEAD_DIM, nb), (0, 4, 1, 2, 3)),
        jnp.transpose(vnew_t.reshape(DEPTH, 1, A_KV_HEADS, HEAD_DIM, nb), (0, 4, 1, 2, 3)),
        jnp.swapaxes(shift_t, 1, 2),
        jnp.transpose(wkvo_t.reshape(DEPTH, B_HEADS, HEAD_DIM, HEAD_DIM, nb), (0, 4, 1, 2, 3)),
        jnp.swapaxes(convo_t, 1, 2),
        hlo,
    )

    yp = x_prompt
    outs_p = []
    for l in range(DEPTH):
        yp, kv_p, sh_p, wkv_p, conv_p, h_p = _prompt_layer(l, yp, mod, wts, cos_p, sin_p, nb, tb)
        kv_p = jnp.transpose(kv_p.reshape(bp, 2, A_KV_HEADS, HEAD_DIM, WINDOW), (1, 0, 4, 2, 3))
        outs_p.append((
            kv_p[0], kv_p[1],
            sh_p[:, SUBLANES - 1, :B_SHIFT],
            jnp.transpose(wkv_p.reshape(bp, HEAD_DIM, B_HEADS, HEAD_DIM), (0, 2, 1, 3)),
            conv_p[:, SUBLANES - (CONV_W - 1):, :],
            h_p[:, SUBLANES - 1, :],
        ))
    sp = [jnp.stack(z) for z in zip(*outs_p)]
    return (yp, ys.reshape(nb, 1, D_MODEL), *sp, *outs_s)


def kernel(x_prompt, x_sample, c_prompt, c_sample, cache_swa_k, cache_swa_v, state_rwkv_shift, state_rwkv_wkv, state_lru_conv, state_lru_h, norm_pre, norm_post, w_mod, b_mod, w_in, w_out, attn_sinks, rwkv_mu, rwkv_w0, rwkv_w_up, rwkv_a0, rwkv_a_up, rwkv_k_k, rwkv_k_a, rwkv_r_k, rwkv_lnx_w, rwkv_lnx_b, lru_conv_w, lru_conv_b, lru_gate_a_w, lru_gate_a_b, lru_gate_x_w, lru_gate_x_b, lru_lambda):
    p = dict(norm_pre=norm_pre, norm_post=norm_post, w_mod=w_mod, b_mod=b_mod, w_in=w_in, w_out=w_out,
             attn_sinks=attn_sinks, rwkv_mu=rwkv_mu, rwkv_w0=rwkv_w0, rwkv_w_up=rwkv_w_up, rwkv_a0=rwkv_a0,
             rwkv_a_up=rwkv_a_up, rwkv_k_k=rwkv_k_k, rwkv_k_a=rwkv_k_a, rwkv_r_k=rwkv_r_k,
             rwkv_lnx_w=rwkv_lnx_w, rwkv_lnx_b=rwkv_lnx_b, lru_conv_w=lru_conv_w, lru_conv_b=lru_conv_b,
             lru_gate_a_w=lru_gate_a_w, lru_gate_a_b=lru_gate_a_b, lru_gate_x_w=lru_gate_x_w,
             lru_gate_x_b=lru_gate_x_b, lru_lambda=lru_lambda)
    return _forward(x_prompt, x_sample, c_prompt, c_sample, cache_swa_k, cache_swa_v, state_rwkv_shift,
                    state_rwkv_wkv, state_lru_conv, state_lru_h, p, TIME_BLOCK)
```

```python
import functools

import numpy as np
import jax
import jax.numpy as jnp
from jax import lax
from jax.experimental import pallas as pl
from jax.experimental.pallas import tpu as pltpu

F32 = jnp.float32
BF16 = jnp.bfloat16

D_MODEL = 1024
DEPTH = 2
A_HEADS = 8
A_KV_HEADS = 2
HEAD_DIM = 64
A_WIDTH = A_HEADS * HEAD_DIM
KV_WIDTH = A_KV_HEADS * HEAD_DIM
WINDOW = 128
Q_BLOCK = 128
ROPE_THETA = 10000.0
B_WIDTH = 256
B_HEADS = 4
LORA = 32
B_SHIFT = 3 * B_WIDTH + 2 * LORA
B_SHIFT_PAD = 3 * B_WIDTH + 128
LNX_EPS = 1e-5 * 8 ** 2
C_WIDTH = 256
CONV_W = 4
LRU_C = 8.0
EPS = 1e-6
PAST_LEN = 8192
EXP_NEG_HALF = float(np.exp(-0.5))
LOG2_E = float(np.log2(np.e))
Q_SCALE = HEAD_DIM ** -0.5 * LOG2_E

LANES = 128
SUBLANES = 8
VMEM_LIMIT_BYTES = 56 * 1024 * 1024

OFF_Q = 0
OFF_K = OFF_Q + A_WIDTH
OFF_V = OFF_K + KV_WIDTH
OFF_GA = OFF_V + KV_WIDTH
OFF_PB = OFF_GA + A_WIDTH
OFF_GB = OFF_PB + B_SHIFT_PAD
OFF_XC = OFF_GB + B_WIDTH
OFF_GC = OFF_XC + C_WIDTH
P_PAD = OFF_GC + C_WIDTH
P_IN = P_PAD - (B_SHIFT_PAD - B_SHIFT)

CHUNK = 64
TIME_BLOCK = 512
SEQS_PER_STEP = 1
PROJ_TILE = 256

(V_W0, V_A0, V_KK, V_KA, V_LNW, V_LNB, V_RK, V_CB, V_GAB, V_GXB, V_LAM,
 V_CW0, V_CW1, V_CW2, V_CW3) = range(15)
VEC_ROWS = 16

(T_DECAY, T_NKK, T_KKA, T_KMOD, T_R, T_V) = range(6)

_NT = (((1,), (1,)), ((), ()))
_TN = (((0,), (0,)), ((), ()))


def _bf(x):
    return x.astype(BF16)


def _dot(a, b):
    return jnp.dot(a, b, preferred_element_type=F32)


def _dot_b(a, b):
    return jnp.dot(a, b, preferred_element_type=F32).astype(BF16)


def _dot_nt(a, b):
    return lax.dot_general(a, b, _NT, preferred_element_type=F32)


def _dot_tn(a, b):
    return lax.dot_general(a, b, _TN, preferred_element_type=F32)


def _iota(shape, dim):
    return lax.broadcasted_iota(jnp.int32, shape, dim)


def _sigmoid(x):
    return jax.nn.sigmoid(x)


def _silu(x):
    return x * _sigmoid(x)


def _softplus(x):
    return jnp.maximum(x, 0.0) + jnp.log1p(jnp.exp(-jnp.abs(x)))


def _head_ones(n):
    r = _iota((n, n), 0) // HEAD_DIM
    c = _iota((n, n), 1) // HEAD_DIM
    return jnp.where(r == c, 1.0, 0.0).astype(BF16)


def _head_sum(x, ones_bd):
    return _dot(_bf(x), ones_bd)


def _rms_scale(x):
    ms = jnp.mean(x * x, axis=-1, keepdims=True)
    return x * lax.rsqrt(ms + EPS)


def _affine_scan(a, b, h0):
    rows, n = a.shape
    groups = rows // SUBLANES
    a = a.reshape(groups, SUBLANES, n)
    b = b.reshape(groups, SUBLANES, n)
    sub = _iota((groups, SUBLANES, n), 1)
    d = 1
    while d < SUBLANES:
        keep = sub >= d
        a_sh = jnp.where(keep, pltpu.roll(a, d, 1), 1.0)
        b_sh = jnp.where(keep, pltpu.roll(b, d, 1), 0.0)
        b = b + a * b_sh
        a = a * a_sh
        d *= 2
    outs = []
    carry = h0
    for g in range(groups):
        hg = b[g] + a[g] * carry
        outs.append(hg)
        carry = hg[SUBLANES - 1:SUBLANES, :]
    return jnp.concatenate(outs, axis=0)


def _shift_rows(x, j, tail):
    rows, n = x.shape
    groups = rows // SUBLANES
    x3 = jnp.concatenate([tail, x], axis=0).reshape(groups + 1, SUBLANES, n)
    r3 = pltpu.roll(x3, j, 1)
    sub = _iota((groups, SUBLANES, n), 1)
    return jnp.where(sub >= j, r3[1:], r3[:-1]).reshape(rows, n)


def _cumsum_chunks(x, chunk):
    rows, n = x.shape
    groups = rows // SUBLANES
    x = x.reshape(groups, SUBLANES, n)
    sub = _iota((groups, SUBLANES, n), 1)
    d = 1
    while d < SUBLANES:
        x = x + jnp.where(sub >= d, pltpu.roll(x, d, 1), 0.0)
        d *= 2
    outs = []
    for g in range(groups):
        xg = x[g]
        if (g * SUBLANES) % chunk != 0:
            xg = xg + outs[-1][SUBLANES - 1:SUBLANES, :]
        outs.append(xg)
    return jnp.concatenate(outs, axis=0)


def _rope128(z, cos, sin_signed):
    lane = _iota(z.shape, 1)
    first = (lane & 32) == 0
    sw = jnp.where(first, pltpu.roll(z, 96, 1), pltpu.roll(z, 32, 1))
    return z * cos + sw * sin_signed


def _rwkv_pre(xs, vec, wlora, ones_bd):
    r = xs[:, 0:B_WIDTH]
    k = xs[:, B_WIDTH:2 * B_WIDTH]
    v = xs[:, 2 * B_WIDTH:3 * B_WIDTH]
    lor = xs[:, 3 * B_WIDTH:B_SHIFT_PAD]
    lane = _iota(lor.shape, 1)
    z = jnp.where(lane < LORA, jnp.tanh(lor), lor)
    wa = _dot(_bf(z), wlora)
    zw = vec[V_W0:V_W0 + 1, :] + wa[:, :B_WIDTH]
    ld = -EXP_NEG_HALF * _sigmoid(zw)
    a = _sigmoid(vec[V_A0:V_A0 + 1, :] + wa[:, B_WIDTH:])
    kk = k * vec[V_KK:V_KK + 1, :]
    kkn = kk * jnp.minimum(lax.rsqrt(_head_sum(kk * kk, ones_bd)), 1e12)
    kmod = k * (1.0 + (a - 1.0) * vec[V_KA:V_KA + 1, :])
    return r, k, v, ld, a, kkn, kmod


def _rwkv_post(y, r, kmod, v, gb, vec, ones_bd):
    inv = 1.0 / HEAD_DIM
    mu = _head_sum(y, ones_bd) * inv
    yc = y - mu
    var = _head_sum(yc * yc, ones_bd) * inv
    yn = yc * lax.rsqrt(var + LNX_EPS)
    yn = yn * vec[V_LNW:V_LNW + 1, :] + vec[V_LNB:V_LNB + 1, :]
    bonus = _head_sum(r * kmod * vec[V_RK:V_RK + 1, :], ones_bd) * v
    return (yn + bonus) * _silu(gb)


def _lru_gates(u, vec, wgate):
    gates = _dot(_bf(u), wgate)
    gr = _sigmoid(gates[:, :C_WIDTH] + vec[V_GAB:V_GAB + 1, :])
    gi = _sigmoid(gates[:, C_WIDTH:] + vec[V_GXB:V_GXB + 1, :])
    sp = _softplus(-vec[V_LAM:V_LAM + 1, :])
    log_a = -LRU_C * gr * sp
    a = jnp.exp(log_a)
    mult = jnp.sqrt(1.0 - a * a)
    return a, mult, gi


def _project(h, win_ref, proj_s, group):
    src, dst, n = group
    proj_s[:, dst:dst + n] = _dot_nt(h, win_ref[src:src + n, :])


_DELTA = B_SHIFT_PAD - B_SHIFT
GRP_PB = (OFF_PB, OFF_PB, B_SHIFT_PAD)
GRP_QKV = (OFF_Q, OFF_Q, OFF_GA - OFF_Q)
GRP_GA = (OFF_GA, OFF_GA, A_WIDTH)
GRP_GB = (OFF_GB - _DELTA, OFF_GB, B_WIDTH)
GRP_XC = (OFF_XC - _DELTA, OFF_XC, C_WIDTH)
GRP_GC = (OFF_GC - _DELTA, OFF_GC, C_WIDTH)


def _mod_kernel(c_ref, w_ref, b_ref, o_ref):
    o_ref[...] = _dot(_bf(_silu(c_ref[...])), _bf(w_ref[...])) + b_ref[...]


def _mod_call(c_all, w_mod, b_mod):
    rows = c_all.shape[0]
    return pl.pallas_call(
        _mod_kernel,
        grid=(DEPTH, 3),
        in_specs=[
            pl.BlockSpec((rows, D_MODEL), lambda l, j: (0, 0)),
            pl.BlockSpec((None, D_MODEL, D_MODEL), lambda l, j: (l, 0, j)),
            pl.BlockSpec((None, 1, D_MODEL), lambda l, j: (l, 0, j)),
        ],
        out_specs=pl.BlockSpec((None, None, rows, D_MODEL), lambda l, j: (l, j, 0, 0)),
        out_shape=jax.ShapeDtypeStruct((DEPTH, 3, rows, D_MODEL), F32),
        compiler_params=pltpu.CompilerParams(
            dimension_semantics=("arbitrary", "arbitrary"), vmem_limit_bytes=VMEM_LIMIT_BYTES),
        name="adaln_mod",
    )(c_all, w_mod, b_mod.reshape(DEPTH, 1, 3 * D_MODEL))


def _prompt_kernel(sinks_ref, x_ref, mod_ref, npre_ref, npost_ref, win_ref, wout_ref,
                   cos_ref, sin_ref, mu_ref, vec_ref, wlora_ref, wgate_ref,
                   y_ref, kv_ref, shift_ref, wkv_ref, conv_ref, hlru_ref,
                   *scratch, layer):
    n_seq = x_ref.shape[0]
    tb = x_ref.shape[1]
    t = pl.program_id(1)
    per_seq = len(scratch) // n_seq
    seq_scratch = [scratch[q * per_seq:(q + 1) * per_seq] for q in range(n_seq)]

    @pl.when(t == 0)
    def _():
        for kvprev, pprev, xcprev, hprev, gstate, _, _, _ in seq_scratch:
            kvprev[...] = jnp.zeros_like(kvprev)
            pprev[...] = jnp.zeros_like(pprev)
            xcprev[...] = jnp.zeros_like(xcprev)
            hprev[...] = jnp.zeros_like(hprev)
            gstate[...] = jnp.zeros_like(gstate)

    vec = vec_ref[...]
    ones_bd = _head_ones(B_WIDTH)
    cos = cos_ref[...]
    sin = sin_ref[...]
    nck = tb // CHUNK
    sls = [slice(c * CHUNK, (c + 1) * CHUNK) for c in range(nck)]
    rng = range(nck)
    pairs = [(c, i) for c in range(A_HEADS // 2) for i in range(tb // Q_BLOCK)]

    lo_kv = _iota((tb + WINDOW, KV_WIDTH), 1) < HEAD_DIM
    lo_q = _iota((Q_BLOCK, LANES), 1) < HEAD_DIM
    qi = _iota((Q_BLOCK, 2 * Q_BLOCK), 0)
    sj = _iota((Q_BLOCK, 2 * Q_BLOCK), 1)
    band = (sj >= qi) & (sj <= qi + WINDOW)
    first_lo = jnp.where(t == 0, Q_BLOCK, 0)
    neg_inf = -jnp.inf
    reset = (_iota((tb, C_WIDTH), 0) + t * tb) == 0

    ri = _iota((CHUNK, B_HEADS * CHUNK), 0)
    cj = _iota((CHUNK, B_HEADS * CHUNK), 1) % CHUNK
    strict_b = jnp.where(ri > cj, 1.0, 0.0).astype(BF16)
    incl_b = jnp.where(ri >= cj, 1.0, 0.0).astype(BF16)
    eye_f = jnp.where(ri == cj, 1.0, 0.0)
    bd_heads = (_iota((B_WIDTH, B_WIDTH), 0) // HEAD_DIM) == (_iota((B_WIDTH, B_WIDTH), 1) // HEAD_DIM)

    def variants(z):
        zr = pltpu.roll(z, HEAD_DIM, 1)
        a_ = _bf(jnp.where(lo_kv, z, 0.0))
        b_ = _bf(jnp.where(lo_kv, 0.0, z))
        c_ = _bf(jnp.where(lo_kv, zr, 0.0))
        d_ = _bf(jnp.where(lo_kv, 0.0, zr))
        return ((a_, d_), (c_, b_))

    def win_stack(var, c, i):
        g, r0 = c // 2, i * Q_BLOCK
        return jnp.concatenate([var[g][0][r0:r0 + 2 * Q_BLOCK], var[g][1][r0:r0 + 2 * Q_BLOCK]], axis=0)

    def stk(z):
        zb = _bf(z)
        return jnp.concatenate([zb] * B_HEADS, axis=0) * ones_bd

    def side(q, out):
        kvprev, _, xcprev, hprev, _, proj, mix, _ = seq_scratch[q]

        k_rot = _rope128(proj[:, OFF_K:OFF_K + KV_WIDTH], cos, sin)
        v_att = proj[:, OFF_V:OFF_V + KV_WIDTH]
        kvar = variants(jnp.concatenate([kvprev[:, 0:KV_WIDTH], k_rot], axis=0))
        vvar = variants(jnp.concatenate([kvprev[:, KV_WIDTH:2 * KV_WIDTH], v_att], axis=0))
        out["k_tail"] = k_rot[tb - WINDOW:]
        out["v_tail"] = v_att[tb - WINDOW:]
        yield
        qcols = [_bf(_rope128(proj[:, OFF_Q + LANES * c:OFF_Q + LANES * (c + 1)], cos, sin))
                 for c in range(A_HEADS // 2)]
        scores = [_dot_nt(qcols[c][i * Q_BLOCK:(i + 1) * Q_BLOCK], win_stack(kvar, c, i))
                  for c, i in pairs]
        yield

        xc = proj[:, OFF_XC:OFF_XC + C_WIDTH]
        u = vec[V_CB:V_CB + 1, :] + xc * vec[V_CW3:V_CW3 + 1, :]
        for j in range(1, CONV_W):
            u = u + _shift_rows(xc, j, xcprev[...]) * vec[V_CW3 - j:V_CW3 - j + 1, :]
        a_l, mult, gi = _lru_gates(u, vec, wgate_ref[...])
        a_l = jnp.where(reset, 0.0, a_l)
        mult = jnp.where(reset, 1.0, mult)
        b_l = mult * gi * u
        out["xc_tail"] = xc[tb - SUBLANES:, :]
        yield

        probs, norms = [], []
        for (c, i), s in zip(pairs, scores):
            mask = band & (sj >= first_lo) if i == 0 else band
            ps, invs = [], []
            for hh in range(2):
                sink = sinks_ref[layer, 2 * c + hh] * LOG2_E
                sh = jnp.where(mask, s[:, 2 * Q_BLOCK * hh:2 * Q_BLOCK * (hh + 1)], neg_inf)
                m = jnp.maximum(jnp.max(sh, axis=-1, keepdims=True), sink)
                p = jnp.exp2(sh - m)
                invs.append(1.0 / (jnp.sum(p, axis=-1, keepdims=True) + jnp.exp2(sink - m)))
                ps.append(_bf(p))
            probs.append(jnp.concatenate(ps, axis=1))
            norms.append(jnp.where(lo_q, invs[0], invs[1]))
            if i == tb // Q_BLOCK - 1:
                yield
        for (c, i), p, inv in zip(pairs, probs, norms):
            r0 = i * Q_BLOCK
            o = _dot(p, win_stack(vvar, c, i)) * inv
            ga = proj[r0:r0 + Q_BLOCK, OFF_GA + LANES * c:OFF_GA + LANES * (c + 1)]
            mix[r0:r0 + Q_BLOCK, LANES * c:LANES * (c + 1)] = o * _silu(ga)
        yield
        h_l = _affine_scan(a_l, b_l, hprev[SUBLANES - 1:SUBLANES, :])
        gc = proj[:, OFF_GC:OFF_GC + C_WIDTH]
        mix[:, A_WIDTH + B_WIDTH:] = h_l * _silu(gc)
        out["h_tail"] = h_l[tb - SUBLANES:, :]

    def sequence(q, out):
        kvprev, pprev, xcprev, hprev, gstate, proj, mix, ywkv = seq_scratch[q]

        x = x_ref[q]
        brow = pl.ds(pl.program_id(0) * n_seq + q, 1)
        shift_m = mod_ref[0, brow, :]
        scale_m = mod_ref[1, brow, :]
        gate_m = mod_ref[2, brow, :]
        h = _bf(_rms_scale(x) * (npre_ref[...] * (1.0 + scale_m)) + shift_m)
        for src, dst, n in (GRP_PB, GRP_QKV, GRP_XC, GRP_GA, GRP_GB, GRP_GC):
            for off in range(0, n, PROJ_TILE):
                _project(h, win_ref, proj, (src + off, dst + off, min(PROJ_TILE, n - off)))
                yield

        pb = proj[:, OFF_PB:OFF_PB + B_SHIFT_PAD]
        prev = _shift_rows(pb, 1, pprev[...])
        xs = pb + (prev - pb) * mu_ref[...]
        yield
        r, k, v, ld, a, kkn, kmod = _rwkv_pre(xs, vec, wlora_ref[...], ones_bd)
        kka = kkn * a
        yield

        lcum = _cumsum_chunks(ld, CHUNK)
        e_l = jnp.exp(lcum)
        e_nl = jnp.exp(-lcum)
        rt = r * e_l
        at = -kkn * jnp.exp(lcum - ld)
        ktil = kmod * e_nl
        btil = kka * e_nl
        gam = [e_l[s.stop - 1:s.stop, :] for s in sls]
        khat = [ktil[sls[c]] * gam[c] for c in rng]
        bhat = [btil[sls[c]] * gam[c] for c in rng]
        yield
        a_s = [stk(at[s]) for s in sls]
        b_s = [stk(btil[s]) for s in sls]
        k_s = [stk(ktil[s]) for s in sls]
        yield
        v_s = [stk(v[s]) for s in sls]
        bh_s = [stk(bhat[c]) for c in rng]
        sc = [_dot_nt(_bf(jnp.concatenate([at[sls[c]], rt[sls[c]]], axis=0)),
                      jnp.concatenate([b_s[c], k_s[c]], axis=0)) for c in rng]
        yield "side may start"

        pw = [_bf(sc[c][0:CHUNK, 0:B_WIDTH]) * strict_b for c in rng]
        lak = [_bf(sc[c][0:CHUNK, B_WIDTH:]) * strict_b for c in rng]
        mrb = [_bf(sc[c][CHUNK:, 0:B_WIDTH]) * incl_b for c in rng]
        mrk = [_bf(sc[c][CHUNK:, B_WIDTH:]) * incl_b for c in rng]
        tacc = [pw[c].astype(F32) + eye_f for c in rng]
        pw = [_dot_b(pw[c], stk(pw[c])) for c in rng]
        yield
        for _ in range(4):
            res = [_dot(jnp.concatenate([_bf(tacc[c]), pw[c]], axis=0), stk(pw[c])) for c in rng]
            tacc = [tacc[c] + res[c][0:CHUNK] for c in rng]
            pw = [_bf(res[c][CHUNK:]) for c in rng]
            yield
        tinv = [_bf(tacc[c] + _dot(_bf(tacc[c]), stk(pw[c]))) for c in rng]
        lv = [_dot_b(lak[c], v_s[c]) for c in rng]
        yield
        w_c = [_dot_b(tinv[c], a_s[c]) for c in rng]
        z_c = [_dot(tinv[c], stk(lv[c])) for c in rng]
        yield
        w_s = [stk(w_c[c]) for c in rng]
        rp = [_bf(rt[sls[c]] + _dot(mrb[c], w_s[c])) for c in rng]
        y0 = [_dot(jnp.concatenate([mrb[c], mrk[c]], axis=1), jnp.concatenate([stk(z_c[c]), v_s[c]], axis=0))
              for c in rng]
        pp = [_bf(_dot_tn(w_s[c], bh_s[c])) for c in rng]
        qq = [jnp.where(bd_heads,
                        _dot_tn(_bf(jnp.concatenate([z_c[c], v[sls[c]]], axis=0)),
                                _bf(jnp.concatenate([bhat[c], khat[c]], axis=0))), 0.0) for c in rng]
        yield
        gmat = gstate[...]
        for c in rng:
            gb16 = _bf(gmat)
            ywkv[sls[c], :] = _dot_nt(rp[c], gb16) + y0[c]
            gmat = gam[c] * gmat + _dot(gb16, pp[c]) + qq[c]
        gstate[...] = gmat
        yield "side must be done"

        gb = proj[:, OFF_GB:OFF_GB + B_WIDTH]
        mix[:, A_WIDTH:A_WIDTH + B_WIDTH] = _rwkv_post(ywkv[...], r, kmod, v, gb, vec, ones_bd)

        yo = _dot(_bf(mix[...]), wout_ref[...])
        y_ref[q] = x + (gate_m * npost_ref[...]) * _rms_scale(yo)

        k_tail, v_tail = out["k_tail"], out["v_tail"]
        kvprev[:, 0:KV_WIDTH] = k_tail
        kvprev[:, KV_WIDTH:2 * KV_WIDTH] = v_tail
        pprev[...] = pb[tb - SUBLANES:, :]
        xcprev[...] = out["xc_tail"]
        hprev[...] = out["h_tail"]
        shift_ref[q] = pb[tb - SUBLANES:, :]
        wkv_ref[q] = gmat
        conv_ref[q] = out["xc_tail"]
        hlru_ref[q] = out["h_tail"]

        @pl.when(t == pl.num_programs(1) - 1)
        def _():
            kv_ref[q, 0] = k_tail.T
            kv_ref[q, 1] = v_tail.T

    for q in range(n_seq):
        shared = {}
        main, extra = sequence(q, shared), side(q, shared)
        while next(main) != "side may start":
            pass
        extra_done = False
        while True:
            if not extra_done:
                try:
                    next(extra)
                except StopIteration:
                    extra_done = True
            if next(main) == "side must be done":
                break
        for _ in extra:
            pass
        for _ in main:
            pass


def _weight_specs(layer_of, grid_rank):
    def lay(*shape):
        zeros = (0,) * len(shape)
        if grid_rank == 1:
            return pl.BlockSpec((None, *shape), lambda i: (layer_of(i), *zeros))
        return pl.BlockSpec((None, *shape), lambda a, b: (layer_of(a, b), *zeros))
    return [
        lay(1, D_MODEL),
        lay(1, D_MODEL),
        lay(P_IN, D_MODEL),
        lay(D_MODEL, D_MODEL),
    ], [
        lay(1, B_SHIFT_PAD),
        lay(VEC_ROWS, B_WIDTH),
        lay(LANES, 2 * B_WIDTH),
        lay(C_WIDTH, 2 * C_WIDTH),
    ]


def _prompt_layer(layer, x, mod, wts, cos_t, sin_t, mod_row0, tb):
    bsz, seq, _ = x.shape
    nt = seq // tb
    sps = SEQS_PER_STEP
    per_b3 = lambda b, t: (b, 0, 0)
    w_head, w_tail = _weight_specs(lambda b, t: layer, 2)
    return pl.pallas_call(
        functools.partial(_prompt_kernel, layer=layer),
        grid=(bsz // sps, nt),
        in_specs=[
            pl.BlockSpec(memory_space=pltpu.SMEM),
            pl.BlockSpec((sps, tb, D_MODEL), lambda b, t: (b, t, 0)),
            pl.BlockSpec((None, 3, SUBLANES, D_MODEL),
                         lambda b, t: (layer, 0, mod_row0 // SUBLANES, 0)),
            *w_head,
            pl.BlockSpec((tb, LANES), lambda b, t: (t, 0)),
            pl.BlockSpec((tb, LANES), lambda b, t: (t, 0)),
            *w_tail,
        ],
        out_specs=[
            pl.BlockSpec((sps, tb, D_MODEL), lambda b, t: (b, t, 0)),
            pl.BlockSpec((sps, 2, KV_WIDTH, WINDOW), lambda b, t: (b, 0, 0, 0)),
            pl.BlockSpec((sps, SUBLANES, B_SHIFT_PAD), per_b3),
            pl.BlockSpec((sps, B_WIDTH, B_WIDTH), per_b3),
            pl.BlockSpec((sps, SUBLANES, C_WIDTH), per_b3),
            pl.BlockSpec((sps, SUBLANES, C_WIDTH), per_b3),
        ],
        out_shape=[
            jax.ShapeDtypeStruct((bsz, seq, D_MODEL), F32),
            jax.ShapeDtypeStruct((bsz, 2, KV_WIDTH, WINDOW), F32),
            jax.ShapeDtypeStruct((bsz, SUBLANES, B_SHIFT_PAD), F32),
            jax.ShapeDtypeStruct((bsz, B_WIDTH, B_WIDTH), F32),
            jax.ShapeDtypeStruct((bsz, SUBLANES, C_WIDTH), F32),
            jax.ShapeDtypeStruct((bsz, SUBLANES, C_WIDTH), F32),
        ],
        scratch_shapes=[
            pltpu.VMEM((WINDOW, 2 * KV_WIDTH), F32),
            pltpu.VMEM((SUBLANES, B_SHIFT_PAD), F32),
            pltpu.VMEM((SUBLANES, C_WIDTH), F32),
            pltpu.VMEM((SUBLANES, C_WIDTH), F32),
            pltpu.VMEM((B_WIDTH, B_WIDTH), F32),
            pltpu.VMEM((tb, P_PAD), F32),
            pltpu.VMEM((tb, D_MODEL), F32),
            pltpu.VMEM((tb, B_WIDTH), F32),
        ] * sps,
        compiler_params=pltpu.CompilerParams(
            dimension_semantics=("arbitrary", "arbitrary"), vmem_limit_bytes=VMEM_LIMIT_BYTES),
        name="prompt_layer",
    )(wts["sinks"], x, mod, wts["norm_pre"], wts["norm_post"], wts["w_in"], wts["w_out"], cos_t, sin_t,
      wts["mu"], wts["vec"], wts["wlora"], wts["wgate"])


def _sample_kernel(sinks_ref, x_ref, mod_ref, npre_ref, npost_ref, win_ref, wout_ref,
                   cos_ref, sin_ref, mu_ref, vec_ref, wlora_ref, wgate_ref,
                   kt_ref, vt_ref, sprev_ref, wkv_ref, conv_ref, hl_ref,
                   y_ref, knew_ref, vnew_ref, shift_ref, wkvo_ref, convo_ref, hlo_ref,
                   ys_s, proj_s, mix_s, q_s, o_s, knew_s, rw_s, rkv_s, yt_s):
    layer = pl.program_id(0)
    j = pl.program_id(1)
    nb = x_ref.shape[0]
    bb = nb // B_HEADS
    n_slot = kt_ref.shape[1]
    vec = vec_ref[...]
    cos = cos_ref[...]
    sin = sin_ref[...]

    @pl.when(jnp.logical_and(layer == 0, j == 0))
    def _():
        ys_s[...] = x_ref[...]

    @pl.when(j == 0)
    def _():
        ones_bd = _head_ones(B_WIDTH)
        x = ys_s[...]
        h = _bf(_rms_scale(x) * npre_ref[...] * (1.0 + mod_ref[1]) + mod_ref[0])
        for grp in (GRP_QKV, GRP_GA, GRP_PB, GRP_GB, GRP_XC, GRP_GC):
            _project(h, win_ref, proj_s, grp)
        k_new = _rope128(proj_s[:, OFF_K:OFF_K + KV_WIDTH], cos, sin)
        knew_s[...] = k_new
        knew_ref[...] = k_new.T
        vnew_ref[...] = proj_s[:, OFF_V:OFF_V + KV_WIDTH].T
        pb = proj_s[:, OFF_PB:OFF_PB + B_SHIFT_PAD]
        pb_t = pb.T
        shift_ref[...] = pb_t[0:B_SHIFT, :]
        prev = jnp.concatenate([sprev_ref[...], jnp.zeros((B_SHIFT_PAD - B_SHIFT, nb), F32)], axis=0).T
        xs = pb + (prev - pb) * mu_ref[...]
        r, k, v, ld, a, kkn, kmod = _rwkv_pre(xs, vec, wlora_ref[...], ones_bd)
        rw_s[T_DECAY] = jnp.exp(ld).T
        rw_s[T_NKK] = (-kkn).T
        rw_s[T_KKA] = (kkn * a).T
        rw_s[T_KMOD] = kmod.T
        rw_s[T_R] = r.T
        rw_s[T_V] = v.T
        rkv_s[0] = r
        rkv_s[1] = kmod
        rkv_s[2] = v

    rs = pl.ds(pl.multiple_of(j * bb, SUBLANES), bb)
    lane = _iota((bb, LANES), 1)
    lo = lane < HEAD_DIM
    for c in range(A_HEADS // 2):
        g = c // 2
        in_g = lo if g == 0 else jnp.logical_not(lo)
        qcol = _rope128(proj_s[rs, OFF_Q + LANES * c:OFF_Q + LANES * (c + 1)], cos, sin)
        qrol = pltpu.roll(qcol, HEAD_DIM, 1)
        for hh in range(2):
            hd = 2 * c + hh
            q_s[hd * bb:(hd + 1) * bb, :] = jnp.where(in_g, qcol if hh == g else qrol, 0.0)
    knew_blk = knew_s[rs, :]
    vnew_blk = proj_s[rs, OFF_V:OFF_V + KV_WIDTH]
    hrow = _iota((A_HEADS, 1), 0)
    sink_col = jnp.zeros((A_HEADS, 1), F32)
    for hd in range(A_HEADS):
        sink_col = jnp.where(hrow == hd, sinks_ref[layer, hd] * LOG2_E, sink_col)

    qbs = [q_s[pl.ds(b, A_HEADS, stride=bb), :] for b in range(bb)]
    s_c = [_dot(_bf(qbs[b]), _bf(kt_ref[b * KV_WIDTH:(b + 1) * KV_WIDTH, :])) for b in range(bb)]
    s_n = [jnp.sum(qbs[b] * knew_blk[b:b + 1, :], axis=-1, keepdims=True) for b in range(bb)]
    p_c, p_n = [], []
    for b in range(bb):
        m = jnp.maximum(jnp.maximum(jnp.max(s_c[b], axis=-1, keepdims=True), s_n[b]), sink_col)
        e_c = jnp.exp2(s_c[b] - m)
        e_n = jnp.exp2(s_n[b] - m)
        inv = 1.0 / (jnp.sum(e_c, axis=-1, keepdims=True) + e_n + jnp.exp2(sink_col - m))
        p_c.append(_bf(e_c * inv))
        p_n.append(e_n * inv)
    for b in range(bb):
        o = _dot_nt(p_c[b], _bf(vt_ref[b * KV_WIDTH:(b + 1) * KV_WIDTH, :]))
        o_s[pl.ds(b, A_HEADS, stride=bb), :] = o + p_n[b] * vnew_blk[b:b + 1, :]
    for c in range(A_HEADS // 2):
        g = c // 2
        halves = []
        for hh in range(2):
            oh = o_s[(2 * c + hh) * bb:(2 * c + hh + 1) * bb, :]
            halves.append(oh if hh == g else pltpu.roll(oh, HEAD_DIM, 1))
        ga = proj_s[rs, OFF_GA + LANES * c:OFF_GA + LANES * (c + 1)]
        mix_s[rs, LANES * c:LANES * (c + 1)] = jnp.where(lo, halves[0], halves[1]) * _silu(ga)

    hrows = pl.ds(pl.multiple_of(j * HEAD_DIM, HEAD_DIM), HEAD_DIM)
    w_t = rw_s[T_DECAY, hrows, :]
    nkk_t = rw_s[T_NKK, hrows, :]
    kka_t = rw_s[T_KKA, hrows, :]
    k_t = rw_s[T_KMOD, hrows, :]
    r_t = rw_s[T_R, hrows, :]

    def value_row(vi, carry):
        krows = pl.ds(pl.multiple_of(vi * HEAD_DIM, HEAD_DIM), HEAD_DIM)
        st = wkv_ref[krows, :]
        sa = jnp.sum(st * nkk_t, axis=0, keepdims=True)
        st_new = st * w_t + sa * kka_t + rw_s[T_V, pl.ds(j * HEAD_DIM + vi, 1), :] * k_t
        wkvo_ref[krows, :] = st_new
        yt_s[pl.ds(j * HEAD_DIM + vi, 1), :] = jnp.sum(st_new * r_t, axis=0, keepdims=True)
        return carry

    lax.fori_loop(0, HEAD_DIM, value_row, 0, unroll=4)

    @pl.when(j == pl.num_programs(1) - 1)
    def _():
        ones_bd = _head_ones(B_WIDTH)
        x = ys_s[...]
        gb = proj_s[:, OFF_GB:OFF_GB + B_WIDTH]
        mix_s[:, A_WIDTH:A_WIDTH + B_WIDTH] = _rwkv_post(yt_s[...].T, rkv_s[0], rkv_s[1], rkv_s[2], gb, vec, ones_bd)
        xc = proj_s[:, OFF_XC:OFF_XC + C_WIDTH]
        gc = proj_s[:, OFF_GC:OFF_GC + C_WIDTH]
        u = vec[V_CB:V_CB + 1, :] + xc * vec[V_CW3:V_CW3 + 1, :]
        for i in range(CONV_W - 1):
            u = u + conv_ref[i] * vec[V_CW0 + i:V_CW0 + i + 1, :]
        a_l, mult, gi = _lru_gates(u, vec, wgate_ref[...])
        h_l = a_l * hl_ref[...] + mult * gi * u
        hlo_ref[...] = h_l
        convo_ref[0] = conv_ref[1]
        convo_ref[1] = conv_ref[2]
        convo_ref[2] = xc
        mix_s[:, A_WIDTH + B_WIDTH:] = h_l * _silu(gc)
        yo = _dot(_bf(mix_s[...]), wout_ref[...])
        y = x + mod_ref[2] * (_rms_scale(yo) * npost_ref[...])
        ys_s[...] = y
        y_ref[...] = y


def _sample_layers(x, mod, wts, cos_r, sin_r, kt, vt, sprev_t, wkv_t, conv_t, hl):
    nb = x.shape[0]
    bb = nb // B_HEADS
    n_slot = kt.shape[2]
    n_hv = HEAD_DIM * HEAD_DIM
    w_head, w_tail = _weight_specs(lambda l, j: l, 2)
    const2 = lambda l, j: (0, 0)
    lay3 = lambda l, j: (l, 0, 0)
    return pl.pallas_call(
        _sample_kernel,
        grid=(DEPTH, B_HEADS),
        in_specs=[
            pl.BlockSpec(memory_space=pltpu.SMEM),
            pl.BlockSpec((nb, D_MODEL), const2),
            pl.BlockSpec((None, 3, nb, D_MODEL), lambda l, j: (l, 0, 0, 0)),
            *w_head,
            pl.BlockSpec((1, LANES), const2),
            pl.BlockSpec((1, LANES), const2),
            *w_tail,
            pl.BlockSpec((None, bb * KV_WIDTH, n_slot), lambda l, j: (l, j, 0)),
            pl.BlockSpec((None, bb * KV_WIDTH, n_slot), lambda l, j: (l, j, 0)),
            pl.BlockSpec((None, B_SHIFT, nb), lay3),
            pl.BlockSpec((None, n_hv, nb), lambda l, j: (l, j, 0)),
            pl.BlockSpec((None, CONV_W - 1, nb, C_WIDTH), lambda l, j: (l, 0, 0, 0)),
            pl.BlockSpec((None, nb, C_WIDTH), lay3),
        ],
        out_specs=[
            pl.BlockSpec((nb, D_MODEL), const2),
            pl.BlockSpec((None, KV_WIDTH, nb), lay3),
            pl.BlockSpec((None, KV_WIDTH, nb), lay3),
            pl.BlockSpec((None, B_SHIFT, nb), lay3),
            pl.BlockSpec((None, n_hv, nb), lambda l, j: (l, j, 0)),
            pl.BlockSpec((None, CONV_W - 1, nb, C_WIDTH), lambda l, j: (l, 0, 0, 0)),
            pl.BlockSpec((None, nb, C_WIDTH), lay3),
        ],
        out_shape=[
            jax.ShapeDtypeStruct((nb, D_MODEL), F32),
            jax.ShapeDtypeStruct((DEPTH, KV_WIDTH, nb), F32),
            jax.ShapeDtypeStruct((DEPTH, KV_WIDTH, nb), F32),
            jax.ShapeDtypeStruct((DEPTH, B_SHIFT, nb), F32),
            jax.ShapeDtypeStruct((DEPTH, B_HEADS * n_hv, nb), F32),
            jax.ShapeDtypeStruct((DEPTH, CONV_W - 1, nb, C_WIDTH), F32),
            jax.ShapeDtypeStruct((DEPTH, nb, C_WIDTH), F32),
        ],
        scratch_shapes=[
            pltpu.VMEM((nb, D_MODEL), F32),
            pltpu.VMEM((nb, P_PAD), F32),
            pltpu.VMEM((nb, D_MODEL), F32),
            pltpu.VMEM((A_HEADS * bb, LANES), F32),
            pltpu.VMEM((A_HEADS * bb, LANES), F32),
            pltpu.VMEM((nb, KV_WIDTH), F32),
            pltpu.VMEM((6, B_WIDTH, nb), F32),
            pltpu.VMEM((3, nb, B_WIDTH), F32),
            pltpu.VMEM((B_WIDTH, nb), F32),
        ],
        compiler_params=pltpu.CompilerParams(
            dimension_semantics=("arbitrary", "arbitrary"), vmem_limit_bytes=VMEM_LIMIT_BYTES),
        name="sample_layers",
    )(wts["sinks"], x, mod, wts["norm_pre"], wts["norm_post"], wts["w_in"], wts["w_out"], cos_r, sin_r,
      wts["mu"], wts["vec"], wts["wlora"], wts["wgate"], kt, vt, sprev_t, wkv_t, conv_t, hl)


def _rope_lanes():
    half = HEAD_DIM // 2
    inv_freq = ROPE_THETA ** (-jnp.arange(half, dtype=F32) / half)
    freq = jnp.tile(inv_freq, LANES // half)
    sign = jnp.tile(jnp.concatenate([-jnp.ones((half,), F32), jnp.ones((half,), F32)]), LANES // HEAD_DIM)
    return freq, sign


def _rope_row(pos):
    freq, sign = _rope_lanes()
    ang = jnp.float32(pos) * freq[None, :]
    return jnp.cos(ang), jnp.sin(ang) * sign[None, :]


def _rope_tables(seq):
    freq, sign = _rope_lanes()
    hi = (jnp.arange(seq // Q_BLOCK, dtype=F32) * Q_BLOCK)[:, None] * freq[None, :]
    lo = jnp.arange(Q_BLOCK, dtype=F32)[:, None] * freq[None, :]
    ch, sh = jnp.cos(hi)[:, None, :], jnp.sin(hi)[:, None, :]
    cl, sl = jnp.cos(lo)[None, :, :], jnp.sin(lo)[None, :, :]
    cos_t = (ch * cl - sh * sl).reshape(seq, LANES)
    sin_t = ((sh * cl + ch * sl) * sign).reshape(seq, LANES)
    return cos_t, sin_t


def _block_diag(w):
    dl, n, d, e = w.shape
    eye = jnp.eye(n, dtype=w.dtype)
    return (eye[None, :, None, :, None] * w[:, :, :, None, :]).reshape(dl, n * d, n * e)


def _prep_weights(p):
    dl = p["w_in"].shape[0]
    w_in_t = jnp.swapaxes(p["w_in"], 1, 2)
    q_scale = jnp.where(jnp.arange(w_in_t.shape[1]) < A_WIDTH, Q_SCALE, 1.0).astype(F32)
    w_in_t = (w_in_t * q_scale[None, :, None]).astype(BF16)
    z32 = jnp.zeros((dl, LORA, B_WIDTH), F32)
    wlora = jnp.concatenate([
        jnp.concatenate([p["rwkv_w_up"], z32], axis=2),
        jnp.concatenate([z32, p["rwkv_a_up"]], axis=2),
        jnp.zeros((dl, LANES - 2 * LORA, 2 * B_WIDTH), F32)], axis=1).astype(BF16)
    wgate = jnp.concatenate([_block_diag(p["lru_gate_a_w"]), _block_diag(p["lru_gate_x_w"])], axis=2).astype(BF16)
    cw = p["lru_conv_w"]
    rows = [p["rwkv_w0"], p["rwkv_a0"], p["rwkv_k_k"], p["rwkv_k_a"], p["rwkv_lnx_w"], p["rwkv_lnx_b"],
            p["rwkv_r_k"].reshape(dl, B_WIDTH), p["lru_conv_b"], p["lru_gate_a_b"], p["lru_gate_x_b"],
            p["lru_lambda"], cw[:, 0], cw[:, 1], cw[:, 2], cw[:, 3], jnp.zeros((dl, B_WIDTH), F32)]
    return {
        "sinks": p["attn_sinks"],
        "norm_pre": p["norm_pre"].reshape(dl, 1, D_MODEL),
        "norm_post": p["norm_post"].reshape(dl, 1, D_MODEL),
        "w_in": w_in_t,
        "w_out": p["w_out"].astype(BF16),
        "mu": jnp.concatenate([p["rwkv_mu"], jnp.zeros((dl, B_SHIFT_PAD - B_SHIFT), F32)],
                              axis=1).reshape(dl, 1, B_SHIFT_PAD),
        "vec": jnp.stack(rows, axis=1),
        "wlora": wlora,
        "wgate": wgate,
    }


def _forward(x_prompt, x_sample, c_prompt, c_sample, cache_swa_k, cache_swa_v, state_rwkv_shift,
             state_rwkv_wkv, state_lru_conv, state_lru_h, p, tb):
    bp, seq, _ = x_prompt.shape
    nb = x_sample.shape[0]
    n_slot = cache_swa_k.shape[2]
    assert nb % SUBLANES == 0 and nb % B_HEADS == 0 and bp <= SUBLANES and seq % tb == 0

    pad_rows = (-(nb + bp)) % SUBLANES
    c_all = jnp.concatenate([c_sample, c_prompt, jnp.zeros((pad_rows, D_MODEL), F32)], axis=0)
    mod = _mod_call(c_all, p["w_mod"], p["b_mod"])
    wts = _prep_weights(p)

    cos_p, sin_p = _rope_tables(seq)
    cos_s, sin_s = _rope_row(PAST_LEN)

    kt = jnp.transpose(cache_swa_k, (0, 1, 3, 4, 2)).reshape(DEPTH, nb * KV_WIDTH, n_slot)
    vt = jnp.transpose(cache_swa_v, (0, 1, 3, 4, 2)).reshape(DEPTH, nb * KV_WIDTH, n_slot)
    sprev_t = jnp.swapaxes(state_rwkv_shift, 1, 2)
    wkv_t = jnp.transpose(state_rwkv_wkv, (0, 2, 3, 4, 1)).reshape(DEPTH, B_HEADS * HEAD_DIM * HEAD_DIM, nb)
    conv_t = jnp.swapaxes(state_lru_conv, 1, 2)
    ys, knew_t, vnew_t, shift_t, wkvo_t, convo_t, hlo = _sample_layers(
        x_sample.reshape(nb, D_MODEL), mod, wts, cos_s, sin_s, kt, vt, sprev_t, wkv_t, conv_t, state_lru_h)
    outs_s = (
        jnp.transpose(knew_t.reshape(DEPTH, 1, A_KV_HEADS, HEAD_DIM, nb), (0, 4, 1, 2, 3)),
        jnp.transpose(vnew_t.reshape(DEPTH, 1, A_KV_HEADS, HEAD_DIM, nb), (0, 4, 1, 2, 3)),
        jnp.swapaxes(shift_t, 1, 2),
        jnp.transpose(wkvo_t.reshape(DEPTH, B_HEADS, HEAD_DIM, HEAD_DIM, nb), (0, 4, 1, 2, 3)),
        jnp.swapaxes(convo_t, 1, 2),
        hlo,
    )

    yp = x_prompt
    outs_p = []
    for l in range(DEPTH):
        yp, kv_p, sh_p, wkv_p, conv_p, h_p = _prompt_layer(l, yp, mod, wts, cos_p, sin_p, nb, tb)
        kv_p = jnp.transpose(kv_p.reshape(bp, 2, A_KV_HEADS, HEAD_DIM, WINDOW), (1, 0, 4, 2, 3))
        outs_p.append((
            kv_p[0], kv_p[1],
            sh_p[:, SUBLANES - 1, :B_SHIFT],
            jnp.stack([wkv_p[:, HEAD_DIM * hd:HEAD_DIM * (hd + 1), HEAD_DIM * hd:HEAD_DIM * (hd + 1)]
                       for hd in range(B_HEADS)], axis=1),
            conv_p[:, SUBLANES - (CONV_W - 1):, :],
            h_p[:, SUBLANES - 1, :],
        ))
    sp = [jnp.stack(z) for z in zip(*outs_p)]
    return (yp, ys.reshape(nb, 1, D_MODEL), *sp, *outs_s)


def kernel(x_prompt, x_sample, c_prompt, c_sample, cache_swa_k, cache_swa_v, state_rwkv_shift, state_rwkv_wkv, state_lru_conv, state_lru_h, norm_pre, norm_post, w_mod, b_mod, w_in, w_out, attn_sinks, rwkv_mu, rwkv_w0, rwkv_w_up, rwkv_a0, rwkv_a_up, rwkv_k_k, rwkv_k_a, rwkv_r_k, rwkv_lnx_w, rwkv_lnx_b, lru_conv_w, lru_conv_b, lru_gate_a_w, lru_gate_a_b, lru_gate_x_w, lru_gate_x_b, lru_lambda):
    p = dict(norm_pre=norm_pre, norm_post=norm_post, w_mod=w_mod, b_mod=b_mod, w_in=w_in, w_out=w_out,
             attn_sinks=attn_sinks, rwkv_mu=rwkv_mu, rwkv_w0=rwkv_w0, rwkv_w_up=rwkv_w_up, rwkv_a0=rwkv_a0,
             rwkv_a_up=rwkv_a_up, rwkv_k_k=rwkv_k_k, rwkv_k_a=rwkv_k_a, rwkv_r_k=rwkv_r_k,
             rwkv_lnx_w=rwkv_lnx_w, rwkv_lnx_b=rwkv_lnx_b, lru_conv_w=lru_conv_w, lru_conv_b=lru_conv_b,
             lru_gate_a_w=lru_gate_a_w, lru_gate_a_b=lru_gate_a_b, lru_gate_x_w=lru_gate_x_w,
             lru_gate_x_b=lru_gate_x_b, lru_lambda=lru_lambda)
    return _forward(x_prompt, x_sample, c_prompt, c_sample, cache_swa_k, cache_swa_v, state_rwkv_shift,
                    state_rwkv_wkv, state_lru_conv, state_lru_h, p, TIME_BLOCK)
```

```python
import functools

import numpy as np
import jax
import jax.numpy as jnp
from jax import lax
from jax.experimental import pallas as pl
from jax.experimental.pallas import tpu as pltpu

F32 = jnp.float32
BF16 = jnp.bfloat16

D_MODEL = 1024
DEPTH = 2
A_HEADS = 8
A_KV_HEADS = 2
HEAD_DIM = 64
A_WIDTH = A_HEADS * HEAD_DIM
KV_WIDTH = A_KV_HEADS * HEAD_DIM
WINDOW = 128
Q_BLOCK = 128
ROPE_THETA = 10000.0
B_WIDTH = 256
B_HEADS = 4
LORA = 32
B_SHIFT = 3 * B_WIDTH + 2 * LORA
B_SHIFT_PAD = 3 * B_WIDTH + 128
LNX_EPS = 1e-5 * 8 ** 2
C_WIDTH = 256
CONV_W = 4
LRU_C = 8.0
EPS = 1e-6
PAST_LEN = 8192
EXP_NEG_HALF = float(np.exp(-0.5))
LOG2_E = float(np.log2(np.e))
Q_SCALE = HEAD_DIM ** -0.5 * LOG2_E

LANES = 128
SUBLANES = 8
VMEM_LIMIT_BYTES = 56 * 1024 * 1024

OFF_Q = 0
OFF_K = OFF_Q + A_WIDTH
OFF_V = OFF_K + KV_WIDTH
OFF_GA = OFF_V + KV_WIDTH
OFF_PB = OFF_GA + A_WIDTH
OFF_GB = OFF_PB + B_SHIFT_PAD
OFF_XC = OFF_GB + B_WIDTH
OFF_GC = OFF_XC + C_WIDTH
P_PAD = OFF_GC + C_WIDTH
P_IN = P_PAD - (B_SHIFT_PAD - B_SHIFT)

CHUNK = 64
TIME_BLOCK = 512
SEQS_PER_STEP = 1
PROJ_TILE = 256

(V_W0, V_A0, V_KK, V_KA, V_LNW, V_LNB, V_RK, V_CB, V_GAB, V_GXB, V_LAM,
 V_CW0, V_CW1, V_CW2, V_CW3) = range(15)
VEC_ROWS = 16

(T_DECAY, T_NKK, T_KKA, T_KMOD, T_R, T_V) = range(6)

_NT = (((1,), (1,)), ((), ()))
_TN = (((0,), (0,)), ((), ()))


def _bf(x):
    return x.astype(BF16)


def _dot(a, b):
    return jnp.dot(a, b, preferred_element_type=F32)


def _dot_b(a, b):
    return jnp.dot(a, b, preferred_element_type=F32).astype(BF16)


def _dot_nt(a, b):
    return lax.dot_general(a, b, _NT, preferred_element_type=F32)


def _dot_tn(a, b):
    return lax.dot_general(a, b, _TN, preferred_element_type=F32)


def _iota(shape, dim):
    return lax.broadcasted_iota(jnp.int32, shape, dim)


def _sigmoid(x):
    return jax.nn.sigmoid(x)


def _silu(x):
    return x * _sigmoid(x)


def _softplus(x):
    return jnp.maximum(x, 0.0) + jnp.log1p(jnp.exp(-jnp.abs(x)))


def _head_ones(n):
    r = _iota((n, n), 0) // HEAD_DIM
    c = _iota((n, n), 1) // HEAD_DIM
    return jnp.where(r == c, 1.0, 0.0).astype(BF16)


def _head_sum(x, ones_bd):
    return _dot(_bf(x), ones_bd)


def _rms_scale(x):
    ms = jnp.mean(x * x, axis=-1, keepdims=True)
    return x * lax.rsqrt(ms + EPS)


def _affine_scan(a, b, h0):
    rows, n = a.shape
    groups = rows // SUBLANES
    a = a.reshape(groups, SUBLANES, n)
    b = b.reshape(groups, SUBLANES, n)
    sub = _iota((1, SUBLANES, n), 1)
    d = 1
    while d < SUBLANES:
        keep = sub >= d
        a_sh = jnp.where(keep, pltpu.roll(a, d, 1), 1.0)
        b_sh = jnp.where(keep, pltpu.roll(b, d, 1), 0.0)
        b = b + a * b_sh
        a = a * a_sh
        d *= 2
    outs = []
    carry = h0
    for g in range(groups):
        hg = b[g] + a[g] * carry
        outs.append(hg)
        carry = hg[SUBLANES - 1:SUBLANES, :]
    return jnp.concatenate(outs, axis=0)


def _shift_rows(x, j, tail):
    rows, n = x.shape
    groups = rows // SUBLANES
    x3 = jnp.concatenate([tail, x], axis=0).reshape(groups + 1, SUBLANES, n)
    r3 = pltpu.roll(x3, j, 1)
    sub = _iota((1, SUBLANES, n), 1)
    return jnp.where(sub >= j, r3[1:], r3[:-1]).reshape(rows, n)


def _cumsum_chunks(x, chunk):
    rows, n = x.shape
    groups = rows // SUBLANES
    x = x.reshape(groups, SUBLANES, n)
    sub = _iota((1, SUBLANES, n), 1)
    d = 1
    while d < SUBLANES:
        x = x + jnp.where(sub >= d, pltpu.roll(x, d, 1), 0.0)
        d *= 2
    outs = []
    for g in range(groups):
        xg = x[g]
        if (g * SUBLANES) % chunk != 0:
            xg = xg + outs[-1][SUBLANES - 1:SUBLANES, :]
        outs.append(xg)
    return jnp.concatenate(outs, axis=0)


def _rope128(z, cos, sin_signed):
    lane = _iota(z.shape, 1)
    first = (lane & 32) == 0
    sw = jnp.where(first, pltpu.roll(z, 96, 1), pltpu.roll(z, 32, 1))
    return z * cos + sw * sin_signed


def _rwkv_pre(xs, vec, wlora, ones_bd):
    r = xs[:, 0:B_WIDTH]
    k = xs[:, B_WIDTH:2 * B_WIDTH]
    v = xs[:, 2 * B_WIDTH:3 * B_WIDTH]
    lor = xs[:, 3 * B_WIDTH:B_SHIFT_PAD]
    lane = _iota(lor.shape, 1)
    z = jnp.where(lane < LORA, jnp.tanh(lor), lor)
    wa = _dot(_bf(z), wlora)
    zw = vec[V_W0:V_W0 + 1, :] + wa[:, :B_WIDTH]
    ld = -EXP_NEG_HALF * _sigmoid(zw)
    a = _sigmoid(vec[V_A0:V_A0 + 1, :] + wa[:, B_WIDTH:])
    kk = k * vec[V_KK:V_KK + 1, :]
    kkn = kk * jnp.minimum(lax.rsqrt(_head_sum(kk * kk, ones_bd)), 1e12)
    kmod = k * (1.0 + (a - 1.0) * vec[V_KA:V_KA + 1, :])
    return r, k, v, ld, a, kkn, kmod


def _rwkv_post(y, r, kmod, v, gb, vec, ones_bd):
    inv = 1.0 / HEAD_DIM
    mu = _head_sum(y, ones_bd) * inv
    yc = y - mu
    var = _head_sum(yc * yc, ones_bd) * inv
    yn = yc * lax.rsqrt(var + LNX_EPS)
    yn = yn * vec[V_LNW:V_LNW + 1, :] + vec[V_LNB:V_LNB + 1, :]
    bonus = _head_sum(r * kmod * vec[V_RK:V_RK + 1, :], ones_bd) * v
    return (yn + bonus) * _silu(gb)


def _lru_gates(u, vec, wgate):
    gates = _dot(_bf(u), wgate)
    gr = _sigmoid(gates[:, :C_WIDTH] + vec[V_GAB:V_GAB + 1, :])
    gi = _sigmoid(gates[:, C_WIDTH:] + vec[V_GXB:V_GXB + 1, :])
    sp = _softplus(-vec[V_LAM:V_LAM + 1, :])
    log_a = -LRU_C * gr * sp
    a = jnp.exp(log_a)
    mult = jnp.sqrt(1.0 - a * a)
    return a, mult, gi


def _project(h, win_ref, proj_s, group):
    src, dst, n = group
    proj_s[:, dst:dst + n] = _dot_nt(h, win_ref[src:src + n, :])


_DELTA = B_SHIFT_PAD - B_SHIFT
GRP_PB = (OFF_PB, OFF_PB, B_SHIFT_PAD)
GRP_QKV = (OFF_Q, OFF_Q, OFF_GA - OFF_Q)
GRP_GA = (OFF_GA, OFF_GA, A_WIDTH)
GRP_GB = (OFF_GB - _DELTA, OFF_GB, B_WIDTH)
GRP_XC = (OFF_XC - _DELTA, OFF_XC, C_WIDTH)
GRP_GC = (OFF_GC - _DELTA, OFF_GC, C_WIDTH)


def _mod_kernel(c_ref, w_ref, b_ref, o_ref):
    o_ref[...] = _dot(_bf(_silu(c_ref[...])), _bf(w_ref[...])) + b_ref[...]


def _mod_call(c_all, w_mod, b_mod):
    rows = c_all.shape[0]
    return pl.pallas_call(
        _mod_kernel,
        grid=(DEPTH, 3),
        in_specs=[
            pl.BlockSpec((rows, D_MODEL), lambda l, j: (0, 0)),
            pl.BlockSpec((None, D_MODEL, D_MODEL), lambda l, j: (l, 0, j)),
            pl.BlockSpec((None, 1, D_MODEL), lambda l, j: (l, 0, j)),
        ],
        out_specs=pl.BlockSpec((None, None, rows, D_MODEL), lambda l, j: (l, j, 0, 0)),
        out_shape=jax.ShapeDtypeStruct((DEPTH, 3, rows, D_MODEL), F32),
        compiler_params=pltpu.CompilerParams(
            dimension_semantics=("arbitrary", "arbitrary"), vmem_limit_bytes=VMEM_LIMIT_BYTES),
        name="adaln_mod",
    )(c_all, w_mod, b_mod.reshape(DEPTH, 1, 3 * D_MODEL))


def _prompt_kernel(sinks_ref, x_ref, mod_ref, npre_ref, npost_ref, win_ref, wout_ref,
                   cos_ref, sin_ref, mu_ref, vec_ref, wlora_ref, wgate_ref,
                   y_ref, kv_ref, shift_ref, wkv_ref, conv_ref, hlru_ref,
                   *scratch, layer):
    n_seq = x_ref.shape[0]
    tb = x_ref.shape[1]
    t = pl.program_id(1)
    per_seq = len(scratch) // n_seq
    seq_scratch = [scratch[q * per_seq:(q + 1) * per_seq] for q in range(n_seq)]

    @pl.when(t == 0)
    def _():
        for kvprev, pprev, xcprev, hprev, gstate, _, _, _ in seq_scratch:
            kvprev[...] = jnp.zeros_like(kvprev)
            pprev[...] = jnp.zeros_like(pprev)
            xcprev[...] = jnp.zeros_like(xcprev)
            hprev[...] = jnp.zeros_like(hprev)
            gstate[...] = jnp.zeros_like(gstate)

    vec = vec_ref[...]
    ones_bd = _head_ones(B_WIDTH)
    cos = cos_ref[...]
    sin = sin_ref[...]
    nck = tb // CHUNK
    sls = [slice(c * CHUNK, (c + 1) * CHUNK) for c in range(nck)]
    rng = range(nck)
    pairs = [(c, i) for c in range(A_HEADS // 2) for i in range(tb // Q_BLOCK)]

    lo_kv = _iota((tb + WINDOW, KV_WIDTH), 1) < HEAD_DIM
    lo_q = _iota((Q_BLOCK, LANES), 1) < HEAD_DIM
    qi = _iota((Q_BLOCK, 2 * Q_BLOCK), 0)
    sj = _iota((Q_BLOCK, 2 * Q_BLOCK), 1)
    band = (sj >= qi) & (sj <= qi + WINDOW)
    first_lo = jnp.where(t == 0, Q_BLOCK, 0)
    neg_inf = -jnp.inf
    reset = (_iota((SUBLANES, C_WIDTH), 0) + t * tb) == 0

    ri = _iota((CHUNK, B_HEADS * CHUNK), 0)
    cj = _iota((CHUNK, B_HEADS * CHUNK), 1) % CHUNK
    strict_b = jnp.where(ri > cj, 1.0, 0.0).astype(BF16)
    incl_b = jnp.where(ri >= cj, 1.0, 0.0).astype(BF16)
    eye_f = jnp.where(ri == cj, 1.0, 0.0)
    bd_heads = (_iota((B_WIDTH, B_WIDTH), 0) // HEAD_DIM) == (_iota((B_WIDTH, B_WIDTH), 1) // HEAD_DIM)

    def variants(z):
        zr = pltpu.roll(z, HEAD_DIM, 1)
        a_ = _bf(jnp.where(lo_kv, z, 0.0))
        b_ = _bf(jnp.where(lo_kv, 0.0, z))
        c_ = _bf(jnp.where(lo_kv, zr, 0.0))
        d_ = _bf(jnp.where(lo_kv, 0.0, zr))
        return ((a_, d_), (c_, b_))

    def win_stack(var, c, i):
        g, r0 = c // 2, i * Q_BLOCK
        return jnp.concatenate([var[g][0][r0:r0 + 2 * Q_BLOCK], var[g][1][r0:r0 + 2 * Q_BLOCK]], axis=0)

    def stk(z):
        zb = _bf(z)
        return jnp.concatenate([zb] * B_HEADS, axis=0) * ones_bd

    def side(q, out):
        kvprev, _, xcprev, hprev, _, proj, mix, _ = seq_scratch[q]

        k_rot = _rope128(proj[:, OFF_K:OFF_K + KV_WIDTH], cos, sin)
        v_att = proj[:, OFF_V:OFF_V + KV_WIDTH]
        kvar = variants(jnp.concatenate([kvprev[:, 0:KV_WIDTH], k_rot], axis=0))
        vvar = variants(jnp.concatenate([kvprev[:, KV_WIDTH:2 * KV_WIDTH], v_att], axis=0))
        out["k_tail"] = k_rot[tb - WINDOW:]
        out["v_tail"] = v_att[tb - WINDOW:]
        yield
        qcols = [_bf(_rope128(proj[:, OFF_Q + LANES * c:OFF_Q + LANES * (c + 1)], cos, sin))
                 for c in range(A_HEADS // 2)]
        scores = [_dot_nt(qcols[c][i * Q_BLOCK:(i + 1) * Q_BLOCK], win_stack(kvar, c, i))
                  for c, i in pairs]
        yield

        xc = proj[:, OFF_XC:OFF_XC + C_WIDTH]
        u = vec[V_CB:V_CB + 1, :] + xc * vec[V_CW3:V_CW3 + 1, :]
        for j in range(1, CONV_W):
            u = u + _shift_rows(xc, j, xcprev[...]) * vec[V_CW3 - j:V_CW3 - j + 1, :]
        a_l, mult, gi = _lru_gates(u, vec, wgate_ref[...])
        a_l = jnp.concatenate([jnp.where(reset, 0.0, a_l[0:SUBLANES]), a_l[SUBLANES:]], axis=0)
        mult = jnp.concatenate([jnp.where(reset, 1.0, mult[0:SUBLANES]), mult[SUBLANES:]], axis=0)
        b_l = mult * gi * u
        out["xc_tail"] = xc[tb - SUBLANES:, :]
        yield

        probs, norms = [], []
        for (c, i), s in zip(pairs, scores):
            mask = band & (sj >= first_lo) if i == 0 else band
            ps, invs = [], []
            for hh in range(2):
                sink = sinks_ref[layer, 2 * c + hh] * LOG2_E
                sh = jnp.where(mask, s[:, 2 * Q_BLOCK * hh:2 * Q_BLOCK * (hh + 1)], neg_inf)
                m = jnp.maximum(jnp.max(sh, axis=-1, keepdims=True), sink)
                p = jnp.exp2(sh - m)
                invs.append(1.0 / (jnp.sum(p, axis=-1, keepdims=True) + jnp.exp2(sink - m)))
                ps.append(_bf(p))
            probs.append(jnp.concatenate(ps, axis=1))
            norms.append(jnp.where(lo_q, invs[0], invs[1]))
            if i == tb // Q_BLOCK - 1:
                yield
        for (c, i), p, inv in zip(pairs, probs, norms):
            r0 = i * Q_BLOCK
            o = _dot(p, win_stack(vvar, c, i)) * inv
            ga = proj[r0:r0 + Q_BLOCK, OFF_GA + LANES * c:OFF_GA + LANES * (c + 1)]
            mix[r0:r0 + Q_BLOCK, LANES * c:LANES * (c + 1)] = o * _silu(ga)
        yield
        h_l = _affine_scan(a_l, b_l, hprev[SUBLANES - 1:SUBLANES, :])
        gc = proj[:, OFF_GC:OFF_GC + C_WIDTH]
        mix[:, A_WIDTH + B_WIDTH:] = h_l * _silu(gc)
        out["h_tail"] = h_l[tb - SUBLANES:, :]

    def sequence(q, out):
        kvprev, pprev, xcprev, hprev, gstate, proj, mix, ywkv = seq_scratch[q]

        x = x_ref[q]
        brow = pl.ds(pl.program_id(0) * n_seq + q, 1)
        shift_m = mod_ref[0, brow, :]
        scale_m = mod_ref[1, brow, :]
        gate_m = mod_ref[2, brow, :]
        h = _bf(_rms_scale(x) * (npre_ref[...] * (1.0 + scale_m)) + shift_m)
        for src, dst, n in (GRP_PB, GRP_QKV, GRP_XC, GRP_GA, GRP_GB, GRP_GC):
            for off in range(0, n, PROJ_TILE):
                _project(h, win_ref, proj, (src + off, dst + off, min(PROJ_TILE, n - off)))
                yield

        pb = proj[:, OFF_PB:OFF_PB + B_SHIFT_PAD]
        prev = _shift_rows(pb, 1, pprev[...])
        xs = pb + (prev - pb) * mu_ref[...]
        yield
        r, k, v, ld, a, kkn, kmod = _rwkv_pre(xs, vec, wlora_ref[...], ones_bd)
        kka = kkn * a
        yield

        lcum = _cumsum_chunks(ld, CHUNK)
        e_l = jnp.exp(lcum)
        e_nl = jnp.exp(-lcum)
        rt = r * e_l
        at = -kkn * jnp.exp(lcum - ld)
        ktil = kmod * e_nl
        btil = kka * e_nl
        gam = [e_l[s.stop - 1:s.stop, :] for s in sls]
        khat = [ktil[sls[c]] * gam[c] for c in rng]
        bhat = [btil[sls[c]] * gam[c] for c in rng]
        yield
        a_s = [stk(at[s]) for s in sls]
        b_s = [stk(btil[s]) for s in sls]
        k_s = [stk(ktil[s]) for s in sls]
        yield
        v_s = [stk(v[s]) for s in sls]
        bh_s = [stk(bhat[c]) for c in rng]
        sc = [_dot_nt(_bf(jnp.concatenate([at[sls[c]], rt[sls[c]]], axis=0)),
                      jnp.concatenate([b_s[c], k_s[c]], axis=0)) for c in rng]
        yield "side may start"

        pw = [_bf(sc[c][0:CHUNK, 0:B_WIDTH]) * strict_b for c in rng]
        lak = [_bf(sc[c][0:CHUNK, B_WIDTH:]) * strict_b for c in rng]
        mrb = [_bf(sc[c][CHUNK:, 0:B_WIDTH]) * incl_b for c in rng]
        mrk = [_bf(sc[c][CHUNK:, B_WIDTH:]) * incl_b for c in rng]
        tacc = [pw[c].astype(F32) + eye_f for c in rng]
        pw = [_dot_b(pw[c], stk(pw[c])) for c in rng]
        yield
        for _ in range(4):
            res = [_dot(jnp.concatenate([_bf(tacc[c]), pw[c]], axis=0), stk(pw[c])) for c in rng]
            tacc = [tacc[c] + res[c][0:CHUNK] for c in rng]
            pw = [_bf(res[c][CHUNK:]) for c in rng]
            yield
        tinv = [_bf(tacc[c] + _dot(_bf(tacc[c]), stk(pw[c]))) for c in rng]
        lv = [_dot_b(lak[c], v_s[c]) for c in rng]
        yield
        w_c = [_dot_b(tinv[c], a_s[c]) for c in rng]
        z_c = [_dot(tinv[c], stk(lv[c])) for c in rng]
        yield
        w_s = [stk(w_c[c]) for c in rng]
        rp = [_bf(rt[sls[c]] + _dot(mrb[c], w_s[c])) for c in rng]
        y0 = [_dot(jnp.concatenate([mrb[c], mrk[c]], axis=1), jnp.concatenate([stk(z_c[c]), v_s[c]], axis=0))
              for c in rng]
        pp = [_bf(_dot_tn(w_s[c], bh_s[c])) for c in rng]
        qq = [jnp.where(bd_heads,
                        _dot_tn(_bf(jnp.concatenate([z_c[c], v[sls[c]]], axis=0)),
                                _bf(jnp.concatenate([bhat[c], khat[c]], axis=0))), 0.0) for c in rng]
        yield
        gmat = gstate[...]
        for c in rng:
            gb16 = _bf(gmat)
            ywkv[sls[c], :] = _dot_nt(rp[c], gb16) + y0[c]
            gmat = gam[c] * gmat + _dot(gb16, pp[c]) + qq[c]
        gstate[...] = gmat
        yield "side must be done"

        gb = proj[:, OFF_GB:OFF_GB + B_WIDTH]
        mix[:, A_WIDTH:A_WIDTH + B_WIDTH] = _rwkv_post(ywkv[...], r, kmod, v, gb, vec, ones_bd)

        yo = _dot(_bf(mix[...]), wout_ref[...])
        y_ref[q] = x + (gate_m * npost_ref[...]) * _rms_scale(yo)

        k_tail, v_tail = out["k_tail"], out["v_tail"]
        kvprev[:, 0:KV_WIDTH] = k_tail
        kvprev[:, KV_WIDTH:2 * KV_WIDTH] = v_tail
        pprev[...] = pb[tb - SUBLANES:, :]
        xcprev[...] = out["xc_tail"]
        hprev[...] = out["h_tail"]
        shift_ref[q] = pb[tb - SUBLANES:, :]
        wkv_ref[q] = gmat
        conv_ref[q] = out["xc_tail"]
        hlru_ref[q] = out["h_tail"]

        @pl.when(t == pl.num_programs(1) - 1)
        def _():
            kv_ref[q, 0] = k_tail.T
            kv_ref[q, 1] = v_tail.T

    for q in range(n_seq):
        shared = {}
        main, extra = sequence(q, shared), side(q, shared)
        while next(main) != "side may start":
            pass
        extra_done = False
        while True:
            if not extra_done:
                try:
                    next(extra)
                except StopIteration:
                    extra_done = True
            if next(main) == "side must be done":
                break
        for _ in extra:
            pass
        for _ in main:
            pass


def _weight_specs(layer_of, grid_rank):
    def lay(*shape):
        zeros = (0,) * len(shape)
        if grid_rank == 1:
            return pl.BlockSpec((None, *shape), lambda i: (layer_of(i), *zeros))
        return pl.BlockSpec((None, *shape), lambda a, b: (layer_of(a, b), *zeros))
    return [
        lay(1, D_MODEL),
        lay(1, D_MODEL),
        lay(P_IN, D_MODEL),
        lay(D_MODEL, D_MODEL),
    ], [
        lay(1, B_SHIFT_PAD),
        lay(VEC_ROWS, B_WIDTH),
        lay(LANES, 2 * B_WIDTH),
        lay(C_WIDTH, 2 * C_WIDTH),
    ]


def _prompt_layer(layer, x, mod, wts, cos_t, sin_t, mod_row0, tb):
    bsz, seq, _ = x.shape
    nt = seq // tb
    sps = SEQS_PER_STEP
    per_b3 = lambda b, t: (b, 0, 0)
    w_head, w_tail = _weight_specs(lambda b, t: layer, 2)
    return pl.pallas_call(
        functools.partial(_prompt_kernel, layer=layer),
        grid=(bsz // sps, nt),
        in_specs=[
            pl.BlockSpec(memory_space=pltpu.SMEM),
            pl.BlockSpec((sps, tb, D_MODEL), lambda b, t: (b, t, 0)),
            pl.BlockSpec((None, 3, SUBLANES, D_MODEL),
                         lambda b, t: (layer, 0, mod_row0 // SUBLANES, 0)),
            *w_head,
            pl.BlockSpec((tb, LANES), lambda b, t: (t, 0)),
            pl.BlockSpec((tb, LANES), lambda b, t: (t, 0)),
            *w_tail,
        ],
        out_specs=[
            pl.BlockSpec((sps, tb, D_MODEL), lambda b, t: (b, t, 0)),
            pl.BlockSpec((sps, 2, KV_WIDTH, WINDOW), lambda b, t: (b, 0, 0, 0)),
            pl.BlockSpec((sps, SUBLANES, B_SHIFT_PAD), per_b3),
            pl.BlockSpec((sps, B_WIDTH, B_WIDTH), per_b3),
            pl.BlockSpec((sps, SUBLANES, C_WIDTH), per_b3),
            pl.BlockSpec((sps, SUBLANES, C_WIDTH), per_b3),
        ],
        out_shape=[
            jax.ShapeDtypeStruct((bsz, seq, D_MODEL), F32),
            jax.ShapeDtypeStruct((bsz, 2, KV_WIDTH, WINDOW), F32),
            jax.ShapeDtypeStruct((bsz, SUBLANES, B_SHIFT_PAD), F32),
            jax.ShapeDtypeStruct((bsz, B_WIDTH, B_WIDTH), F32),
            jax.ShapeDtypeStruct((bsz, SUBLANES, C_WIDTH), F32),
            jax.ShapeDtypeStruct((bsz, SUBLANES, C_WIDTH), F32),
        ],
        scratch_shapes=[
            pltpu.VMEM((WINDOW, 2 * KV_WIDTH), F32),
            pltpu.VMEM((SUBLANES, B_SHIFT_PAD), F32),
            pltpu.VMEM((SUBLANES, C_WIDTH), F32),
            pltpu.VMEM((SUBLANES, C_WIDTH), F32),
            pltpu.VMEM((B_WIDTH, B_WIDTH), F32),
            pltpu.VMEM((tb, P_PAD), F32),
            pltpu.VMEM((tb, D_MODEL), F32),
            pltpu.VMEM((tb, B_WIDTH), F32),
        ] * sps,
        compiler_params=pltpu.CompilerParams(
            dimension_semantics=("arbitrary", "arbitrary"), vmem_limit_bytes=VMEM_LIMIT_BYTES),
        name="prompt_layer",
    )(wts["sinks"], x, mod, wts["norm_pre"], wts["norm_post"], wts["w_in"], wts["w_out"], cos_t, sin_t,
      wts["mu"], wts["vec"], wts["wlora"], wts["wgate"])


def _sample_kernel(sinks_ref, x_ref, mod_ref, npre_ref, npost_ref, win_ref, wout_ref,
                   cos_ref, sin_ref, mu_ref, vec_ref, wlora_ref, wgate_ref,
                   kt_ref, vt_ref, sprev_ref, wkv_ref, conv_ref, hl_ref,
                   y_ref, knew_ref, vnew_ref, shift_ref, wkvo_ref, convo_ref, hlo_ref,
                   ys_s, proj_s, mix_s, q_s, o_s, knew_s, rw_s, rkv_s, yt_s):
    layer = pl.program_id(0)
    j = pl.program_id(1)
    nb = x_ref.shape[0]
    bb = nb // B_HEADS
    n_slot = kt_ref.shape[1]
    vec = vec_ref[...]
    cos = cos_ref[...]
    sin = sin_ref[...]

    @pl.when(jnp.logical_and(layer == 0, j == 0))
    def _():
        ys_s[...] = x_ref[...]

    @pl.when(j == 0)
    def _():
        ones_bd = _head_ones(B_WIDTH)
        x = ys_s[...]
        h = _bf(_rms_scale(x) * npre_ref[...] * (1.0 + mod_ref[1]) + mod_ref[0])
        for grp in (GRP_QKV, GRP_GA, GRP_PB, GRP_GB, GRP_XC, GRP_GC):
            _project(h, win_ref, proj_s, grp)
        k_new = _rope128(proj_s[:, OFF_K:OFF_K + KV_WIDTH], cos, sin)
        knew_s[...] = k_new
        knew_ref[...] = k_new.T
        vnew_ref[...] = proj_s[:, OFF_V:OFF_V + KV_WIDTH].T
        pb = proj_s[:, OFF_PB:OFF_PB + B_SHIFT_PAD]
        pb_t = pb.T
        shift_ref[...] = pb_t[0:B_SHIFT, :]
        prev = jnp.concatenate([sprev_ref[...], jnp.zeros((B_SHIFT_PAD - B_SHIFT, nb), F32)], axis=0).T
        xs = pb + (prev - pb) * mu_ref[...]
        r, k, v, ld, a, kkn, kmod = _rwkv_pre(xs, vec, wlora_ref[...], ones_bd)
        rw_s[T_DECAY] = jnp.exp(ld).T
        rw_s[T_NKK] = (-kkn).T
        rw_s[T_KKA] = (kkn * a).T
        rw_s[T_KMOD] = kmod.T
        rw_s[T_R] = r.T
        rw_s[T_V] = v.T
        rkv_s[0] = r
        rkv_s[1] = kmod
        rkv_s[2] = v

    rs = pl.ds(pl.multiple_of(j * bb, SUBLANES), bb)
    lane = _iota((bb, LANES), 1)
    lo = lane < HEAD_DIM
    for c in range(A_HEADS // 2):
        g = c // 2
        in_g = lo if g == 0 else jnp.logical_not(lo)
        qcol = _rope128(proj_s[rs, OFF_Q + LANES * c:OFF_Q + LANES * (c + 1)], cos, sin)
        qrol = pltpu.roll(qcol, HEAD_DIM, 1)
        for hh in range(2):
            hd = 2 * c + hh
            q_s[hd * bb:(hd + 1) * bb, :] = jnp.where(in_g, qcol if hh == g else qrol, 0.0)
    knew_blk = knew_s[rs, :]
    vnew_blk = proj_s[rs, OFF_V:OFF_V + KV_WIDTH]
    hrow = _iota((A_HEADS, 1), 0)
    sink_col = jnp.zeros((A_HEADS, 1), F32)
    for hd in range(A_HEADS):
        sink_col = jnp.where(hrow == hd, sinks_ref[layer, hd] * LOG2_E, sink_col)

    qbs = [q_s[pl.ds(b, A_HEADS, stride=bb), :] for b in range(bb)]
    s_c = [_dot(_bf(qbs[b]), _bf(kt_ref[b * KV_WIDTH:(b + 1) * KV_WIDTH, :])) for b in range(bb)]
    s_n = [jnp.sum(qbs[b] * knew_blk[b:b + 1, :], axis=-1, keepdims=True) for b in range(bb)]
    p_c, p_n = [], []
    for b in range(bb):
        m = jnp.maximum(jnp.maximum(jnp.max(s_c[b], axis=-1, keepdims=True), s_n[b]), sink_col)
        e_c = jnp.exp2(s_c[b] - m)
        e_n = jnp.exp2(s_n[b] - m)
        inv = 1.0 / (jnp.sum(e_c, axis=-1, keepdims=True) + e_n + jnp.exp2(sink_col - m))
        p_c.append(_bf(e_c * inv))
        p_n.append(e_n * inv)
    for b in range(bb):
        o = _dot_nt(p_c[b], _bf(vt_ref[b * KV_WIDTH:(b + 1) * KV_WIDTH, :]))
        o_s[pl.ds(b, A_HEADS, stride=bb), :] = o + p_n[b] * vnew_blk[b:b + 1, :]
    for c in range(A_HEADS // 2):
        g = c // 2
        halves = []
        for hh in range(2):
            oh = o_s[(2 * c + hh) * bb:(2 * c + hh + 1) * bb, :]
            halves.append(oh if hh == g else pltpu.roll(oh, HEAD_DIM, 1))
        ga = proj_s[rs, OFF_GA + LANES * c:OFF_GA + LANES * (c + 1)]
        mix_s[rs, LANES * c:LANES * (c + 1)] = jnp.where(lo, halves[0], halves[1]) * _silu(ga)

    hrows = pl.ds(pl.multiple_of(j * HEAD_DIM, HEAD_DIM), HEAD_DIM)
    w_t = rw_s[T_DECAY, hrows, :]
    nkk_t = rw_s[T_NKK, hrows, :]
    kka_t = rw_s[T_KKA, hrows, :]
    k_t = rw_s[T_KMOD, hrows, :]
    r_t = rw_s[T_R, hrows, :]

    def value_row(vi, carry):
        krows = pl.ds(pl.multiple_of(vi * HEAD_DIM, HEAD_DIM), HEAD_DIM)
        st = wkv_ref[krows, :]
        sa = jnp.sum(st * nkk_t, axis=0, keepdims=True)
        st_new = st * w_t + sa * kka_t + rw_s[T_V, pl.ds(j * HEAD_DIM + vi, 1), :] * k_t
        wkvo_ref[krows, :] = st_new
        yt_s[pl.ds(j * HEAD_DIM + vi, 1), :] = jnp.sum(st_new * r_t, axis=0, keepdims=True)
        return carry

    lax.fori_loop(0, HEAD_DIM, value_row, 0, unroll=4)

    @pl.when(j == pl.num_programs(1) - 1)
    def _():
        ones_bd = _head_ones(B_WIDTH)
        x = ys_s[...]
        gb = proj_s[:, OFF_GB:OFF_GB + B_WIDTH]
        mix_s[:, A_WIDTH:A_WIDTH + B_WIDTH] = _rwkv_post(yt_s[...].T, rkv_s[0], rkv_s[1], rkv_s[2], gb, vec, ones_bd)
        xc = proj_s[:, OFF_XC:OFF_XC + C_WIDTH]
        gc = proj_s[:, OFF_GC:OFF_GC + C_WIDTH]
        u = vec[V_CB:V_CB + 1, :] + xc * vec[V_CW3:V_CW3 + 1, :]
        for i in range(CONV_W - 1):
            u = u + conv_ref[i] * vec[V_CW0 + i:V_CW0 + i + 1, :]
        a_l, mult, gi = _lru_gates(u, vec, wgate_ref[...])
        h_l = a_l * hl_ref[...] + mult * gi * u
        hlo_ref[...] = h_l
        convo_ref[0] = conv_ref[1]
        convo_ref[1] = conv_ref[2]
        convo_ref[2] = xc
        mix_s[:, A_WIDTH + B_WIDTH:] = h_l * _silu(gc)
        yo = _dot(_bf(mix_s[...]), wout_ref[...])
        y = x + mod_ref[2] * (_rms_scale(yo) * npost_ref[...])
        ys_s[...] = y
        y_ref[...] = y


def _sample_layers(x, mod, wts, cos_r, sin_r, kt, vt, sprev_t, wkv_t, conv_t, hl):
    nb = x.shape[0]
    bb = nb // B_HEADS
    n_slot = kt.shape[2]
    n_hv = HEAD_DIM * HEAD_DIM
    w_head, w_tail = _weight_specs(lambda l, j: l, 2)
    const2 = lambda l, j: (0, 0)
    lay3 = lambda l, j: (l, 0, 0)
    return pl.pallas_call(
        _sample_kernel,
        grid=(DEPTH, B_HEADS),
        in_specs=[
            pl.BlockSpec(memory_space=pltpu.SMEM),
            pl.BlockSpec((nb, D_MODEL), const2),
            pl.BlockSpec((None, 3, nb, D_MODEL), lambda l, j: (l, 0, 0, 0)),
            *w_head,
            pl.BlockSpec((1, LANES), const2),
            pl.BlockSpec((1, LANES), const2),
            *w_tail,
            pl.BlockSpec((None, bb * KV_WIDTH, n_slot), lambda l, j: (l, j, 0)),
            pl.BlockSpec((None, bb * KV_WIDTH, n_slot), lambda l, j: (l, j, 0)),
            pl.BlockSpec((None, B_SHIFT, nb), lay3),
            pl.BlockSpec((None, n_hv, nb), lambda l, j: (l, j, 0)),
            pl.BlockSpec((None, CONV_W - 1, nb, C_WIDTH), lambda l, j: (l, 0, 0, 0)),
            pl.BlockSpec((None, nb, C_WIDTH), lay3),
        ],
        out_specs=[
            pl.BlockSpec((nb, D_MODEL), const2),
            pl.BlockSpec((None, KV_WIDTH, nb), lay3),
            pl.BlockSpec((None, KV_WIDTH, nb), lay3),
            pl.BlockSpec((None, B_SHIFT, nb), lay3),
            pl.BlockSpec((None, n_hv, nb), lambda l, j: (l, j, 0)),
            pl.BlockSpec((None, CONV_W - 1, nb, C_WIDTH), lambda l, j: (l, 0, 0, 0)),
            pl.BlockSpec((None, nb, C_WIDTH), lay3),
        ],
        out_shape=[
            jax.ShapeDtypeStruct((nb, D_MODEL), F32),
            jax.ShapeDtypeStruct((DEPTH, KV_WIDTH, nb), F32),
            jax.ShapeDtypeStruct((DEPTH, KV_WIDTH, nb), F32),
            jax.ShapeDtypeStruct((DEPTH, B_SHIFT, nb), F32),
            jax.ShapeDtypeStruct((DEPTH, B_HEADS * n_hv, nb), F32),
            jax.ShapeDtypeStruct((DEPTH, CONV_W - 1, nb, C_WIDTH), F32),
            jax.ShapeDtypeStruct((DEPTH, nb, C_WIDTH), F32),
        ],
        scratch_shapes=[
            pltpu.VMEM((nb, D_MODEL), F32),
            pltpu.VMEM((nb, P_PAD), F32),
            pltpu.VMEM((nb, D_MODEL), F32),
            pltpu.VMEM((A_HEADS * bb, LANES), F32),
            pltpu.VMEM((A_HEADS * bb, LANES), F32),
            pltpu.VMEM((nb, KV_WIDTH), F32),
            pltpu.VMEM((6, B_WIDTH, nb), F32),
            pltpu.VMEM((3, nb, B_WIDTH), F32),
            pltpu.VMEM((B_WIDTH, nb), F32),
        ],
        compiler_params=pltpu.CompilerParams(
            dimension_semantics=("arbitrary", "arbitrary"), vmem_limit_bytes=VMEM_LIMIT_BYTES),
        name="sample_layers",
    )(wts["sinks"], x, mod, wts["norm_pre"], wts["norm_post"], wts["w_in"], wts["w_out"], cos_r, sin_r,
      wts["mu"], wts["vec"], wts["wlora"], wts["wgate"], kt, vt, sprev_t, wkv_t, conv_t, hl)


def _rope_lanes():
    half = HEAD_DIM // 2
    inv_freq = ROPE_THETA ** (-jnp.arange(half, dtype=F32) / half)
    freq = jnp.tile(inv_freq, LANES // half)
    sign = jnp.tile(jnp.concatenate([-jnp.ones((half,), F32), jnp.ones((half,), F32)]), LANES // HEAD_DIM)
    return freq, sign


def _rope_row(pos):
    freq, sign = _rope_lanes()
    ang = jnp.float32(pos) * freq[None, :]
    return jnp.cos(ang), jnp.sin(ang) * sign[None, :]


def _rope_tables(seq):
    freq, sign = _rope_lanes()
    hi = (jnp.arange(seq // Q_BLOCK, dtype=F32) * Q_BLOCK)[:, None] * freq[None, :]
    lo = jnp.arange(Q_BLOCK, dtype=F32)[:, None] * freq[None, :]
    ch, sh = jnp.cos(hi)[:, None, :], jnp.sin(hi)[:, None, :]
    cl, sl = jnp.cos(lo)[None, :, :], jnp.sin(lo)[None, :, :]
    cos_t = (ch * cl - sh * sl).reshape(seq, LANES)
    sin_t = ((sh * cl + ch * sl) * sign).reshape(seq, LANES)
    return cos_t, sin_t


def _block_diag(w):
    dl, n, d, e = w.shape
    eye = jnp.eye(n, dtype=w.dtype)
    return (eye[None, :, None, :, None] * w[:, :, :, None, :]).reshape(dl, n * d, n * e)


def _prep_weights(p):
    dl = p["w_in"].shape[0]
    w_in_t = jnp.swapaxes(p["w_in"], 1, 2)
    q_scale = jnp.where(jnp.arange(w_in_t.shape[1]) < A_WIDTH, Q_SCALE, 1.0).astype(F32)
    w_in_t = (w_in_t * q_scale[None, :, None]).astype(BF16)
    z32 = jnp.zeros((dl, LORA, B_WIDTH), F32)
    wlora = jnp.concatenate([
        jnp.concatenate([p["rwkv_w_up"], z32], axis=2),
        jnp.concatenate([z32, p["rwkv_a_up"]], axis=2),
        jnp.zeros((dl, LANES - 2 * LORA, 2 * B_WIDTH), F32)], axis=1).astype(BF16)
    wgate = jnp.concatenate([_block_diag(p["lru_gate_a_w"]), _block_diag(p["lru_gate_x_w"])], axis=2).astype(BF16)
    cw = p["lru_conv_w"]
    rows = [p["rwkv_w0"], p["rwkv_a0"], p["rwkv_k_k"], p["rwkv_k_a"], p["rwkv_lnx_w"], p["rwkv_lnx_b"],
            p["rwkv_r_k"].reshape(dl, B_WIDTH), p["lru_conv_b"], p["lru_gate_a_b"], p["lru_gate_x_b"],
            p["lru_lambda"], cw[:, 0], cw[:, 1], cw[:, 2], cw[:, 3], jnp.zeros((dl, B_WIDTH), F32)]
    return {
        "sinks": p["attn_sinks"],
        "norm_pre": p["norm_pre"].reshape(dl, 1, D_MODEL),
        "norm_post": p["norm_post"].reshape(dl, 1, D_MODEL),
        "w_in": w_in_t,
        "w_out": p["w_out"].astype(BF16),
        "mu": jnp.concatenate([p["rwkv_mu"], jnp.zeros((dl, B_SHIFT_PAD - B_SHIFT), F32)],
                              axis=1).reshape(dl, 1, B_SHIFT_PAD),
        "vec": jnp.stack(rows, axis=1),
        "wlora": wlora,
        "wgate": wgate,
    }


def _forward(x_prompt, x_sample, c_prompt, c_sample, cache_swa_k, cache_swa_v, state_rwkv_shift,
             state_rwkv_wkv, state_lru_conv, state_lru_h, p, tb):
    bp, seq, _ = x_prompt.shape
    nb = x_sample.shape[0]
    n_slot = cache_swa_k.shape[2]
    assert nb % SUBLANES == 0 and nb % B_HEADS == 0 and bp <= SUBLANES and seq % tb == 0

    pad_rows = (-(nb + bp)) % SUBLANES
    c_all = jnp.concatenate([c_sample, c_prompt, jnp.zeros((pad_rows, D_MODEL), F32)], axis=0)
    mod = _mod_call(c_all, p["w_mod"], p["b_mod"])
    wts = _prep_weights(p)

    cos_p, sin_p = _rope_tables(seq)
    cos_s, sin_s = _rope_row(PAST_LEN)

    kt = jnp.transpose(cache_swa_k, (0, 1, 3, 4, 2)).reshape(DEPTH, nb * KV_WIDTH, n_slot)
    vt = jnp.transpose(cache_swa_v, (0, 1, 3, 4, 2)).reshape(DEPTH, nb * KV_WIDTH, n_slot)
    sprev_t = jnp.swapaxes(state_rwkv_shift, 1, 2)
    wkv_t = jnp.transpose(state_rwkv_wkv, (0, 2, 3, 4, 1)).reshape(DEPTH, B_HEADS * HEAD_DIM * HEAD_DIM, nb)
    conv_t = jnp.swapaxes(state_lru_conv, 1, 2)
    ys, knew_t, vnew_t, shift_t, wkvo_t, convo_t, hlo = _sample_layers(
        x_sample.reshape(nb, D_MODEL), mod, wts, cos_s, sin_s, kt, vt, sprev_t, wkv_t, conv_t, state_lru_h)
    outs_s = (
        jnp.transpose(knew_t.reshape(DEPTH, 1, A_KV_HEADS, HEAD_DIM, nb), (0, 4, 1, 2, 3)),
        jnp.transpose(vnew_t.reshape(DEPTH, 1, A_KV_HEADS, HEAD_DIM, nb), (0, 4, 1, 2, 3)),
        jnp.swapaxes(shift_t, 1, 2),
        jnp.transpose(wkvo_t.reshape(DEPTH, B_HEADS, HEAD_DIM, HEAD_DIM, nb), (0, 4, 1, 2, 3)),
        jnp.swapaxes(convo_t, 1, 2),
        hlo,
    )

    yp = x_prompt
    outs_p = []
    for l in range(DEPTH):
        yp, kv_p, sh_p, wkv_p, conv_p, h_p = _prompt_layer(l, yp, mod, wts, cos_p, sin_p, nb, tb)
        kv_p = jnp.transpose(kv_p.reshape(bp, 2, A_KV_HEADS, HEAD_DIM, WINDOW), (1, 0, 4, 2, 3))
        outs_p.append((
            kv_p[0], kv_p[1],
            sh_p[:, SUBLANES - 1, :B_SHIFT],
            jnp.stack([wkv_p[:, HEAD_DIM * hd:HEAD_DIM * (hd + 1), HEAD_DIM * hd:HEAD_DIM * (hd + 1)]
                       for hd in range(B_HEADS)], axis=1),
            conv_p[:, SUBLANES - (CONV_W - 1):, :],
            h_p[:, SUBLANES - 1, :],
        ))
    sp = [jnp.stack(z) for z in zip(*outs_p)]
    return (yp, ys.reshape(nb, 1, D_MODEL), *sp, *outs_s)


def kernel(x_prompt, x_sample, c_prompt, c_sample, cache_swa_k, cache_swa_v, state_rwkv_shift, state_rwkv_wkv, state_lru_conv, state_lru_h, norm_pre, norm_post, w_mod, b_mod, w_in, w_out, attn_sinks, rwkv_mu, rwkv_w0, rwkv_w_up, rwkv_a0, rwkv_a_up, rwkv_k_k, rwkv_k_a, rwkv_r_k, rwkv_lnx_w, rwkv_lnx_b, lru_conv_w, lru_conv_b, lru_gate_a_w, lru_gate_a_b, lru_gate_x_w, lru_gate_x_b, lru_lambda):
    p = dict(norm_pre=norm_pre, norm_post=norm_post, w_mod=w_mod, b_mod=b_mod, w_in=w_in, w_out=w_out,
             attn_sinks=attn_sinks, rwkv_mu=rwkv_mu, rwkv_w0=rwkv_w0, rwkv_w_up=rwkv_w_up, rwkv_a0=rwkv_a0,
             rwkv_a_up=rwkv_a_up, rwkv_k_k=rwkv_k_k, rwkv_k_a=rwkv_k_a, rwkv_r_k=rwkv_r_k,
             rwkv_lnx_w=rwkv_lnx_w, rwkv_lnx_b=rwkv_lnx_b, lru_conv_w=lru_conv_w, lru_conv_b=lru_conv_b,
             lru_gate_a_w=lru_gate_a_w, lru_gate_a_b=lru_gate_a_b, lru_gate_x_w=lru_gate_x_w,
             lru_gate_x_b=lru_gate_x_b, lru_lambda=lru_lambda)
    return _forward(x_prompt, x_sample, c_prompt, c_sample, cache_swa_k, cache_swa_v, state_rwkv_shift,
                    state_rwkv_wkv, state_lru_conv, state_lru_h, p, TIME_BLOCK)
```

```python
import functools

import numpy as np
import jax
import jax.numpy as jnp
from jax import lax
from jax.experimental import pallas as pl
from jax.experimental.pallas import tpu as pltpu

F32 = jnp.float32
BF16 = jnp.bfloat16

D_MODEL = 1024
DEPTH = 2
A_HEADS = 8
A_KV_HEADS = 2
HEAD_DIM = 64
A_WIDTH = A_HEADS * HEAD_DIM
KV_WIDTH = A_KV_HEADS * HEAD_DIM
WINDOW = 128
Q_BLOCK = 128
ROPE_THETA = 10000.0
B_WIDTH = 256
B_HEADS = 4
LORA = 32
B_SHIFT = 3 * B_WIDTH + 2 * LORA
B_SHIFT_PAD = 3 * B_WIDTH + 128
LNX_EPS = 1e-5 * 8 ** 2
C_WIDTH = 256
CONV_W = 4
LRU_C = 8.0
EPS = 1e-6
PAST_LEN = 8192
EXP_NEG_HALF = float(np.exp(-0.5))
LOG2_E = float(np.log2(np.e))
Q_SCALE = HEAD_DIM ** -0.5 * LOG2_E

LANES = 128
SUBLANES = 8
VMEM_LIMIT_BYTES = 56 * 1024 * 1024

OFF_Q = 0
OFF_K = OFF_Q + A_WIDTH
OFF_V = OFF_K + KV_WIDTH
OFF_GA = OFF_V + KV_WIDTH
OFF_PB = OFF_GA + A_WIDTH
OFF_GB = OFF_PB + B_SHIFT_PAD
OFF_XC = OFF_GB + B_WIDTH
OFF_GC = OFF_XC + C_WIDTH
P_PAD = OFF_GC + C_WIDTH
P_IN = P_PAD - (B_SHIFT_PAD - B_SHIFT)

CHUNK = 64
TIME_BLOCK = 512
SEQS_PER_STEP = 1
PROJ_TILE = 256

(V_W0, V_A0, V_KK, V_KA, V_LNW, V_LNB, V_RK, V_CB, V_GAB, V_GXB, V_LAM,
 V_CW0, V_CW1, V_CW2, V_CW3) = range(15)
VEC_ROWS = 16

(T_DECAY, T_NKK, T_KKA, T_KMOD, T_R, T_V) = range(6)

_NT = (((1,), (1,)), ((), ()))
_TN = (((0,), (0,)), ((), ()))


def _bf(x):
    return x.astype(BF16)


def _dot(a, b):
    return jnp.dot(a, b, preferred_element_type=F32)


def _dot_b(a, b):
    return jnp.dot(a, b, preferred_element_type=F32).astype(BF16)


def _dot_nt(a, b):
    return lax.dot_general(a, b, _NT, preferred_element_type=F32)


def _dot_tn(a, b):
    return lax.dot_general(a, b, _TN, preferred_element_type=F32)


def _iota(shape, dim):
    return lax.broadcasted_iota(jnp.int32, shape, dim)


def _sigmoid(x):
    return jax.nn.sigmoid(x)


def _silu(x):
    return x * _sigmoid(x)


def _softplus(x):
    return jnp.maximum(x, 0.0) + jnp.log1p(jnp.exp(-jnp.abs(x)))


def _head_ones(n):
    r = _iota((n, n), 0) // HEAD_DIM
    c = _iota((n, n), 1) // HEAD_DIM
    return jnp.where(r == c, 1.0, 0.0).astype(BF16)


def _head_sum(x, ones_bd):
    return _dot(_bf(x), ones_bd)


def _rms_scale(x):
    ms = jnp.mean(x * x, axis=-1, keepdims=True)
    return x * lax.rsqrt(ms + EPS)


def _affine_scan(a, b, h0):
    rows, n = a.shape
    groups = rows // SUBLANES
    a = a.reshape(groups, SUBLANES, n)
    b = b.reshape(groups, SUBLANES, n)
    sub = _iota((1, SUBLANES, n), 1)
    d = 1
    while d < SUBLANES:
        keep = sub >= d
        a_sh = jnp.where(keep, pltpu.roll(a, d, 1), 1.0)
        b_sh = jnp.where(keep, pltpu.roll(b, d, 1), 0.0)
        b = b + a * b_sh
        a = a * a_sh
        d *= 2
    outs = []
    carry = h0
    for g in range(groups):
        hg = b[g] + a[g] * carry
        outs.append(hg)
        carry = hg[SUBLANES - 1:SUBLANES, :]
    return jnp.concatenate(outs, axis=0)


def _shift_rows(x, j, tail):
    rows, n = x.shape
    groups = rows // SUBLANES
    x3 = jnp.concatenate([tail, x], axis=0).reshape(groups + 1, SUBLANES, n)
    r3 = pltpu.roll(x3, j, 1)
    sub = _iota((1, SUBLANES, n), 1)
    return jnp.where(sub >= j, r3[1:], r3[:-1]).reshape(rows, n)


def _cumsum_chunks(x, chunk):
    rows, n = x.shape
    groups = rows // SUBLANES
    x = x.reshape(groups, SUBLANES, n)
    sub = _iota((1, SUBLANES, n), 1)
    d = 1
    while d < SUBLANES:
        x = x + jnp.where(sub >= d, pltpu.roll(x, d, 1), 0.0)
        d *= 2
    outs = []
    for g in range(groups):
        xg = x[g]
        if (g * SUBLANES) % chunk != 0:
            xg = xg + outs[-1][SUBLANES - 1:SUBLANES, :]
        outs.append(xg)
    return jnp.concatenate(outs, axis=0)


def _rope128(z, cos, sin_signed):
    lane = _iota(z.shape, 1)
    first = (lane & 32) == 0
    sw = jnp.where(first, pltpu.roll(z, 96, 1), pltpu.roll(z, 32, 1))
    return z * cos + sw * sin_signed


def _rwkv_pre(xs, vec, wlora, ones_bd):
    r = xs[:, 0:B_WIDTH]
    k = xs[:, B_WIDTH:2 * B_WIDTH]
    v = xs[:, 2 * B_WIDTH:3 * B_WIDTH]
    lor = xs[:, 3 * B_WIDTH:B_SHIFT_PAD]
    lane = _iota(lor.shape, 1)
    z = jnp.where(lane < LORA, jnp.tanh(lor), lor)
    wa = _dot(_bf(z), wlora)
    zw = vec[V_W0:V_W0 + 1, :] + wa[:, :B_WIDTH]
    ld = -EXP_NEG_HALF * _sigmoid(zw)
    a = _sigmoid(vec[V_A0:V_A0 + 1, :] + wa[:, B_WIDTH:])
    kk = k * vec[V_KK:V_KK + 1, :]
    kkn = kk * jnp.minimum(lax.rsqrt(_head_sum(kk * kk, ones_bd)), 1e12)
    kmod = k * (1.0 + (a - 1.0) * vec[V_KA:V_KA + 1, :])
    return r, k, v, ld, a, kkn, kmod


def _rwkv_post(y, r, kmod, v, gb, vec, ones_bd):
    inv = 1.0 / HEAD_DIM
    mu = _head_sum(y, ones_bd) * inv
    yc = y - mu
    var = _head_sum(yc * yc, ones_bd) * inv
    yn = yc * lax.rsqrt(var + LNX_EPS)
    yn = yn * vec[V_LNW:V_LNW + 1, :] + vec[V_LNB:V_LNB + 1, :]
    bonus = _head_sum(r * kmod * vec[V_RK:V_RK + 1, :], ones_bd) * v
    return (yn + bonus) * _silu(gb)


def _lru_gates(u, vec, wgate):
    gates = _dot(_bf(u), wgate)
    gr = _sigmoid(gates[:, :C_WIDTH] + vec[V_GAB:V_GAB + 1, :])
    gi = _sigmoid(gates[:, C_WIDTH:] + vec[V_GXB:V_GXB + 1, :])
    sp = _softplus(-vec[V_LAM:V_LAM + 1, :])
    log_a = -LRU_C * gr * sp
    a = jnp.exp(log_a)
    mult = jnp.sqrt(1.0 - a * a)
    return a, mult, gi


def _project(h, win_ref, proj_s, group):
    src, dst, n = group
    proj_s[:, dst:dst + n] = _dot_nt(h, win_ref[src:src + n, :])


_DELTA = B_SHIFT_PAD - B_SHIFT
GRP_PB = (OFF_PB, OFF_PB, B_SHIFT_PAD)
GRP_QKV = (OFF_Q, OFF_Q, OFF_GA - OFF_Q)
GRP_GA = (OFF_GA, OFF_GA, A_WIDTH)
GRP_GB = (OFF_GB - _DELTA, OFF_GB, B_WIDTH)
GRP_XC = (OFF_XC - _DELTA, OFF_XC, C_WIDTH)
GRP_GC = (OFF_GC - _DELTA, OFF_GC, C_WIDTH)


def _mod_kernel(c_ref, w_ref, b_ref, o_ref):
    o_ref[...] = _dot(_bf(_silu(c_ref[...])), _bf(w_ref[...])) + b_ref[...]


def _mod_call(c_all, w_mod, b_mod):
    rows = c_all.shape[0]
    return pl.pallas_call(
        _mod_kernel,
        grid=(DEPTH, 3),
        in_specs=[
            pl.BlockSpec((rows, D_MODEL), lambda l, j: (0, 0)),
            pl.BlockSpec((None, D_MODEL, D_MODEL), lambda l, j: (l, 0, j)),
            pl.BlockSpec((None, 1, D_MODEL), lambda l, j: (l, 0, j)),
        ],
        out_specs=pl.BlockSpec((None, None, rows, D_MODEL), lambda l, j: (l, j, 0, 0)),
        out_shape=jax.ShapeDtypeStruct((DEPTH, 3, rows, D_MODEL), F32),
        compiler_params=pltpu.CompilerParams(
            dimension_semantics=("arbitrary", "arbitrary"), vmem_limit_bytes=VMEM_LIMIT_BYTES),
        name="adaln_mod",
    )(c_all, w_mod, b_mod.reshape(DEPTH, 1, 3 * D_MODEL))


def _prompt_kernel(sinks_ref, x_ref, mod_ref, npre_ref, npost_ref, win_ref, wout_ref,
                   cos_ref, sin_ref, mu_ref, vec_ref, wlora_ref, wgate_ref,
                   y_ref, kv_ref, shift_ref, wkv_ref, conv_ref, hlru_ref,
                   *scratch, layer):
    n_seq = x_ref.shape[0]
    tb = x_ref.shape[1]
    t = pl.program_id(1)
    per_seq = len(scratch) // n_seq
    seq_scratch = [scratch[q * per_seq:(q + 1) * per_seq] for q in range(n_seq)]

    @pl.when(t == 0)
    def _():
        for kvprev, pprev, xcprev, hprev, gstate, _, _, _ in seq_scratch:
            kvprev[...] = jnp.zeros_like(kvprev)
            pprev[...] = jnp.zeros_like(pprev)
            xcprev[...] = jnp.zeros_like(xcprev)
            hprev[...] = jnp.zeros_like(hprev)
            gstate[...] = jnp.zeros_like(gstate)

    vec = vec_ref[...]
    ones_bd = _head_ones(B_WIDTH)
    cos = cos_ref[...]
    sin = sin_ref[...]
    nck = tb // CHUNK
    sls = [slice(c * CHUNK, (c + 1) * CHUNK) for c in range(nck)]
    rng = range(nck)
    pairs = [(c, i) for c in range(A_HEADS // 2) for i in range(tb // Q_BLOCK)]

    lo_kv = _iota((tb + WINDOW, KV_WIDTH), 1) < HEAD_DIM
    lo_q = _iota((Q_BLOCK, LANES), 1) < HEAD_DIM
    qi = _iota((Q_BLOCK, 2 * Q_BLOCK), 0)
    sj = _iota((Q_BLOCK, 2 * Q_BLOCK), 1)
    band = (sj >= qi) & (sj <= qi + WINDOW)
    first_lo = jnp.where(t == 0, Q_BLOCK, 0)
    neg_inf = -jnp.inf
    reset = (_iota((SUBLANES, C_WIDTH), 0) + t * tb) == 0

    ri = _iota((CHUNK, B_HEADS * CHUNK), 0)
    cj = _iota((CHUNK, B_HEADS * CHUNK), 1) % CHUNK
    strict_b = jnp.where(ri > cj, 1.0, 0.0).astype(BF16)
    incl_b = jnp.where(ri >= cj, 1.0, 0.0).astype(BF16)
    eye_f = jnp.where(ri == cj, 1.0, 0.0)
    bd_heads = (_iota((B_WIDTH, B_WIDTH), 0) // HEAD_DIM) == (_iota((B_WIDTH, B_WIDTH), 1) // HEAD_DIM)

    def variants(z):
        zr = pltpu.roll(z, HEAD_DIM, 1)
        a_ = _bf(jnp.where(lo_kv, z, 0.0))
        b_ = _bf(jnp.where(lo_kv, 0.0, z))
        c_ = _bf(jnp.where(lo_kv, zr, 0.0))
        d_ = _bf(jnp.where(lo_kv, 0.0, zr))
        return ((a_, d_), (c_, b_))

    def win_stack(var, c, i):
        g, r0 = c // 2, i * Q_BLOCK
        return jnp.concatenate([var[g][0][r0:r0 + 2 * Q_BLOCK], var[g][1][r0:r0 + 2 * Q_BLOCK]], axis=0)

    def stk(z):
        zb = _bf(z)
        return jnp.concatenate([zb] * B_HEADS, axis=0) * ones_bd

    def side(q, out):
        kvprev, _, xcprev, hprev, _, proj, mix, _ = seq_scratch[q]

        k_rot = _rope128(proj[:, OFF_K:OFF_K + KV_WIDTH], cos, sin)
        v_att = proj[:, OFF_V:OFF_V + KV_WIDTH]
        kvar = variants(jnp.concatenate([kvprev[:, 0:KV_WIDTH], k_rot], axis=0))
        vvar = variants(jnp.concatenate([kvprev[:, KV_WIDTH:2 * KV_WIDTH], v_att], axis=0))
        out["k_tail"] = k_rot[tb - WINDOW:]
        out["v_tail"] = v_att[tb - WINDOW:]
        yield
        qcols = [_bf(_rope128(proj[:, OFF_Q + LANES * c:OFF_Q + LANES * (c + 1)], cos, sin))
                 for c in range(A_HEADS // 2)]
        scores = [_dot_nt(qcols[c][i * Q_BLOCK:(i + 1) * Q_BLOCK], win_stack(kvar, c, i))
                  for c, i in pairs]
        yield

        xc = proj[:, OFF_XC:OFF_XC + C_WIDTH]
        u = vec[V_CB:V_CB + 1, :] + xc * vec[V_CW3:V_CW3 + 1, :]
        for j in range(1, CONV_W):
            u = u + _shift_rows(xc, j, xcprev[...]) * vec[V_CW3 - j:V_CW3 - j + 1, :]
        a_l, mult, gi = _lru_gates(u, vec, wgate_ref[...])
        a_l = jnp.concatenate([jnp.where(reset, 0.0, a_l[0:SUBLANES]), a_l[SUBLANES:]], axis=0)
        mult = jnp.concatenate([jnp.where(reset, 1.0, mult[0:SUBLANES]), mult[SUBLANES:]], axis=0)
        b_l = mult * gi * u
        out["xc_tail"] = xc[tb - SUBLANES:, :]
        yield

        probs, norms = [], []
        for (c, i), s in zip(pairs, scores):
            mask = band & (sj >= first_lo) if i == 0 else band
            ps, invs = [], []
            for hh in range(2):
                sink = sinks_ref[layer, 2 * c + hh] * LOG2_E
                sh = jnp.where(mask, s[:, 2 * Q_BLOCK * hh:2 * Q_BLOCK * (hh + 1)], neg_inf)
                m = jnp.maximum(jnp.max(sh, axis=-1, keepdims=True), sink)
                p = jnp.exp2(sh - m)
                invs.append(1.0 / (jnp.sum(p, axis=-1, keepdims=True) + jnp.exp2(sink - m)))
                ps.append(_bf(p))
            probs.append(jnp.concatenate(ps, axis=1))
            norms.append(jnp.where(lo_q, invs[0], invs[1]))
            if i == tb // Q_BLOCK - 1:
                yield
        for (c, i), p, inv in zip(pairs, probs, norms):
            r0 = i * Q_BLOCK
            o = _dot(p, win_stack(vvar, c, i)) * inv
            ga = proj[r0:r0 + Q_BLOCK, OFF_GA + LANES * c:OFF_GA + LANES * (c + 1)]
            mix[r0:r0 + Q_BLOCK, LANES * c:LANES * (c + 1)] = o * _silu(ga)
        yield
        h_l = _affine_scan(a_l, b_l, hprev[SUBLANES - 1:SUBLANES, :])
        gc = proj[:, OFF_GC:OFF_GC + C_WIDTH]
        mix[:, A_WIDTH + B_WIDTH:] = h_l * _silu(gc)
        out["h_tail"] = h_l[tb - SUBLANES:, :]

    def sequence(q, out):
        kvprev, pprev, xcprev, hprev, gstate, proj, mix, ywkv = seq_scratch[q]

        x = x_ref[q]
        brow = pl.ds(pl.program_id(0) * n_seq + q, 1)
        shift_m = mod_ref[0, brow, :]
        scale_m = mod_ref[1, brow, :]
        gate_m = mod_ref[2, brow, :]
        h = _bf(_rms_scale(x) * (npre_ref[...] * (1.0 + scale_m)) + shift_m)
        for src, dst, n in (GRP_PB, GRP_QKV, GRP_XC, GRP_GA, GRP_GB, GRP_GC):
            for off in range(0, n, PROJ_TILE):
                _project(h, win_ref, proj, (src + off, dst + off, min(PROJ_TILE, n - off)))
                yield

        pb = proj[:, OFF_PB:OFF_PB + B_SHIFT_PAD]
        prev = _shift_rows(pb, 1, pprev[...])
        xs = pb + (prev - pb) * mu_ref[...]
        yield
        r, k, v, ld, a, kkn, kmod = _rwkv_pre(xs, vec, wlora_ref[...], ones_bd)
        kka = kkn * a
        yield

        lcum = _cumsum_chunks(ld, CHUNK)
        e_l = jnp.exp(lcum)
        e_nl = jnp.exp(-lcum)
        rt = r * e_l
        at = -kkn * jnp.exp(lcum - ld)
        ktil = kmod * e_nl
        btil = kka * e_nl
        gam = [e_l[s.stop - 1:s.stop, :] for s in sls]
        khat = [ktil[sls[c]] * gam[c] for c in rng]
        bhat = [btil[sls[c]] * gam[c] for c in rng]
        yield
        a_s = [stk(at[s]) for s in sls]
        b_s = [stk(btil[s]) for s in sls]
        k_s = [stk(ktil[s]) for s in sls]
        yield
        v_s = [stk(v[s]) for s in sls]
        bh_s = [stk(bhat[c]) for c in rng]
        sc = [_dot_nt(_bf(jnp.concatenate([at[sls[c]], rt[sls[c]]], axis=0)),
                      jnp.concatenate([b_s[c], k_s[c]], axis=0)) for c in rng]
        yield "side may start"

        pw = [_bf(sc[c][0:CHUNK, 0:B_WIDTH]) * strict_b for c in rng]
        lak = [_bf(sc[c][0:CHUNK, B_WIDTH:]) * strict_b for c in rng]
        mrb = [_bf(sc[c][CHUNK:, 0:B_WIDTH]) * incl_b for c in rng]
        mrk = [_bf(sc[c][CHUNK:, B_WIDTH:]) * incl_b for c in rng]
        tacc = [pw[c].astype(F32) + eye_f for c in rng]
        pw = [_dot_b(pw[c], stk(pw[c])) for c in rng]
        yield
        for _ in range(4):
            res = [_dot(jnp.concatenate([_bf(tacc[c]), pw[c]], axis=0), stk(pw[c])) for c in rng]
            tacc = [tacc[c] + res[c][0:CHUNK] for c in rng]
            pw = [_bf(res[c][CHUNK:]) for c in rng]
            yield
        tinv = [_bf(tacc[c] + _dot(_bf(tacc[c]), stk(pw[c]))) for c in rng]
        lmv = [_dot(jnp.concatenate([lak[c], mrk[c]], axis=0), v_s[c]) for c in rng]
        lv = [_bf(lmv[c][0:CHUNK]) for c in rng]
        yield
        w_c = [_dot_b(tinv[c], a_s[c]) for c in rng]
        z_c = [_dot(tinv[c], stk(lv[c])) for c in rng]
        yield
        w_s = [stk(w_c[c]) for c in rng]
        rp = [_bf(rt[sls[c]] + _dot(mrb[c], w_s[c])) for c in rng]
        y0 = [_dot(mrb[c], stk(z_c[c])) + lmv[c][CHUNK:] for c in rng]
        pp = [_bf(_dot_tn(w_s[c], bh_s[c])) for c in rng]
        qq = [jnp.where(bd_heads,
                        _dot_tn(_bf(jnp.concatenate([z_c[c], v[sls[c]]], axis=0)),
                                _bf(jnp.concatenate([bhat[c], khat[c]], axis=0))), 0.0) for c in rng]
        yield
        gmat = gstate[...]
        for c in rng:
            gb16 = _bf(gmat)
            ywkv[sls[c], :] = _dot_nt(rp[c], gb16) + y0[c]
            gmat = gam[c] * gmat + _dot(gb16, pp[c]) + qq[c]
        gstate[...] = gmat
        yield "side must be done"

        gb = proj[:, OFF_GB:OFF_GB + B_WIDTH]
        mix[:, A_WIDTH:A_WIDTH + B_WIDTH] = _rwkv_post(ywkv[...], r, kmod, v, gb, vec, ones_bd)

        yo = _dot(_bf(mix[...]), wout_ref[...])
        y_ref[q] = x + (gate_m * npost_ref[...]) * _rms_scale(yo)

        k_tail, v_tail = out["k_tail"], out["v_tail"]
        kvprev[:, 0:KV_WIDTH] = k_tail
        kvprev[:, KV_WIDTH:2 * KV_WIDTH] = v_tail
        pprev[...] = pb[tb - SUBLANES:, :]
        xcprev[...] = out["xc_tail"]
        hprev[...] = out["h_tail"]
        shift_ref[q] = pb[tb - SUBLANES:, :]
        wkv_ref[q] = gmat
        conv_ref[q] = out["xc_tail"]
        hlru_ref[q] = out["h_tail"]

        @pl.when(t == pl.num_programs(1) - 1)
        def _():
            kv_ref[q, 0] = k_tail.T
            kv_ref[q, 1] = v_tail.T

    for q in range(n_seq):
        shared = {}
        main, extra = sequence(q, shared), side(q, shared)
        while next(main) != "side may start":
            pass
        extra_done = False
        while True:
            if not extra_done:
                try:
                    next(extra)
                except StopIteration:
                    extra_done = True
            if next(main) == "side must be done":
                break
        for _ in extra:
            pass
        for _ in main:
            pass


def _weight_specs(layer_of, grid_rank):
    def lay(*shape):
        zeros = (0,) * len(shape)
        if grid_rank == 1:
            return pl.BlockSpec((None, *shape), lambda i: (layer_of(i), *zeros))
        return pl.BlockSpec((None, *shape), lambda a, b: (layer_of(a, b), *zeros))
    return [
        lay(1, D_MODEL),
        lay(1, D_MODEL),
        lay(P_IN, D_MODEL),
        lay(D_MODEL, D_MODEL),
    ], [
        lay(1, B_SHIFT_PAD),
        lay(VEC_ROWS, B_WIDTH),
        lay(LANES, 2 * B_WIDTH),
        lay(C_WIDTH, 2 * C_WIDTH),
    ]


def _prompt_layer(layer, x, mod, wts, cos_t, sin_t, mod_row0, tb):
    bsz, seq, _ = x.shape
    nt = seq // tb
    sps = SEQS_PER_STEP
    per_b3 = lambda b, t: (b, 0, 0)
    w_head, w_tail = _weight_specs(lambda b, t: layer, 2)
    return pl.pallas_call(
        functools.partial(_prompt_kernel, layer=layer),
        grid=(bsz // sps, nt),
        in_specs=[
            pl.BlockSpec(memory_space=pltpu.SMEM),
            pl.BlockSpec((sps, tb, D_MODEL), lambda b, t: (b, t, 0)),
            pl.BlockSpec((None, 3, SUBLANES, D_MODEL),
                         lambda b, t: (layer, 0, mod_row0 // SUBLANES, 0)),
            *w_head,
            pl.BlockSpec((tb, LANES), lambda b, t: (t, 0)),
            pl.BlockSpec((tb, LANES), lambda b, t: (t, 0)),
            *w_tail,
        ],
        out_specs=[
            pl.BlockSpec((sps, tb, D_MODEL), lambda b, t: (b, t, 0)),
            pl.BlockSpec((sps, 2, KV_WIDTH, WINDOW), lambda b, t: (b, 0, 0, 0)),
            pl.BlockSpec((sps, SUBLANES, B_SHIFT_PAD), per_b3),
            pl.BlockSpec((sps, B_WIDTH, B_WIDTH), per_b3),
            pl.BlockSpec((sps, SUBLANES, C_WIDTH), per_b3),
            pl.BlockSpec((sps, SUBLANES, C_WIDTH), per_b3),
        ],
        out_shape=[
            jax.ShapeDtypeStruct((bsz, seq, D_MODEL), F32),
            jax.ShapeDtypeStruct((bsz, 2, KV_WIDTH, WINDOW), F32),
            jax.ShapeDtypeStruct((bsz, SUBLANES, B_SHIFT_PAD), F32),
            jax.ShapeDtypeStruct((bsz, B_WIDTH, B_WIDTH), F32),
            jax.ShapeDtypeStruct((bsz, SUBLANES, C_WIDTH), F32),
            jax.ShapeDtypeStruct((bsz, SUBLANES, C_WIDTH), F32),
        ],
        scratch_shapes=[
            pltpu.VMEM((WINDOW, 2 * KV_WIDTH), F32),
            pltpu.VMEM((SUBLANES, B_SHIFT_PAD), F32),
            pltpu.VMEM((SUBLANES, C_WIDTH), F32),
            pltpu.VMEM((SUBLANES, C_WIDTH), F32),
            pltpu.VMEM((B_WIDTH, B_WIDTH), F32),
            pltpu.VMEM((tb, P_PAD), F32),
            pltpu.VMEM((tb, D_MODEL), F32),
            pltpu.VMEM((tb, B_WIDTH), F32),
        ] * sps,
        compiler_params=pltpu.CompilerParams(
            dimension_semantics=("arbitrary", "arbitrary"), vmem_limit_bytes=VMEM_LIMIT_BYTES),
        name="prompt_layer",
    )(wts["sinks"], x, mod, wts["norm_pre"], wts["norm_post"], wts["w_in"], wts["w_out"], cos_t, sin_t,
      wts["mu"], wts["vec"], wts["wlora"], wts["wgate"])


def _sample_kernel(sinks_ref, x_ref, mod_ref, npre_ref, npost_ref, win_ref, wout_ref,
                   cos_ref, sin_ref, mu_ref, vec_ref, wlora_ref, wgate_ref,
                   kt_ref, vt_ref, sprev_ref, wkv_ref, conv_ref, hl_ref,
                   y_ref, knew_ref, vnew_ref, shift_ref, wkvo_ref, convo_ref, hlo_ref,
                   ys_s, proj_s, mix_s, q_s, o_s, knew_s, rw_s, rkv_s, yt_s):
    layer = pl.program_id(0)
    j = pl.program_id(1)
    nb = x_ref.shape[0]
    bb = nb // B_HEADS
    n_slot = kt_ref.shape[1]
    vec = vec_ref[...]
    cos = cos_ref[...]
    sin = sin_ref[...]

    @pl.when(jnp.logical_and(layer == 0, j == 0))
    def _():
        ys_s[...] = x_ref[...]

    @pl.when(j == 0)
    def _():
        ones_bd = _head_ones(B_WIDTH)
        x = ys_s[...]
        h = _bf(_rms_scale(x) * npre_ref[...] * (1.0 + mod_ref[1]) + mod_ref[0])
        for grp in (GRP_QKV, GRP_GA, GRP_PB, GRP_GB, GRP_XC, GRP_GC):
            _project(h, win_ref, proj_s, grp)
        k_new = _rope128(proj_s[:, OFF_K:OFF_K + KV_WIDTH], cos, sin)
        knew_s[...] = k_new
        knew_ref[...] = k_new.T
        vnew_ref[...] = proj_s[:, OFF_V:OFF_V + KV_WIDTH].T
        pb = proj_s[:, OFF_PB:OFF_PB + B_SHIFT_PAD]
        pb_t = pb.T
        shift_ref[...] = pb_t[0:B_SHIFT, :]
        prev = jnp.concatenate([sprev_ref[...], jnp.zeros((B_SHIFT_PAD - B_SHIFT, nb), F32)], axis=0).T
        xs = pb + (prev - pb) * mu_ref[...]
        r, k, v, ld, a, kkn, kmod = _rwkv_pre(xs, vec, wlora_ref[...], ones_bd)
        rw_s[T_DECAY] = jnp.exp(ld).T
        rw_s[T_NKK] = (-kkn).T
        rw_s[T_KKA] = (kkn * a).T
        rw_s[T_KMOD] = kmod.T
        rw_s[T_R] = r.T
        rw_s[T_V] = v.T
        rkv_s[0] = r
        rkv_s[1] = kmod
        rkv_s[2] = v

    rs = pl.ds(pl.multiple_of(j * bb, SUBLANES), bb)
    lane = _iota((bb, LANES), 1)
    lo = lane < HEAD_DIM
    for c in range(A_HEADS // 2):
        g = c // 2
        in_g = lo if g == 0 else jnp.logical_not(lo)
        qcol = _rope128(proj_s[rs, OFF_Q + LANES * c:OFF_Q + LANES * (c + 1)], cos, sin)
        qrol = pltpu.roll(qcol, HEAD_DIM, 1)
        for hh in range(2):
            hd = 2 * c + hh
            q_s[hd * bb:(hd + 1) * bb, :] = jnp.where(in_g, qcol if hh == g else qrol, 0.0)
    knew_blk = knew_s[rs, :]
    vnew_blk = proj_s[rs, OFF_V:OFF_V + KV_WIDTH]
    hrow = _iota((A_HEADS, 1), 0)
    sink_col = jnp.zeros((A_HEADS, 1), F32)
    for hd in range(A_HEADS):
        sink_col = jnp.where(hrow == hd, sinks_ref[layer, hd] * LOG2_E, sink_col)

    qbs = [q_s[pl.ds(b, A_HEADS, stride=bb), :] for b in range(bb)]
    s_c = [_dot(_bf(qbs[b]), _bf(kt_ref[b * KV_WIDTH:(b + 1) * KV_WIDTH, :])) for b in range(bb)]
    s_n = [jnp.sum(qbs[b] * knew_blk[b:b + 1, :], axis=-1, keepdims=True) for b in range(bb)]
    p_c, p_n = [], []
    for b in range(bb):
        m = jnp.maximum(jnp.maximum(jnp.max(s_c[b], axis=-1, keepdims=True), s_n[b]), sink_col)
        e_c = jnp.exp2(s_c[b] - m)
        e_n = jnp.exp2(s_n[b] - m)
        inv = 1.0 / (jnp.sum(e_c, axis=-1, keepdims=True) + e_n + jnp.exp2(sink_col - m))
        p_c.append(_bf(e_c * inv))
        p_n.append(e_n * inv)
    for b in range(bb):
        o = _dot_nt(p_c[b], _bf(vt_ref[b * KV_WIDTH:(b + 1) * KV_WIDTH, :]))
        o_s[pl.ds(b, A_HEADS, stride=bb), :] = o + p_n[b] * vnew_blk[b:b + 1, :]
    for c in range(A_HEADS // 2):
        g = c // 2
        halves = []
        for hh in range(2):
            oh = o_s[(2 * c + hh) * bb:(2 * c + hh + 1) * bb, :]
            halves.append(oh if hh == g else pltpu.roll(oh, HEAD_DIM, 1))
        ga = proj_s[rs, OFF_GA + LANES * c:OFF_GA + LANES * (c + 1)]
        mix_s[rs, LANES * c:LANES * (c + 1)] = jnp.where(lo, halves[0], halves[1]) * _silu(ga)

    hrows = pl.ds(pl.multiple_of(j * HEAD_DIM, HEAD_DIM), HEAD_DIM)
    w_t = rw_s[T_DECAY, hrows, :]
    nkk_t = rw_s[T_NKK, hrows, :]
    kka_t = rw_s[T_KKA, hrows, :]
    k_t = rw_s[T_KMOD, hrows, :]
    r_t = rw_s[T_R, hrows, :]

    def value_row(vi, carry):
        krows = pl.ds(pl.multiple_of(vi * HEAD_DIM, HEAD_DIM), HEAD_DIM)
        st = wkv_ref[krows, :]
        sa = jnp.sum(st * nkk_t, axis=0, keepdims=True)
        st_new = st * w_t + sa * kka_t + rw_s[T_V, pl.ds(j * HEAD_DIM + vi, 1), :] * k_t
        wkvo_ref[krows, :] = st_new
        yt_s[pl.ds(j * HEAD_DIM + vi, 1), :] = jnp.sum(st_new * r_t, axis=0, keepdims=True)
        return carry

    lax.fori_loop(0, HEAD_DIM, value_row, 0, unroll=4)

    @pl.when(j == pl.num_programs(1) - 1)
    def _():
        ones_bd = _head_ones(B_WIDTH)
        x = ys_s[...]
        gb = proj_s[:, OFF_GB:OFF_GB + B_WIDTH]
        mix_s[:, A_WIDTH:A_WIDTH + B_WIDTH] = _rwkv_post(yt_s[...].T, rkv_s[0], rkv_s[1], rkv_s[2], gb, vec, ones_bd)
        xc = proj_s[:, OFF_XC:OFF_XC + C_WIDTH]
        gc = proj_s[:, OFF_GC:OFF_GC + C_WIDTH]
        u = vec[V_CB:V_CB + 1, :] + xc * vec[V_CW3:V_CW3 + 1, :]
        for i in range(CONV_W - 1):
            u = u + conv_ref[i] * vec[V_CW0 + i:V_CW0 + i + 1, :]
        a_l, mult, gi = _lru_gates(u, vec, wgate_ref[...])
        h_l = a_l * hl_ref[...] + mult * gi * u
        hlo_ref[...] = h_l
        convo_ref[0] = conv_ref[1]
        convo_ref[1] = conv_ref[2]
        convo_ref[2] = xc
        mix_s[:, A_WIDTH + B_WIDTH:] = h_l * _silu(gc)
        yo = _dot(_bf(mix_s[...]), wout_ref[...])
        y = x + mod_ref[2] * (_rms_scale(yo) * npost_ref[...])
        ys_s[...] = y
        y_ref[...] = y


def _sample_layers(x, mod, wts, cos_r, sin_r, kt, vt, sprev_t, wkv_t, conv_t, hl):
    nb = x.shape[0]
    bb = nb // B_HEADS
    n_slot = kt.shape[2]
    n_hv = HEAD_DIM * HEAD_DIM
    w_head, w_tail = _weight_specs(lambda l, j: l, 2)
    const2 = lambda l, j: (0, 0)
    lay3 = lambda l, j: (l, 0, 0)
    return pl.pallas_call(
        _sample_kernel,
        grid=(DEPTH, B_HEADS),
        in_specs=[
            pl.BlockSpec(memory_space=pltpu.SMEM),
            pl.BlockSpec((nb, D_MODEL), const2),
            pl.BlockSpec((None, 3, nb, D_MODEL), lambda l, j: (l, 0, 0, 0)),
            *w_head,
            pl.BlockSpec((1, LANES), const2),
            pl.BlockSpec((1, LANES), const2),
            *w_tail,
            pl.BlockSpec((None, bb * KV_WIDTH, n_slot), lambda l, j: (l, j, 0)),
            pl.BlockSpec((None, bb * KV_WIDTH, n_slot), lambda l, j: (l, j, 0)),
            pl.BlockSpec((None, B_SHIFT, nb), lay3),
            pl.BlockSpec((None, n_hv, nb), lambda l, j: (l, j, 0)),
            pl.BlockSpec((None, CONV_W - 1, nb, C_WIDTH), lambda l, j: (l, 0, 0, 0)),
            pl.BlockSpec((None, nb, C_WIDTH), lay3),
        ],
        out_specs=[
            pl.BlockSpec((nb, D_MODEL), const2),
            pl.BlockSpec((None, KV_WIDTH, nb), lay3),
            pl.BlockSpec((None, KV_WIDTH, nb), lay3),
            pl.BlockSpec((None, B_SHIFT, nb), lay3),
            pl.BlockSpec((None, n_hv, nb), lambda l, j: (l, j, 0)),
            pl.BlockSpec((None, CONV_W - 1, nb, C_WIDTH), lambda l, j: (l, 0, 0, 0)),
            pl.BlockSpec((None, nb, C_WIDTH), lay3),
        ],
        out_shape=[
            jax.ShapeDtypeStruct((nb, D_MODEL), F32),
            jax.ShapeDtypeStruct((DEPTH, KV_WIDTH, nb), F32),
            jax.ShapeDtypeStruct((DEPTH, KV_WIDTH, nb), F32),
            jax.ShapeDtypeStruct((DEPTH, B_SHIFT, nb), F32),
            jax.ShapeDtypeStruct((DEPTH, B_HEADS * n_hv, nb), F32),
            jax.ShapeDtypeStruct((DEPTH, CONV_W - 1, nb, C_WIDTH), F32),
            jax.ShapeDtypeStruct((DEPTH, nb, C_WIDTH), F32),
        ],
        scratch_shapes=[
            pltpu.VMEM((nb, D_MODEL), F32),
            pltpu.VMEM((nb, P_PAD), F32),
            pltpu.VMEM((nb, D_MODEL), F32),
            pltpu.VMEM((A_HEADS * bb, LANES), F32),
            pltpu.VMEM((A_HEADS * bb, LANES), F32),
            pltpu.VMEM((nb, KV_WIDTH), F32),
            pltpu.VMEM((6, B_WIDTH, nb), F32),
            pltpu.VMEM((3, nb, B_WIDTH), F32),
            pltpu.VMEM((B_WIDTH, nb), F32),
        ],
        compiler_params=pltpu.CompilerParams(
            dimension_semantics=("arbitrary", "arbitrary"), vmem_limit_bytes=VMEM_LIMIT_BYTES),
        name="sample_layers",
    )(wts["sinks"], x, mod, wts["norm_pre"], wts["norm_post"], wts["w_in"], wts["w_out"], cos_r, sin_r,
      wts["mu"], wts["vec"], wts["wlora"], wts["wgate"], kt, vt, sprev_t, wkv_t, conv_t, hl)


def _rope_lanes():
    half = HEAD_DIM // 2
    inv_freq = ROPE_THETA ** (-jnp.arange(half, dtype=F32) / half)
    freq = jnp.tile(inv_freq, LANES // half)
    sign = jnp.tile(jnp.concatenate([-jnp.ones((half,), F32), jnp.ones((half,), F32)]), LANES // HEAD_DIM)
    return freq, sign


def _rope_row(pos):
    freq, sign = _rope_lanes()
    ang = jnp.float32(pos) * freq[None, :]
    return jnp.cos(ang), jnp.sin(ang) * sign[None, :]


def _rope_tables(seq):
    freq, sign = _rope_lanes()
    hi = (jnp.arange(seq // Q_BLOCK, dtype=F32) * Q_BLOCK)[:, None] * freq[None, :]
    lo = jnp.arange(Q_BLOCK, dtype=F32)[:, None] * freq[None, :]
    ch, sh = jnp.cos(hi)[:, None, :], jnp.sin(hi)[:, None, :]
    cl, sl = jnp.cos(lo)[None, :, :], jnp.sin(lo)[None, :, :]
    cos_t = (ch * cl - sh * sl).reshape(seq, LANES)
    sin_t = ((sh * cl + ch * sl) * sign).reshape(seq, LANES)
    return cos_t, sin_t


def _block_diag(w):
    dl, n, d, e = w.shape
    eye = jnp.eye(n, dtype=w.dtype)
    return (eye[None, :, None, :, None] * w[:, :, :, None, :]).reshape(dl, n * d, n * e)


def _prep_weights(p):
    dl = p["w_in"].shape[0]
    w_in_t = jnp.swapaxes(p["w_in"], 1, 2)
    q_scale = jnp.where(jnp.arange(w_in_t.shape[1]) < A_WIDTH, Q_SCALE, 1.0).astype(F32)
    w_in_t = (w_in_t * q_scale[None, :, None]).astype(BF16)
    z32 = jnp.zeros((dl, LORA, B_WIDTH), F32)
    wlora = jnp.concatenate([
        jnp.concatenate([p["rwkv_w_up"], z32], axis=2),
        jnp.concatenate([z32, p["rwkv_a_up"]], axis=2),
        jnp.zeros((dl, LANES - 2 * LORA, 2 * B_WIDTH), F32)], axis=1).astype(BF16)
    wgate = jnp.concatenate([_block_diag(p["lru_gate_a_w"]), _block_diag(p["lru_gate_x_w"])], axis=2).astype(BF16)
    cw = p["lru_conv_w"]
    rows = [p["rwkv_w0"], p["rwkv_a0"], p["rwkv_k_k"], p["rwkv_k_a"], p["rwkv_lnx_w"], p["rwkv_lnx_b"],
            p["rwkv_r_k"].reshape(dl, B_WIDTH), p["lru_conv_b"], p["lru_gate_a_b"], p["lru_gate_x_b"],
            p["lru_lambda"], cw[:, 0], cw[:, 1], cw[:, 2], cw[:, 3], jnp.zeros((dl, B_WIDTH), F32)]
    return {
        "sinks": p["attn_sinks"],
        "norm_pre": p["norm_pre"].reshape(dl, 1, D_MODEL),
        "norm_post": p["norm_post"].reshape(dl, 1, D_MODEL),
        "w_in": w_in_t,
        "w_out": p["w_out"].astype(BF16),
        "mu": jnp.concatenate([p["rwkv_mu"], jnp.zeros((dl, B_SHIFT_PAD - B_SHIFT), F32)],
                              axis=1).reshape(dl, 1, B_SHIFT_PAD),
        "vec": jnp.stack(rows, axis=1),
        "wlora": wlora,
        "wgate": wgate,
    }


def _forward(x_prompt, x_sample, c_prompt, c_sample, cache_swa_k, cache_swa_v, state_rwkv_shift,
             state_rwkv_wkv, state_lru_conv, state_lru_h, p, tb):
    bp, seq, _ = x_prompt.shape
    nb = x_sample.shape[0]
    n_slot = cache_swa_k.shape[2]
    assert nb % SUBLANES == 0 and nb % B_HEADS == 0 and bp <= SUBLANES and seq % tb == 0

    pad_rows = (-(nb + bp)) % SUBLANES
    c_all = jnp.concatenate([c_sample, c_prompt, jnp.zeros((pad_rows, D_MODEL), F32)], axis=0)
    mod = _mod_call(c_all, p["w_mod"], p["b_mod"])
    wts = _prep_weights(p)

    cos_p, sin_p = _rope_tables(seq)
    cos_s, sin_s = _rope_row(PAST_LEN)

    kt = jnp.transpose(cache_swa_k, (0, 1, 3, 4, 2)).reshape(DEPTH, nb * KV_WIDTH, n_slot)
    vt = jnp.transpose(cache_swa_v, (0, 1, 3, 4, 2)).reshape(DEPTH, nb * KV_WIDTH, n_slot)
    sprev_t = jnp.swapaxes(state_rwkv_shift, 1, 2)
    wkv_t = jnp.transpose(state_rwkv_wkv, (0, 2, 3, 4, 1)).reshape(DEPTH, B_HEADS * HEAD_DIM * HEAD_DIM, nb)
    conv_t = jnp.swapaxes(state_lru_conv, 1, 2)
    ys, knew_t, vnew_t, shift_t, wkvo_t, convo_t, hlo = _sample_layers(
        x_sample.reshape(nb, D_MODEL), mod, wts, cos_s, sin_s, kt, vt, sprev_t, wkv_t, conv_t, state_lru_h)
    outs_s = (
        jnp.transpose(knew_t.reshape(DEPTH, 1, A_KV_HEADS, HEAD_DIM, nb), (0, 4, 1, 2, 3)),
        jnp.transpose(vnew_t.reshape(DEPTH, 1, A_KV_HEADS, HEAD_DIM, nb), (0, 4, 1, 2, 3)),
        jnp.swapaxes(shift_t, 1, 2),
        jnp.transpose(wkvo_t.reshape(DEPTH, B_HEADS, HEAD_DIM, HEAD_DIM, nb), (0, 4, 1, 2, 3)),
        jnp.swapaxes(convo_t, 1, 2),
        hlo,
    )

    yp = x_prompt
    outs_p = []
    for l in range(DEPTH):
        yp, kv_p, sh_p, wkv_p, conv_p, h_p = _prompt_layer(l, yp, mod, wts, cos_p, sin_p, nb, tb)
        kv_p = jnp.transpose(kv_p.reshape(bp, 2, A_KV_HEADS, HEAD_DIM, WINDOW), (1, 0, 4, 2, 3))
        outs_p.append((
            kv_p[0], kv_p[1],
            sh_p[:, SUBLANES - 1, :B_SHIFT],
            jnp.stack([wkv_p[:, HEAD_DIM * hd:HEAD_DIM * (hd + 1), HEAD_DIM * hd:HEAD_DIM * (hd + 1)]
                       for hd in range(B_HEADS)], axis=1),
            conv_p[:, SUBLANES - (CONV_W - 1):, :],
            h_p[:, SUBLANES - 1, :],
        ))
    sp = [jnp.stack(z) for z in zip(*outs_p)]
    return (yp, ys.reshape(nb, 1, D_MODEL), *sp, *outs_s)


def kernel(x_prompt, x_sample, c_prompt, c_sample, cache_swa_k, cache_swa_v, state_rwkv_shift, state_rwkv_wkv, state_lru_conv, state_lru_h, norm_pre, norm_post, w_mod, b_mod, w_in, w_out, attn_sinks, rwkv_mu, rwkv_w0, rwkv_w_up, rwkv_a0, rwkv_a_up, rwkv_k_k, rwkv_k_a, rwkv_r_k, rwkv_lnx_w, rwkv_lnx_b, lru_conv_w, lru_conv_b, lru_gate_a_w, lru_gate_a_b, lru_gate_x_w, lru_gate_x_b, lru_lambda):
    p = dict(norm_pre=norm_pre, norm_post=norm_post, w_mod=w_mod, b_mod=b_mod, w_in=w_in, w_out=w_out,
             attn_sinks=attn_sinks, rwkv_mu=rwkv_mu, rwkv_w0=rwkv_w0, rwkv_w_up=rwkv_w_up, rwkv_a0=rwkv_a0,
             rwkv_a_up=rwkv_a_up, rwkv_k_k=rwkv_k_k, rwkv_k_a=rwkv_k_a, rwkv_r_k=rwkv_r_k,
             rwkv_lnx_w=rwkv_lnx_w, rwkv_lnx_b=rwkv_lnx_b, lru_conv_w=lru_conv_w, lru_conv_b=lru_conv_b,
             lru_gate_a_w=lru_gate_a_w, lru_gate_a_b=lru_gate_a_b, lru_gate_x_w=lru_gate_x_w,
             lru_gate_x_b=lru_gate_x_b, lru_lambda=lru_lambda)
    return _forward(x_prompt, x_sample, c_prompt, c_sample, cache_swa_k, cache_swa_v, state_rwkv_shift,
                    state_rwkv_wkv, state_lru_conv, state_lru_h, p, TIME_BLOCK)
```

```python
import functools

import numpy as np
import jax
import jax.numpy as jnp
from jax import lax
from jax.experimental import pallas as pl
from jax.experimental.pallas import tpu as pltpu

F32 = jnp.float32
BF16 = jnp.bfloat16

D_MODEL = 1024
DEPTH = 2
A_HEADS = 8
A_KV_HEADS = 2
HEAD_DIM = 64
A_WIDTH = A_HEADS * HEAD_DIM
KV_WIDTH = A_KV_HEADS * HEAD_DIM
WINDOW = 128
Q_BLOCK = 128
ROPE_THETA = 10000.0
B_WIDTH = 256
B_HEADS = 4
LORA = 32
B_SHIFT = 3 * B_WIDTH + 2 * LORA
B_SHIFT_PAD = 3 * B_WIDTH + 128
LNX_EPS = 1e-5 * 8 ** 2
C_WIDTH = 256
CONV_W = 4
LRU_C = 8.0
EPS = 1e-6
PAST_LEN = 8192
EXP_NEG_HALF = float(np.exp(-0.5))
LOG2_E = float(np.log2(np.e))
Q_SCALE = HEAD_DIM ** -0.5 * LOG2_E

LANES = 128
SUBLANES = 8
VMEM_LIMIT_BYTES = 56 * 1024 * 1024

OFF_Q = 0
OFF_K = OFF_Q + A_WIDTH
OFF_V = OFF_K + KV_WIDTH
OFF_GA = OFF_V + KV_WIDTH
OFF_PB = OFF_GA + A_WIDTH
OFF_GB = OFF_PB + B_SHIFT_PAD
OFF_XC = OFF_GB + B_WIDTH
OFF_GC = OFF_XC + C_WIDTH
P_PAD = OFF_GC + C_WIDTH
P_IN = P_PAD - (B_SHIFT_PAD - B_SHIFT)

CHUNK = 64
TIME_BLOCK = 512
SEQS_PER_STEP = 1
PROJ_TILE = 256

(V_W0, V_A0, V_KK, V_KA, V_LNW, V_LNB, V_RK, V_CB, V_GAB, V_GXB, V_LAM,
 V_CW0, V_CW1, V_CW2, V_CW3) = range(15)
VEC_ROWS = 16

(T_DECAY, T_NKK, T_KKA, T_KMOD, T_R, T_V) = range(6)

_NT = (((1,), (1,)), ((), ()))
_TN = (((0,), (0,)), ((), ()))


def _bf(x):
    return x.astype(BF16)


def _dot(a, b):
    return jnp.dot(a, b, preferred_element_type=F32)


def _dot_b(a, b):
    return jnp.dot(a, b, preferred_element_type=F32).astype(BF16)


def _dot_nt(a, b):
    return lax.dot_general(a, b, _NT, preferred_element_type=F32)


def _dot_tn(a, b):
    return lax.dot_general(a, b, _TN, preferred_element_type=F32)


def _iota(shape, dim):
    return lax.broadcasted_iota(jnp.int32, shape, dim)


def _sigmoid(x):
    return jax.nn.sigmoid(x)


def _silu(x):
    return x * _sigmoid(x)


def _softplus(x):
    return jnp.maximum(x, 0.0) + jnp.log1p(jnp.exp(-jnp.abs(x)))


def _head_ones(n):
    r = _iota((n, n), 0) // HEAD_DIM
    c = _iota((n, n), 1) // HEAD_DIM
    return jnp.where(r == c, 1.0, 0.0).astype(BF16)


def _head_sum(x, ones_bd):
    return _dot(_bf(x), ones_bd)


def _rms_scale(x):
    ms = jnp.mean(x * x, axis=-1, keepdims=True)
    return x * lax.rsqrt(ms + EPS)


def _affine_scan(a, b, h0):
    rows, n = a.shape
    groups = rows // SUBLANES
    a = a.reshape(groups, SUBLANES, n)
    b = b.reshape(groups, SUBLANES, n)
    sub = _iota((1, SUBLANES, n), 1)
    d = 1
    while d < SUBLANES:
        keep = sub >= d
        a_sh = jnp.where(keep, pltpu.roll(a, d, 1), 1.0)
        b_sh = jnp.where(keep, pltpu.roll(b, d, 1), 0.0)
        b = b + a * b_sh
        a = a * a_sh
        d *= 2
    outs = []
    carry = h0
    for g in range(groups):
        hg = b[g] + a[g] * carry
        outs.append(hg)
        carry = hg[SUBLANES - 1:SUBLANES, :]
    return jnp.concatenate(outs, axis=0)


def _shift_rows(x, j, tail):
    rows, n = x.shape
    groups = rows // SUBLANES
    x3 = jnp.concatenate([tail, x], axis=0).reshape(groups + 1, SUBLANES, n)
    r3 = pltpu.roll(x3, j, 1)
    sub = _iota((1, SUBLANES, n), 1)
    return jnp.where(sub >= j, r3[1:], r3[:-1]).reshape(rows, n)


def _cumsum_chunks(x, chunk):
    rows, n = x.shape
    groups = rows // SUBLANES
    x = x.reshape(groups, SUBLANES, n)
    sub = _iota((1, SUBLANES, n), 1)
    d = 1
    while d < SUBLANES:
        x = x + jnp.where(sub >= d, pltpu.roll(x, d, 1), 0.0)
        d *= 2
    outs = []
    for g in range(groups):
        xg = x[g]
        if (g * SUBLANES) % chunk != 0:
            xg = xg + outs[-1][SUBLANES - 1:SUBLANES, :]
        outs.append(xg)
    return jnp.concatenate(outs, axis=0)


def _rope128(z, cos, sin_signed):
    lane = _iota(z.shape, 1)
    first = (lane & 32) == 0
    sw = jnp.where(first, pltpu.roll(z, 96, 1), pltpu.roll(z, 32, 1))
    return z * cos + sw * sin_signed


def _rwkv_pre(xs, vec, wlora, ones_bd):
    r = xs[:, 0:B_WIDTH]
    k = xs[:, B_WIDTH:2 * B_WIDTH]
    v = xs[:, 2 * B_WIDTH:3 * B_WIDTH]
    lor = xs[:, 3 * B_WIDTH:B_SHIFT_PAD]
    lane = _iota(lor.shape, 1)
    z = jnp.where(lane < LORA, jnp.tanh(lor), lor)
    wa = _dot(_bf(z), wlora)
    zw = vec[V_W0:V_W0 + 1, :] + wa[:, :B_WIDTH]
    ld = -EXP_NEG_HALF * _sigmoid(zw)
    a = _sigmoid(vec[V_A0:V_A0 + 1, :] + wa[:, B_WIDTH:])
    kk = k * vec[V_KK:V_KK + 1, :]
    kkn = kk * jnp.minimum(lax.rsqrt(_head_sum(kk * kk, ones_bd)), 1e12)
    kmod = k * (1.0 + (a - 1.0) * vec[V_KA:V_KA + 1, :])
    return r, k, v, ld, a, kkn, kmod


def _rwkv_post(y, r, kmod, v, gb, vec, ones_bd):
    inv = 1.0 / HEAD_DIM
    mu = _head_sum(y, ones_bd) * inv
    yc = y - mu
    var = _head_sum(yc * yc, ones_bd) * inv
    yn = yc * lax.rsqrt(var + LNX_EPS)
    yn = yn * vec[V_LNW:V_LNW + 1, :] + vec[V_LNB:V_LNB + 1, :]
    bonus = _head_sum(r * kmod * vec[V_RK:V_RK + 1, :], ones_bd) * v
    return (yn + bonus) * _silu(gb)


def _lru_gates(u, vec, wgate):
    gates = _dot(_bf(u), wgate)
    gr = _sigmoid(gates[:, :C_WIDTH] + vec[V_GAB:V_GAB + 1, :])
    gi = _sigmoid(gates[:, C_WIDTH:] + vec[V_GXB:V_GXB + 1, :])
    sp = _softplus(-vec[V_LAM:V_LAM + 1, :])
    log_a = -LRU_C * gr * sp
    a = jnp.exp(log_a)
    mult = jnp.sqrt(1.0 - a * a)
    return a, mult, gi


def _project(h, win_ref, proj_s, group):
    src, dst, n = group
    proj_s[:, dst:dst + n] = _dot_nt(h, win_ref[src:src + n, :])


_DELTA = B_SHIFT_PAD - B_SHIFT
GRP_PB = (OFF_PB, OFF_PB, B_SHIFT_PAD)
GRP_QKV = (OFF_Q, OFF_Q, OFF_GA - OFF_Q)
GRP_GA = (OFF_GA, OFF_GA, A_WIDTH)
GRP_GB = (OFF_GB - _DELTA, OFF_GB, B_WIDTH)
GRP_XC = (OFF_XC - _DELTA, OFF_XC, C_WIDTH)
GRP_GC = (OFF_GC - _DELTA, OFF_GC, C_WIDTH)


def _mod_kernel(c_ref, w_ref, b_ref, o_ref):
    o_ref[...] = _dot(_bf(_silu(c_ref[...])), _bf(w_ref[...])) + b_ref[...]


def _mod_call(c_all, w_mod, b_mod):
    rows = c_all.shape[0]
    return pl.pallas_call(
        _mod_kernel,
        grid=(DEPTH, 3),
        in_specs=[
            pl.BlockSpec((rows, D_MODEL), lambda l, j: (0, 0)),
            pl.BlockSpec((None, D_MODEL, D_MODEL), lambda l, j: (l, 0, j)),
            pl.BlockSpec((None, 1, D_MODEL), lambda l, j: (l, 0, j)),
        ],
        out_specs=pl.BlockSpec((None, None, rows, D_MODEL), lambda l, j: (l, j, 0, 0)),
        out_shape=jax.ShapeDtypeStruct((DEPTH, 3, rows, D_MODEL), F32),
        compiler_params=pltpu.CompilerParams(
            dimension_semantics=("arbitrary", "arbitrary"), vmem_limit_bytes=VMEM_LIMIT_BYTES),
        name="adaln_mod",
    )(c_all, w_mod, b_mod.reshape(DEPTH, 1, 3 * D_MODEL))


def _prompt_kernel(sinks_ref, x_ref, mod_ref, npre_ref, npost_ref, win_ref, wout_ref,
                   cos_ref, sin_ref, mu_ref, vec_ref, wlora_ref, wgate_ref,
                   y_ref, kv_ref, shift_ref, wkv_ref, conv_ref, hlru_ref,
                   *scratch, layer):
    n_seq = x_ref.shape[0]
    tb = x_ref.shape[1]
    t = pl.program_id(1)
    per_seq = len(scratch) // n_seq
    seq_scratch = [scratch[q * per_seq:(q + 1) * per_seq] for q in range(n_seq)]

    @pl.when(t == 0)
    def _():
        for kvprev, pprev, xcprev, hprev, gstate, _, _, _ in seq_scratch:
            kvprev[...] = jnp.zeros_like(kvprev)
            pprev[...] = jnp.zeros_like(pprev)
            xcprev[...] = jnp.zeros_like(xcprev)
            hprev[...] = jnp.zeros_like(hprev)
            gstate[...] = jnp.zeros_like(gstate)

    vec = vec_ref[...]
    ones_bd = _head_ones(B_WIDTH)
    cos = cos_ref[...]
    sin = sin_ref[...]
    nck = tb // CHUNK
    sls = [slice(c * CHUNK, (c + 1) * CHUNK) for c in range(nck)]
    rng = range(nck)
    pairs = [(c, i) for c in range(A_HEADS // 2) for i in range(tb // Q_BLOCK)]

    lo_kv = _iota((tb + WINDOW, KV_WIDTH), 1) < HEAD_DIM
    lo_q = _iota((Q_BLOCK, LANES), 1) < HEAD_DIM
    qi = _iota((Q_BLOCK, 2 * Q_BLOCK), 0)
    sj = _iota((Q_BLOCK, 2 * Q_BLOCK), 1)
    band = (sj >= qi) & (sj <= qi + WINDOW)
    first_lo = jnp.where(t == 0, Q_BLOCK, 0)
    neg_inf = -jnp.inf
    reset = (_iota((SUBLANES, C_WIDTH), 0) + t * tb) == 0

    ri = _iota((CHUNK, B_HEADS * CHUNK), 0)
    cj = _iota((CHUNK, B_HEADS * CHUNK), 1) % CHUNK
    strict_b = jnp.where(ri > cj, 1.0, 0.0).astype(BF16)
    incl_b = jnp.where(ri >= cj, 1.0, 0.0).astype(BF16)
    eye_f = jnp.where(ri == cj, 1.0, 0.0)
    bd_heads = (_iota((B_WIDTH, B_WIDTH), 0) // HEAD_DIM) == (_iota((B_WIDTH, B_WIDTH), 1) // HEAD_DIM)

    def variants(z):
        zr = pltpu.roll(z, HEAD_DIM, 1)
        a_ = _bf(jnp.where(lo_kv, z, 0.0))
        b_ = _bf(jnp.where(lo_kv, 0.0, z))
        c_ = _bf(jnp.where(lo_kv, zr, 0.0))
        d_ = _bf(jnp.where(lo_kv, 0.0, zr))
        return ((a_, d_), (c_, b_))

    def win_stack(var, c, i):
        g, r0 = c // 2, i * Q_BLOCK
        return jnp.concatenate([var[g][0][r0:r0 + 2 * Q_BLOCK], var[g][1][r0:r0 + 2 * Q_BLOCK]], axis=0)

    def stk(z):
        zb = _bf(z)
        return jnp.concatenate([zb] * B_HEADS, axis=0) * ones_bd

    def side(q, out):
        kvprev, _, xcprev, hprev, _, proj, mix, _ = seq_scratch[q]

        k_rot = _rope128(proj[:, OFF_K:OFF_K + KV_WIDTH], cos, sin)
        v_att = proj[:, OFF_V:OFF_V + KV_WIDTH]
        kvar = variants(jnp.concatenate([kvprev[:, 0:KV_WIDTH], k_rot], axis=0))
        vvar = variants(jnp.concatenate([kvprev[:, KV_WIDTH:2 * KV_WIDTH], v_att], axis=0))
        out["k_tail"] = k_rot[tb - WINDOW:]
        out["v_tail"] = v_att[tb - WINDOW:]
        yield
        qcols = [_bf(_rope128(proj[:, OFF_Q + LANES * c:OFF_Q + LANES * (c + 1)], cos, sin))
                 for c in range(A_HEADS // 2)]
        grouped = {(g, i): _dot_nt(jnp.concatenate([qcols[2 * g][i * Q_BLOCK:(i + 1) * Q_BLOCK],
                                                    qcols[2 * g + 1][i * Q_BLOCK:(i + 1) * Q_BLOCK]], axis=0),
                                   win_stack(kvar, 2 * g, i))
                   for g in range(A_KV_HEADS) for i in range(tb // Q_BLOCK)}
        scores = [grouped[(c // 2, i)][(c % 2) * Q_BLOCK:(c % 2 + 1) * Q_BLOCK] for c, i in pairs]
        yield

        xc = proj[:, OFF_XC:OFF_XC + C_WIDTH]
        u = vec[V_CB:V_CB + 1, :] + xc * vec[V_CW3:V_CW3 + 1, :]
        for j in range(1, CONV_W):
            u = u + _shift_rows(xc, j, xcprev[...]) * vec[V_CW3 - j:V_CW3 - j + 1, :]
        a_l, mult, gi = _lru_gates(u, vec, wgate_ref[...])
        a_l = jnp.concatenate([jnp.where(reset, 0.0, a_l[0:SUBLANES]), a_l[SUBLANES:]], axis=0)
        mult = jnp.concatenate([jnp.where(reset, 1.0, mult[0:SUBLANES]), mult[SUBLANES:]], axis=0)
        b_l = mult * gi * u
        out["xc_tail"] = xc[tb - SUBLANES:, :]
        yield

        probs, norms = [], []
        for (c, i), s in zip(pairs, scores):
            mask = band & (sj >= first_lo) if i == 0 else band
            ps, invs = [], []
            for hh in range(2):
                sink = sinks_ref[layer, 2 * c + hh] * LOG2_E
                sh = jnp.where(mask, s[:, 2 * Q_BLOCK * hh:2 * Q_BLOCK * (hh + 1)], neg_inf)
                m = jnp.maximum(jnp.max(sh, axis=-1, keepdims=True), sink)
                p = jnp.exp2(sh - m)
                invs.append(1.0 / (jnp.sum(p, axis=-1, keepdims=True) + jnp.exp2(sink - m)))
                ps.append(_bf(p))
            probs.append(jnp.concatenate(ps, axis=1))
            norms.append(jnp.where(lo_q, invs[0], invs[1]))
            if i == tb // Q_BLOCK - 1:
                yield
        by_pair = dict(zip(pairs, probs))
        pv = {(g, i): _dot(jnp.concatenate([by_pair[(2 * g, i)], by_pair[(2 * g + 1, i)]], axis=0),
                           win_stack(vvar, 2 * g, i))
              for g in range(A_KV_HEADS) for i in range(tb // Q_BLOCK)}
        for (c, i), inv in zip(pairs, norms):
            r0 = i * Q_BLOCK
            o = pv[(c // 2, i)][(c % 2) * Q_BLOCK:(c % 2 + 1) * Q_BLOCK] * inv
            ga = proj[r0:r0 + Q_BLOCK, OFF_GA + LANES * c:OFF_GA + LANES * (c + 1)]
            mix[r0:r0 + Q_BLOCK, LANES * c:LANES * (c + 1)] = o * _silu(ga)
        yield
        h_l = _affine_scan(a_l, b_l, hprev[SUBLANES - 1:SUBLANES, :])
        gc = proj[:, OFF_GC:OFF_GC + C_WIDTH]
        mix[:, A_WIDTH + B_WIDTH:] = h_l * _silu(gc)
        out["h_tail"] = h_l[tb - SUBLANES:, :]

    def sequence(q, out):
        kvprev, pprev, xcprev, hprev, gstate, proj, mix, ywkv = seq_scratch[q]

        x = x_ref[q]
        brow = pl.ds(pl.program_id(0) * n_seq + q, 1)
        shift_m = mod_ref[0, brow, :]
        scale_m = mod_ref[1, brow, :]
        gate_m = mod_ref[2, brow, :]
        h = _bf(_rms_scale(x) * (npre_ref[...] * (1.0 + scale_m)) + shift_m)
        for src, dst, n in (GRP_PB, GRP_QKV, GRP_XC, GRP_GA, GRP_GB, GRP_GC):
            for off in range(0, n, PROJ_TILE):
                _project(h, win_ref, proj, (src + off, dst + off, min(PROJ_TILE, n - off)))
                yield

        pb = proj[:, OFF_PB:OFF_PB + B_SHIFT_PAD]
        prev = _shift_rows(pb, 1, pprev[...])
        xs = pb + (prev - pb) * mu_ref[...]
        yield
        r, k, v, ld, a, kkn, kmod = _rwkv_pre(xs, vec, wlora_ref[...], ones_bd)
        kka = kkn * a
        yield

        lcum = _cumsum_chunks(ld, CHUNK)
        e_l = jnp.exp(lcum)
        e_nl = jnp.exp(-lcum)
        rt = r * e_l
        at = -kkn * jnp.exp(lcum - ld)
        ktil = kmod * e_nl
        btil = kka * e_nl
        gam = [e_l[s.stop - 1:s.stop, :] for s in sls]
        khat = [ktil[sls[c]] * gam[c] for c in rng]
        bhat = [btil[sls[c]] * gam[c] for c in rng]
        yield
        a_s = [stk(at[s]) for s in sls]
        b_s = [stk(btil[s]) for s in sls]
        k_s = [stk(ktil[s]) for s in sls]
        yield
        v_s = [stk(v[s]) for s in sls]
        bh_s = [stk(bhat[c]) for c in rng]
        sc = [_dot_nt(_bf(jnp.concatenate([at[sls[c]], rt[sls[c]]], axis=0)),
                      jnp.concatenate([b_s[c], k_s[c]], axis=0)) for c in rng]
        yield "side may start"

        pw = [_bf(sc[c][0:CHUNK, 0:B_WIDTH]) * strict_b for c in rng]
        lak = [_bf(sc[c][0:CHUNK, B_WIDTH:]) * strict_b for c in rng]
        mrb = [_bf(sc[c][CHUNK:, 0:B_WIDTH]) * incl_b for c in rng]
        mrk = [_bf(sc[c][CHUNK:, B_WIDTH:]) * incl_b for c in rng]
        tacc = [pw[c].astype(F32) + eye_f for c in rng]
        pw = [_dot_b(pw[c], stk(pw[c])) for c in rng]
        yield
        for _ in range(4):
            res = [_dot(jnp.concatenate([_bf(tacc[c]), pw[c]], axis=0), stk(pw[c])) for c in rng]
            tacc = [tacc[c] + res[c][0:CHUNK] for c in rng]
            pw = [_bf(res[c][CHUNK:]) for c in rng]
            yield
        tinv = [_bf(tacc[c] + _dot(_bf(tacc[c]), stk(pw[c]))) for c in rng]
        lmv = [_dot(jnp.concatenate([lak[c], mrk[c]], axis=0), v_s[c]) for c in rng]
        lv = [_bf(lmv[c][0:CHUNK]) for c in rng]
        yield
        w_c = [_dot_b(tinv[c], a_s[c]) for c in rng]
        z_c = [_dot(tinv[c], stk(lv[c])) for c in rng]
        yield
        w_s = [stk(w_c[c]) for c in rng]
        rp = [_bf(rt[sls[c]] + _dot(mrb[c], w_s[c])) for c in rng]
        y0 = [_dot(mrb[c], stk(z_c[c])) + lmv[c][CHUNK:] for c in rng]
        pp = [_bf(_dot_tn(w_s[c], bh_s[c])) for c in rng]
        qq = [jnp.where(bd_heads,
                        _dot_tn(_bf(jnp.concatenate([z_c[c], v[sls[c]]], axis=0)),
                                _bf(jnp.concatenate([bhat[c], khat[c]], axis=0))), 0.0) for c in rng]
        yield
        gmat = gstate[...]
        for c in rng:
            gb16 = _bf(gmat)
            ywkv[sls[c], :] = _dot_nt(rp[c], gb16) + y0[c]
            gmat = gam[c] * gmat + _dot(gb16, pp[c]) + qq[c]
        gstate[...] = gmat
        yield "side must be done"

        gb = proj[:, OFF_GB:OFF_GB + B_WIDTH]
        mix[:, A_WIDTH:A_WIDTH + B_WIDTH] = _rwkv_post(ywkv[...], r, kmod, v, gb, vec, ones_bd)

        yo = _dot(_bf(mix[...]), wout_ref[...])
        y_ref[q] = x + (gate_m * npost_ref[...]) * _rms_scale(yo)

        k_tail, v_tail = out["k_tail"], out["v_tail"]
        kvprev[:, 0:KV_WIDTH] = k_tail
        kvprev[:, KV_WIDTH:2 * KV_WIDTH] = v_tail
        pprev[...] = pb[tb - SUBLANES:, :]
        xcprev[...] = out["xc_tail"]
        hprev[...] = out["h_tail"]
        shift_ref[q] = pb[tb - SUBLANES:, :]
        wkv_ref[q] = gmat
        conv_ref[q] = out["xc_tail"]
        hlru_ref[q] = out["h_tail"]

        @pl.when(t == pl.num_programs(1) - 1)
        def _():
            kv_ref[q, 0] = k_tail.T
            kv_ref[q, 1] = v_tail.T

    for q in range(n_seq):
        shared = {}
        main, extra = sequence(q, shared), side(q, shared)
        while next(main) != "side may start":
            pass
        extra_done = False
        while True:
            if not extra_done:
                try:
                    next(extra)
                except StopIteration:
                    extra_done = True
            if next(main) == "side must be done":
                break
        for _ in extra:
            pass
        for _ in main:
            pass


def _weight_specs(layer_of, grid_rank):
    def lay(*shape):
        zeros = (0,) * len(shape)
        if grid_rank == 1:
            return pl.BlockSpec((None, *shape), lambda i: (layer_of(i), *zeros))
        return pl.BlockSpec((None, *shape), lambda a, b: (layer_of(a, b), *zeros))
    return [
        lay(1, D_MODEL),
        lay(1, D_MODEL),
        lay(P_IN, D_MODEL),
        lay(D_MODEL, D_MODEL),
    ], [
        lay(1, B_SHIFT_PAD),
        lay(VEC_ROWS, B_WIDTH),
        lay(LANES, 2 * B_WIDTH),
        lay(C_WIDTH, 2 * C_WIDTH),
    ]


def _prompt_layer(layer, x, mod, wts, cos_t, sin_t, mod_row0, tb):
    bsz, seq, _ = x.shape
    nt = seq // tb
    sps = SEQS_PER_STEP
    per_b3 = lambda b, t: (b, 0, 0)
    w_head, w_tail = _weight_specs(lambda b, t: layer, 2)
    return pl.pallas_call(
        functools.partial(_prompt_kernel, layer=layer),
        grid=(bsz // sps, nt),
        in_specs=[
            pl.BlockSpec(memory_space=pltpu.SMEM),
            pl.BlockSpec((sps, tb, D_MODEL), lambda b, t: (b, t, 0)),
            pl.BlockSpec((None, 3, SUBLANES, D_MODEL),
                         lambda b, t: (layer, 0, mod_row0 // SUBLANES, 0)),
            *w_head,
            pl.BlockSpec((tb, LANES), lambda b, t: (t, 0)),
            pl.BlockSpec((tb, LANES), lambda b, t: (t, 0)),
            *w_tail,
        ],
        out_specs=[
            pl.BlockSpec((sps, tb, D_MODEL), lambda b, t: (b, t, 0)),
            pl.BlockSpec((sps, 2, KV_WIDTH, WINDOW), lambda b, t: (b, 0, 0, 0)),
            pl.BlockSpec((sps, SUBLANES, B_SHIFT_PAD), per_b3),
            pl.BlockSpec((sps, B_WIDTH, B_WIDTH), per_b3),
            pl.BlockSpec((sps, SUBLANES, C_WIDTH), per_b3),
            pl.BlockSpec((sps, SUBLANES, C_WIDTH), per_b3),
        ],
        out_shape=[
            jax.ShapeDtypeStruct((bsz, seq, D_MODEL), F32),
            jax.ShapeDtypeStruct((bsz, 2, KV_WIDTH, WINDOW), F32),
            jax.ShapeDtypeStruct((bsz, SUBLANES, B_SHIFT_PAD), F32),
            jax.ShapeDtypeStruct((bsz, B_WIDTH, B_WIDTH), F32),
            jax.ShapeDtypeStruct((bsz, SUBLANES, C_WIDTH), F32),
            jax.ShapeDtypeStruct((bsz, SUBLANES, C_WIDTH), F32),
        ],
        scratch_shapes=[
            pltpu.VMEM((WINDOW, 2 * KV_WIDTH), F32),
            pltpu.VMEM((SUBLANES, B_SHIFT_PAD), F32),
            pltpu.VMEM((SUBLANES, C_WIDTH), F32),
            pltpu.VMEM((SUBLANES, C_WIDTH), F32),
            pltpu.VMEM((B_WIDTH, B_WIDTH), F32),
            pltpu.VMEM((tb, P_PAD), F32),
            pltpu.VMEM((tb, D_MODEL), F32),
            pltpu.VMEM((tb, B_WIDTH), F32),
        ] * sps,
        compiler_params=pltpu.CompilerParams(
            dimension_semantics=("arbitrary", "arbitrary"), vmem_limit_bytes=VMEM_LIMIT_BYTES),
        name="prompt_layer",
    )(wts["sinks"], x, mod, wts["norm_pre"], wts["norm_post"], wts["w_in"], wts["w_out"], cos_t, sin_t,
      wts["mu"], wts["vec"], wts["wlora"], wts["wgate"])


def _sample_kernel(sinks_ref, x_ref, mod_ref, npre_ref, npost_ref, win_ref, wout_ref,
                   cos_ref, sin_ref, mu_ref, vec_ref, wlora_ref, wgate_ref,
                   kt_ref, vt_ref, sprev_ref, wkv_ref, conv_ref, hl_ref,
                   y_ref, knew_ref, vnew_ref, shift_ref, wkvo_ref, convo_ref, hlo_ref,
                   ys_s, proj_s, mix_s, q_s, o_s, knew_s, rw_s, rkv_s, yt_s):
    layer = pl.program_id(0)
    j = pl.program_id(1)
    nb = x_ref.shape[0]
    bb = nb // B_HEADS
    n_slot = kt_ref.shape[1]
    vec = vec_ref[...]
    cos = cos_ref[...]
    sin = sin_ref[...]

    @pl.when(jnp.logical_and(layer == 0, j == 0))
    def _():
        ys_s[...] = x_ref[...]

    @pl.when(j == 0)
    def _():
        ones_bd = _head_ones(B_WIDTH)
        x = ys_s[...]
        h = _bf(_rms_scale(x) * npre_ref[...] * (1.0 + mod_ref[1]) + mod_ref[0])
        for grp in (GRP_QKV, GRP_GA, GRP_PB, GRP_GB, GRP_XC, GRP_GC):
            _project(h, win_ref, proj_s, grp)
        k_new = _rope128(proj_s[:, OFF_K:OFF_K + KV_WIDTH], cos, sin)
        knew_s[...] = k_new
        knew_ref[...] = k_new.T
        vnew_ref[...] = proj_s[:, OFF_V:OFF_V + KV_WIDTH].T
        pb = proj_s[:, OFF_PB:OFF_PB + B_SHIFT_PAD]
        pb_t = pb.T
        shift_ref[...] = pb_t[0:B_SHIFT, :]
        prev = jnp.concatenate([sprev_ref[...], jnp.zeros((B_SHIFT_PAD - B_SHIFT, nb), F32)], axis=0).T
        xs = pb + (prev - pb) * mu_ref[...]
        r, k, v, ld, a, kkn, kmod = _rwkv_pre(xs, vec, wlora_ref[...], ones_bd)
        rw_s[T_DECAY] = jnp.exp(ld).T
        rw_s[T_NKK] = (-kkn).T
        rw_s[T_KKA] = (kkn * a).T
        rw_s[T_KMOD] = kmod.T
        rw_s[T_R] = r.T
        rw_s[T_V] = v.T
        rkv_s[0] = r
        rkv_s[1] = kmod
        rkv_s[2] = v

    rs = pl.ds(pl.multiple_of(j * bb, SUBLANES), bb)
    lane = _iota((bb, LANES), 1)
    lo = lane < HEAD_DIM
    for c in range(A_HEADS // 2):
        g = c // 2
        in_g = lo if g == 0 else jnp.logical_not(lo)
        qcol = _rope128(proj_s[rs, OFF_Q + LANES * c:OFF_Q + LANES * (c + 1)], cos, sin)
        qrol = pltpu.roll(qcol, HEAD_DIM, 1)
        for hh in range(2):
            hd = 2 * c + hh
            q_s[hd * bb:(hd + 1) * bb, :] = jnp.where(in_g, qcol if hh == g else qrol, 0.0)
    knew_blk = knew_s[rs, :]
    vnew_blk = proj_s[rs, OFF_V:OFF_V + KV_WIDTH]
    hrow = _iota((A_HEADS, 1), 0)
    sink_col = jnp.zeros((A_HEADS, 1), F32)
    for hd in range(A_HEADS):
        sink_col = jnp.where(hrow == hd, sinks_ref[layer, hd] * LOG2_E, sink_col)

    qbs = [q_s[pl.ds(b, A_HEADS, stride=bb), :] for b in range(bb)]
    s_c = [_dot(_bf(qbs[b]), _bf(kt_ref[b * KV_WIDTH:(b + 1) * KV_WIDTH, :])) for b in range(bb)]
    s_n = [jnp.sum(qbs[b] * knew_blk[b:b + 1, :], axis=-1, keepdims=True) for b in range(bb)]
    p_c, p_n = [], []
    for b in range(bb):
        m = jnp.maximum(jnp.maximum(jnp.max(s_c[b], axis=-1, keepdims=True), s_n[b]), sink_col)
        e_c = jnp.exp2(s_c[b] - m)
        e_n = jnp.exp2(s_n[b] - m)
        inv = 1.0 / (jnp.sum(e_c, axis=-1, keepdims=True) + e_n + jnp.exp2(sink_col - m))
        p_c.append(_bf(e_c * inv))
        p_n.append(e_n * inv)
    for b in range(bb):
        o = _dot_nt(p_c[b], _bf(vt_ref[b * KV_WIDTH:(b + 1) * KV_WIDTH, :]))
        o_s[pl.ds(b, A_HEADS, stride=bb), :] = o + p_n[b] * vnew_blk[b:b + 1, :]
    for c in range(A_HEADS // 2):
        g = c // 2
        halves = []
        for hh in range(2):
            oh = o_s[(2 * c + hh) * bb:(2 * c + hh + 1) * bb, :]
            halves.append(oh if hh == g else pltpu.roll(oh, HEAD_DIM, 1))
        ga = proj_s[rs, OFF_GA + LANES * c:OFF_GA + LANES * (c + 1)]
        mix_s[rs, LANES * c:LANES * (c + 1)] = jnp.where(lo, halves[0], halves[1]) * _silu(ga)

    hrows = pl.ds(pl.multiple_of(j * HEAD_DIM, HEAD_DIM), HEAD_DIM)
    w_t = rw_s[T_DECAY, hrows, :]
    nkk_t = rw_s[T_NKK, hrows, :]
    kka_t = rw_s[T_KKA, hrows, :]
    k_t = rw_s[T_KMOD, hrows, :]
    r_t = rw_s[T_R, hrows, :]

    def value_row(vi, carry):
        krows = pl.ds(pl.multiple_of(vi * HEAD_DIM, HEAD_DIM), HEAD_DIM)
        st = wkv_ref[krows, :]
        sa = jnp.sum(st * nkk_t, axis=0, keepdims=True)
        st_new = st * w_t + sa * kka_t + rw_s[T_V, pl.ds(j * HEAD_DIM + vi, 1), :] * k_t
        wkvo_ref[krows, :] = st_new
        yt_s[pl.ds(j * HEAD_DIM + vi, 1), :] = jnp.sum(st_new * r_t, axis=0, keepdims=True)
        return carry

    lax.fori_loop(0, HEAD_DIM, value_row, 0, unroll=4)

    @pl.when(j == pl.num_programs(1) - 1)
    def _():
        ones_bd = _head_ones(B_WIDTH)
        x = ys_s[...]
        gb = proj_s[:, OFF_GB:OFF_GB + B_WIDTH]
        mix_s[:, A_WIDTH:A_WIDTH + B_WIDTH] = _rwkv_post(yt_s[...].T, rkv_s[0], rkv_s[1], rkv_s[2], gb, vec, ones_bd)
        xc = proj_s[:, OFF_XC:OFF_XC + C_WIDTH]
        gc = proj_s[:, OFF_GC:OFF_GC + C_WIDTH]
        u = vec[V_CB:V_CB + 1, :] + xc * vec[V_CW3:V_CW3 + 1, :]
        for i in range(CONV_W - 1):
            u = u + conv_ref[i] * vec[V_CW0 + i:V_CW0 + i + 1, :]
        a_l, mult, gi = _lru_gates(u, vec, wgate_ref[...])
        h_l = a_l * hl_ref[...] + mult * gi * u
        hlo_ref[...] = h_l
        convo_ref[0] = conv_ref[1]
        convo_ref[1] = conv_ref[2]
        convo_ref[2] = xc
        mix_s[:, A_WIDTH + B_WIDTH:] = h_l * _silu(gc)
        yo = _dot(_bf(mix_s[...]), wout_ref[...])
        y = x + mod_ref[2] * (_rms_scale(yo) * npost_ref[...])
        ys_s[...] = y
        y_ref[...] = y


def _sample_layers(x, mod, wts, cos_r, sin_r, kt, vt, sprev_t, wkv_t, conv_t, hl):
    nb = x.shape[0]
    bb = nb // B_HEADS
    n_slot = kt.shape[2]
    n_hv = HEAD_DIM * HEAD_DIM
    w_head, w_tail = _weight_specs(lambda l, j: l, 2)
    const2 = lambda l, j: (0, 0)
    lay3 = lambda l, j: (l, 0, 0)
    return pl.pallas_call(
        _sample_kernel,
        grid=(DEPTH, B_HEADS),
        in_specs=[
            pl.BlockSpec(memory_space=pltpu.SMEM),
            pl.BlockSpec((nb, D_MODEL), const2),
            pl.BlockSpec((None, 3, nb, D_MODEL), lambda l, j: (l, 0, 0, 0)),
            *w_head,
            pl.BlockSpec((1, LANES), const2),
            pl.BlockSpec((1, LANES), const2),
            *w_tail,
            pl.BlockSpec((None, bb * KV_WIDTH, n_slot), lambda l, j: (l, j, 0)),
            pl.BlockSpec((None, bb * KV_WIDTH, n_slot), lambda l, j: (l, j, 0)),
            pl.BlockSpec((None, B_SHIFT, nb), lay3),
            pl.BlockSpec((None, n_hv, nb), lambda l, j: (l, j, 0)),
            pl.BlockSpec((None, CONV_W - 1, nb, C_WIDTH), lambda l, j: (l, 0, 0, 0)),
            pl.BlockSpec((None, nb, C_WIDTH), lay3),
        ],
        out_specs=[
            pl.BlockSpec((nb, D_MODEL), const2),
            pl.BlockSpec((None, KV_WIDTH, nb), lay3),
            pl.BlockSpec((None, KV_WIDTH, nb), lay3),
            pl.BlockSpec((None, B_SHIFT, nb), lay3),
            pl.BlockSpec((None, n_hv, nb), lambda l, j: (l, j, 0)),
            pl.BlockSpec((None, CONV_W - 1, nb, C_WIDTH), lambda l, j: (l, 0, 0, 0)),
            pl.BlockSpec((None, nb, C_WIDTH), lay3),
        ],
        out_shape=[
            jax.ShapeDtypeStruct((nb, D_MODEL), F32),
            jax.ShapeDtypeStruct((DEPTH, KV_WIDTH, nb), F32),
            jax.ShapeDtypeStruct((DEPTH, KV_WIDTH, nb), F32),
            jax.ShapeDtypeStruct((DEPTH, B_SHIFT, nb), F32),
            jax.ShapeDtypeStruct((DEPTH, B_HEADS * n_hv, nb), F32),
            jax.ShapeDtypeStruct((DEPTH, CONV_W - 1, nb, C_WIDTH), F32),
            jax.ShapeDtypeStruct((DEPTH, nb, C_WIDTH), F32),
        ],
        scratch_shapes=[
            pltpu.VMEM((nb, D_MODEL), F32),
            pltpu.VMEM((nb, P_PAD), F32),
            pltpu.VMEM((nb, D_MODEL), F32),
            pltpu.VMEM((A_HEADS * bb, LANES), F32),
            pltpu.VMEM((A_HEADS * bb, LANES), F32),
            pltpu.VMEM((nb, KV_WIDTH), F32),
            pltpu.VMEM((6, B_WIDTH, nb), F32),
            pltpu.VMEM((3, nb, B_WIDTH), F32),
            pltpu.VMEM((B_WIDTH, nb), F32),
        ],
        compiler_params=pltpu.CompilerParams(
            dimension_semantics=("arbitrary", "arbitrary"), vmem_limit_bytes=VMEM_LIMIT_BYTES),
        name="sample_layers",
    )(wts["sinks"], x, mod, wts["norm_pre"], wts["norm_post"], wts["w_in"], wts["w_out"], cos_r, sin_r,
      wts["mu"], wts["vec"], wts["wlora"], wts["wgate"], kt, vt, sprev_t, wkv_t, conv_t, hl)


def _rope_lanes():
    half = HEAD_DIM // 2
    inv_freq = ROPE_THETA ** (-jnp.arange(half, dtype=F32) / half)
    freq = jnp.tile(inv_freq, LANES // half)
    sign = jnp.tile(jnp.concatenate([-jnp.ones((half,), F32), jnp.ones((half,), F32)]), LANES // HEAD_DIM)
    return freq, sign


def _rope_row(pos):
    freq, sign = _rope_lanes()
    ang = jnp.float32(pos) * freq[None, :]
    return jnp.cos(ang), jnp.sin(ang) * sign[None, :]


def _rope_tables(seq):
    freq, sign = _rope_lanes()
    hi = (jnp.arange(seq // Q_BLOCK, dtype=F32) * Q_BLOCK)[:, None] * freq[None, :]
    lo = jnp.arange(Q_BLOCK, dtype=F32)[:, None] * freq[None, :]
    ch, sh = jnp.cos(hi)[:, None, :], jnp.sin(hi)[:, None, :]
    cl, sl = jnp.cos(lo)[None, :, :], jnp.sin(lo)[None, :, :]
    cos_t = (ch * cl - sh * sl).reshape(seq, LANES)
    sin_t = ((sh * cl + ch * sl) * sign).reshape(seq, LANES)
    return cos_t, sin_t


def _block_diag(w):
    dl, n, d, e = w.shape
    eye = jnp.eye(n, dtype=w.dtype)
    return (eye[None, :, None, :, None] * w[:, :, :, None, :]).reshape(dl, n * d, n * e)


def _prep_weights(p):
    dl = p["w_in"].shape[0]
    w_in_t = jnp.swapaxes(p["w_in"], 1, 2)
    q_scale = jnp.where(jnp.arange(w_in_t.shape[1]) < A_WIDTH, Q_SCALE, 1.0).astype(F32)
    w_in_t = (w_in_t * q_scale[None, :, None]).astype(BF16)
    z32 = jnp.zeros((dl, LORA, B_WIDTH), F32)
    wlora = jnp.concatenate([
        jnp.concatenate([p["rwkv_w_up"], z32], axis=2),
        jnp.concatenate([z32, p["rwkv_a_up"]], axis=2),
        jnp.zeros((dl, LANES - 2 * LORA, 2 * B_WIDTH), F32)], axis=1).astype(BF16)
    wgate = jnp.concatenate([_block_diag(p["lru_gate_a_w"]), _block_diag(p["lru_gate_x_w"])], axis=2).astype(BF16)
    cw = p["lru_conv_w"]
    rows = [p["rwkv_w0"], p["rwkv_a0"], p["rwkv_k_k"], p["rwkv_k_a"], p["rwkv_lnx_w"], p["rwkv_lnx_b"],
            p["rwkv_r_k"].reshape(dl, B_WIDTH), p["lru_conv_b"], p["lru_gate_a_b"], p["lru_gate_x_b"],
            p["lru_lambda"], cw[:, 0], cw[:, 1], cw[:, 2], cw[:, 3], jnp.zeros((dl, B_WIDTH), F32)]
    return {
        "sinks": p["attn_sinks"],
        "norm_pre": p["norm_pre"].reshape(dl, 1, D_MODEL),
        "norm_post": p["norm_post"].reshape(dl, 1, D_MODEL),
        "w_in": w_in_t,
        "w_out": p["w_out"].astype(BF16),
        "mu": jnp.concatenate([p["rwkv_mu"], jnp.zeros((dl, B_SHIFT_PAD - B_SHIFT), F32)],
                              axis=1).reshape(dl, 1, B_SHIFT_PAD),
        "vec": jnp.stack(rows, axis=1),
        "wlora": wlora,
        "wgate": wgate,
    }


def _forward(x_prompt, x_sample, c_prompt, c_sample, cache_swa_k, cache_swa_v, state_rwkv_shift,
             state_rwkv_wkv, state_lru_conv, state_lru_h, p, tb):
    bp, seq, _ = x_prompt.shape
    nb = x_sample.shape[0]
    n_slot = cache_swa_k.shape[2]
    assert nb % SUBLANES == 0 and nb % B_HEADS == 0 and bp <= SUBLANES and seq % tb == 0

    pad_rows = (-(nb + bp)) % SUBLANES
    c_all = jnp.concatenate([c_sample, c_prompt, jnp.zeros((pad_rows, D_MODEL), F32)], axis=0)
    mod = _mod_call(c_all, p["w_mod"], p["b_mod"])
    wts = _prep_weights(p)

    cos_p, sin_p = _rope_tables(seq)
    cos_s, sin_s = _rope_row(PAST_LEN)

    kt = jnp.transpose(cache_swa_k, (0, 1, 3, 4, 2)).reshape(DEPTH, nb * KV_WIDTH, n_slot)
    vt = jnp.transpose(cache_swa_v, (0, 1, 3, 4, 2)).reshape(DEPTH, nb * KV_WIDTH, n_slot)
    sprev_t = jnp.swapaxes(state_rwkv_shift, 1, 2)
    wkv_t = jnp.transpose(state_rwkv_wkv, (0, 2, 3, 4, 1)).reshape(DEPTH, B_HEADS * HEAD_DIM * HEAD_DIM, nb)
    conv_t = jnp.swapaxes(state_lru_conv, 1, 2)
    ys, knew_t, vnew_t, shift_t, wkvo_t, convo_t, hlo = _sample_layers(
        x_sample.reshape(nb, D_MODEL), mod, wts, cos_s, sin_s, kt, vt, sprev_t, wkv_t, conv_t, state_lru_h)
    outs_s = (
        jnp.transpose(knew_t.reshape(DEPTH, 1, A_KV_HEADS, HEAD_DIM, nb), (0, 4, 1, 2, 3)),
        jnp.transpose(vnew_t.reshape(DEPTH, 1, A_KV_HEADS, HEAD_DIM, nb), (0, 4, 1, 2, 3)),
        jnp.swapaxes(shift_t, 1, 2),
        jnp.transpose(wkvo_t.reshape(DEPTH, B_HEADS, HEAD_DIM, HEAD_DIM, nb), (0, 4, 1, 2, 3)),
        jnp.swapaxes(convo_t, 1, 2),
        hlo,
    )

    yp = x_prompt
    outs_p = []
    for l in range(DEPTH):
        yp, kv_p, sh_p, wkv_p, conv_p, h_p = _prompt_layer(l, yp, mod, wts, cos_p, sin_p, nb, tb)
        kv_p = jnp.transpose(kv_p.reshape(bp, 2, A_KV_HEADS, HEAD_DIM, WINDOW), (1, 0, 4, 2, 3))
        outs_p.append((
            kv_p[0], kv_p[1],
            sh_p[:, SUBLANES - 1, :B_SHIFT],
            jnp.stack([wkv_p[:, HEAD_DIM * hd:HEAD_DIM * (hd + 1), HEAD_DIM * hd:HEAD_DIM * (hd + 1)]
                       for hd in range(B_HEADS)], axis=1),
            conv_p[:, SUBLANES - (CONV_W - 1):, :],
            h_p[:, SUBLANES - 1, :],
        ))
    sp = [jnp.stack(z) for z in zip(*outs_p)]
    return (yp, ys.reshape(nb, 1, D_MODEL), *sp, *outs_s)


def kernel(x_prompt, x_sample, c_prompt, c_sample, cache_swa_k, cache_swa_v, state_rwkv_shift, state_rwkv_wkv, state_lru_conv, state_lru_h, norm_pre, norm_post, w_mod, b_mod, w_in, w_out, attn_sinks, rwkv_mu, rwkv_w0, rwkv_w_up, rwkv_a0, rwkv_a_up, rwkv_k_k, rwkv_k_a, rwkv_r_k, rwkv_lnx_w, rwkv_lnx_b, lru_conv_w, lru_conv_b, lru_gate_a_w, lru_gate_a_b, lru_gate_x_w, lru_gate_x_b, lru_lambda):
    p = dict(norm_pre=norm_pre, norm_post=norm_post, w_mod=w_mod, b_mod=b_mod, w_in=w_in, w_out=w_out,
             attn_sinks=attn_sinks, rwkv_mu=rwkv_mu, rwkv_w0=rwkv_w0, rwkv_w_up=rwkv_w_up, rwkv_a0=rwkv_a0,
             rwkv_a_up=rwkv_a_up, rwkv_k_k=rwkv_k_k, rwkv_k_a=rwkv_k_a, rwkv_r_k=rwkv_r_k,
             rwkv_lnx_w=rwkv_lnx_w, rwkv_lnx_b=rwkv_lnx_b, lru_conv_w=lru_conv_w, lru_conv_b=lru_conv_b,
             lru_gate_a_w=lru_gate_a_w, lru_gate_a_b=lru_gate_a_b, lru_gate_x_w=lru_gate_x_w,
             lru_gate_x_b=lru_gate_x_b, lru_lambda=lru_lambda)
    return _forward(x_prompt, x_sample, c_prompt, c_sample, cache_swa_k, cache_swa_v, state_rwkv_shift,
                    state_rwkv_wkv, state_lru_conv, state_lru_h, p, TIME_BLOCK)
```
